```python
import jax, jax.numpy as jnp
from jax import lax
import numpy as np

D_MODEL = 4096
BATCH = 4
SEQ = 4096
DEPTH = 1

CTX_LEN = 256
GRID_W = 64
HEAD_DIM = 128
N_Q_HEADS = 16
N_KV_HEADS = 4
Q_PER_KV = N_Q_HEADS // N_KV_HEADS
ATTN_WIDTH = N_Q_HEADS * HEAD_DIM
KV_WIDTH = N_KV_HEADS * HEAD_DIM
CONV_WIDTH = D_MODEL // 2
CONV_TAPS = 31
ROPE_THETA = 10000.0
ROPE_AXIS_DIM = HEAD_DIM // 2
Q_BLOCK = 128
N_GROUPS = 4
EXPERTS_PER_GROUP = 8
N_EXPERTS = N_GROUPS * EXPERTS_PER_GROUP
TOP_K = 2
EXPERT_FF = 1024
MOE_BLOCK = 128
N_MOD = 6
EPS = 1e-6

Q_OFF = 0
K_OFF = Q_OFF + ATTN_WIDTH
V_OFF = K_OFF + KV_WIDTH
GLU_OFF = V_OFF + KV_WIDTH
GATE_OFF = GLU_OFF + 2 * CONV_WIDTH
IN_WIDTH = GATE_OFF + 2 * D_MODEL

kernel_name = "hybrid_gqa_conformer_hmoe_dit"


def rmsnorm(x, g):
    xf = x.astype(jnp.float32)
    y = xf * lax.rsqrt(jnp.mean(xf * xf, axis=-1, keepdims=True) + EPS)
    return (y * g.astype(jnp.float32)).astype(x.dtype)


def modulate(x, shift, scale):
    return x * (1 + scale) + shift


def axial_rope_tables(n_tokens):
    rows = n_tokens // GRID_W
    row, col = jnp.meshgrid(jnp.arange(rows), jnp.arange(GRID_W), indexing="ij")
    pos = jnp.stack([row.reshape(-1), col.reshape(-1)], axis=-1).astype(jnp.float32)
    inv = ROPE_THETA ** (-jnp.arange(0, ROPE_AXIS_DIM, 2, dtype=jnp.float32) / ROPE_AXIS_DIM)
    ang = pos[:, :, None] * inv[None, None, :]
    return jnp.cos(ang), jnp.sin(ang)


def apply_rope(x, cos, sin):
    xf = x.astype(jnp.float32).reshape(*x.shape[:-1], 2, 2, ROPE_AXIS_DIM // 2)
    x1, x2 = xf[..., 0, :], xf[..., 1, :]
    c = cos[None, :, None]
    s = sin[None, :, None]
    out = jnp.stack([x1 * c - x2 * s, x2 * c + x1 * s], axis=-2)
    return out.reshape(x.shape).astype(x.dtype)


def attend(q, k, v):
    s = jnp.einsum("bqkgd,bskd->bkgqs", q, k, preferred_element_type=jnp.float32) * (HEAD_DIM ** -0.5)
    p = jax.nn.softmax(s, axis=-1).astype(v.dtype)
    return jnp.einsum("bkgqs,bskd->bqkgd", p, v)


def latent_attention(q, k_all, v_all):
    b, s = q.shape[:2]
    nb = s // Q_BLOCK
    qb = q.reshape(b, nb, Q_BLOCK, N_KV_HEADS, Q_PER_KV, HEAD_DIM).transpose(1, 0, 2, 3, 4, 5)
    o = lax.map(lambda qi: attend(qi, k_all, v_all), qb)
    return o.transpose(1, 0, 2, 3, 4, 5).reshape(b, s, ATTN_WIDTH)


def conformer_conv(glu, w_dw, b_dw, ln_g, ln_b):
    a, gt = jnp.split(glu, 2, axis=-1)
    u = a * jax.nn.sigmoid(gt)
    y = lax.conv_general_dilated(u, w_dw.astype(u.dtype), window_strides=(1,),
                                 padding=[(CONV_TAPS // 2, CONV_TAPS // 2)],
                                 dimension_numbers=("NWC", "WIO", "NWC"),
                                 feature_group_count=CONV_WIDTH)
    y = y.astype(jnp.float32) + b_dw.astype(jnp.float32)
    mu = jnp.mean(y, axis=-1, keepdims=True)
    var = jnp.mean(jnp.square(y - mu), axis=-1, keepdims=True)
    y = (y - mu) * lax.rsqrt(var + EPS) * ln_g.astype(jnp.float32) + ln_b.astype(jnp.float32)
    return jax.nn.silu(y).astype(glu.dtype)


def merge_branches(attn_flat, glu, gates, w_dw, b_dw, ln_g, ln_b, w_attn_out, w_conv_out, w_out):
    a = attn_flat @ w_attn_out
    cb = conformer_conv(glu, w_dw, b_dw, ln_g, ln_b) @ w_conv_out
    g_a, g_c = jnp.split(jax.nn.sigmoid(gates), 2, axis=-1)
    return (g_a * a + g_c * cb) @ w_out


def hier_moe(h, w_rg, b_rg, w_re, b_re, w_eg, w_eu, w_ed):
    n, d = h.shape
    hf = h.astype(jnp.float32)
    g_logits = hf @ w_rg.astype(jnp.float32) + b_rg.astype(jnp.float32)
    g_prob = jax.nn.softmax(g_logits, axis=-1)
    g_sel = jnp.argmax(g_logits, axis=-1)
    p_g = jnp.take_along_axis(g_prob, g_sel[:, None], axis=1)[:, 0]
    e_logits = (hf @ w_re.astype(jnp.float32) + b_re.astype(jnp.float32)).reshape(n, N_GROUPS, EXPERTS_PER_GROUP)
    e_in = jnp.take_along_axis(e_logits, g_sel[:, None, None], axis=1)[:, 0]
    top_v, top_i = lax.top_k(e_in, TOP_K)
    weights = p_g[:, None] * jax.nn.softmax(top_v, axis=-1)
    expert = g_sel[:, None].astype(jnp.int32) * EXPERTS_PER_GROUP + top_i.astype(jnp.int32)

    nk = n * TOP_K
    flat_e = expert.reshape(-1)
    flat_w = weights.reshape(-1)
    flat_t = jnp.arange(nk, dtype=jnp.int32) // TOP_K
    order = jnp.argsort(flat_e)
    se, st, sw = flat_e[order], flat_t[order], flat_w[order]
    counts = jnp.zeros((N_EXPERTS,), jnp.int32).at[flat_e].add(1)
    padded = ((counts + MOE_BLOCK - 1) // MOE_BLOCK) * MOE_BLOCK
    pend = jnp.cumsum(padded)
    pstart = pend - padded
    cstart = jnp.cumsum(counts) - counts
    dest = pstart[se] + jnp.arange(nk, dtype=jnp.int32) - cstart[se]
    nblk = (nk + N_EXPERTS * (MOE_BLOCK - 1) + MOE_BLOCK - 1) // MOE_BLOCK
    total = nblk * MOE_BLOCK
    buf_t = jnp.zeros((total,), jnp.int32).at[dest].set(st)
    buf_w = jnp.zeros((total,), jnp.float32).at[dest].set(sw)
    blk_e = jnp.minimum(jnp.searchsorted(pend, jnp.arange(nblk, dtype=jnp.int32) * MOE_BLOCK, side="right"),
                        N_EXPERTS - 1)

    def run(args):
        t, w, e = args
        xb = h[t]
        y = (jax.nn.silu(xb @ w_eg[e]) * (xb @ w_eu[e])) @ w_ed[e]
        return y * w[:, None].astype(y.dtype)

    ys = lax.map(run, (buf_t.reshape(nblk, MOE_BLOCK), buf_w.reshape(nblk, MOE_BLOCK), blk_e))
    return jnp.zeros_like(h).at[buf_t].add(ys.reshape(total, d))


def hybrid_layer(x, xc, c, c_ctx, cos, sin, norm1_g, w_mod, b_mod, w_in, q_norm_g, k_norm_g,
                 w_attn_out, conv_dw_w, conv_dw_b, conv_ln_g, conv_ln_b, w_conv_out, w_out,
                 norm2_g, w_rg, b_rg, w_re, b_re, w_eg, w_eu, w_ed, last):
    b, s, d = x.shape
    n_ctx = xc.shape[1]
    mod = (jax.nn.silu(c) @ w_mod + b_mod)[:, None, :]
    mod_c = (jax.nn.silu(c_ctx) @ w_mod + b_mod)[None, None, :]
    sh1, sc1, ga1, sh2, sc2, ga2 = jnp.split(mod, N_MOD, axis=-1)
    csh1, csc1, cga1, csh2, csc2, cga2 = jnp.split(mod_c, N_MOD, axis=-1)
    conv_args = (conv_dw_w, conv_dw_b, conv_ln_g, conv_ln_b, w_attn_out, w_conv_out, w_out)

    h = modulate(rmsnorm(x, norm1_g), sh1, sc1)
    hc = modulate(rmsnorm(xc, norm1_g), csh1, csc1)
    p = h @ w_in
    q = apply_rope(rmsnorm(p[..., Q_OFF:K_OFF].reshape(b, s, N_Q_HEADS, HEAD_DIM), q_norm_g), cos, sin)
    k = apply_rope(rmsnorm(p[..., K_OFF:V_OFF].reshape(b, s, N_KV_HEADS, HEAD_DIM), k_norm_g), cos, sin)
    v = p[..., V_OFF:GLU_OFF].reshape(b, s, N_KV_HEADS, HEAD_DIM)
    col0 = K_OFF if last else 0
    col1 = GLU_OFF if last else IN_WIDTH
    pc = hc @ w_in[:, col0:col1]
    kc = rmsnorm(pc[..., K_OFF - col0:V_OFF - col0].reshape(b, n_ctx, N_KV_HEADS, HEAD_DIM), k_norm_g)
    vc = pc[..., V_OFF - col0:GLU_OFF - col0].reshape(b, n_ctx, N_KV_HEADS, HEAD_DIM)
    k_all = jnp.concatenate([k, kc], axis=1)
    v_all = jnp.concatenate([v, vc], axis=1)
    attn = latent_attention(q.reshape(b, s, N_KV_HEADS, Q_PER_KV, HEAD_DIM), k_all, v_all)
    x = x + ga1 * merge_branches(attn, p[..., GLU_OFF:GATE_OFF], p[..., GATE_OFF:], *conv_args)
    if not last:
        qc = rmsnorm(pc[..., Q_OFF:K_OFF].reshape(b, n_ctx, N_KV_HEADS, Q_PER_KV, HEAD_DIM), q_norm_g)
        attn_c = attend(qc, kc, vc).reshape(b, n_ctx, ATTN_WIDTH)
        xc = xc + cga1 * merge_branches(attn_c, pc[..., GLU_OFF:GATE_OFF], pc[..., GATE_OFF:], *conv_args)

    h2 = modulate(rmsnorm(x, norm2_g), sh2, sc2).reshape(b * s, d)
    if last:
        y = hier_moe(h2, w_rg, b_rg, w_re, b_re, w_eg, w_eu, w_ed)
    else:
        h2c = modulate(rmsnorm(xc, norm2_g), csh2, csc2).reshape(b * n_ctx, d)
        y_all = hier_moe(jnp.concatenate([h2, h2c], axis=0), w_rg, b_rg, w_re, b_re, w_eg, w_eu, w_ed)
        y = y_all[:b * s]
        xc = xc + cga2 * y_all[b * s:].reshape(b, n_ctx, d)
    x = x + ga2 * y.reshape(b, s, d)
    return x, xc


def setup_inputs(seed: int = 0) -> dict:
    key = jax.random.key(seed)
    ks = jax.random.split(key, 32)
    f32 = jnp.float32
    L, D = DEPTH, D_MODEL

    def nrm(k, shape, scale):
        return jax.random.normal(k, shape, f32) * scale

    return {
        "x": nrm(ks[0], (BATCH, SEQ, D), 1.0),
        "c": nrm(ks[1], (BATCH, D), 1.0),
        "ctx": nrm(ks[2], (BATCH, CTX_LEN, D), 1.0),
        "c_ctx": nrm(ks[3], (D,), 1.0),
        "norm1_g": 1.0 + nrm(ks[4], (L, D), 0.02),
        "w_mod": nrm(ks[5], (L, D, N_MOD * D), 0.5 * D ** -0.5),
        "b_mod": nrm(ks[6], (L, N_MOD * D), 0.02),
        "w_in": nrm(ks[7], (L, D, IN_WIDTH), D ** -0.5),
        "q_norm_g": 1.0 + nrm(ks[8], (L, HEAD_DIM), 0.02),
        "k_norm_g": 1.0 + nrm(ks[9], (L, HEAD_DIM), 0.02),
        "w_attn_out": nrm(ks[10], (L, ATTN_WIDTH, D), ATTN_WIDTH ** -0.5),
        "conv_dw_w": nrm(ks[11], (L, CONV_TAPS, 1, CONV_WIDTH), CONV_TAPS ** -0.5),
        "conv_dw_b": nrm(ks[12], (L, CONV_WIDTH), 0.02),
        "conv_ln_g": 1.0 + nrm(ks[13], (L, CONV_WIDTH), 0.02),
        "conv_ln_b": nrm(ks[14], (L, CONV_WIDTH), 0.02),
        "w_conv_out": nrm(ks[15], (L, CONV_WIDTH, D), CONV_WIDTH ** -0.5),
        "w_out": nrm(ks[16], (L, D, D), D ** -0.5),
        "norm2_g": 1.0 + nrm(ks[17], (L, D), 0.02),
        "w_router_group": nrm(ks[18], (L, D, N_GROUPS), D ** -0.5),
        "b_router_group": nrm(ks[19], (L, N_GROUPS), 0.01),
        "w_router_expert": nrm(ks[20], (L, D, N_EXPERTS), D ** -0.5),
        "b_router_expert": nrm(ks[21], (L, N_EXPERTS), 0.01),
        "w_exp_gate": nrm(ks[22], (L, N_EXPERTS, D, EXPERT_FF), D ** -0.5),
        "w_exp_up": nrm(ks[23], (L, N_EXPERTS, D, EXPERT_FF), D ** -0.5),
        "w_exp_down": nrm(ks[24], (L, N_EXPERTS, EXPERT_FF, D), EXPERT_FF ** -0.5),
        "norm_f_g": 1.0 + nrm(ks[25], (D,), 0.02),
    }


def reference(x, c, ctx, c_ctx, norm1_g, w_mod, b_mod, w_in, q_norm_g, k_norm_g, w_attn_out,
              conv_dw_w, conv_dw_b, conv_ln_g, conv_ln_b, w_conv_out, w_out, norm2_g,
              w_router_group, b_router_group, w_router_expert, b_router_expert,
              w_exp_gate, w_exp_up, w_exp_down, norm_f_g):
    cos, sin = axial_rope_tables(x.shape[1])
    xc = ctx
    for l in range(DEPTH):
        x, xc = hybrid_layer(
            x, xc, c, c_ctx, cos, sin, norm1_g[l], w_mod[l], b_mod[l], w_in[l], q_norm_g[l], k_norm_g[l],
            w_attn_out[l], conv_dw_w[l], conv_dw_b[l], conv_ln_g[l], conv_ln_b[l], w_conv_out[l], w_out[l],
            norm2_g[l], w_router_group[l], b_router_group[l], w_router_expert[l], b_router_expert[l],
            w_exp_gate[l], w_exp_up[l], w_exp_down[l], last=(l == DEPTH - 1))
    return rmsnorm(x, norm_f_g)
```

```python
import functools

import jax
import jax.numpy as jnp
from jax import lax
from jax.experimental import pallas as pl
from jax.experimental.pallas import tpu as pltpu

F32 = jnp.float32
BF16 = jnp.bfloat16

GRID_W = 64
HEAD_DIM = 128
N_Q_HEADS = 16
N_KV_HEADS = 4
Q_PER_KV = N_Q_HEADS // N_KV_HEADS
ATTN_WIDTH = N_Q_HEADS * HEAD_DIM
KV_WIDTH = N_KV_HEADS * HEAD_DIM
CONV_TAPS = 31
CONV_HALO = 16
ROPE_THETA = 10000.0
ROPE_AXIS_DIM = HEAD_DIM // 2
N_GROUPS = 4
EXPERTS_PER_GROUP = 8
N_EXPERTS = N_GROUPS * EXPERTS_PER_GROUP
TOP_K = 2
N_MOD = 6
EPS = 1e-6
LOG2E = 1.4426950408889634
ROUTER_LANES = 128

V7X_VMEM_LIMIT = 56 * 1024 * 1024


def _params(*sem):
    return pltpu.CompilerParams(dimension_semantics=sem, vmem_limit_bytes=V7X_VMEM_LIMIT)


def _sigmoid(x):
    return 1.0 / (1.0 + jnp.exp(-x))


def _silu(x):
    return x * _sigmoid(x)


def _rms(x, g):
    return x * lax.rsqrt(jnp.mean(x * x, axis=-1, keepdims=True) + EPS) * g


def _mod_kernel(c_ref, w_ref, b_ref, o_ref):
    s = _silu(c_ref[...]).astype(BF16)
    o_ref[...] = jnp.dot(s, w_ref[...].astype(BF16), preferred_element_type=F32) + b_ref[...]


def _mod_vectors(cvec, w_mod, b_mod, tn=512):
    m, d = cvec.shape
    n = w_mod.shape[1]
    return pl.pallas_call(
        _mod_kernel,
        grid=(n // tn,),
        in_specs=[pl.BlockSpec((m, d), lambda j: (0, 0)),
                  pl.BlockSpec((d, tn), lambda j: (0, j)),
                  pl.BlockSpec((1, tn), lambda j: (0, j))],
        out_specs=pl.BlockSpec((m, tn), lambda j: (0, j)),
        out_shape=jax.ShapeDtypeStruct((m, n), F32),
        compiler_params=_params("arbitrary"),
        name="mod_vectors",
    )(cvec, w_mod, b_mod)


def _norm_mod_kernel(x_ref, g_ref, sh_ref, sc_ref, o_ref):
    y = _rms(x_ref[0], g_ref[...])
    o_ref[0] = (y * (1.0 + sc_ref[0]) + sh_ref[0]).astype(o_ref.dtype)


def _norm_mod(x, g, shift, scale, tl):
    b, l, d = x.shape
    per_batch = shift.shape[0] > 1
    mod_map = (lambda bi, li: (bi, 0, 0)) if per_batch else (lambda bi, li: (0, 0, 0))
    return pl.pallas_call(
        _norm_mod_kernel,
        grid=(b, l // tl),
        in_specs=[pl.BlockSpec((1, tl, d), lambda bi, li: (bi, li, 0)),
                  pl.BlockSpec((1, d), lambda bi, li: (0, 0)),
                  pl.BlockSpec((1, 1, d), mod_map),
                  pl.BlockSpec((1, 1, d), mod_map)],
        out_specs=pl.BlockSpec((1, tl, d), lambda bi, li: (bi, li, 0)),
        out_shape=jax.ShapeDtypeStruct((b, l, d), BF16),
        compiler_params=_params("parallel", "parallel"),
        name="norm_modulate",
    )(x, g, shift, scale)


def _head_norm_rope(a, g, cos, sin):
    y = _rms(a, g)
    if cos is None:
        return y
    lane = lax.broadcasted_iota(jnp.int32, y.shape, 1)
    quarter = ROPE_AXIS_DIM // 2
    partner = jnp.where((lane % ROPE_AXIS_DIM) < quarter,
                        pltpu.roll(y, HEAD_DIM - quarter, 1), pltpu.roll(y, quarter, 1))
    return y * cos + partner * sin


def _q_proj_kernel(h_ref, w_ref, g_ref, cos_ref, sin_ref, o_ref, *, scale):
    acc = jnp.dot(h_ref[0], w_ref[...], preferred_element_type=F32)
    for hh in range(o_ref.shape[1]):
        a = acc[:, hh * HEAD_DIM:(hh + 1) * HEAD_DIM]
        y = _head_norm_rope(a, g_ref[...], cos_ref[...], sin_ref[...])
        o_ref[0, hh] = (y * scale).astype(o_ref.dtype)


def _q_proj(h, w, g, cos_t, sin_t, col_off, scale, tm, tn=1024):
    b, l, d = h.shape
    jb = col_off // tn
    hpt = tn // HEAD_DIM
    return pl.pallas_call(
        functools.partial(_q_proj_kernel, scale=scale),
        grid=(b, l // tm, ATTN_WIDTH // tn),
        in_specs=[pl.BlockSpec((1, tm, d), lambda bi, i, j: (bi, i, 0)),
                  pl.BlockSpec((d, tn), lambda bi, i, j: (0, jb + j)),
                  pl.BlockSpec((1, HEAD_DIM), lambda bi, i, j: (0, 0)),
                  pl.BlockSpec((tm, HEAD_DIM), lambda bi, i, j: (i, 0)),
                  pl.BlockSpec((tm, HEAD_DIM), lambda bi, i, j: (i, 0))],
        out_specs=pl.BlockSpec((1, hpt, tm, HEAD_DIM), lambda bi, i, j: (bi, j, i, 0)),
        out_shape=jax.ShapeDtypeStruct((b, N_Q_HEADS, l, HEAD_DIM), BF16),
        compiler_params=_params("parallel", "parallel", "arbitrary"),
        name="q_proj",
    )(h, w, g, cos_t, sin_t)


def _kv_proj_kernel(h_ref, w_ref, g_ref, *rest, rope):
    if rope:
        cos_ref, sin_ref, k_ref, v_ref = rest
        cos, sin = cos_ref[...], sin_ref[...]
    else:
        k_ref, v_ref = rest
        cos = sin = None
    acc = jnp.dot(h_ref[0], w_ref[...], preferred_element_type=F32)
    for hh in range(N_KV_HEADS):
        a = acc[:, hh * HEAD_DIM:(hh + 1) * HEAD_DIM]
        k_ref[0, :, hh * HEAD_DIM:(hh + 1) * HEAD_DIM] = _head_norm_rope(a, g_ref[...], cos, sin).astype(k_ref.dtype)
    v_ref[0] = acc[:, KV_WIDTH:].astype(v_ref.dtype)


def _kv_proj(h, w, g, cos_t, sin_t, col_off, tm):
    b, l, d = h.shape
    tn = 2 * KV_WIDTH
    jb = col_off // tn
    rope = cos_t is not None
    in_specs = [pl.BlockSpec((1, tm, d), lambda bi, i: (bi, i, 0)),
                pl.BlockSpec((d, tn), lambda bi, i: (0, jb)),
                pl.BlockSpec((1, HEAD_DIM), lambda bi, i: (0, 0))]
    args = [h, w, g]
    if rope:
        in_specs += [pl.BlockSpec((tm, HEAD_DIM), lambda bi, i: (i, 0))] * 2
        args += [cos_t, sin_t]
    return pl.pallas_call(
        functools.partial(_kv_proj_kernel, rope=rope),
        grid=(b, l // tm),
        in_specs=in_specs,
        out_specs=[pl.BlockSpec((1, tm, KV_WIDTH), lambda bi, i: (bi, i, 0))] * 2,
        out_shape=[jax.ShapeDtypeStruct((b, l, KV_WIDTH), BF16)] * 2,
        compiler_params=_params("parallel", "parallel"),
        name="kv_proj_rope" if rope else "kv_proj_ctx",
    )(*args)


def _glu_proj_kernel(h_ref, wa_ref, wg_ref, o_ref):
    a = jnp.dot(h_ref[...], wa_ref[...], preferred_element_type=F32)
    gt = jnp.dot(h_ref[...], wg_ref[...], preferred_element_type=F32)
    o_ref[...] = (a * _sigmoid(gt)).astype(o_ref.dtype)


def _glu_proj(h2d, w, col_off, width, tm, tn=512):
    m, d = h2d.shape
    ja = col_off // tn
    jg = (col_off + width) // tn
    return pl.pallas_call(
        _glu_proj_kernel,
        grid=(m // tm, width // tn),
        in_specs=[pl.BlockSpec((tm, d), lambda i, j: (i, 0)),
                  pl.BlockSpec((d, tn), lambda i, j: (0, ja + j)),
                  pl.BlockSpec((d, tn), lambda i, j: (0, jg + j))],
        out_specs=pl.BlockSpec((tm, tn), lambda i, j: (i, j)),
        out_shape=jax.ShapeDtypeStruct((m, width), BF16),
        compiler_params=_params("parallel", "arbitrary"),
        name="glu_proj",
    )(h2d, w, w)


def _gate_proj_kernel(h_ref, w_ref, o_ref):
    acc = jnp.dot(h_ref[...], w_ref[...], preferred_element_type=F32)
    o_ref[...] = _sigmoid(acc).astype(o_ref.dtype)


def _gate_proj(h2d, w, col_off, width, tm, tn=1024):
    m, d = h2d.shape
    jb = col_off // tn
    return pl.pallas_call(
        _gate_proj_kernel,
        grid=(m // tm, width // tn),
        in_specs=[pl.BlockSpec((tm, d), lambda i, j: (i, 0)),
                  pl.BlockSpec((d, tn), lambda i, j: (0, jb + j))],
        out_specs=pl.BlockSpec((tm, tn), lambda i, j: (i, j)),
        out_shape=jax.ShapeDtypeStruct((m, width), BF16),
        compiler_params=_params("parallel", "arbitrary"),
        name="gate_proj",
    )(h2d, w)


def _attn_kernel(q_ref, k_ref, v_ref, o_ref, *, tk):
    g, tq, dh = q_ref.shape[1:]
    q = q_ref[0].reshape(g * tq, dh)
    n_chunks = k_ref.shape[1] // tk

    def chunk(ci, carry):
        m, l, acc = carry
        start = pl.multiple_of(ci * tk, tk)
        kc = k_ref[0, pl.ds(start, tk), :]
        vc = v_ref[0, pl.ds(start, tk), :]
        s = lax.dot_general(q, kc, (((1,), (1,)), ((), ())), preferred_element_type=F32)
        m_new = jnp.maximum(m, jnp.max(s, axis=-1, keepdims=True))
        alpha = jnp.exp2(m - m_new)
        p = jnp.exp2(s - m_new)
        l = alpha * l + jnp.sum(p, axis=-1, keepdims=True)
        acc = alpha * acc + jnp.dot(p.astype(BF16), vc, preferred_element_type=F32)
        return m_new, l, acc

    m0 = jnp.full((g * tq, 1), -jnp.inf, F32)
    l0 = jnp.zeros((g * tq, 1), F32)
    a0 = jnp.zeros((g * tq, dh), F32)
    _, l, acc = lax.fori_loop(0, n_chunks, chunk, (m0, l0, a0))
    o = acc / l
    for gi in range(g):
        o_ref[0, :, gi * dh:(gi + 1) * dh] = o[gi * tq:(gi + 1) * tq].astype(o_ref.dtype)


def _attention(q, k, v, tq, tk):
    b, _, l, dh = q.shape
    lk = k.shape[1]
    gdh = Q_PER_KV * dh
    return pl.pallas_call(
        functools.partial(_attn_kernel, tk=tk),
        grid=(b, N_KV_HEADS, l // tq),
        in_specs=[pl.BlockSpec((1, Q_PER_KV, tq, dh), lambda bi, kh, qi: (bi, kh, qi, 0)),
                  pl.BlockSpec((1, lk, dh), lambda bi, kh, qi: (bi, 0, kh)),
                  pl.BlockSpec((1, lk, dh), lambda bi, kh, qi: (bi, 0, kh))],
        out_specs=pl.BlockSpec((1, tq, gdh), lambda bi, kh, qi: (bi, qi, kh)),
        out_shape=jax.ShapeDtypeStruct((b, l, ATTN_WIDTH), BF16),
        compiler_params=_params("parallel", "parallel", "arbitrary"),
        name="attention",
    )(q, k, v)


def _conv_kernel(prev_ref, cur_ref, next_ref, w_ref, b_ref, g_ref, beta_ref, o_ref, win_ref, y_ref, *, cc):
    li = pl.program_id(1)
    tl, c = cur_ref.shape[1:]
    halo = prev_ref.shape[1]
    prev = prev_ref[0].astype(F32)
    nxt = next_ref[0].astype(F32)
    win_ref[0:halo, :] = jnp.where(li > 0, prev, 0.0)
    win_ref[halo:halo + tl, :] = cur_ref[0].astype(F32)
    win_ref[halo + tl:, :] = jnp.where(li < pl.num_programs(1) - 1, nxt, 0.0)
    base = halo - CONV_TAPS // 2
    for c0 in range(0, c, cc):
        acc = jnp.zeros((tl, cc), F32) + b_ref[:, c0:c0 + cc]
        for t in range(CONV_TAPS):
            acc = acc + win_ref[base + t:base + t + tl, c0:c0 + cc] * w_ref[t:t + 1, c0:c0 + cc]
        y_ref[:, c0:c0 + cc] = acc
    y = y_ref[...]
    mu = jnp.mean(y, axis=-1, keepdims=True)
    yc = y - mu
    var = jnp.mean(yc * yc, axis=-1, keepdims=True)
    z = yc * lax.rsqrt(var + EPS) * g_ref[...] + beta_ref[...]
    o_ref[0] = _silu(z).astype(o_ref.dtype)


def _conv_module(u, w_dw, b_dw, ln_g, ln_b, tl, cc=256):
    b, l, c = u.shape
    hb = tl // CONV_HALO
    n_halo = l // CONV_HALO
    return pl.pallas_call(
        functools.partial(_conv_kernel, cc=cc),
        grid=(b, l // tl),
        in_specs=[pl.BlockSpec((1, CONV_HALO, c), lambda bi, li: (bi, jnp.maximum(li * hb - 1, 0), 0)),
                  pl.BlockSpec((1, tl, c), lambda bi, li: (bi, li, 0)),
                  pl.BlockSpec((1, CONV_HALO, c), lambda bi, li: (bi, jnp.minimum((li + 1) * hb, n_halo - 1), 0)),
                  pl.BlockSpec((CONV_TAPS, c), lambda bi, li: (0, 0)),
                  pl.BlockSpec((1, c), lambda bi, li: (0, 0)),
                  pl.BlockSpec((1, c), lambda bi, li: (0, 0)),
                  pl.BlockSpec((1, c), lambda bi, li: (0, 0))],
        out_specs=pl.BlockSpec((1, tl, c), lambda bi, li: (bi, li, 0)),
        out_shape=jax.ShapeDtypeStruct((b, l, c), BF16),
        scratch_shapes=[pltpu.VMEM((tl + 2 * CONV_HALO, c), F32), pltpu.VMEM((tl, c), F32)],
        compiler_params=_params("parallel", "arbitrary"),
        name="conv_module",
    )(u, u, u, w_dw, b_dw, ln_g, ln_b)


def _merge_kernel(a_ref, c_ref, wa_ref, wc_ref, ga_ref, gc_ref, o_ref):
    a = jnp.dot(a_ref[...], wa_ref[...], preferred_element_type=F32)
    cb = jnp.dot(c_ref[...], wc_ref[...], preferred_element_type=F32)
    o_ref[...] = (ga_ref[...].astype(F32) * a + gc_ref[...].astype(F32) * cb).astype(o_ref.dtype)


def _merge(attn, conv, wa, wc, gates, tm=512, tn=1024):
    m, ka = attn.shape
    kc = conv.shape[1]
    d = wa.shape[1]
    nj = d // tn
    return pl.pallas_call(
        _merge_kernel,
        grid=(nj, m // tm),
        in_specs=[pl.BlockSpec((tm, ka), lambda j, i: (i, 0)),
                  pl.BlockSpec((tm, kc), lambda j, i: (i, 0)),
                  pl.BlockSpec((ka, tn), lambda j, i: (0, j)),
                  pl.BlockSpec((kc, tn), lambda j, i: (0, j)),
                  pl.BlockSpec((tm, tn), lambda j, i: (i, j)),
                  pl.BlockSpec((tm, tn), lambda j, i: (i, nj + j))],
        out_specs=pl.BlockSpec((tm, tn), lambda j, i: (i, j)),
        out_shape=jax.ShapeDtypeStruct((m, d), BF16),
        compiler_params=_params("parallel", "arbitrary"),
        name="merge_branches",
    )(attn, conv, wa, wc, gates, gates)


def _out_proj_kernel(m_ref, w_ref, x_ref, ga_ref, o_ref):
    acc = jnp.dot(m_ref[0], w_ref[...], preferred_element_type=F32)
    o_ref[0] = x_ref[0] + ga_ref[0] * acc


def _out_proj(mrg, w, x, gate, tm=1024, tn=1024):
    b, l, d = x.shape
    return pl.pallas_call(
        _out_proj_kernel,
        grid=(b, l // tm, d // tn),
        in_specs=[pl.BlockSpec((1, tm, d), lambda bi, i, j: (bi, i, 0)),
                  pl.BlockSpec((d, tn), lambda bi, i, j: (0, j)),
                  pl.BlockSpec((1, tm, tn), lambda bi, i, j: (bi, i, j)),
                  pl.BlockSpec((1, 1, tn), lambda bi, i, j: (bi, 0, j))],
        out_specs=pl.BlockSpec((1, tm, tn), lambda bi, i, j: (bi, i, j)),
        out_shape=jax.ShapeDtypeStruct((b, l, d), F32),
        compiler_params=_params("parallel", "parallel", "arbitrary"),
        name="out_proj_residual",
    )(mrg, w, x, gate)


def _pack_halves(y):
    n = y.shape[1] // 2
    return pltpu.pack_elementwise([y[:, :n], y[:, n:]], packed_dtype=BF16)


def _unpack_halves(p):
    lo = pltpu.unpack_elementwise(p, index=0, packed_dtype=BF16, unpacked_dtype=F32)
    hi = pltpu.unpack_elementwise(p, index=1, packed_dtype=BF16, unpacked_dtype=F32)
    return lo, hi


def _norm2_router_kernel(x_ref, g_ref, sh_ref, sc_ref, wr_ref, br_ref, hp_ref, lg_ref):
    y = _rms(x_ref[0], g_ref[...]) * (1.0 + sc_ref[0]) + sh_ref[0]
    hp_ref[0] = _pack_halves(y)
    lg_ref[0] = jnp.dot(y, wr_ref[...], preferred_element_type=F32,
                        precision=lax.Precision.HIGHEST) + br_ref[...]


def _norm2_router(x, g, shift, scale, w_r, b_r, tl):
    b, l, d = x.shape
    return pl.pallas_call(
        _norm2_router_kernel,
        grid=(b, l // tl),
        in_specs=[pl.BlockSpec((1, tl, d), lambda bi, li: (bi, li, 0)),
                  pl.BlockSpec((1, d), lambda bi, li: (0, 0)),
                  pl.BlockSpec((1, 1, d), lambda bi, li: (bi, 0, 0)),
                  pl.BlockSpec((1, 1, d), lambda bi, li: (bi, 0, 0)),
                  pl.BlockSpec((d, ROUTER_LANES), lambda bi, li: (0, 0)),
                  pl.BlockSpec((1, ROUTER_LANES), lambda bi, li: (0, 0))],
        out_specs=[pl.BlockSpec((1, tl, d // 2), lambda bi, li: (bi, li, 0)),
                   pl.BlockSpec((1, tl, ROUTER_LANES), lambda bi, li: (bi, li, 0))],
        out_shape=[jax.ShapeDtypeStruct((b, l, d // 2), jnp.int32),
                   jax.ShapeDtypeStruct((b, l, ROUTER_LANES), F32)],
        compiler_params=_params("parallel", "parallel"),
        name="norm2_router",
    )(x, g, shift, scale, w_r, b_r)


def _row_copy(src_ref, dst_ref, src_row, dst_row, sem):
    return pltpu.make_async_copy(src_ref.at[pl.ds(src_row, 1)], dst_ref.at[pl.ds(dst_row, 1)], sem)


def _gather_rows(idx_ref, src_ref, dst_ref, sem):
    rows = dst_ref.shape[0]

    def start(r, c):
        _row_copy(src_ref, dst_ref, idx_ref[0, 0, r], r, sem).start()
        return c

    def wait(r, c):
        _row_copy(src_ref, dst_ref, idx_ref[0, 0, r], r, sem).wait()
        return c

    lax.fori_loop(0, rows, start, 0)
    lax.fori_loop(0, rows, wait, 0)


def _dispatch_kernel(idx_ref, src_ref, o_ref, sem):
    _gather_rows(idx_ref, src_ref, o_ref, sem)


def _dispatch(hp, buf_t, rows):
    total = buf_t.shape[0]
    w = hp.shape[1]
    steps = total // rows
    return pl.pallas_call(
        _dispatch_kernel,
        grid=(steps,),
        in_specs=[pl.BlockSpec((1, 1, rows), lambda i: (i, 0, 0), memory_space=pltpu.SMEM),
                  pl.BlockSpec(memory_space=pl.ANY)],
        out_specs=pl.BlockSpec((rows, w), lambda i: (i, 0)),
        out_shape=jax.ShapeDtypeStruct((total, w), hp.dtype),
        scratch_shapes=[pltpu.SemaphoreType.DMA(())],
        compiler_params=_params("arbitrary"),
        name="moe_dispatch",
    )(buf_t.reshape(steps, 1, rows), hp)


def _expert_up_kernel(be_ref, nu_ref, x_ref, wg_ref, wu_ref, o_ref):
    @pl.when(pl.program_id(1) < nu_ref[0])
    def _():
        lo, hi = _unpack_halves(x_ref[...])
        lo = lo.astype(BF16)
        hi = hi.astype(BF16)
        half = lo.shape[1]

        def mm(w_ref):
            return (jnp.dot(lo, w_ref[0, :half, :].astype(BF16), preferred_element_type=F32)
                    + jnp.dot(hi, w_ref[0, half:, :].astype(BF16), preferred_element_type=F32))

        o_ref[...] = (_silu(mm(wg_ref)) * mm(wu_ref)).astype(o_ref.dtype)

    @pl.when(pl.program_id(1) >= nu_ref[0])
    def _():
        o_ref[...] = jnp.zeros_like(o_ref)


def _expert_up(xs, w_gate, w_up, blk_e, n_used, tmb, tf=256):
    total, half = xs.shape
    _, d, ff = w_gate.shape
    nblk = total // tmb

    def blk(bi, nu):
        return jnp.minimum(bi, nu[0] - 1)

    return pl.pallas_call(
        _expert_up_kernel,
        grid_spec=pltpu.PrefetchScalarGridSpec(
            num_scalar_prefetch=2,
            grid=(ff // tf, nblk),
            in_specs=[pl.BlockSpec((tmb, half), lambda c, bi, be, nu: (blk(bi, nu), 0)),
                      pl.BlockSpec((1, d, tf), lambda c, bi, be, nu: (be[bi], 0, c)),
                      pl.BlockSpec((1, d, tf), lambda c, bi, be, nu: (be[bi], 0, c))],
            out_specs=pl.BlockSpec((tmb, tf), lambda c, bi, be, nu: (bi, c))),
        out_shape=jax.ShapeDtypeStruct((total, ff), BF16),
        compiler_params=_params("arbitrary", "arbitrary"),
        name="expert_up",
    )(blk_e, n_used, xs, w_gate, w_up)


def _expert_down_kernel(be_ref, nu_ref, a_ref, wlo_ref, whi_ref, rw_ref, o_ref):
    @pl.when(pl.program_id(1) < nu_ref[0])
    def _():
        a = a_ref[...]
        rw = rw_ref[...]
        ylo = jnp.dot(a, wlo_ref[0].astype(BF16), preferred_element_type=F32) * rw
        yhi = jnp.dot(a, whi_ref[0].astype(BF16), preferred_element_type=F32) * rw
        o_ref[...] = pltpu.pack_elementwise([ylo, yhi], packed_dtype=BF16)

    @pl.when(pl.program_id(1) >= nu_ref[0])
    def _():
        o_ref[...] = jnp.zeros_like(o_ref)


def _expert_down(act, w_down, row_w, blk_e, n_used, tmb, tn=512):
    total, ff = act.shape
    d = w_down.shape[2]
    half = d // 2
    nblk = total // tmb
    nc = half // tn

    def blk(bi, nu):
        return jnp.minimum(bi, nu[0] - 1)

    return pl.pallas_call(
        _expert_down_kernel,
        grid_spec=pltpu.PrefetchScalarGridSpec(
            num_scalar_prefetch=2,
            grid=(nc, nblk),
            in_specs=[pl.BlockSpec((tmb, ff), lambda c, bi, be, nu: (blk(bi, nu), 0)),
                      pl.BlockSpec((1, ff, tn), lambda c, bi, be, nu: (be[bi], 0, c)),
                      pl.BlockSpec((1, ff, tn), lambda c, bi, be, nu: (be[bi], 0, nc + c)),
                      pl.BlockSpec((tmb, 1), lambda c, bi, be, nu: (blk(bi, nu), 0))],
            out_specs=pl.BlockSpec((tmb, tn), lambda c, bi, be, nu: (bi, c))),
        out_shape=jax.ShapeDtypeStruct((total, half), jnp.int32),
        compiler_params=_params("arbitrary", "arbitrary"),
        name="expert_down",
    )(blk_e, n_used, act, w_down, w_down, row_w)


def _combine_kernel(p0_ref, p1_ref, ys_ref, x_ref, ga_ref, g_ref, o_ref, y0_ref, y1_ref, sem0, sem1):
    rows = y0_ref.shape[0]

    def start(r, c):
        _row_copy(ys_ref, y0_ref, p0_ref[0, 0, r], r, sem0).start()
        _row_copy(ys_ref, y1_ref, p1_ref[0, 0, r], r, sem1).start()
        return c

    def wait(r, c):
        _row_copy(ys_ref, y0_ref, p0_ref[0, 0, r], r, sem0).wait()
        _row_copy(ys_ref, y1_ref, p1_ref[0, 0, r], r, sem1).wait()
        return c

    lax.fori_loop(0, rows, start, 0)
    lax.fori_loop(0, rows, wait, 0)
    lo0, hi0 = _unpack_halves(y0_ref[...])
    lo1, hi1 = _unpack_halves(y1_ref[...])
    half = lo0.shape[1]
    x = x_ref[0]
    ga = ga_ref[0]
    zlo = x[:, :half] + ga[:, :half] * (lo0 + lo1)
    zhi = x[:, half:] + ga[:, half:] * (hi0 + hi1)
    ms = (jnp.sum(zlo * zlo, axis=-1, keepdims=True) + jnp.sum(zhi * zhi, axis=-1, keepdims=True)) / (2 * half)
    inv = lax.rsqrt(ms + EPS)
    o_ref[0, :, :half] = zlo * inv * g_ref[:, :half]
    o_ref[0, :, half:] = zhi * inv * g_ref[:, half:]


def _combine(ys, pos0, pos1, x, gate, g, rows):
    b, l, d = x.shape
    half = d // 2
    lb = l // rows
    return pl.pallas_call(
        _combine_kernel,
        grid=(b, lb),
        in_specs=[pl.BlockSpec((1, 1, rows), lambda bi, i: (bi * lb + i, 0, 0), memory_space=pltpu.SMEM),
                  pl.BlockSpec((1, 1, rows), lambda bi, i: (bi * lb + i, 0, 0), memory_space=pltpu.SMEM),
                  pl.BlockSpec(memory_space=pl.ANY),
                  pl.BlockSpec((1, rows, d), lambda bi, i: (bi, i, 0)),
                  pl.BlockSpec((1, 1, d), lambda bi, i: (bi, 0, 0)),
                  pl.BlockSpec((1, d), lambda bi, i: (0, 0))],
        out_specs=pl.BlockSpec((1, rows, d), lambda bi, i: (bi, i, 0)),
        out_shape=jax.ShapeDtypeStruct((b, l, d), F32),
        scratch_shapes=[pltpu.VMEM((rows, half), ys.dtype), pltpu.VMEM((rows, half), ys.dtype),
                        pltpu.SemaphoreType.DMA(()), pltpu.SemaphoreType.DMA(())],
        compiler_params=_params("arbitrary", "arbitrary"),
        name="moe_combine_norm",
    )(pos0.reshape(b * lb, 1, rows), pos1.reshape(b * lb, 1, rows), ys, x, gate, g)


def _route(logits):
    n = logits.shape[0]
    g_logits = logits[:, :N_GROUPS]
    e_logits = logits[:, N_GROUPS:N_GROUPS + N_EXPERTS].reshape(n, N_GROUPS, EXPERTS_PER_GROUP)
    g_prob = jax.nn.softmax(g_logits, axis=-1)
    g_sel = jnp.argmax(g_logits, axis=-1)
    p_g = jnp.take_along_axis(g_prob, g_sel[:, None], axis=1)[:, 0]
    e_in = jnp.take_along_axis(e_logits, g_sel[:, None, None], axis=1)[:, 0]
    top_v, top_i = lax.top_k(e_in, TOP_K)
    weights = p_g[:, None] * jax.nn.softmax(top_v, axis=-1)
    expert = g_sel[:, None].astype(jnp.int32) * EXPERTS_PER_GROUP + top_i.astype(jnp.int32)
    return expert, weights


def _block_layout(expert, weights, tmb):
    n = expert.shape[0]
    nk = n * TOP_K
    flat_e = expert.reshape(-1)
    flat_w = weights.reshape(-1)
    flat_t = jnp.arange(nk, dtype=jnp.int32) // TOP_K
    order = jnp.argsort(flat_e)
    se = flat_e[order]
    counts = jnp.sum((flat_e[:, None] == jnp.arange(N_EXPERTS, dtype=jnp.int32)[None, :]).astype(jnp.int32), axis=0)
    padded = ((counts + tmb - 1) // tmb) * tmb
    pend = jnp.cumsum(padded)
    pstart = pend - padded
    cstart = jnp.cumsum(counts) - counts
    dest = pstart[se] + jnp.arange(nk, dtype=jnp.int32) - cstart[se]
    nblk = (nk + N_EXPERTS * (tmb - 1) + tmb - 1) // tmb
    total = nblk * tmb
    buf_t = jnp.zeros((total,), jnp.int32).at[dest].set(flat_t[order])
    buf_w = jnp.zeros((total,), F32).at[dest].set(flat_w[order])
    pos = jnp.zeros((nk,), jnp.int32).at[order].set(dest).reshape(n, TOP_K)
    n_used = (pend[-1] // tmb).astype(jnp.int32)
    blk_ids = jnp.minimum(jnp.arange(nblk, dtype=jnp.int32), n_used - 1)
    blk_e = jnp.minimum(jnp.searchsorted(pend, blk_ids * tmb, side="right"), N_EXPERTS - 1).astype(jnp.int32)
    return buf_t, buf_w, pos, blk_e, n_used.reshape(1)


def _rope_tables(n_tokens):
    rows = n_tokens // GRID_W
    row, col = jnp.meshgrid(jnp.arange(rows), jnp.arange(GRID_W), indexing="ij")
    pos = jnp.stack([row.reshape(-1), col.reshape(-1)], axis=-1).astype(F32)
    inv = ROPE_THETA ** (-jnp.arange(0, ROPE_AXIS_DIM, 2, dtype=F32) / ROPE_AXIS_DIM)
    ang = pos[:, :, None] * inv[None, None, :]
    cos, sin = jnp.cos(ang), jnp.sin(ang)
    cos_t = jnp.concatenate([cos[:, 0], cos[:, 0], cos[:, 1], cos[:, 1]], axis=-1)
    sin_t = jnp.concatenate([-sin[:, 0], sin[:, 0], -sin[:, 1], sin[:, 1]], axis=-1)
    return cos_t, sin_t


def kernel(x, c, ctx, c_ctx, norm1_g, w_mod, b_mod, w_in, q_norm_g, k_norm_g, w_attn_out, conv_dw_w, conv_dw_b, conv_ln_g, conv_ln_b, w_conv_out, w_out, norm2_g, w_router_group, b_router_group, w_router_expert, b_router_expert, w_exp_gate, w_exp_up, w_exp_down, norm_f_g):
    b, s, d = x.shape
    n_ctx = ctx.shape[1]
    assert w_in.shape[0] == 1, "single-layer stack"
    conv_width = conv_dw_w.shape[-1]
    k_off = ATTN_WIDTH
    glu_off = k_off + 2 * KV_WIDTH
    gate_off = glu_off + 2 * conv_width

    n_c = b + 1
    cvec = jnp.zeros((8 * ((n_c + 7) // 8), d), F32).at[:b].set(c).at[b].set(c_ctx)
    mod = _mod_vectors(cvec, w_mod[0], b_mod.reshape(1, -1))
    sh1, sc1, ga1, sh2, sc2, ga2 = [mod[:b, i * d:(i + 1) * d].reshape(b, 1, d) for i in range(N_MOD)]
    csh1, csc1 = [mod[b:b + 1, i * d:(i + 1) * d].reshape(1, 1, d) for i in range(2)]

    g1 = norm1_g.reshape(1, d)
    h = _norm_mod(x, g1, sh1, sc1, tl=512)
    hc = _norm_mod(ctx, g1, csh1, csc1, tl=n_ctx)
    w_in_b = w_in[0].astype(BF16)
    cos_t, sin_t = _rope_tables(s)
    qg = q_norm_g.reshape(1, HEAD_DIM)
    kg = k_norm_g.reshape(1, HEAD_DIM)
    q = _q_proj(h, w_in_b, qg, cos_t, sin_t, 0, HEAD_DIM ** -0.5 * LOG2E, tm=1024)
    k, v = _kv_proj(h, w_in_b, kg, cos_t, sin_t, k_off, tm=1024)
    kc, vc = _kv_proj(hc, w_in_b, kg, None, None, k_off, tm=n_ctx)
    k_all = jnp.concatenate([k, kc], axis=1)
    v_all = jnp.concatenate([v, vc], axis=1)
    lk = s + n_ctx
    attn = _attention(q, k_all, v_all, tq=128, tk=lk // 4)

    h2d = h.reshape(b * s, d)
    u = _glu_proj(h2d, w_in_b, glu_off, conv_width, tm=1024)
    conv = _conv_module(u.reshape(b, s, conv_width), conv_dw_w.reshape(CONV_TAPS, conv_width),
                        conv_dw_b.reshape(1, -1), conv_ln_g.reshape(1, -1), conv_ln_b.reshape(1, -1), tl=64)
    gates = _gate_proj(h2d, w_in_b, gate_off, 2 * d, tm=1024)
    mrg = _merge(attn.reshape(b * s, ATTN_WIDTH), conv.reshape(b * s, conv_width),
                 w_attn_out[0].astype(BF16), w_conv_out[0].astype(BF16), gates)
    x1 = _out_proj(mrg.reshape(b, s, d), w_out[0].astype(BF16), x, ga1)

    w_r = jnp.zeros((d, ROUTER_LANES), F32).at[:, :N_GROUPS].set(w_router_group[0]) \
        .at[:, N_GROUPS:N_GROUPS + N_EXPERTS].set(w_router_expert[0])
    b_r = jnp.zeros((1, ROUTER_LANES), F32).at[0, :N_GROUPS].set(b_router_group[0]) \
        .at[0, N_GROUPS:N_GROUPS + N_EXPERTS].set(b_router_expert[0])
    hp, logits = _norm2_router(x1, norm2_g.reshape(1, d), sh2, sc2, w_r, b_r, tl=256)
    n = b * s
    expert, weights = _route(logits.reshape(n, ROUTER_LANES))
    tmb = 512
    buf_t, buf_w, pos, blk_e, n_used = _block_layout(expert, weights, tmb)
    xs = _dispatch(hp.reshape(n, d // 2), buf_t, rows=tmb)
    act = _expert_up(xs, w_exp_gate[0], w_exp_up[0], blk_e, n_used, tmb)
    ys = _expert_down(act, w_exp_down[0], buf_w.reshape(-1, 1), blk_e, n_used, tmb)
    return _combine(ys, pos[:, 0], pos[:, 1], x1, ga2, norm_f_g.reshape(1, d), rows=256)
```

```python
import functools

import jax
import jax.numpy as jnp
from jax import lax
from jax.experimental import pallas as pl
from jax.experimental.pallas import tpu as pltpu

F32 = jnp.float32
BF16 = jnp.bfloat16

GRID_W = 64
HEAD_DIM = 128
N_Q_HEADS = 16
N_KV_HEADS = 4
Q_PER_KV = N_Q_HEADS // N_KV_HEADS
ATTN_WIDTH = N_Q_HEADS * HEAD_DIM
KV_WIDTH = N_KV_HEADS * HEAD_DIM
CONV_TAPS = 31
CONV_HALO = 16
ROPE_THETA = 10000.0
ROPE_AXIS_DIM = HEAD_DIM // 2
N_GROUPS = 4
EXPERTS_PER_GROUP = 8
N_EXPERTS = N_GROUPS * EXPERTS_PER_GROUP
TOP_K = 2
N_MOD = 6
EPS = 1e-6
LOG2E = 1.4426950408889634
LANES = 128
SUBLANES = 8
ROUTER_LANES = LANES
SLAB_PITCH = 24

V7X_VMEM_LIMIT = 56 * 1024 * 1024


def _params(*sem):
    return pltpu.CompilerParams(dimension_semantics=sem, vmem_limit_bytes=V7X_VMEM_LIMIT)


def _sigmoid(x):
    return 1.0 / (1.0 + jnp.exp(-x))


def _silu(x):
    return x * _sigmoid(x)


def _rms(x, g):
    return x * lax.rsqrt(jnp.mean(x * x, axis=-1, keepdims=True) + EPS) * g


def _mod_kernel(c_ref, w_ref, b_ref, o_ref):
    s = _silu(c_ref[...]).astype(BF16)
    o_ref[...] = jnp.dot(s, w_ref[...].astype(BF16), preferred_element_type=F32) + b_ref[...]


def _mod_vectors(cvec, w_mod, b_mod, tn=512):
    m, d = cvec.shape
    n = w_mod.shape[1]
    return pl.pallas_call(
        _mod_kernel,
        grid=(n // tn,),
        in_specs=[pl.BlockSpec((m, d), lambda j: (0, 0)),
                  pl.BlockSpec((d, tn), lambda j: (0, j)),
                  pl.BlockSpec((1, tn), lambda j: (0, j))],
        out_specs=pl.BlockSpec((m, tn), lambda j: (0, j)),
        out_shape=jax.ShapeDtypeStruct((m, n), F32),
        compiler_params=_params("arbitrary"),
        name="mod_vectors",
    )(cvec, w_mod, b_mod)


def _norm_mod_kernel(x_ref, g_ref, sh_ref, sc_ref, o_ref):
    y = _rms(x_ref[0], g_ref[...])
    o_ref[0] = (y * (1.0 + sc_ref[0]) + sh_ref[0]).astype(o_ref.dtype)


def _norm_mod(x, g, shift, scale, tl):
    b, l, d = x.shape
    per_batch = shift.shape[0] > 1
    mod_map = (lambda bi, li: (bi, 0, 0)) if per_batch else (lambda bi, li: (0, 0, 0))
    return pl.pallas_call(
        _norm_mod_kernel,
        grid=(b, l // tl),
        in_specs=[pl.BlockSpec((1, tl, d), lambda bi, li: (bi, li, 0)),
                  pl.BlockSpec((1, d), lambda bi, li: (0, 0)),
                  pl.BlockSpec((1, 1, d), mod_map),
                  pl.BlockSpec((1, 1, d), mod_map)],
        out_specs=pl.BlockSpec((1, tl, d), lambda bi, li: (bi, li, 0)),
        out_shape=jax.ShapeDtypeStruct((b, l, d), BF16),
        compiler_params=_params("parallel", "parallel"),
        name="norm_modulate",
    )(x, g, shift, scale)


def _head_norm_rope(a, g, cos, sin):
    y = _rms(a, g)
    if cos is None:
        return y
    lane = lax.broadcasted_iota(jnp.int32, y.shape, 1)
    quarter = ROPE_AXIS_DIM // 2
    partner = jnp.where((lane % ROPE_AXIS_DIM) < quarter,
                        pltpu.roll(y, HEAD_DIM - quarter, 1), pltpu.roll(y, quarter, 1))
    return y * cos + partner * sin


def _q_proj_kernel(h_ref, w_ref, g_ref, cos_ref, sin_ref, o_ref, *, scale):
    acc = jnp.dot(h_ref[0], w_ref[...], preferred_element_type=F32)
    for hh in range(o_ref.shape[1]):
        a = acc[:, hh * HEAD_DIM:(hh + 1) * HEAD_DIM]
        y = _head_norm_rope(a, g_ref[...], cos_ref[...], sin_ref[...])
        o_ref[0, hh] = (y * scale).astype(o_ref.dtype)


def _q_proj(h, w, g, cos_t, sin_t, col_off, scale, tm, tn=1024):
    b, l, d = h.shape
    jb = col_off // tn
    hpt = tn // HEAD_DIM
    return pl.pallas_call(
        functools.partial(_q_proj_kernel, scale=scale),
        grid=(b, l // tm, ATTN_WIDTH // tn),
        in_specs=[pl.BlockSpec((1, tm, d), lambda bi, i, j: (bi, i, 0)),
                  pl.BlockSpec((d, tn), lambda bi, i, j: (0, jb + j)),
                  pl.BlockSpec((1, HEAD_DIM), lambda bi, i, j: (0, 0)),
                  pl.BlockSpec((tm, HEAD_DIM), lambda bi, i, j: (i, 0)),
                  pl.BlockSpec((tm, HEAD_DIM), lambda bi, i, j: (i, 0))],
        out_specs=pl.BlockSpec((1, hpt, tm, HEAD_DIM), lambda bi, i, j: (bi, j, i, 0)),
        out_shape=jax.ShapeDtypeStruct((b, N_Q_HEADS, l, HEAD_DIM), BF16),
        compiler_params=_params("parallel", "parallel", "arbitrary"),
        name="q_proj",
    )(h, w, g, cos_t, sin_t)


def _kv_proj_kernel(h_ref, w_ref, g_ref, *rest, rope):
    if rope:
        cos_ref, sin_ref, k_ref, v_ref = rest
        cos, sin = cos_ref[...], sin_ref[...]
    else:
        k_ref, v_ref = rest
        cos = sin = None
    acc = jnp.dot(h_ref[0], w_ref[...], preferred_element_type=F32)
    for hh in range(N_KV_HEADS):
        a = acc[:, hh * HEAD_DIM:(hh + 1) * HEAD_DIM]
        k_ref[0, :, hh * HEAD_DIM:(hh + 1) * HEAD_DIM] = _head_norm_rope(a, g_ref[...], cos, sin).astype(k_ref.dtype)
    v_ref[0] = acc[:, KV_WIDTH:].astype(v_ref.dtype)


def _kv_proj(h, w, g, cos_t, sin_t, col_off, tm):
    b, l, d = h.shape
    tn = 2 * KV_WIDTH
    jb = col_off // tn
    rope = cos_t is not None
    in_specs = [pl.BlockSpec((1, tm, d), lambda bi, i: (bi, i, 0)),
                pl.BlockSpec((d, tn), lambda bi, i: (0, jb)),
                pl.BlockSpec((1, HEAD_DIM), lambda bi, i: (0, 0))]
    args = [h, w, g]
    if rope:
        in_specs += [pl.BlockSpec((tm, HEAD_DIM), lambda bi, i: (i, 0))] * 2
        args += [cos_t, sin_t]
    return pl.pallas_call(
        functools.partial(_kv_proj_kernel, rope=rope),
        grid=(b, l // tm),
        in_specs=in_specs,
        out_specs=[pl.BlockSpec((1, tm, KV_WIDTH), lambda bi, i: (bi, i, 0))] * 2,
        out_shape=[jax.ShapeDtypeStruct((b, l, KV_WIDTH), BF16)] * 2,
        compiler_params=_params("parallel", "parallel"),
        name="kv_proj_rope" if rope else "kv_proj_ctx",
    )(*args)


def _glu_proj_kernel(h_ref, wa_ref, wg_ref, o_ref):
    a = jnp.dot(h_ref[...], wa_ref[...], preferred_element_type=F32)
    gt = jnp.dot(h_ref[...], wg_ref[...], preferred_element_type=F32)
    o_ref[...] = (a * _sigmoid(gt)).astype(o_ref.dtype)


def _glu_proj(h2d, w, col_off, width, tm, tn=512):
    m, d = h2d.shape
    ja = col_off // tn
    jg = (col_off + width) // tn
    return pl.pallas_call(
        _glu_proj_kernel,
        grid=(m // tm, width // tn),
        in_specs=[pl.BlockSpec((tm, d), lambda i, j: (i, 0)),
                  pl.BlockSpec((d, tn), lambda i, j: (0, ja + j)),
                  pl.BlockSpec((d, tn), lambda i, j: (0, jg + j))],
        out_specs=pl.BlockSpec((tm, tn), lambda i, j: (i, j)),
        out_shape=jax.ShapeDtypeStruct((m, width), BF16),
        compiler_params=_params("parallel", "arbitrary"),
        name="glu_proj",
    )(h2d, w, w)


def _gate_proj_kernel(h_ref, w_ref, o_ref):
    acc = jnp.dot(h_ref[...], w_ref[...], preferred_element_type=F32)
    o_ref[...] = _sigmoid(acc).astype(o_ref.dtype)


def _gate_proj(h2d, w, col_off, width, tm, tn=1024):
    m, d = h2d.shape
    jb = col_off // tn
    return pl.pallas_call(
        _gate_proj_kernel,
        grid=(m // tm, width // tn),
        in_specs=[pl.BlockSpec((tm, d), lambda i, j: (i, 0)),
                  pl.BlockSpec((d, tn), lambda i, j: (0, jb + j))],
        out_specs=pl.BlockSpec((tm, tn), lambda i, j: (i, j)),
        out_shape=jax.ShapeDtypeStruct((m, width), BF16),
        compiler_params=_params("parallel", "arbitrary"),
        name="gate_proj",
    )(h2d, w)


def _attn_kernel(q_ref, k_ref, v_ref, kc_ref, vc_ref, o_ref,
                 s0_ref, s1_ref, p0_ref, p1_ref, m_ref, al_ref, acc_ref, *, tk, rb):
    g, tq, dh = q_ref.shape[1:]
    rows = g * tq
    q = q_ref[0].reshape(rows, dh)
    chunks = [(k_ref, v_ref, c * tk, tk) for c in range(k_ref.shape[1] // tk)]
    chunks.append((kc_ref, vc_ref, 0, kc_ref.shape[1]))
    s_refs = (s0_ref, s1_ref)
    p_refs = (p0_ref, p1_ref)

    def scores(j):
        kr, _, st, n = chunks[j]
        s_refs[j % 2][:, :n] = lax.dot_general(q, kr[0, st:st + n, :], (((1,), (1,)), ((), ())),
                                               preferred_element_type=F32)

    def softmax(j):
        n = chunks[j][3]
        s_ref, p_ref = s_refs[j % 2], p_refs[j % 2]
        for r0 in range(0, rows, rb):
            sb = s_ref[r0:r0 + rb, :n]
            mn = jnp.max(sb, axis=-1, keepdims=True)
            if j > 0:
                mo = m_ref[r0:r0 + rb, :]
                mn = jnp.maximum(mo, mn)
                al_ref[r0:r0 + rb, :] = jnp.exp2(mo - mn)
            m_ref[r0:r0 + rb, :] = mn
            p_ref[r0:r0 + rb, :n] = jnp.exp2(sb - mn).astype(BF16)

    def weighted_values(j):
        _, vr, st, n = chunks[j]
        ones_col = (lax.broadcasted_iota(jnp.int32, (n, dh), 1) == 0).astype(BF16)
        v1 = jnp.concatenate([vr[0, st:st + n, :], ones_col], axis=1)
        upd = jnp.dot(p_refs[j % 2][:, :n], v1, preferred_element_type=F32)
        if j == 0:
            acc_ref[...] = upd
        else:
            acc_ref[...] = al_ref[...] * acc_ref[...] + upd

    scores(0)
    for j in range(len(chunks)):
        if j + 1 < len(chunks):
            scores(j + 1)
        softmax(j)
        weighted_values(j)

    acc = acc_ref[...]
    o = acc[:, :dh] / acc[:, dh:dh + 1]
    for gi in range(g):
        o_ref[0, :, gi * dh:(gi + 1) * dh] = o[gi * tq:(gi + 1) * tq].astype(o_ref.dtype)


def _attention(q, k, v, kc, vc, tq, tk, rb=16):
    b, _, l, dh = q.shape
    lc = kc.shape[1]
    rows = Q_PER_KV * tq
    gdh = Q_PER_KV * dh
    return pl.pallas_call(
        functools.partial(_attn_kernel, tk=tk, rb=rb),
        grid=(b, N_KV_HEADS, l // tq),
        in_specs=[pl.BlockSpec((1, Q_PER_KV, tq, dh), lambda bi, kh, qi: (bi, kh, qi, 0)),
                  pl.BlockSpec((1, l, dh), lambda bi, kh, qi: (bi, 0, kh)),
                  pl.BlockSpec((1, l, dh), lambda bi, kh, qi: (bi, 0, kh)),
                  pl.BlockSpec((1, lc, dh), lambda bi, kh, qi: (bi, 0, kh)),
                  pl.BlockSpec((1, lc, dh), lambda bi, kh, qi: (bi, 0, kh))],
        out_specs=pl.BlockSpec((1, tq, gdh), lambda bi, kh, qi: (bi, qi, kh)),
        out_shape=jax.ShapeDtypeStruct((b, l, ATTN_WIDTH), BF16),
        scratch_shapes=[pltpu.VMEM((rows, tk), F32), pltpu.VMEM((rows, tk), F32),
                        pltpu.VMEM((rows, tk), BF16), pltpu.VMEM((rows, tk), BF16),
                        pltpu.VMEM((rows, 1), F32), pltpu.VMEM((rows, 1), F32),
                        pltpu.VMEM((rows, 2 * dh), F32)],
        compiler_params=_params("parallel", "parallel", "arbitrary"),
        name="attention",
    )(q, k, v, kc, vc)


def _conv_kernel(prev_ref, cur_ref, next_ref, w_ref, b_ref, g_ref, beta_ref, o_ref, win_ref, y_ref, *, tc):
    li = pl.program_id(1)
    tl, c = cur_ref.shape[1:]
    halo = prev_ref.shape[1]
    nt = c // LANES

    def put_tokens(vals, tok0):
        for j in range(nt):
            win_ref[pl.ds(tok0 * nt + j, vals.shape[0], stride=nt), :] = vals[:, j * LANES:(j + 1) * LANES]

    put_tokens(jnp.where(li > 0, prev_ref[0].astype(F32), 0.0), 0)
    put_tokens(cur_ref[0].astype(F32), halo)
    put_tokens(jnp.where(li < pl.num_programs(1) - 1, next_ref[0].astype(F32), 0.0), halo + tl)

    first = halo - CONV_TAPS // 2
    bias = b_ref[...][None]

    def token_chunk(ci, carry):
        tok = ci * tc
        acc = jnp.zeros((tc, nt, LANES), F32) + bias
        for t in range(CONV_TAPS):
            r0 = pl.multiple_of((tok + first + t) * nt, nt)
            acc = acc + win_ref[pl.ds(r0, tc * nt), :].reshape(tc, nt, LANES) * w_ref[t][None]
        y_ref[pl.ds(pl.multiple_of(tok * nt, nt), tc * nt), :] = acc.reshape(tc * nt, LANES)
        return carry

    lax.fori_loop(0, tl // tc, token_chunk, 0)
    y = jnp.concatenate([y_ref[pl.ds(j, tl, stride=nt), :] for j in range(nt)], axis=1)
    mu = jnp.mean(y, axis=-1, keepdims=True)
    yc = y - mu
    var = jnp.mean(yc * yc, axis=-1, keepdims=True)
    z = yc * lax.rsqrt(var + EPS) * g_ref[...] + beta_ref[...]
    o_ref[0] = _silu(z).astype(o_ref.dtype)


def _conv_module(u, w_dw, b_dw, ln_g, ln_b, tl, tc=16):
    b, l, c = u.shape
    nt = c // LANES
    hb = tl // CONV_HALO
    n_halo = l // CONV_HALO
    return pl.pallas_call(
        functools.partial(_conv_kernel, tc=tc),
        grid=(b, l // tl),
        in_specs=[pl.BlockSpec((1, CONV_HALO, c), lambda bi, li: (bi, jnp.maximum(li * hb - 1, 0), 0)),
                  pl.BlockSpec((1, tl, c), lambda bi, li: (bi, li, 0)),
                  pl.BlockSpec((1, CONV_HALO, c), lambda bi, li: (bi, jnp.minimum((li + 1) * hb, n_halo - 1), 0)),
                  pl.BlockSpec((CONV_TAPS, nt, LANES), lambda bi, li: (0, 0, 0)),
                  pl.BlockSpec((nt, LANES), lambda bi, li: (0, 0)),
                  pl.BlockSpec((1, c), lambda bi, li: (0, 0)),
                  pl.BlockSpec((1, c), lambda bi, li: (0, 0))],
        out_specs=pl.BlockSpec((1, tl, c), lambda bi, li: (bi, li, 0)),
        out_shape=jax.ShapeDtypeStruct((b, l, c), BF16),
        scratch_shapes=[pltpu.VMEM(((tl + 2 * CONV_HALO) * nt, LANES), F32), pltpu.VMEM((tl * nt, LANES), F32)],
        compiler_params=_params("parallel", "arbitrary"),
        name="conv_module",
    )(u, u, u, w_dw, b_dw, ln_g, ln_b)


def _merge_kernel(a_ref, c_ref, wa_ref, wc_ref, ga_ref, gc_ref, o_ref):
    a = jnp.dot(a_ref[...], wa_ref[...], preferred_element_type=F32)
    cb = jnp.dot(c_ref[...], wc_ref[...], preferred_element_type=F32)
    o_ref[...] = (ga_ref[...].astype(F32) * a + gc_ref[...].astype(F32) * cb).astype(o_ref.dtype)


def _merge(attn, conv, wa, wc, gates, tm=512, tn=1024):
    m, ka = attn.shape
    kc = conv.shape[1]
    d = wa.shape[1]
    nj = d // tn
    return pl.pallas_call(
        _merge_kernel,
        grid=(nj, m // tm),
        in_specs=[pl.BlockSpec((tm, ka), lambda j, i: (i, 0)),
                  pl.BlockSpec((tm, kc), lambda j, i: (i, 0)),
                  pl.BlockSpec((ka, tn), lambda j, i: (0, j)),
                  pl.BlockSpec((kc, tn), lambda j, i: (0, j)),
                  pl.BlockSpec((tm, tn), lambda j, i: (i, j)),
                  pl.BlockSpec((tm, tn), lambda j, i: (i, nj + j))],
        out_specs=pl.BlockSpec((tm, tn), lambda j, i: (i, j)),
        out_shape=jax.ShapeDtypeStruct((m, d), BF16),
        compiler_params=_params("parallel", "arbitrary"),
        name="merge_branches",
    )(attn, conv, wa, wc, gates, gates)


def _out_proj_kernel(m_ref, w_ref, x_ref, ga_ref, o_ref):
    acc = jnp.dot(m_ref[0], w_ref[...], preferred_element_type=F32)
    o_ref[0] = x_ref[0] + ga_ref[0] * acc


def _out_proj(mrg, w, x, gate, tm=1024, tn=1024):
    b, l, d = x.shape
    return pl.pallas_call(
        _out_proj_kernel,
        grid=(b, l // tm, d // tn),
        in_specs=[pl.BlockSpec((1, tm, d), lambda bi, i, j: (bi, i, 0)),
                  pl.BlockSpec((d, tn), lambda bi, i, j: (0, j)),
                  pl.BlockSpec((1, tm, tn), lambda bi, i, j: (bi, i, j)),
                  pl.BlockSpec((1, 1, tn), lambda bi, i, j: (bi, 0, j))],
        out_specs=pl.BlockSpec((1, tm, tn), lambda bi, i, j: (bi, i, j)),
        out_shape=jax.ShapeDtypeStruct((b, l, d), F32),
        compiler_params=_params("parallel", "parallel", "arbitrary"),
        name="out_proj_residual",
    )(mrg, w, x, gate)


def _pack_halves(y):
    n = y.shape[1] // 2
    return pltpu.pack_elementwise([y[:, :n], y[:, n:]], packed_dtype=BF16)


def _unpack_halves(p):
    lo = pltpu.unpack_elementwise(p, index=0, packed_dtype=BF16, unpacked_dtype=F32)
    hi = pltpu.unpack_elementwise(p, index=1, packed_dtype=BF16, unpacked_dtype=F32)
    return lo, hi


def _norm2_router_kernel(x_ref, g_ref, sh_ref, sc_ref, wr_ref, br_ref, hp_ref, lg_ref):
    y = _rms(x_ref[0], g_ref[...]) * (1.0 + sc_ref[0]) + sh_ref[0]
    packed = _pack_halves(y)
    tl = packed.shape[0]
    nt = packed.shape[1] // LANES
    for j in range(nt):
        hp_ref[pl.ds(j, tl, stride=nt), :] = packed[:, j * LANES:(j + 1) * LANES]
    lg_ref[0] = jnp.dot(y.astype(BF16), wr_ref[...], preferred_element_type=F32) + br_ref[...]


def _norm2_router(x, g, shift, scale, w_r, b_r, tl):
    b, l, d = x.shape
    nt = d // 2 // LANES
    lb = l // tl
    return pl.pallas_call(
        _norm2_router_kernel,
        grid=(b, lb),
        in_specs=[pl.BlockSpec((1, tl, d), lambda bi, li: (bi, li, 0)),
                  pl.BlockSpec((1, d), lambda bi, li: (0, 0)),
                  pl.BlockSpec((1, 1, d), lambda bi, li: (bi, 0, 0)),
                  pl.BlockSpec((1, 1, d), lambda bi, li: (bi, 0, 0)),
                  pl.BlockSpec((d, ROUTER_LANES), lambda bi, li: (0, 0)),
                  pl.BlockSpec((1, ROUTER_LANES), lambda bi, li: (0, 0))],
        out_specs=[pl.BlockSpec((tl * nt, LANES), lambda bi, li: (bi * lb + li, 0)),
                   pl.BlockSpec((1, tl, ROUTER_LANES), lambda bi, li: (bi, li, 0))],
        out_shape=[jax.ShapeDtypeStruct((b * l * nt, LANES), jnp.int32),
                   jax.ShapeDtypeStruct((b, l, ROUTER_LANES), F32)],
        compiler_params=_params("parallel", "parallel"),
        name="norm2_router",
    )(x, g, shift, scale, w_r, b_r)


def _dispatch_kernel(nu_ref, idx_ref, src_ref, o_ref, slab_ref, sem):
    rows = o_ref.shape[0]
    nt = src_ref.shape[1]

    def slab_copy(r):
        dst = slab_ref.at[pl.ds(pl.multiple_of(r * SLAB_PITCH, SUBLANES), nt)]
        return pltpu.make_async_copy(src_ref.at[idx_ref[0, 0, r]], dst, sem)

    @pl.when(pl.program_id(0) < nu_ref[0])
    def _():
        def start(r, c):
            slab_copy(r).start()
            return c

        def wait(r, c):
            slab_copy(r).wait()
            return c

        lax.fori_loop(0, rows, start, 0)
        lax.fori_loop(0, rows, wait, 0)
        for j in range(nt):
            o_ref[:, j * LANES:(j + 1) * LANES] = slab_ref[pl.ds(j, rows, stride=SLAB_PITCH), :]

    @pl.when(pl.program_id(0) >= nu_ref[0])
    def _():
        o_ref[...] = jnp.zeros_like(o_ref)


def _dispatch(hp3, buf_t, n_used, rows):
    total = buf_t.shape[0]
    nt = hp3.shape[1]
    steps = total // rows
    return pl.pallas_call(
        _dispatch_kernel,
        grid_spec=pltpu.PrefetchScalarGridSpec(
            num_scalar_prefetch=1,
            grid=(steps,),
            in_specs=[pl.BlockSpec((1, 1, rows), lambda i, nu: (i, 0, 0), memory_space=pltpu.SMEM),
                      pl.BlockSpec(memory_space=pl.ANY)],
            out_specs=pl.BlockSpec((rows, nt * LANES), lambda i, nu: (i, 0)),
            scratch_shapes=[pltpu.VMEM((rows * SLAB_PITCH, LANES), hp3.dtype),
                            pltpu.SemaphoreType.DMA(())]),
        out_shape=jax.ShapeDtypeStruct((total, nt * LANES), hp3.dtype),
        compiler_params=_params("arbitrary"),
        name="moe_dispatch",
    )(n_used, buf_t.reshape(steps, 1, rows), hp3)


def _expert_up_kernel(be_ref, nu_ref, x_ref, wg_ref, wu_ref, o_ref):
    @pl.when(pl.program_id(1) < nu_ref[0])
    def _():
        lo, hi = _unpack_halves(x_ref[...])
        lo = lo.astype(BF16)
        hi = hi.astype(BF16)
        half = lo.shape[1]

        def mm(w_ref):
            return (jnp.dot(lo, w_ref[0, :half, :].astype(BF16), preferred_element_type=F32)
                    + jnp.dot(hi, w_ref[0, half:, :].astype(BF16), preferred_element_type=F32))

        o_ref[...] = (_silu(mm(wg_ref)) * mm(wu_ref)).astype(o_ref.dtype)

    @pl.when(pl.program_id(1) >= nu_ref[0])
    def _():
        o_ref[...] = jnp.zeros_like(o_ref)


def _expert_up(xs, w_gate, w_up, blk_e, n_used, tmb, tf=256):
    total, half = xs.shape
    _, d, ff = w_gate.shape
    nblk = total // tmb

    def blk(bi, nu):
        return jnp.minimum(bi, nu[0] - 1)

    return pl.pallas_call(
        _expert_up_kernel,
        grid_spec=pltpu.PrefetchScalarGridSpec(
            num_scalar_prefetch=2,
            grid=(ff // tf, nblk),
            in_specs=[pl.BlockSpec((tmb, half), lambda c, bi, be, nu: (blk(bi, nu), 0)),
                      pl.BlockSpec((1, d, tf), lambda c, bi, be, nu: (be[bi], 0, c)),
                      pl.BlockSpec((1, d, tf), lambda c, bi, be, nu: (be[bi], 0, c))],
            out_specs=pl.BlockSpec((tmb, tf), lambda c, bi, be, nu: (bi, c))),
        out_shape=jax.ShapeDtypeStruct((total, ff), BF16),
        compiler_params=_params("arbitrary", "arbitrary"),
        name="expert_up",
    )(blk_e, n_used, xs, w_gate, w_up)


def _expert_down_kernel(be_ref, nu_ref, a_ref, wlo_ref, whi_ref, rw_ref, o_ref):
    @pl.when(pl.program_id(1) < nu_ref[0])
    def _():
        a = a_ref[...]
        rw = rw_ref[...]
        ylo = jnp.dot(a, wlo_ref[0].astype(BF16), preferred_element_type=F32) * rw
        yhi = jnp.dot(a, whi_ref[0].astype(BF16), preferred_element_type=F32) * rw
        o_ref[...] = pltpu.pack_elementwise([ylo, yhi], packed_dtype=BF16)

    @pl.when(pl.program_id(1) >= nu_ref[0])
    def _():
        o_ref[...] = jnp.zeros_like(o_ref)


def _expert_down(act, w_down, row_w, blk_e, n_used, tmb, tn=512):
    total, ff = act.shape
    d = w_down.shape[2]
    half = d // 2
    nblk = total // tmb
    nc = half // tn

    def blk(bi, nu):
        return jnp.minimum(bi, nu[0] - 1)

    return pl.pallas_call(
        _expert_down_kernel,
        grid_spec=pltpu.PrefetchScalarGridSpec(
            num_scalar_prefetch=2,
            grid=(nc, nblk),
            in_specs=[pl.BlockSpec((tmb, ff), lambda c, bi, be, nu: (blk(bi, nu), 0)),
                      pl.BlockSpec((1, ff, tn), lambda c, bi, be, nu: (be[bi], 0, c)),
                      pl.BlockSpec((1, ff, tn), lambda c, bi, be, nu: (be[bi], 0, nc + c)),
                      pl.BlockSpec((tmb, 1), lambda c, bi, be, nu: (blk(bi, nu), 0))],
            out_specs=pl.BlockSpec((tmb, tn), lambda c, bi, be, nu: (bi, c))),
        out_shape=jax.ShapeDtypeStruct((total, half), jnp.int32),
        compiler_params=_params("arbitrary", "arbitrary"),
        name="expert_down",
    )(blk_e, n_used, act, w_down, w_down, row_w)


def _row_copy(src_ref, dst_ref, src_row, dst_row, sem):
    return pltpu.make_async_copy(src_ref.at[pl.ds(src_row, 1)], dst_ref.at[pl.ds(dst_row, 1)], sem)


def _combine_kernel(p0_ref, p1_ref, ys_ref, x_ref, ga_ref, g_ref, o_ref, y0_ref, y1_ref, sem0, sem1):
    rows = y0_ref.shape[0]

    def start(r, c):
        _row_copy(ys_ref, y0_ref, p0_ref[0, 0, r], r, sem0).start()
        _row_copy(ys_ref, y1_ref, p1_ref[0, 0, r], r, sem1).start()
        return c

    def wait(r, c):
        _row_copy(ys_ref, y0_ref, p0_ref[0, 0, r], r, sem0).wait()
        _row_copy(ys_ref, y1_ref, p1_ref[0, 0, r], r, sem1).wait()
        return c

    lax.fori_loop(0, rows, start, 0)
    lax.fori_loop(0, rows, wait, 0)
    lo0, hi0 = _unpack_halves(y0_ref[...])
    lo1, hi1 = _unpack_halves(y1_ref[...])
    half = lo0.shape[1]
    x = x_ref[0]
    ga = ga_ref[0]
    zlo = x[:, :half] + ga[:, :half] * (lo0 + lo1)
    zhi = x[:, half:] + ga[:, half:] * (hi0 + hi1)
    ms = (jnp.sum(zlo * zlo, axis=-1, keepdims=True) + jnp.sum(zhi * zhi, axis=-1, keepdims=True)) / (2 * half)
    inv = lax.rsqrt(ms + EPS)
    o_ref[0, :, :half] = zlo * inv * g_ref[:, :half]
    o_ref[0, :, half:] = zhi * inv * g_ref[:, half:]


def _combine(ys, pos0, pos1, x, gate, g, rows):
    b, l, d = x.shape
    half = d // 2
    lb = l // rows
    return pl.pallas_call(
        _combine_kernel,
        grid=(b, lb),
        in_specs=[pl.BlockSpec((1, 1, rows), lambda bi, i: (bi * lb + i, 0, 0), memory_space=pltpu.SMEM),
                  pl.BlockSpec((1, 1, rows), lambda bi, i: (bi * lb + i, 0, 0), memory_space=pltpu.SMEM),
                  pl.BlockSpec(memory_space=pl.ANY),
                  pl.BlockSpec((1, rows, d), lambda bi, i: (bi, i, 0)),
                  pl.BlockSpec((1, 1, d), lambda bi, i: (bi, 0, 0)),
                  pl.BlockSpec((1, d), lambda bi, i: (0, 0))],
        out_specs=pl.BlockSpec((1, rows, d), lambda bi, i: (bi, i, 0)),
        out_shape=jax.ShapeDtypeStruct((b, l, d), F32),
        scratch_shapes=[pltpu.VMEM((rows, half), ys.dtype), pltpu.VMEM((rows, half), ys.dtype),
                        pltpu.SemaphoreType.DMA(()), pltpu.SemaphoreType.DMA(())],
        compiler_params=_params("arbitrary", "arbitrary"),
        name="moe_combine_norm",
    )(pos0.reshape(b * lb, 1, rows), pos1.reshape(b * lb, 1, rows), ys, x, gate, g)


def _route(logits):
    n = logits.shape[0]
    g_logits = logits[:, :N_GROUPS]
    e_logits = logits[:, N_GROUPS:N_GROUPS + N_EXPERTS].reshape(n, N_GROUPS, EXPERTS_PER_GROUP)
    g_prob = jax.nn.softmax(g_logits, axis=-1)
    g_sel = jnp.argmax(g_logits, axis=-1)
    p_g = jnp.take_along_axis(g_prob, g_sel[:, None], axis=1)[:, 0]
    e_in = jnp.take_along_axis(e_logits, g_sel[:, None, None], axis=1)[:, 0]
    top_v, top_i = lax.top_k(e_in, TOP_K)
    weights = p_g[:, None] * jax.nn.softmax(top_v, axis=-1)
    expert = g_sel[:, None].astype(jnp.int32) * EXPERTS_PER_GROUP + top_i.astype(jnp.int32)
    return expert, weights


def _block_layout(expert, weights, tmb):
    n = expert.shape[0]
    nk = n * TOP_K
    flat_e = expert.reshape(-1)
    flat_w = weights.reshape(-1)
    pair = jnp.arange(nk, dtype=jnp.int32)
    order = jnp.argsort(flat_e).astype(jnp.int32)
    se = flat_e[order]
    counts = jnp.sum((flat_e[:, None] == jnp.arange(N_EXPERTS, dtype=jnp.int32)[None, :]).astype(jnp.int32), axis=0)
    padded = ((counts + tmb - 1) // tmb) * tmb
    pend = jnp.cumsum(padded)
    pstart = pend - padded
    cstart = jnp.cumsum(counts) - counts
    dest = pstart[se] + pair - cstart[se]
    nblk = (nk + N_EXPERTS * (tmb - 1) + tmb - 1) // tmb
    total = nblk * tmb
    n_used = (pend[-1] // tmb).astype(jnp.int32)
    blk_raw = jnp.minimum(jnp.searchsorted(pend, jnp.arange(nblk, dtype=jnp.int32) * tmb, side="right"),
                          N_EXPERTS - 1).astype(jnp.int32)
    slot = jnp.arange(total, dtype=jnp.int32)
    e_slot = blk_raw[slot // tmb]
    within = slot - pstart[e_slot]
    valid = within < counts[e_slot]
    src = order[jnp.clip(within + cstart[e_slot], 0, nk - 1)]
    buf_t = jnp.where(valid, src // TOP_K, 0).astype(jnp.int32)
    buf_w = jnp.where(valid, flat_w[src], 0.0).astype(F32)
    pos = lax.sort((order, dest), num_keys=1)[1].reshape(n, TOP_K)
    blk_e = blk_raw[jnp.minimum(jnp.arange(nblk, dtype=jnp.int32), n_used - 1)]
    return buf_t, buf_w, pos, blk_e, n_used.reshape(1)


def _rope_tables(n_tokens):
    rows = n_tokens // GRID_W
    row, col = jnp.meshgrid(jnp.arange(rows), jnp.arange(GRID_W), indexing="ij")
    pos = jnp.stack([row.reshape(-1), col.reshape(-1)], axis=-1).astype(F32)
    inv = ROPE_THETA ** (-jnp.arange(0, ROPE_AXIS_DIM, 2, dtype=F32) / ROPE_AXIS_DIM)
    ang = pos[:, :, None] * inv[None, None, :]
    cos, sin = jnp.cos(ang), jnp.sin(ang)
    cos_t = jnp.concatenate([cos[:, 0], cos[:, 0], cos[:, 1], cos[:, 1]], axis=-1)
    sin_t = jnp.concatenate([-sin[:, 0], sin[:, 0], -sin[:, 1], sin[:, 1]], axis=-1)
    return cos_t, sin_t


def kernel(x, c, ctx, c_ctx, norm1_g, w_mod, b_mod, w_in, q_norm_g, k_norm_g, w_attn_out, conv_dw_w, conv_dw_b, conv_ln_g, conv_ln_b, w_conv_out, w_out, norm2_g, w_router_group, b_router_group, w_router_expert, b_router_expert, w_exp_gate, w_exp_up, w_exp_down, norm_f_g):
    b, s, d = x.shape
    n_ctx = ctx.shape[1]
    assert w_in.shape[0] == 1, "single-layer stack"
    conv_width = conv_dw_w.shape[-1]
    k_off = ATTN_WIDTH
    glu_off = k_off + 2 * KV_WIDTH
    gate_off = glu_off + 2 * conv_width

    n_c = b + 1
    cvec = jnp.zeros((SUBLANES * ((n_c + SUBLANES - 1) // SUBLANES), d), F32).at[:b].set(c).at[b].set(c_ctx)
    mod = _mod_vectors(cvec, w_mod[0], b_mod.reshape(1, -1))
    sh1, sc1, ga1, sh2, sc2, ga2 = [mod[:b, i * d:(i + 1) * d].reshape(b, 1, d) for i in range(N_MOD)]
    csh1, csc1 = [mod[b:b + 1, i * d:(i + 1) * d].reshape(1, 1, d) for i in range(2)]

    g1 = norm1_g.reshape(1, d)
    h = _norm_mod(x, g1, sh1, sc1, tl=512)
    hc = _norm_mod(ctx, g1, csh1, csc1, tl=n_ctx)
    w_in_b = w_in[0].astype(BF16)
    cos_t, sin_t = _rope_tables(s)
    qg = q_norm_g.reshape(1, HEAD_DIM)
    kg = k_norm_g.reshape(1, HEAD_DIM)
    q = _q_proj(h, w_in_b, qg, cos_t, sin_t, 0, HEAD_DIM ** -0.5 * LOG2E, tm=1024)
    k, v = _kv_proj(h, w_in_b, kg, cos_t, sin_t, k_off, tm=1024)
    kc, vc = _kv_proj(hc, w_in_b, kg, None, None, k_off, tm=n_ctx)
    attn = _attention(q, k, v, kc, vc, tq=128, tk=1024)

    h2d = h.reshape(b * s, d)
    u = _glu_proj(h2d, w_in_b, glu_off, conv_width, tm=1024)
    conv = _conv_module(u.reshape(b, s, conv_width), conv_dw_w.reshape(CONV_TAPS, conv_width // LANES, LANES),
                        conv_dw_b.reshape(conv_width // LANES, LANES), conv_ln_g.reshape(1, -1),
                        conv_ln_b.reshape(1, -1), tl=128)
    gates = _gate_proj(h2d, w_in_b, gate_off, 2 * d, tm=1024)
    mrg = _merge(attn.reshape(b * s, ATTN_WIDTH), conv.reshape(b * s, conv_width),
                 w_attn_out[0].astype(BF16), w_conv_out[0].astype(BF16), gates)
    x1 = _out_proj(mrg.reshape(b, s, d), w_out[0].astype(BF16), x, ga1)

    w_r = jnp.zeros((d, ROUTER_LANES), F32).at[:, :N_GROUPS].set(w_router_group[0]) \
        .at[:, N_GROUPS:N_GROUPS + N_EXPERTS].set(w_router_expert[0])
    b_r = jnp.zeros((1, ROUTER_LANES), F32).at[0, :N_GROUPS].set(b_router_group[0]) \
        .at[0, N_GROUPS:N_GROUPS + N_EXPERTS].set(b_router_expert[0])
    hp, logits = _norm2_router(x1, norm2_g.reshape(1, d), sh2, sc2, w_r.astype(BF16), b_r, tl=256)
    n = b * s
    expert, weights = _route(logits.reshape(n, ROUTER_LANES))
    tmb = 512
    buf_t, buf_w, pos, blk_e, n_used = _block_layout(expert, weights, tmb)
    xs = _dispatch(hp.reshape(n, d // 2 // LANES, LANES), buf_t, n_used, rows=tmb)
    act = _expert_up(xs, w_exp_gate[0], w_exp_up[0], blk_e, n_used, tmb)
    ys = _expert_down(act, w_exp_down[0], buf_w.reshape(-1, 1), blk_e, n_used, tmb)
    return _combine(ys, pos[:, 0], pos[:, 1], x1, ga2, norm_f_g.reshape(1, d), rows=256)
```

```python
import functools

import jax
import jax.numpy as jnp
from jax import lax
from jax.experimental import pallas as pl
from jax.experimental.pallas import tpu as pltpu

F32 = jnp.float32
BF16 = jnp.bfloat16

GRID_W = 64
HEAD_DIM = 128
N_Q_HEADS = 16
N_KV_HEADS = 4
Q_PER_KV = N_Q_HEADS // N_KV_HEADS
ATTN_WIDTH = N_Q_HEADS * HEAD_DIM
KV_WIDTH = N_KV_HEADS * HEAD_DIM
CONV_TAPS = 31
CONV_HALO = 16
ROPE_THETA = 10000.0
ROPE_AXIS_DIM = HEAD_DIM // 2
N_GROUPS = 4
EXPERTS_PER_GROUP = 8
N_EXPERTS = N_GROUPS * EXPERTS_PER_GROUP
TOP_K = 2
N_MOD = 6
EPS = 1e-6
LOG2E = 1.4426950408889634
LANES = 128
SUBLANES = 8
ROUTER_LANES = LANES

V7X_VMEM_LIMIT = 56 * 1024 * 1024


def _params(*sem):
    return pltpu.CompilerParams(dimension_semantics=sem, vmem_limit_bytes=V7X_VMEM_LIMIT)


def _sigmoid(x):
    return 1.0 / (1.0 + jnp.exp(-x))


def _silu(x):
    return x * _sigmoid(x)


def _rms(x, g):
    return x * lax.rsqrt(jnp.mean(x * x, axis=-1, keepdims=True) + EPS) * g


def _mod_kernel(c_ref, w_ref, b_ref, o_ref):
    s = _silu(c_ref[...]).astype(BF16)
    o_ref[...] = jnp.dot(s, w_ref[...].astype(BF16), preferred_element_type=F32) + b_ref[...]


def _mod_vectors(cvec, w_mod, b_mod, tn=512):
    m, d = cvec.shape
    n = w_mod.shape[1]
    return pl.pallas_call(
        _mod_kernel,
        grid=(n // tn,),
        in_specs=[pl.BlockSpec((m, d), lambda j: (0, 0)),
                  pl.BlockSpec((d, tn), lambda j: (0, j)),
                  pl.BlockSpec((1, tn), lambda j: (0, j))],
        out_specs=pl.BlockSpec((m, tn), lambda j: (0, j)),
        out_shape=jax.ShapeDtypeStruct((m, n), F32),
        compiler_params=_params("arbitrary"),
        name="mod_vectors",
    )(cvec, w_mod, b_mod)


def _norm_mod_kernel(x_ref, g_ref, sh_ref, sc_ref, o_ref):
    y = _rms(x_ref[0], g_ref[...])
    o_ref[0] = (y * (1.0 + sc_ref[0]) + sh_ref[0]).astype(o_ref.dtype)


def _norm_mod(x, g, shift, scale, tl):
    b, l, d = x.shape
    per_batch = shift.shape[0] > 1
    mod_map = (lambda bi, li: (bi, 0, 0)) if per_batch else (lambda bi, li: (0, 0, 0))
    return pl.pallas_call(
        _norm_mod_kernel,
        grid=(b, l // tl),
        in_specs=[pl.BlockSpec((1, tl, d), lambda bi, li: (bi, li, 0)),
                  pl.BlockSpec((1, d), lambda bi, li: (0, 0)),
                  pl.BlockSpec((1, 1, d), mod_map),
                  pl.BlockSpec((1, 1, d), mod_map)],
        out_specs=pl.BlockSpec((1, tl, d), lambda bi, li: (bi, li, 0)),
        out_shape=jax.ShapeDtypeStruct((b, l, d), BF16),
        compiler_params=_params("parallel", "parallel"),
        name="norm_modulate",
    )(x, g, shift, scale)


def _head_norm_rope(a, g, cos, sin):
    y = _rms(a, g)
    if cos is None:
        return y
    lane = lax.broadcasted_iota(jnp.int32, y.shape, 1)
    quarter = ROPE_AXIS_DIM // 2
    partner = jnp.where((lane % ROPE_AXIS_DIM) < quarter,
                        pltpu.roll(y, HEAD_DIM - quarter, 1), pltpu.roll(y, quarter, 1))
    return y * cos + partner * sin


def _q_proj_kernel(h_ref, w_ref, g_ref, cos_ref, sin_ref, o_ref, *, scale):
    acc = jnp.dot(h_ref[0], w_ref[...], preferred_element_type=F32)
    for hh in range(o_ref.shape[1]):
        a = acc[:, hh * HEAD_DIM:(hh + 1) * HEAD_DIM]
        y = _head_norm_rope(a, g_ref[...], cos_ref[...], sin_ref[...])
        o_ref[0, hh] = (y * scale).astype(o_ref.dtype)


def _q_proj(h, w, g, cos_t, sin_t, col_off, scale, tm, tn=1024):
    b, l, d = h.shape
    jb = col_off // tn
    hpt = tn // HEAD_DIM
    return pl.pallas_call(
        functools.partial(_q_proj_kernel, scale=scale),
        grid=(b, l // tm, ATTN_WIDTH // tn),
        in_specs=[pl.BlockSpec((1, tm, d), lambda bi, i, j: (bi, i, 0)),
                  pl.BlockSpec((d, tn), lambda bi, i, j: (0, jb + j)),
                  pl.BlockSpec((1, HEAD_DIM), lambda bi, i, j: (0, 0)),
                  pl.BlockSpec((tm, HEAD_DIM), lambda bi, i, j: (i, 0)),
                  pl.BlockSpec((tm, HEAD_DIM), lambda bi, i, j: (i, 0))],
        out_specs=pl.BlockSpec((1, hpt, tm, HEAD_DIM), lambda bi, i, j: (bi, j, i, 0)),
        out_shape=jax.ShapeDtypeStruct((b, N_Q_HEADS, l, HEAD_DIM), BF16),
        compiler_params=_params("parallel", "parallel", "arbitrary"),
        name="q_proj",
    )(h, w, g, cos_t, sin_t)


def _kv_proj_kernel(h_ref, w_ref, g_ref, *rest, rope):
    if rope:
        cos_ref, sin_ref, k_ref, v_ref = rest
        cos, sin = cos_ref[...], sin_ref[...]
    else:
        k_ref, v_ref = rest
        cos = sin = None
    acc = jnp.dot(h_ref[0], w_ref[...], preferred_element_type=F32)
    for hh in range(N_KV_HEADS):
        a = acc[:, hh * HEAD_DIM:(hh + 1) * HEAD_DIM]
        k_ref[0, :, hh * HEAD_DIM:(hh + 1) * HEAD_DIM] = _head_norm_rope(a, g_ref[...], cos, sin).astype(k_ref.dtype)
    v_ref[0] = acc[:, KV_WIDTH:].astype(v_ref.dtype)


def _kv_proj(h, w, g, cos_t, sin_t, col_off, tm):
    b, l, d = h.shape
    tn = 2 * KV_WIDTH
    jb = col_off // tn
    rope = cos_t is not None
    in_specs = [pl.BlockSpec((1, tm, d), lambda bi, i: (bi, i, 0)),
                pl.BlockSpec((d, tn), lambda bi, i: (0, jb)),
                pl.BlockSpec((1, HEAD_DIM), lambda bi, i: (0, 0))]
    args = [h, w, g]
    if rope:
        in_specs += [pl.BlockSpec((tm, HEAD_DIM), lambda bi, i: (i, 0))] * 2
        args += [cos_t, sin_t]
    return pl.pallas_call(
        functools.partial(_kv_proj_kernel, rope=rope),
        grid=(b, l // tm),
        in_specs=in_specs,
        out_specs=[pl.BlockSpec((1, tm, KV_WIDTH), lambda bi, i: (bi, i, 0))] * 2,
        out_shape=[jax.ShapeDtypeStruct((b, l, KV_WIDTH), BF16)] * 2,
        compiler_params=_params("parallel", "parallel"),
        name="kv_proj_rope" if rope else "kv_proj_ctx",
    )(*args)


def _glu_proj_kernel(h_ref, wa_ref, wg_ref, o_ref):
    a = jnp.dot(h_ref[...], wa_ref[...], preferred_element_type=F32)
    gt = jnp.dot(h_ref[...], wg_ref[...], preferred_element_type=F32)
    o_ref[...] = (a * _sigmoid(gt)).astype(o_ref.dtype)


def _glu_proj(h2d, w, col_off, width, tm, tn=512):
    m, d = h2d.shape
    ja = col_off // tn
    jg = (col_off + width) // tn
    return pl.pallas_call(
        _glu_proj_kernel,
        grid=(m // tm, width // tn),
        in_specs=[pl.BlockSpec((tm, d), lambda i, j: (i, 0)),
                  pl.BlockSpec((d, tn), lambda i, j: (0, ja + j)),
                  pl.BlockSpec((d, tn), lambda i, j: (0, jg + j))],
        out_specs=pl.BlockSpec((tm, tn), lambda i, j: (i, j)),
        out_shape=jax.ShapeDtypeStruct((m, width), BF16),
        compiler_params=_params("parallel", "arbitrary"),
        name="glu_proj",
    )(h2d, w, w)


def _gate_proj_kernel(h_ref, w_ref, o_ref):
    acc = jnp.dot(h_ref[...], w_ref[...], preferred_element_type=F32)
    o_ref[...] = _sigmoid(acc).astype(o_ref.dtype)


def _gate_proj(h2d, w, col_off, width, tm, tn=1024):
    m, d = h2d.shape
    jb = col_off // tn
    return pl.pallas_call(
        _gate_proj_kernel,
        grid=(m // tm, width // tn),
        in_specs=[pl.BlockSpec((tm, d), lambda i, j: (i, 0)),
                  pl.BlockSpec((d, tn), lambda i, j: (0, jb + j))],
        out_specs=pl.BlockSpec((tm, tn), lambda i, j: (i, j)),
        out_shape=jax.ShapeDtypeStruct((m, width), BF16),
        compiler_params=_params("parallel", "arbitrary"),
        name="gate_proj",
    )(h2d, w)


def _attn_kernel(q_ref, k_ref, v_ref, kc_ref, vc_ref, o_ref,
                 s0_ref, s1_ref, p0_ref, p1_ref, m_ref, al_ref, acc_ref, *, tk, rb):
    g, tq, dh = q_ref.shape[1:]
    rows = g * tq
    q = q_ref[0].reshape(rows, dh)
    chunks = [(k_ref, v_ref, c * tk, tk) for c in range(k_ref.shape[1] // tk)]
    chunks.append((kc_ref, vc_ref, 0, kc_ref.shape[1]))
    s_refs = (s0_ref, s1_ref)
    p_refs = (p0_ref, p1_ref)

    def scores(j):
        kr, _, st, n = chunks[j]
        s_refs[j % 2][:, :n] = lax.dot_general(q, kr[0, st:st + n, :], (((1,), (1,)), ((), ())),
                                               preferred_element_type=F32)

    def softmax(j):
        n = chunks[j][3]
        s_ref, p_ref = s_refs[j % 2], p_refs[j % 2]
        for r0 in range(0, rows, rb):
            sb = s_ref[r0:r0 + rb, :n]
            mn = jnp.max(sb, axis=-1, keepdims=True)
            if j > 0:
                mo = m_ref[r0:r0 + rb, :]
                mn = jnp.maximum(mo, mn)
                al_ref[r0:r0 + rb, :] = jnp.exp2(mo - mn)
            m_ref[r0:r0 + rb, :] = mn
            p_ref[r0:r0 + rb, :n] = jnp.exp2(sb - mn).astype(BF16)

    def weighted_values(j):
        _, vr, st, n = chunks[j]
        ones_col = (lax.broadcasted_iota(jnp.int32, (n, dh), 1) == 0).astype(BF16)
        v1 = jnp.concatenate([vr[0, st:st + n, :], ones_col], axis=1)
        upd = jnp.dot(p_refs[j % 2][:, :n], v1, preferred_element_type=F32)
        if j == 0:
            acc_ref[...] = upd
        else:
            acc_ref[...] = al_ref[...] * acc_ref[...] + upd

    scores(0)
    for j in range(len(chunks)):
        if j + 1 < len(chunks):
            scores(j + 1)
        softmax(j)
        weighted_values(j)

    acc = acc_ref[...]
    o = acc[:, :dh] / acc[:, dh:dh + 1]
    for gi in range(g):
        o_ref[0, :, gi * dh:(gi + 1) * dh] = o[gi * tq:(gi + 1) * tq].astype(o_ref.dtype)


def _attention(q, k, v, kc, vc, tq, tk, rb=16):
    b, _, l, dh = q.shape
    lc = kc.shape[1]
    rows = Q_PER_KV * tq
    gdh = Q_PER_KV * dh
    return pl.pallas_call(
        functools.partial(_attn_kernel, tk=tk, rb=rb),
        grid=(b, N_KV_HEADS, l // tq),
        in_specs=[pl.BlockSpec((1, Q_PER_KV, tq, dh), lambda bi, kh, qi: (bi, kh, qi, 0)),
                  pl.BlockSpec((1, l, dh), lambda bi, kh, qi: (bi, 0, kh)),
                  pl.BlockSpec((1, l, dh), lambda bi, kh, qi: (bi, 0, kh)),
                  pl.BlockSpec((1, lc, dh), lambda bi, kh, qi: (bi, 0, kh)),
                  pl.BlockSpec((1, lc, dh), lambda bi, kh, qi: (bi, 0, kh))],
        out_specs=pl.BlockSpec((1, tq, gdh), lambda bi, kh, qi: (bi, qi, kh)),
        out_shape=jax.ShapeDtypeStruct((b, l, ATTN_WIDTH), BF16),
        scratch_shapes=[pltpu.VMEM((rows, tk), F32), pltpu.VMEM((rows, tk), F32),
                        pltpu.VMEM((rows, tk), BF16), pltpu.VMEM((rows, tk), BF16),
                        pltpu.VMEM((rows, 1), F32), pltpu.VMEM((rows, 1), F32),
                        pltpu.VMEM((rows, 2 * dh), F32)],
        compiler_params=_params("parallel", "parallel", "arbitrary"),
        name="attention",
    )(q, k, v, kc, vc)


def _conv_kernel(prev_ref, cur_ref, next_ref, w_ref, b_ref, g_ref, beta_ref, o_ref, win_ref, y_ref, *, tc):
    li = pl.program_id(1)
    tl, c = cur_ref.shape[1:]
    halo = prev_ref.shape[1]
    nt = c // LANES

    def put_tokens(vals, tok0):
        for j in range(nt):
            win_ref[pl.ds(tok0 * nt + j, vals.shape[0], stride=nt), :] = vals[:, j * LANES:(j + 1) * LANES]

    put_tokens(jnp.where(li > 0, prev_ref[0].astype(F32), 0.0), 0)
    put_tokens(cur_ref[0].astype(F32), halo)
    put_tokens(jnp.where(li < pl.num_programs(1) - 1, next_ref[0].astype(F32), 0.0), halo + tl)

    first = halo - CONV_TAPS // 2
    bias = b_ref[...][None]

    def token_chunk(ci, carry):
        tok = ci * tc
        acc = jnp.zeros((tc, nt, LANES), F32) + bias
        for t in range(CONV_TAPS):
            r0 = pl.multiple_of((tok + first + t) * nt, nt)
            acc = acc + win_ref[pl.ds(r0, tc * nt), :].reshape(tc, nt, LANES) * w_ref[t][None]
        y_ref[pl.ds(pl.multiple_of(tok * nt, nt), tc * nt), :] = acc.reshape(tc * nt, LANES)
        return carry

    lax.fori_loop(0, tl // tc, token_chunk, 0)
    y = jnp.concatenate([y_ref[pl.ds(j, tl, stride=nt), :] for j in range(nt)], axis=1)
    mu = jnp.mean(y, axis=-1, keepdims=True)
    yc = y - mu
    var = jnp.mean(yc * yc, axis=-1, keepdims=True)
    z = yc * lax.rsqrt(var + EPS) * g_ref[...] + beta_ref[...]
    o_ref[0] = _silu(z).astype(o_ref.dtype)


def _conv_module(u, w_dw, b_dw, ln_g, ln_b, tl, tc=16):
    b, l, c = u.shape
    nt = c // LANES
    hb = tl // CONV_HALO
    n_halo = l // CONV_HALO
    return pl.pallas_call(
        functools.partial(_conv_kernel, tc=tc),
        grid=(b, l // tl),
        in_specs=[pl.BlockSpec((1, CONV_HALO, c), lambda bi, li: (bi, jnp.maximum(li * hb - 1, 0), 0)),
                  pl.BlockSpec((1, tl, c), lambda bi, li: (bi, li, 0)),
                  pl.BlockSpec((1, CONV_HALO, c), lambda bi, li: (bi, jnp.minimum((li + 1) * hb, n_halo - 1), 0)),
                  pl.BlockSpec((CONV_TAPS, nt, LANES), lambda bi, li: (0, 0, 0)),
                  pl.BlockSpec((nt, LANES), lambda bi, li: (0, 0)),
                  pl.BlockSpec((1, c), lambda bi, li: (0, 0)),
                  pl.BlockSpec((1, c), lambda bi, li: (0, 0))],
        out_specs=pl.BlockSpec((1, tl, c), lambda bi, li: (bi, li, 0)),
        out_shape=jax.ShapeDtypeStruct((b, l, c), BF16),
        scratch_shapes=[pltpu.VMEM(((tl + 2 * CONV_HALO) * nt, LANES), F32), pltpu.VMEM((tl * nt, LANES), F32)],
        compiler_params=_params("parallel", "arbitrary"),
        name="conv_module",
    )(u, u, u, w_dw, b_dw, ln_g, ln_b)


def _merge_kernel(a_ref, c_ref, wa_ref, wc_ref, ga_ref, gc_ref, o_ref):
    a = jnp.dot(a_ref[...], wa_ref[...], preferred_element_type=F32)
    cb = jnp.dot(c_ref[...], wc_ref[...], preferred_element_type=F32)
    o_ref[...] = (ga_ref[...].astype(F32) * a + gc_ref[...].astype(F32) * cb).astype(o_ref.dtype)


def _merge(attn, conv, wa, wc, gates, tm=512, tn=1024):
    m, ka = attn.shape
    kc = conv.shape[1]
    d = wa.shape[1]
    nj = d // tn
    return pl.pallas_call(
        _merge_kernel,
        grid=(nj, m // tm),
        in_specs=[pl.BlockSpec((tm, ka), lambda j, i: (i, 0)),
                  pl.BlockSpec((tm, kc), lambda j, i: (i, 0)),
                  pl.BlockSpec((ka, tn), lambda j, i: (0, j)),
                  pl.BlockSpec((kc, tn), lambda j, i: (0, j)),
                  pl.BlockSpec((tm, tn), lambda j, i: (i, j)),
                  pl.BlockSpec((tm, tn), lambda j, i: (i, nj + j))],
        out_specs=pl.BlockSpec((tm, tn), lambda j, i: (i, j)),
        out_shape=jax.ShapeDtypeStruct((m, d), BF16),
        compiler_params=_params("parallel", "arbitrary"),
        name="merge_branches",
    )(attn, conv, wa, wc, gates, gates)


def _out_proj_kernel(m_ref, w_ref, x_ref, ga_ref, o_ref):
    acc = jnp.dot(m_ref[0], w_ref[...], preferred_element_type=F32)
    o_ref[0] = x_ref[0] + ga_ref[0] * acc


def _out_proj(mrg, w, x, gate, tm=1024, tn=1024):
    b, l, d = x.shape
    return pl.pallas_call(
        _out_proj_kernel,
        grid=(b, l // tm, d // tn),
        in_specs=[pl.BlockSpec((1, tm, d), lambda bi, i, j: (bi, i, 0)),
                  pl.BlockSpec((d, tn), lambda bi, i, j: (0, j)),
                  pl.BlockSpec((1, tm, tn), lambda bi, i, j: (bi, i, j)),
                  pl.BlockSpec((1, 1, tn), lambda bi, i, j: (bi, 0, j))],
        out_specs=pl.BlockSpec((1, tm, tn), lambda bi, i, j: (bi, i, j)),
        out_shape=jax.ShapeDtypeStruct((b, l, d), F32),
        compiler_params=_params("parallel", "parallel", "arbitrary"),
        name="out_proj_residual",
    )(mrg, w, x, gate)


def _pack_halves(y):
    n = y.shape[1] // 2
    return pltpu.pack_elementwise([y[:, :n], y[:, n:]], packed_dtype=BF16)


def _unpack_halves(p):
    lo = pltpu.unpack_elementwise(p, index=0, packed_dtype=BF16, unpacked_dtype=F32)
    hi = pltpu.unpack_elementwise(p, index=1, packed_dtype=BF16, unpacked_dtype=F32)
    return lo, hi


def _norm2_router_kernel(x_ref, g_ref, sh_ref, sc_ref, wr_ref, br_ref, hp_ref, lg_ref):
    y = _rms(x_ref[0], g_ref[...]) * (1.0 + sc_ref[0]) + sh_ref[0]
    packed = _pack_halves(y)
    tl = packed.shape[0]
    nt = packed.shape[1] // LANES
    for j in range(nt):
        hp_ref[pl.ds(j, tl, stride=nt), :] = packed[:, j * LANES:(j + 1) * LANES]
    lg_ref[0] = jnp.dot(y.astype(BF16), wr_ref[...], preferred_element_type=F32) + br_ref[...]


def _norm2_router(x, g, shift, scale, w_r, b_r, tl):
    b, l, d = x.shape
    nt = d // 2 // LANES
    lb = l // tl
    return pl.pallas_call(
        _norm2_router_kernel,
        grid=(b, lb),
        in_specs=[pl.BlockSpec((1, tl, d), lambda bi, li: (bi, li, 0)),
                  pl.BlockSpec((1, d), lambda bi, li: (0, 0)),
                  pl.BlockSpec((1, 1, d), lambda bi, li: (bi, 0, 0)),
                  pl.BlockSpec((1, 1, d), lambda bi, li: (bi, 0, 0)),
                  pl.BlockSpec((d, ROUTER_LANES), lambda bi, li: (0, 0)),
                  pl.BlockSpec((1, ROUTER_LANES), lambda bi, li: (0, 0))],
        out_specs=[pl.BlockSpec((tl * nt, LANES), lambda bi, li: (bi * lb + li, 0)),
                   pl.BlockSpec((1, tl, ROUTER_LANES), lambda bi, li: (bi, li, 0))],
        out_shape=[jax.ShapeDtypeStruct((b * l * nt, LANES), jnp.int32),
                   jax.ShapeDtypeStruct((b, l, ROUTER_LANES), F32)],
        compiler_params=_params("parallel", "parallel"),
        name="norm2_router",
    )(x, g, shift, scale, w_r, b_r)


def _first_lane(mask, lane):
    return jnp.min(jnp.where(mask, lane, LANES), axis=-1, keepdims=True)


def _route_kernel(lg_ref, meta_ref, wts_ref, cnt_ref, carry_ref):
    @pl.when(pl.program_id(0) == 0)
    def _():
        carry_ref[...] = jnp.zeros_like(carry_ref)

    lg = lg_ref[...]
    tb = lg.shape[0]
    lane = lax.broadcasted_iota(jnp.int32, lg.shape, 1)
    neg_inf = jnp.float32(-jnp.inf)
    is_group = lane < N_GROUPS
    gl = jnp.where(is_group, lg, neg_inf)
    g_max = jnp.max(gl, axis=-1, keepdims=True)
    g_sel = _first_lane(gl == g_max, lane)
    p_g = 1.0 / jnp.sum(jnp.where(is_group, jnp.exp(lg - g_max), 0.0), axis=-1, keepdims=True)

    e_idx = lane - N_GROUPS
    in_group = (e_idx >= g_sel * EXPERTS_PER_GROUP) & (e_idx < (g_sel + 1) * EXPERTS_PER_GROUP)
    ev = jnp.where(in_group, lg, neg_inf)
    v1 = jnp.max(ev, axis=-1, keepdims=True)
    i1 = _first_lane(ev == v1, lane)
    ev2 = jnp.where(lane == i1, neg_inf, ev)
    v2 = jnp.max(ev2, axis=-1, keepdims=True)
    i2 = _first_lane(ev2 == v2, lane)
    t = jnp.exp(v2 - v1)
    w1 = p_g / (1.0 + t)
    w2 = w1 * t

    oh1 = lane == i1
    oh2 = lane == i2
    oh = (oh1 | oh2).astype(BF16)
    earlier = (lax.broadcasted_iota(jnp.int32, (tb, tb), 0) > lax.broadcasted_iota(jnp.int32, (tb, tb), 1)).astype(BF16)
    before = jnp.dot(earlier, oh, preferred_element_type=F32) + carry_ref[...]
    r1 = jnp.sum(jnp.where(oh1, before, 0.0), axis=-1, keepdims=True).astype(jnp.int32)
    r2 = jnp.sum(jnp.where(oh2, before, 0.0), axis=-1, keepdims=True).astype(jnp.int32)
    carry_ref[...] += jnp.sum(oh.astype(F32), axis=0, keepdims=True)

    meta_ref[...] = jnp.where(lane == 0, i1 - N_GROUPS, jnp.where(lane == 1, i2 - N_GROUPS,
                              jnp.where(lane == 2, r1, jnp.where(lane == 3, r2, 0))))
    wts_ref[...] = jnp.where(lane == 0, w1, jnp.where(lane == 1, w2, 0.0))
    cnt_ref[...] = carry_ref[...]


def _route(logits, tb=512):
    n = logits.shape[0]
    return pl.pallas_call(
        _route_kernel,
        grid=(n // tb,),
        in_specs=[pl.BlockSpec((tb, LANES), lambda i: (i, 0))],
        out_specs=[pl.BlockSpec((tb, LANES), lambda i: (i, 0)),
                   pl.BlockSpec((tb, LANES), lambda i: (i, 0)),
                   pl.BlockSpec((1, LANES), lambda i: (0, 0))],
        out_shape=[jax.ShapeDtypeStruct((n, LANES), jnp.int32),
                   jax.ShapeDtypeStruct((n, LANES), F32),
                   jax.ShapeDtypeStruct((1, LANES), F32)],
        scratch_shapes=[pltpu.VMEM((1, LANES), F32)],
        compiler_params=_params("arbitrary"),
        name="moe_route",
    )(logits)


def _dest_kernel(meta_ref, pst_ref, o_ref):
    meta = meta_ref[...]
    lane = lax.broadcasted_iota(jnp.int32, meta.shape, 1)
    pst = pst_ref[...]

    def row_of(slot):
        e = meta[:, slot:slot + 1]
        start = jnp.sum(jnp.where(lane == e + N_GROUPS, pst, 0), axis=-1, keepdims=True)
        return start + meta[:, TOP_K + slot:TOP_K + slot + 1]

    o_ref[...] = jnp.where(lane == 0, row_of(0), jnp.where(lane == 1, row_of(1), 0))


def _dest_rows(meta, pst, tb=512):
    n = meta.shape[0]
    return pl.pallas_call(
        _dest_kernel,
        grid=(n // tb,),
        in_specs=[pl.BlockSpec((tb, LANES), lambda i: (i, 0)),
                  pl.BlockSpec((1, LANES), lambda i: (0, 0))],
        out_specs=pl.BlockSpec((tb, LANES), lambda i: (i, 0)),
        out_shape=jax.ShapeDtypeStruct((n, LANES), jnp.int32),
        compiler_params=_params("parallel"),
        name="moe_dest_rows",
    )(meta, pst)


def _zero_tail_kernel(lb_ref, o_ref):
    o_ref[...] = jnp.zeros_like(o_ref)


def _zero_tails(last_blk, total, tmb, nt):
    return pl.pallas_call(
        _zero_tail_kernel,
        grid_spec=pltpu.PrefetchScalarGridSpec(
            num_scalar_prefetch=1,
            grid=(last_blk.shape[0],),
            in_specs=[],
            out_specs=pl.BlockSpec((tmb * nt, LANES), lambda e, lb: (lb[e], 0))),
        out_shape=jax.ShapeDtypeStruct((total * nt, LANES), jnp.int32),
        compiler_params=_params("arbitrary"),
        name="moe_zero_tails",
    )(last_blk)


def _dispatch_kernel(dest_ref, hp_ref, xs_in_ref, xs_ref, sem):
    pairs = dest_ref.shape[2]

    def slab_copy(p):
        return pltpu.make_async_copy(hp_ref.at[p // TOP_K], xs_ref.at[dest_ref[0, 0, p]], sem)

    def start(p, c):
        slab_copy(p).start()
        return c

    def wait(p, c):
        slab_copy(p).wait()
        return c

    lax.fori_loop(0, pairs, start, 0)
    lax.fori_loop(0, pairs, wait, 0)


def _dispatch(hp3, dest, xs0, rows):
    n, nt, _ = hp3.shape
    steps = n // rows
    return pl.pallas_call(
        _dispatch_kernel,
        grid=(steps,),
        in_specs=[pl.BlockSpec((1, 1, rows * TOP_K), lambda i: (i, 0, 0), memory_space=pltpu.SMEM),
                  pl.BlockSpec((rows, nt, LANES), lambda i: (i, 0, 0)),
                  pl.BlockSpec(memory_space=pl.ANY)],
        out_specs=pl.BlockSpec(memory_space=pl.ANY),
        out_shape=jax.ShapeDtypeStruct(xs0.shape, xs0.dtype),
        scratch_shapes=[pltpu.SemaphoreType.DMA(())],
        input_output_aliases={2: 0},
        compiler_params=_params("arbitrary"),
        name="moe_dispatch",
    )(dest.reshape(steps, 1, rows * TOP_K), hp3, xs0)


def _expert_up_kernel(be_ref, nu_ref, x_ref, wg_ref, wu_ref, o_ref, *, nt):
    @pl.when(pl.program_id(1) < nu_ref[0])
    def _():
        tmb = o_ref.shape[0]
        words = jnp.concatenate([x_ref[pl.ds(j, tmb, stride=nt), :] for j in range(nt)], axis=1)
        lo, hi = _unpack_halves(words)
        lo = lo.astype(BF16)
        hi = hi.astype(BF16)
        half = lo.shape[1]

        def mm(w_ref):
            return (jnp.dot(lo, w_ref[0, :half, :].astype(BF16), preferred_element_type=F32)
                    + jnp.dot(hi, w_ref[0, half:, :].astype(BF16), preferred_element_type=F32))

        o_ref[...] = (_silu(mm(wg_ref)) * mm(wu_ref)).astype(o_ref.dtype)

    @pl.when(pl.program_id(1) >= nu_ref[0])
    def _():
        o_ref[...] = jnp.zeros_like(o_ref)


def _expert_up(xs, w_gate, w_up, blk_e, n_used, tmb, tf=256):
    _, d, ff = w_gate.shape
    nt = d // 2 // LANES
    total = xs.shape[0] // nt
    nblk = total // tmb

    def blk(bi, nu):
        return jnp.minimum(bi, nu[0] - 1)

    return pl.pallas_call(
        functools.partial(_expert_up_kernel, nt=nt),
        grid_spec=pltpu.PrefetchScalarGridSpec(
            num_scalar_prefetch=2,
            grid=(ff // tf, nblk),
            in_specs=[pl.BlockSpec((tmb * nt, LANES), lambda c, bi, be, nu: (blk(bi, nu), 0)),
                      pl.BlockSpec((1, d, tf), lambda c, bi, be, nu: (be[bi], 0, c)),
                      pl.BlockSpec((1, d, tf), lambda c, bi, be, nu: (be[bi], 0, c))],
            out_specs=pl.BlockSpec((tmb, tf), lambda c, bi, be, nu: (bi, c))),
        out_shape=jax.ShapeDtypeStruct((total, ff), BF16),
        compiler_params=_params("arbitrary", "arbitrary"),
        name="expert_up",
    )(blk_e, n_used, xs, w_gate, w_up)


def _expert_down_kernel(be_ref, nu_ref, a_ref, wlo_ref, whi_ref, o_ref):
    @pl.when(pl.program_id(1) < nu_ref[0])
    def _():
        a = a_ref[...]
        ylo = jnp.dot(a, wlo_ref[0].astype(BF16), preferred_element_type=F32)
        yhi = jnp.dot(a, whi_ref[0].astype(BF16), preferred_element_type=F32)
        o_ref[...] = pltpu.pack_elementwise([ylo, yhi], packed_dtype=BF16)

    @pl.when(pl.program_id(1) >= nu_ref[0])
    def _():
        o_ref[...] = jnp.zeros_like(o_ref)


def _expert_down(act, w_down, blk_e, n_used, tmb, tn=1024):
    total, ff = act.shape
    d = w_down.shape[2]
    half = d // 2
    nblk = total // tmb
    nc = half // tn

    def blk(bi, nu):
        return jnp.minimum(bi, nu[0] - 1)

    return pl.pallas_call(
        _expert_down_kernel,
        grid_spec=pltpu.PrefetchScalarGridSpec(
            num_scalar_prefetch=2,
            grid=(nc, nblk),
            in_specs=[pl.BlockSpec((tmb, ff), lambda c, bi, be, nu: (blk(bi, nu), 0)),
                      pl.BlockSpec((1, ff, tn), lambda c, bi, be, nu: (be[bi], 0, c)),
                      pl.BlockSpec((1, ff, tn), lambda c, bi, be, nu: (be[bi], 0, nc + c))],
            out_specs=pl.BlockSpec((tmb, tn), lambda c, bi, be, nu: (bi, c))),
        out_shape=jax.ShapeDtypeStruct((total, half), jnp.int32),
        compiler_params=_params("arbitrary", "arbitrary"),
        name="expert_down",
    )(blk_e, n_used, act, w_down, w_down)


def _row_copy(src_ref, dst_ref, src_row, dst_row, sem):
    return pltpu.make_async_copy(src_ref.at[pl.ds(src_row, 1)], dst_ref.at[pl.ds(dst_row, 1)], sem)


def _combine_kernel(p0_ref, p1_ref, ys_ref, w_ref, x_ref, ga_ref, g_ref, o_ref, y0_ref, y1_ref, sem0, sem1):
    rows = y0_ref.shape[0]

    def start(r, c):
        _row_copy(ys_ref, y0_ref, p0_ref[0, 0, r], r, sem0).start()
        _row_copy(ys_ref, y1_ref, p1_ref[0, 0, r], r, sem1).start()
        return c

    def wait(r, c):
        _row_copy(ys_ref, y0_ref, p0_ref[0, 0, r], r, sem0).wait()
        _row_copy(ys_ref, y1_ref, p1_ref[0, 0, r], r, sem1).wait()
        return c

    lax.fori_loop(0, rows, start, 0)
    lax.fori_loop(0, rows, wait, 0)
    lo0, hi0 = _unpack_halves(y0_ref[...])
    lo1, hi1 = _unpack_halves(y1_ref[...])
    half = lo0.shape[1]
    x = x_ref[0]
    ga = ga_ref[0]
    w0 = w_ref[:, 0:1]
    w1 = w_ref[:, 1:2]
    zlo = x[:, :half] + ga[:, :half] * (w0 * lo0 + w1 * lo1)
    zhi = x[:, half:] + ga[:, half:] * (w0 * hi0 + w1 * hi1)
    ms = (jnp.sum(zlo * zlo, axis=-1, keepdims=True) + jnp.sum(zhi * zhi, axis=-1, keepdims=True)) / (2 * half)
    inv = lax.rsqrt(ms + EPS)
    o_ref[0, :, :half] = zlo * inv * g_ref[:, :half]
    o_ref[0, :, half:] = zhi * inv * g_ref[:, half:]


def _combine(ys, pos0, pos1, wts, x, gate, g, rows):
    b, l, d = x.shape
    half = d // 2
    lb = l // rows
    return pl.pallas_call(
        _combine_kernel,
        grid=(b, lb),
        in_specs=[pl.BlockSpec((1, 1, rows), lambda bi, i: (bi * lb + i, 0, 0), memory_space=pltpu.SMEM),
                  pl.BlockSpec((1, 1, rows), lambda bi, i: (bi * lb + i, 0, 0), memory_space=pltpu.SMEM),
                  pl.BlockSpec(memory_space=pl.ANY),
                  pl.BlockSpec((rows, LANES), lambda bi, i: (bi * lb + i, 0)),
                  pl.BlockSpec((1, rows, d), lambda bi, i: (bi, i, 0)),
                  pl.BlockSpec((1, 1, d), lambda bi, i: (bi, 0, 0)),
                  pl.BlockSpec((1, d), lambda bi, i: (0, 0))],
        out_specs=pl.BlockSpec((1, rows, d), lambda bi, i: (bi, i, 0)),
        out_shape=jax.ShapeDtypeStruct((b, l, d), F32),
        scratch_shapes=[pltpu.VMEM((rows, half), ys.dtype), pltpu.VMEM((rows, half), ys.dtype),
                        pltpu.SemaphoreType.DMA(()), pltpu.SemaphoreType.DMA(())],
        compiler_params=_params("arbitrary", "arbitrary"),
        name="moe_combine_norm",
    )(pos0.reshape(b * lb, 1, rows), pos1.reshape(b * lb, 1, rows), ys, wts, x, gate, g)


def _block_layout(counts, n_pairs, tmb):
    nblk = (n_pairs + N_EXPERTS * (tmb - 1) + tmb - 1) // tmb
    blocks = (counts + tmb - 1) // tmb
    bend = jnp.cumsum(blocks)
    pstart = (bend - blocks) * tmb
    n_used = bend[-1]
    blk_ids = jnp.minimum(jnp.arange(nblk, dtype=jnp.int32), n_used - 1)
    blk_e = jnp.minimum(jnp.searchsorted(bend, blk_ids, side="right"), N_EXPERTS - 1).astype(jnp.int32)
    last_blk = jnp.maximum(bend - 1, 0).astype(jnp.int32)
    return nblk, pstart.astype(jnp.int32), blk_e, last_blk, n_used.astype(jnp.int32).reshape(1)


def _rope_tables(n_tokens):
    rows = n_tokens // GRID_W
    row, col = jnp.meshgrid(jnp.arange(rows), jnp.arange(GRID_W), indexing="ij")
    pos = jnp.stack([row.reshape(-1), col.reshape(-1)], axis=-1).astype(F32)
    inv = ROPE_THETA ** (-jnp.arange(0, ROPE_AXIS_DIM, 2, dtype=F32) / ROPE_AXIS_DIM)
    ang = pos[:, :, None] * inv[None, None, :]
    cos, sin = jnp.cos(ang), jnp.sin(ang)
    cos_t = jnp.concatenate([cos[:, 0], cos[:, 0], cos[:, 1], cos[:, 1]], axis=-1)
    sin_t = jnp.concatenate([-sin[:, 0], sin[:, 0], -sin[:, 1], sin[:, 1]], axis=-1)
    return cos_t, sin_t


def kernel(x, c, ctx, c_ctx, norm1_g, w_mod, b_mod, w_in, q_norm_g, k_norm_g, w_attn_out, conv_dw_w, conv_dw_b, conv_ln_g, conv_ln_b, w_conv_out, w_out, norm2_g, w_router_group, b_router_group, w_router_expert, b_router_expert, w_exp_gate, w_exp_up, w_exp_down, norm_f_g):
    b, s, d = x.shape
    n_ctx = ctx.shape[1]
    assert w_in.shape[0] == 1, "single-layer stack"
    conv_width = conv_dw_w.shape[-1]
    k_off = ATTN_WIDTH
    glu_off = k_off + 2 * KV_WIDTH
    gate_off = glu_off + 2 * conv_width

    n_c = b + 1
    cvec = jnp.zeros((SUBLANES * ((n_c + SUBLANES - 1) // SUBLANES), d), F32).at[:b].set(c).at[b].set(c_ctx)
    mod = _mod_vectors(cvec, w_mod[0], b_mod.reshape(1, -1))
    sh1, sc1, ga1, sh2, sc2, ga2 = [mod[:b, i * d:(i + 1) * d].reshape(b, 1, d) for i in range(N_MOD)]
    csh1, csc1 = [mod[b:b + 1, i * d:(i + 1) * d].reshape(1, 1, d) for i in range(2)]

    g1 = norm1_g.reshape(1, d)
    h = _norm_mod(x, g1, sh1, sc1, tl=512)
    hc = _norm_mod(ctx, g1, csh1, csc1, tl=n_ctx)
    w_in_b = w_in[0].astype(BF16)
    cos_t, sin_t = _rope_tables(s)
    qg = q_norm_g.reshape(1, HEAD_DIM)
    kg = k_norm_g.reshape(1, HEAD_DIM)
    q = _q_proj(h, w_in_b, qg, cos_t, sin_t, 0, HEAD_DIM ** -0.5 * LOG2E, tm=1024)
    k, v = _kv_proj(h, w_in_b, kg, cos_t, sin_t, k_off, tm=1024)
    kc, vc = _kv_proj(hc, w_in_b, kg, None, None, k_off, tm=n_ctx)
    attn = _attention(q, k, v, kc, vc, tq=128, tk=1024)

    h2d = h.reshape(b * s, d)
    u = _glu_proj(h2d, w_in_b, glu_off, conv_width, tm=1024)
    conv = _conv_module(u.reshape(b, s, conv_width), conv_dw_w.reshape(CONV_TAPS, conv_width // LANES, LANES),
                        conv_dw_b.reshape(conv_width // LANES, LANES), conv_ln_g.reshape(1, -1),
                        conv_ln_b.reshape(1, -1), tl=128)
    gates = _gate_proj(h2d, w_in_b, gate_off, 2 * d, tm=1024)
    mrg = _merge(attn.reshape(b * s, ATTN_WIDTH), conv.reshape(b * s, conv_width),
                 w_attn_out[0].astype(BF16), w_conv_out[0].astype(BF16), gates)
    x1 = _out_proj(mrg.reshape(b, s, d), w_out[0].astype(BF16), x, ga1)

    w_r = jnp.zeros((d, ROUTER_LANES), F32).at[:, :N_GROUPS].set(w_router_group[0]) \
        .at[:, N_GROUPS:N_GROUPS + N_EXPERTS].set(w_router_expert[0])
    b_r = jnp.zeros((1, ROUTER_LANES), F32).at[0, :N_GROUPS].set(b_router_group[0]) \
        .at[0, N_GROUPS:N_GROUPS + N_EXPERTS].set(b_router_expert[0])
    hp, logits = _norm2_router(x1, norm2_g.reshape(1, d), sh2, sc2, w_r.astype(BF16), b_r, tl=256)
    n = b * s
    nt = d // 2 // LANES
    tmb = 512
    meta, wts, cnt = _route(logits.reshape(n, ROUTER_LANES))
    counts = cnt[0, N_GROUPS:N_GROUPS + N_EXPERTS].astype(jnp.int32)
    nblk, pstart, blk_e, last_blk, n_used = _block_layout(counts, n * TOP_K, tmb)
    pst = jnp.zeros((1, LANES), jnp.int32).at[0, N_GROUPS:N_GROUPS + N_EXPERTS].set(pstart)
    dest = _dest_rows(meta, pst)[:, :TOP_K]
    xs0 = _zero_tails(last_blk, nblk * tmb, tmb, nt)
    xs = _dispatch(hp.reshape(n, nt, LANES), dest.reshape(-1), xs0.reshape(nblk * tmb, nt, LANES), rows=256)
    act = _expert_up(xs.reshape(nblk * tmb * nt, LANES), w_exp_gate[0], w_exp_up[0], blk_e, n_used, tmb)
    ys = _expert_down(act, w_exp_down[0], blk_e, n_used, tmb)
    return _combine(ys, dest[:, 0], dest[:, 1], wts, x1, ga2, norm_f_g.reshape(1, d), rows=256)
```

```python
import functools

import jax
import jax.numpy as jnp
from jax import lax
from jax.experimental import pallas as pl
from jax.experimental.pallas import tpu as pltpu

F32 = jnp.float32
BF16 = jnp.bfloat16

GRID_W = 64
HEAD_DIM = 128
N_Q_HEADS = 16
N_KV_HEADS = 4
Q_PER_KV = N_Q_HEADS // N_KV_HEADS
ATTN_WIDTH = N_Q_HEADS * HEAD_DIM
KV_WIDTH = N_KV_HEADS * HEAD_DIM
CONV_TAPS = 31
CONV_HALO = 16
ROPE_THETA = 10000.0
ROPE_AXIS_DIM = HEAD_DIM // 2
N_GROUPS = 4
EXPERTS_PER_GROUP = 8
N_EXPERTS = N_GROUPS * EXPERTS_PER_GROUP
TOP_K = 2
N_MOD = 6
EPS = 1e-6
LOG2E = 1.4426950408889634
LANES = 128
SUBLANES = 8
ROUTER_LANES = LANES
SLAB_PITCH = 24
DMA_LOOP_UNROLL = 4

V7X_VMEM_LIMIT = 56 * 1024 * 1024


def _params(*sem):
    return pltpu.CompilerParams(dimension_semantics=sem, vmem_limit_bytes=V7X_VMEM_LIMIT)


def _sigmoid(x):
    return 1.0 / (1.0 + jnp.exp(-x))


def _silu(x):
    return x * _sigmoid(x)


def _rms(x, g):
    return x * lax.rsqrt(jnp.mean(x * x, axis=-1, keepdims=True) + EPS) * g


def _mod_kernel(c_ref, w_ref, b_ref, o_ref):
    s = _silu(c_ref[...]).astype(BF16)
    o_ref[...] = jnp.dot(s, w_ref[...].astype(BF16), preferred_element_type=F32) + b_ref[...]


def _mod_vectors(cvec, w_mod, b_mod, tn=512):
    m, d = cvec.shape
    n = w_mod.shape[1]
    return pl.pallas_call(
        _mod_kernel,
        grid=(n // tn,),
        in_specs=[pl.BlockSpec((m, d), lambda j: (0, 0)),
                  pl.BlockSpec((d, tn), lambda j: (0, j)),
                  pl.BlockSpec((1, tn), lambda j: (0, j))],
        out_specs=pl.BlockSpec((m, tn), lambda j: (0, j)),
        out_shape=jax.ShapeDtypeStruct((m, n), F32),
        compiler_params=_params("arbitrary"),
        name="mod_vectors",
    )(cvec, w_mod, b_mod)


def _norm_mod_kernel(x_ref, g_ref, sh_ref, sc_ref, o_ref):
    y = _rms(x_ref[0], g_ref[...])
    o_ref[0] = (y * (1.0 + sc_ref[0]) + sh_ref[0]).astype(o_ref.dtype)


def _norm_mod(x, g, shift, scale, tl):
    b, l, d = x.shape
    per_batch = shift.shape[0] > 1
    mod_map = (lambda bi, li: (bi, 0, 0)) if per_batch else (lambda bi, li: (0, 0, 0))
    return pl.pallas_call(
        _norm_mod_kernel,
        grid=(b, l // tl),
        in_specs=[pl.BlockSpec((1, tl, d), lambda bi, li: (bi, li, 0)),
                  pl.BlockSpec((1, d), lambda bi, li: (0, 0)),
                  pl.BlockSpec((1, 1, d), mod_map),
                  pl.BlockSpec((1, 1, d), mod_map)],
        out_specs=pl.BlockSpec((1, tl, d), lambda bi, li: (bi, li, 0)),
        out_shape=jax.ShapeDtypeStruct((b, l, d), BF16),
        compiler_params=_params("parallel", "parallel"),
        name="norm_modulate",
    )(x, g, shift, scale)


def _head_norm_rope(a, g, cos, sin):
    y = _rms(a, g)
    if cos is None:
        return y
    lane = lax.broadcasted_iota(jnp.int32, y.shape, 1)
    quarter = ROPE_AXIS_DIM // 2
    partner = jnp.where((lane % ROPE_AXIS_DIM) < quarter,
                        pltpu.roll(y, HEAD_DIM - quarter, 1), pltpu.roll(y, quarter, 1))
    return y * cos + partner * sin


def _q_proj_kernel(h_ref, w_ref, g_ref, cos_ref, sin_ref, o_ref, *, scale):
    acc = jnp.dot(h_ref[0], w_ref[...], preferred_element_type=F32)
    for hh in range(o_ref.shape[1]):
        a = acc[:, hh * HEAD_DIM:(hh + 1) * HEAD_DIM]
        y = _head_norm_rope(a, g_ref[...], cos_ref[...], sin_ref[...])
        o_ref[0, hh] = (y * scale).astype(o_ref.dtype)


def _q_proj(h, w, g, cos_t, sin_t, col_off, scale, tm, tn=1024):
    b, l, d = h.shape
    jb = col_off // tn
    hpt = tn // HEAD_DIM
    return pl.pallas_call(
        functools.partial(_q_proj_kernel, scale=scale),
        grid=(b, l // tm, ATTN_WIDTH // tn),
        in_specs=[pl.BlockSpec((1, tm, d), lambda bi, i, j: (bi, i, 0)),
                  pl.BlockSpec((d, tn), lambda bi, i, j: (0, jb + j)),
                  pl.BlockSpec((1, HEAD_DIM), lambda bi, i, j: (0, 0)),
                  pl.BlockSpec((tm, HEAD_DIM), lambda bi, i, j: (i, 0)),
                  pl.BlockSpec((tm, HEAD_DIM), lambda bi, i, j: (i, 0))],
        out_specs=pl.BlockSpec((1, hpt, tm, HEAD_DIM), lambda bi, i, j: (bi, j, i, 0)),
        out_shape=jax.ShapeDtypeStruct((b, N_Q_HEADS, l, HEAD_DIM), BF16),
        compiler_params=_params("parallel", "parallel", "arbitrary"),
        name="q_proj",
    )(h, w, g, cos_t, sin_t)


def _kv_proj_kernel(h_ref, w_ref, g_ref, *rest, rope):
    if rope:
        cos_ref, sin_ref, k_ref, v_ref = rest
        cos, sin = cos_ref[...], sin_ref[...]
    else:
        k_ref, v_ref = rest
        cos = sin = None
    acc = jnp.dot(h_ref[0], w_ref[...], preferred_element_type=F32)
    for hh in range(N_KV_HEADS):
        a = acc[:, hh * HEAD_DIM:(hh + 1) * HEAD_DIM]
        k_ref[0, :, hh * HEAD_DIM:(hh + 1) * HEAD_DIM] = _head_norm_rope(a, g_ref[...], cos, sin).astype(k_ref.dtype)
    v_ref[0] = acc[:, KV_WIDTH:].astype(v_ref.dtype)


def _kv_proj(h, w, g, cos_t, sin_t, col_off, tm):
    b, l, d = h.shape
    tn = 2 * KV_WIDTH
    jb = col_off // tn
    rope = cos_t is not None
    in_specs = [pl.BlockSpec((1, tm, d), lambda bi, i: (bi, i, 0)),
                pl.BlockSpec((d, tn), lambda bi, i: (0, jb)),
                pl.BlockSpec((1, HEAD_DIM), lambda bi, i: (0, 0))]
    args = [h, w, g]
    if rope:
        in_specs += [pl.BlockSpec((tm, HEAD_DIM), lambda bi, i: (i, 0))] * 2
        args += [cos_t, sin_t]
    return pl.pallas_call(
        functools.partial(_kv_proj_kernel, rope=rope),
        grid=(b, l // tm),
        in_specs=in_specs,
        out_specs=[pl.BlockSpec((1, tm, KV_WIDTH), lambda bi, i: (bi, i, 0))] * 2,
        out_shape=[jax.ShapeDtypeStruct((b, l, KV_WIDTH), BF16)] * 2,
        compiler_params=_params("parallel", "parallel"),
        name="kv_proj_rope" if rope else "kv_proj_ctx",
    )(*args)


def _glu_proj_kernel(h_ref, wa_ref, wg_ref, o_ref):
    a = jnp.dot(h_ref[...], wa_ref[...], preferred_element_type=F32)
    gt = jnp.dot(h_ref[...], wg_ref[...], preferred_element_type=F32)
    o_ref[...] = (a * _sigmoid(gt)).astype(o_ref.dtype)


def _glu_proj(h2d, w, col_off, width, tm, tn=512):
    m, d = h2d.shape
    ja = col_off // tn
    jg = (col_off + width) // tn
    return pl.pallas_call(
        _glu_proj_kernel,
        grid=(m // tm, width // tn),
        in_specs=[pl.BlockSpec((tm, d), lambda i, j: (i, 0)),
                  pl.BlockSpec((d, tn), lambda i, j: (0, ja + j)),
                  pl.BlockSpec((d, tn), lambda i, j: (0, jg + j))],
        out_specs=pl.BlockSpec((tm, tn), lambda i, j: (i, j)),
        out_shape=jax.ShapeDtypeStruct((m, width), BF16),
        compiler_params=_params("parallel", "arbitrary"),
        name="glu_proj",
    )(h2d, w, w)


def _attn_kernel(q_ref, k_ref, v_ref, kc_ref, vc_ref, o_ref,
                 s0_ref, s1_ref, p0_ref, p1_ref, m_ref, al_ref, acc_ref, *, tk, rb):
    g, tq, dh = q_ref.shape[1:]
    rows = g * tq
    q = q_ref[0].reshape(rows, dh)
    chunks = [(k_ref, v_ref, c * tk, tk) for c in range(k_ref.shape[1] // tk)]
    chunks.append((kc_ref, vc_ref, 0, kc_ref.shape[1]))
    s_refs = (s0_ref, s1_ref)
    p_refs = (p0_ref, p1_ref)

    def scores(j):
        kr, _, st, n = chunks[j]
        s_refs[j % 2][:, :n] = lax.dot_general(q, kr[0, st:st + n, :], (((1,), (1,)), ((), ())),
                                               preferred_element_type=F32)

    def softmax(j):
        n = chunks[j][3]
        s_ref, p_ref = s_refs[j % 2], p_refs[j % 2]
        for r0 in range(0, rows, rb):
            sb = s_ref[r0:r0 + rb, :n]
            mn = jnp.max(sb, axis=-1, keepdims=True)
            if j > 0:
                mo = m_ref[r0:r0 + rb, :]
                mn = jnp.maximum(mo, mn)
                al_ref[r0:r0 + rb, :] = jnp.exp2(mo - mn)
            m_ref[r0:r0 + rb, :] = mn
            p_ref[r0:r0 + rb, :n] = jnp.exp2(sb - mn).astype(BF16)

    def weighted_values(j):
        _, vr, st, n = chunks[j]
        ones_col = (lax.broadcasted_iota(jnp.int32, (n, dh), 1) == 0).astype(BF16)
        v1 = jnp.concatenate([vr[0, st:st + n, :], ones_col], axis=1)
        upd = jnp.dot(p_refs[j % 2][:, :n], v1, preferred_element_type=F32)
        if j == 0:
            acc_ref[...] = upd
        else:
            acc_ref[...] = al_ref[...] * acc_ref[...] + upd

    scores(0)
    for j in range(len(chunks)):
        if j + 1 < len(chunks):
            scores(j + 1)
        softmax(j)
        weighted_values(j)

    acc = acc_ref[...]
    o = acc[:, :dh] / acc[:, dh:dh + 1]
    for gi in range(g):
        o_ref[0, :, gi * dh:(gi + 1) * dh] = o[gi * tq:(gi + 1) * tq].astype(o_ref.dtype)


def _attention(q, k, v, kc, vc, tq, tk, rb=16):
    b, _, l, dh = q.shape
    lc = kc.shape[1]
    rows = Q_PER_KV * tq
    gdh = Q_PER_KV * dh
    return pl.pallas_call(
        functools.partial(_attn_kernel, tk=tk, rb=rb),
        grid=(b, N_KV_HEADS, l // tq),
        in_specs=[pl.BlockSpec((1, Q_PER_KV, tq, dh), lambda bi, kh, qi: (bi, kh, qi, 0)),
                  pl.BlockSpec((1, l, dh), lambda bi, kh, qi: (bi, 0, kh)),
                  pl.BlockSpec((1, l, dh), lambda bi, kh, qi: (bi, 0, kh)),
                  pl.BlockSpec((1, lc, dh), lambda bi, kh, qi: (bi, 0, kh)),
                  pl.BlockSpec((1, lc, dh), lambda bi, kh, qi: (bi, 0, kh))],
        out_specs=pl.BlockSpec((1, tq, gdh), lambda bi, kh, qi: (bi, qi, kh)),
        out_shape=jax.ShapeDtypeStruct((b, l, ATTN_WIDTH), BF16),
        scratch_shapes=[pltpu.VMEM((rows, tk), F32), pltpu.VMEM((rows, tk), F32),
                        pltpu.VMEM((rows, tk), BF16), pltpu.VMEM((rows, tk), BF16),
                        pltpu.VMEM((rows, 1), F32), pltpu.VMEM((rows, 1), F32),
                        pltpu.VMEM((rows, 2 * dh), F32)],
        compiler_params=_params("parallel", "parallel", "arbitrary"),
        name="attention",
    )(q, k, v, kc, vc)


def _conv_kernel(prev_ref, cur_ref, next_ref, w_ref, b_ref, g_ref, beta_ref, o_ref, win_ref, y_ref, *, tc):
    li = pl.program_id(1)
    tl, c = cur_ref.shape[1:]
    halo = prev_ref.shape[1]
    nt = c // LANES

    def put_tokens(vals, tok0):
        for j in range(nt):
            win_ref[pl.ds(tok0 * nt + j, vals.shape[0], stride=nt), :] = vals[:, j * LANES:(j + 1) * LANES]

    put_tokens(jnp.where(li > 0, prev_ref[0].astype(F32), 0.0), 0)
    put_tokens(cur_ref[0].astype(F32), halo)
    put_tokens(jnp.where(li < pl.num_programs(1) - 1, next_ref[0].astype(F32), 0.0), halo + tl)

    first = halo - CONV_TAPS // 2
    bias = b_ref[...][None]

    def token_chunk(ci, carry):
        tok = ci * tc
        acc = jnp.zeros((tc, nt, LANES), F32) + bias
        for t in range(CONV_TAPS):
            r0 = pl.multiple_of((tok + first + t) * nt, nt)
            acc = acc + win_ref[pl.ds(r0, tc * nt), :].reshape(tc, nt, LANES) * w_ref[t][None]
        y_ref[pl.ds(pl.multiple_of(tok * nt, nt), tc * nt), :] = acc.reshape(tc * nt, LANES)
        return carry

    lax.fori_loop(0, tl // tc, token_chunk, 0)
    y = jnp.concatenate([y_ref[pl.ds(j, tl, stride=nt), :] for j in range(nt)], axis=1)
    mu = jnp.mean(y, axis=-1, keepdims=True)
    yc = y - mu
    var = jnp.mean(yc * yc, axis=-1, keepdims=True)
    z = yc * lax.rsqrt(var + EPS) * g_ref[...] + beta_ref[...]
    o_ref[0] = _silu(z).astype(o_ref.dtype)


def _conv_module(u, w_dw, b_dw, ln_g, ln_b, tl, tc=16):
    b, l, c = u.shape
    nt = c // LANES
    hb = tl // CONV_HALO
    n_halo = l // CONV_HALO
    return pl.pallas_call(
        functools.partial(_conv_kernel, tc=tc),
        grid=(b, l // tl),
        in_specs=[pl.BlockSpec((1, CONV_HALO, c), lambda bi, li: (bi, jnp.maximum(li * hb - 1, 0), 0)),
                  pl.BlockSpec((1, tl, c), lambda bi, li: (bi, li, 0)),
                  pl.BlockSpec((1, CONV_HALO, c), lambda bi, li: (bi, jnp.minimum((li + 1) * hb, n_halo - 1), 0)),
                  pl.BlockSpec((CONV_TAPS, nt, LANES), lambda bi, li: (0, 0, 0)),
                  pl.BlockSpec((nt, LANES), lambda bi, li: (0, 0)),
                  pl.BlockSpec((1, c), lambda bi, li: (0, 0)),
                  pl.BlockSpec((1, c), lambda bi, li: (0, 0))],
        out_specs=pl.BlockSpec((1, tl, c), lambda bi, li: (bi, li, 0)),
        out_shape=jax.ShapeDtypeStruct((b, l, c), BF16),
        scratch_shapes=[pltpu.VMEM(((tl + 2 * CONV_HALO) * nt, LANES), F32), pltpu.VMEM((tl * nt, LANES), F32)],
        compiler_params=_params("parallel", "arbitrary"),
        name="conv_module",
    )(u, u, u, w_dw, b_dw, ln_g, ln_b)


def _merge_kernel(h_ref, a_ref, c_ref, wga_ref, wgc_ref, wa_ref, wc_ref, o_ref):
    h = h_ref[...]
    g_a = _sigmoid(jnp.dot(h, wga_ref[...], preferred_element_type=F32))
    g_c = _sigmoid(jnp.dot(h, wgc_ref[...], preferred_element_type=F32))
    a = jnp.dot(a_ref[...], wa_ref[...], preferred_element_type=F32)
    cb = jnp.dot(c_ref[...], wc_ref[...], preferred_element_type=F32)
    o_ref[...] = (g_a * a + g_c * cb).astype(o_ref.dtype)


def _merge(h2d, attn, conv, w_in, gate_off, wa, wc, tm=512, tn=512):
    m, d = h2d.shape
    ka = attn.shape[1]
    kc = conv.shape[1]
    nj = d // tn
    ja = gate_off // tn
    jc = (gate_off + d) // tn
    return pl.pallas_call(
        _merge_kernel,
        grid=(nj, m // tm),
        in_specs=[pl.BlockSpec((tm, d), lambda j, i: (i, 0)),
                  pl.BlockSpec((tm, ka), lambda j, i: (i, 0)),
                  pl.BlockSpec((tm, kc), lambda j, i: (i, 0)),
                  pl.BlockSpec((d, tn), lambda j, i: (0, ja + j)),
                  pl.BlockSpec((d, tn), lambda j, i: (0, jc + j)),
                  pl.BlockSpec((ka, tn), lambda j, i: (0, j)),
                  pl.BlockSpec((kc, tn), lambda j, i: (0, j))],
        out_specs=pl.BlockSpec((tm, tn), lambda j, i: (i, j)),
        out_shape=jax.ShapeDtypeStruct((m, d), BF16),
        compiler_params=_params("parallel", "arbitrary"),
        name="merge_branches",
    )(h2d, attn, conv, w_in, w_in, wa, wc)


def _out_proj_kernel(m_ref, w_ref, x_ref, ga_ref, o_ref):
    acc = jnp.dot(m_ref[0], w_ref[...], preferred_element_type=F32)
    o_ref[0] = x_ref[0] + ga_ref[0] * acc


def _out_proj(mrg, w, x, gate, tm=1024, tn=1024):
    b, l, d = x.shape
    return pl.pallas_call(
        _out_proj_kernel,
        grid=(b, l // tm, d // tn),
        in_specs=[pl.BlockSpec((1, tm, d), lambda bi, i, j: (bi, i, 0)),
                  pl.BlockSpec((d, tn), lambda bi, i, j: (0, j)),
                  pl.BlockSpec((1, tm, tn), lambda bi, i, j: (bi, i, j)),
                  pl.BlockSpec((1, 1, tn), lambda bi, i, j: (bi, 0, j))],
        out_specs=pl.BlockSpec((1, tm, tn), lambda bi, i, j: (bi, i, j)),
        out_shape=jax.ShapeDtypeStruct((b, l, d), F32),
        compiler_params=_params("parallel", "parallel", "arbitrary"),
        name="out_proj_residual",
    )(mrg, w, x, gate)


def _pack_halves(y):
    n = y.shape[1] // 2
    return pltpu.pack_elementwise([y[:, :n], y[:, n:]], packed_dtype=BF16)


def _unpack_halves(p):
    lo = pltpu.unpack_elementwise(p, index=0, packed_dtype=BF16, unpacked_dtype=F32)
    hi = pltpu.unpack_elementwise(p, index=1, packed_dtype=BF16, unpacked_dtype=F32)
    return lo, hi


def _norm2_router_kernel(x_ref, g_ref, sh_ref, sc_ref, wr_ref, br_ref, hp_ref, lg_ref):
    y = _rms(x_ref[0], g_ref[...]) * (1.0 + sc_ref[0]) + sh_ref[0]
    packed = _pack_halves(y)
    tl = packed.shape[0]
    nt = packed.shape[1] // LANES
    for j in range(nt):
        hp_ref[pl.ds(j, tl, stride=nt), :] = packed[:, j * LANES:(j + 1) * LANES]
    lg_ref[0] = jnp.dot(y.astype(BF16), wr_ref[...], preferred_element_type=F32) + br_ref[...]


def _norm2_router(x, g, shift, scale, w_r, b_r, tl):
    b, l, d = x.shape
    nt = d // 2 // LANES
    lb = l // tl
    return pl.pallas_call(
        _norm2_router_kernel,
        grid=(b, lb),
        in_specs=[pl.BlockSpec((1, tl, d), lambda bi, li: (bi, li, 0)),
                  pl.BlockSpec((1, d), lambda bi, li: (0, 0)),
                  pl.BlockSpec((1, 1, d), lambda bi, li: (bi, 0, 0)),
                  pl.BlockSpec((1, 1, d), lambda bi, li: (bi, 0, 0)),
                  pl.BlockSpec((d, ROUTER_LANES), lambda bi, li: (0, 0)),
                  pl.BlockSpec((1, ROUTER_LANES), lambda bi, li: (0, 0))],
        out_specs=[pl.BlockSpec((tl * nt, LANES), lambda bi, li: (bi * lb + li, 0)),
                   pl.BlockSpec((1, tl, ROUTER_LANES), lambda bi, li: (bi, li, 0))],
        out_shape=[jax.ShapeDtypeStruct((b * l * nt, LANES), jnp.int32),
                   jax.ShapeDtypeStruct((b, l, ROUTER_LANES), F32)],
        compiler_params=_params("parallel", "parallel"),
        name="norm2_router",
    )(x, g, shift, scale, w_r, b_r)


def _first_lane(mask, lane):
    return jnp.min(jnp.where(mask, lane, LANES), axis=-1, keepdims=True)


def _route_kernel(lg_ref, meta_ref, wts_ref, cnt_ref, carry_ref):
    @pl.when(pl.program_id(0) == 0)
    def _():
        carry_ref[...] = jnp.zeros_like(carry_ref)

    lg = lg_ref[...]
    tb = lg.shape[0]
    lane = lax.broadcasted_iota(jnp.int32, lg.shape, 1)
    neg_inf = jnp.float32(-jnp.inf)
    is_group = lane < N_GROUPS
    gl = jnp.where(is_group, lg, neg_inf)
    g_max = jnp.max(gl, axis=-1, keepdims=True)
    g_sel = _first_lane(gl == g_max, lane)
    p_g = 1.0 / jnp.sum(jnp.where(is_group, jnp.exp(lg - g_max), 0.0), axis=-1, keepdims=True)

    e_idx = lane - N_GROUPS
    in_group = (e_idx >= g_sel * EXPERTS_PER_GROUP) & (e_idx < (g_sel + 1) * EXPERTS_PER_GROUP)
    ev = jnp.where(in_group, lg, neg_inf)
    v1 = jnp.max(ev, axis=-1, keepdims=True)
    i1 = _first_lane(ev == v1, lane)
    ev2 = jnp.where(lane == i1, neg_inf, ev)
    v2 = jnp.max(ev2, axis=-1, keepdims=True)
    i2 = _first_lane(ev2 == v2, lane)
    t = jnp.exp(v2 - v1)
    w1 = p_g / (1.0 + t)
    w2 = w1 * t

    oh1 = lane == i1
    oh2 = lane == i2
    oh = (oh1 | oh2).astype(BF16)
    earlier = (lax.broadcasted_iota(jnp.int32, (tb, tb), 0) > lax.broadcasted_iota(jnp.int32, (tb, tb), 1)).astype(BF16)
    before = jnp.dot(earlier, oh, preferred_element_type=F32) + carry_ref[...]
    r1 = jnp.sum(jnp.where(oh1, before, 0.0), axis=-1, keepdims=True).astype(jnp.int32)
    r2 = jnp.sum(jnp.where(oh2, before, 0.0), axis=-1, keepdims=True).astype(jnp.int32)
    carry_ref[...] += jnp.sum(oh.astype(F32), axis=0, keepdims=True)

    meta_ref[...] = jnp.where(lane == 0, i1 - N_GROUPS, jnp.where(lane == 1, i2 - N_GROUPS,
                              jnp.where(lane == 2, r1, jnp.where(lane == 3, r2, 0))))
    wts_ref[...] = jnp.where(lane == 0, w1, jnp.where(lane == 1, w2, 0.0))
    cnt_ref[...] = carry_ref[...]


def _route(logits, tb=512):
    n = logits.shape[0]
    return pl.pallas_call(
        _route_kernel,
        grid=(n // tb,),
        in_specs=[pl.BlockSpec((tb, LANES), lambda i: (i, 0))],
        out_specs=[pl.BlockSpec((tb, LANES), lambda i: (i, 0)),
                   pl.BlockSpec((tb, LANES), lambda i: (i, 0)),
                   pl.BlockSpec((1, LANES), lambda i: (0, 0))],
        out_shape=[jax.ShapeDtypeStruct((n, LANES), jnp.int32),
                   jax.ShapeDtypeStruct((n, LANES), F32),
                   jax.ShapeDtypeStruct((1, LANES), F32)],
        scratch_shapes=[pltpu.VMEM((1, LANES), F32)],
        compiler_params=_params("arbitrary"),
        name="moe_route",
    )(logits)


def _dest_kernel(meta_ref, pst_ref, o_ref):
    meta = meta_ref[...]
    lane = lax.broadcasted_iota(jnp.int32, meta.shape, 1)
    pst = pst_ref[...]

    def row_of(slot):
        e = meta[:, slot:slot + 1]
        start = jnp.sum(jnp.where(lane == e + N_GROUPS, pst, 0), axis=-1, keepdims=True)
        return start + meta[:, TOP_K + slot:TOP_K + slot + 1]

    o_ref[...] = jnp.where(lane == 0, row_of(0), jnp.where(lane == 1, row_of(1), 0))


def _dest_rows(meta, pst, tb=512):
    n = meta.shape[0]
    return pl.pallas_call(
        _dest_kernel,
        grid=(n // tb,),
        in_specs=[pl.BlockSpec((tb, LANES), lambda i: (i, 0)),
                  pl.BlockSpec((1, LANES), lambda i: (0, 0))],
        out_specs=pl.BlockSpec((tb, LANES), lambda i: (i, 0)),
        out_shape=jax.ShapeDtypeStruct((n, LANES), jnp.int32),
        compiler_params=_params("parallel"),
        name="moe_dest_rows",
    )(meta, pst)


def _zero_tail_kernel(lb_ref, o_ref):
    o_ref[...] = jnp.zeros_like(o_ref)


def _zero_tails(last_blk, total, tmb, nt):
    return pl.pallas_call(
        _zero_tail_kernel,
        grid_spec=pltpu.PrefetchScalarGridSpec(
            num_scalar_prefetch=1,
            grid=(last_blk.shape[0],),
            in_specs=[],
            out_specs=pl.BlockSpec((tmb * nt, LANES), lambda e, lb: (lb[e], 0))),
        out_shape=jax.ShapeDtypeStruct((total * nt, LANES), jnp.int32),
        compiler_params=_params("arbitrary"),
        name="moe_zero_tails",
    )(last_blk)


def _dispatch_kernel(dest_ref, hp_ref, xs_in_ref, xs_ref, sem):
    rows = hp_ref.shape[0]

    def slab_copy(r, k):
        return pltpu.make_async_copy(hp_ref.at[r], xs_ref.at[dest_ref[0, 0, r * TOP_K + k]], sem)

    def start(r, c):
        for k in range(TOP_K):
            slab_copy(r, k).start(priority=k % 2)
        return c

    def wait(r, c):
        for k in range(TOP_K):
            slab_copy(r, k).wait()
        return c

    lax.fori_loop(0, rows, start, 0, unroll=DMA_LOOP_UNROLL)
    lax.fori_loop(0, rows, wait, 0, unroll=DMA_LOOP_UNROLL)


def _dispatch(hp3, dest, xs0, rows):
    n, nt, _ = hp3.shape
    steps = n // rows
    return pl.pallas_call(
        _dispatch_kernel,
        grid=(steps,),
        in_specs=[pl.BlockSpec((1, 1, rows * TOP_K), lambda i: (i, 0, 0), memory_space=pltpu.SMEM),
                  pl.BlockSpec((rows, nt, LANES), lambda i: (i, 0, 0)),
                  pl.BlockSpec(memory_space=pl.ANY)],
        out_specs=pl.BlockSpec(memory_space=pl.ANY),
        out_shape=jax.ShapeDtypeStruct(xs0.shape, xs0.dtype),
        scratch_shapes=[pltpu.SemaphoreType.DMA(())],
        input_output_aliases={2: 0},
        compiler_params=_params("arbitrary"),
        name="moe_dispatch",
    )(dest.reshape(steps, 1, rows * TOP_K), hp3, xs0)


def _expert_up_kernel(be_ref, nu_ref, x_ref, wg_ref, wu_ref, o_ref, *, nt):
    @pl.when(pl.program_id(1) < nu_ref[0])
    def _():
        tmb = o_ref.shape[0]
        words = jnp.concatenate([x_ref[pl.ds(j, tmb, stride=nt), :] for j in range(nt)], axis=1)
        lo, hi = _unpack_halves(words)
        lo = lo.astype(BF16)
        hi = hi.astype(BF16)
        half = lo.shape[1]

        def mm(w_ref):
            return (jnp.dot(lo, w_ref[0, :half, :].astype(BF16), preferred_element_type=F32)
                    + jnp.dot(hi, w_ref[0, half:, :].astype(BF16), preferred_element_type=F32))

        o_ref[...] = (_silu(mm(wg_ref)) * mm(wu_ref)).astype(o_ref.dtype)

    @pl.when(pl.program_id(1) >= nu_ref[0])
    def _():
        o_ref[...] = jnp.zeros_like(o_ref)


def _expert_up(xs, w_gate, w_up, blk_e, n_used, tmb, tf=256):
    _, d, ff = w_gate.shape
    nt = d // 2 // LANES
    total = xs.shape[0] // nt
    nblk = total // tmb

    def blk(bi, nu):
        return jnp.minimum(bi, nu[0] - 1)

    return pl.pallas_call(
        functools.partial(_expert_up_kernel, nt=nt),
        grid_spec=pltpu.PrefetchScalarGridSpec(
            num_scalar_prefetch=2,
            grid=(ff // tf, nblk),
            in_specs=[pl.BlockSpec((tmb * nt, LANES), lambda c, bi, be, nu: (blk(bi, nu), 0)),
                      pl.BlockSpec((1, d, tf), lambda c, bi, be, nu: (be[bi], 0, c)),
                      pl.BlockSpec((1, d, tf), lambda c, bi, be, nu: (be[bi], 0, c))],
            out_specs=pl.BlockSpec((tmb, tf), lambda c, bi, be, nu: (bi, c))),
        out_shape=jax.ShapeDtypeStruct((total, ff), BF16),
        compiler_params=_params("arbitrary", "arbitrary"),
        name="expert_up",
    )(blk_e, n_used, xs, w_gate, w_up)


def _expert_down_kernel(be_ref, nu_ref, a_ref, w_ref, o_ref, *, nt, tn):
    @pl.when(pl.program_id(0) < nu_ref[0])
    def _():
        a = a_ref[...]
        tmb = a.shape[0]
        half = nt * LANES
        for c0 in range(0, half, tn):
            ylo = jnp.dot(a, w_ref[0, :, c0:c0 + tn].astype(BF16), preferred_element_type=F32)
            yhi = jnp.dot(a, w_ref[0, :, half + c0:half + c0 + tn].astype(BF16), preferred_element_type=F32)
            packed = pltpu.pack_elementwise([ylo, yhi], packed_dtype=BF16)
            for j in range(tn // LANES):
                o_ref[pl.ds(c0 // LANES + j, tmb, stride=nt), :] = packed[:, j * LANES:(j + 1) * LANES]

    @pl.when(pl.program_id(0) >= nu_ref[0])
    def _():
        o_ref[...] = jnp.zeros_like(o_ref)


def _expert_down(act, w_down, blk_e, n_used, tmb, tn=512):
    total, ff = act.shape
    d = w_down.shape[2]
    nt = d // 2 // LANES
    nblk = total // tmb

    def blk(bi, nu):
        return jnp.minimum(bi, nu[0] - 1)

    return pl.pallas_call(
        functools.partial(_expert_down_kernel, nt=nt, tn=tn),
        grid_spec=pltpu.PrefetchScalarGridSpec(
            num_scalar_prefetch=2,
            grid=(nblk,),
            in_specs=[pl.BlockSpec((tmb, ff), lambda bi, be, nu: (blk(bi, nu), 0)),
                      pl.BlockSpec((1, ff, d), lambda bi, be, nu: (be[bi], 0, 0))],
            out_specs=pl.BlockSpec((tmb * nt, LANES), lambda bi, be, nu: (bi, 0))),
        out_shape=jax.ShapeDtypeStruct((total * nt, LANES), jnp.int32),
        compiler_params=_params("arbitrary"),
        name="expert_down",
    )(blk_e, n_used, act, w_down)


def _combine_kernel(pos_ref, ys_ref, w_ref, x_ref, ga_ref, g_ref, o_ref, s0_ref, s1_ref, sem0, sem1):
    rows = x_ref.shape[1]
    nt = ys_ref.shape[1]
    half = nt * LANES
    slabs = (s0_ref, s1_ref)
    sems = (sem0, sem1)

    def slab_copy(r, k):
        dst = slabs[k].at[pl.ds(pl.multiple_of(r * SLAB_PITCH, SUBLANES), nt)]
        return pltpu.make_async_copy(ys_ref.at[pos_ref[0, 0, r * TOP_K + k]], dst, sems[k])

    def start(r, c):
        for k in range(TOP_K):
            slab_copy(r, k).start(priority=k % 2)
        return c

    def wait(r, c):
        for k in range(TOP_K):
            slab_copy(r, k).wait()
        return c

    lax.fori_loop(0, rows, start, 0, unroll=DMA_LOOP_UNROLL)
    lax.fori_loop(0, rows, wait, 0, unroll=DMA_LOOP_UNROLL)

    w0 = w_ref[:, 0:1]
    w1 = w_ref[:, 1:2]
    ss = jnp.zeros((rows, 1), F32)
    for j in range(nt):
        lo0, hi0 = _unpack_halves(s0_ref[pl.ds(j, rows, stride=SLAB_PITCH), :])
        lo1, hi1 = _unpack_halves(s1_ref[pl.ds(j, rows, stride=SLAB_PITCH), :])
        for c0, y in ((j * LANES, w0 * lo0 + w1 * lo1), (half + j * LANES, w0 * hi0 + w1 * hi1)):
            z = x_ref[0, :, c0:c0 + LANES] + ga_ref[0, :, c0:c0 + LANES] * y
            ss = ss + jnp.sum(z * z, axis=-1, keepdims=True)
            o_ref[0, :, c0:c0 + LANES] = z
    inv = lax.rsqrt(ss / (2 * half) + EPS)
    o_ref[0] = o_ref[0] * inv * g_ref[...]


def _combine(ys3, pos, wts, x, gate, g, rows):
    b, l, d = x.shape
    lb = l // rows
    return pl.pallas_call(
        _combine_kernel,
        grid=(b, lb),
        in_specs=[pl.BlockSpec((1, 1, rows * TOP_K), lambda bi, i: (bi * lb + i, 0, 0), memory_space=pltpu.SMEM),
                  pl.BlockSpec(memory_space=pl.ANY),
                  pl.BlockSpec((rows, LANES), lambda bi, i: (bi * lb + i, 0)),
                  pl.BlockSpec((1, rows, d), lambda bi, i: (bi, i, 0)),
                  pl.BlockSpec((1, 1, d), lambda bi, i: (bi, 0, 0)),
                  pl.BlockSpec((1, d), lambda bi, i: (0, 0))],
        out_specs=pl.BlockSpec((1, rows, d), lambda bi, i: (bi, i, 0)),
        out_shape=jax.ShapeDtypeStruct((b, l, d), F32),
        scratch_shapes=[pltpu.VMEM((rows * SLAB_PITCH, LANES), ys3.dtype),
                        pltpu.VMEM((rows * SLAB_PITCH, LANES), ys3.dtype),
                        pltpu.SemaphoreType.DMA(()), pltpu.SemaphoreType.DMA(())],
        compiler_params=_params("arbitrary", "arbitrary"),
        name="moe_combine_norm",
    )(pos.reshape(b * lb, 1, rows * TOP_K), ys3, wts, x, gate, g)


def _block_layout(counts, n_pairs, tmb):
    nblk = (n_pairs + N_EXPERTS * (tmb - 1) + tmb - 1) // tmb
    blocks = (counts + tmb - 1) // tmb
    bend = jnp.cumsum(blocks)
    pstart = (bend - blocks) * tmb
    n_used = bend[-1]
    blk_ids = jnp.minimum(jnp.arange(nblk, dtype=jnp.int32), n_used - 1)
    blk_e = jnp.minimum(jnp.searchsorted(bend, blk_ids, side="right"), N_EXPERTS - 1).astype(jnp.int32)
    last_blk = jnp.maximum(bend - 1, 0).astype(jnp.int32)
    return nblk, pstart.astype(jnp.int32), blk_e, last_blk, n_used.astype(jnp.int32).reshape(1)


def _rope_tables(n_tokens):
    rows = n_tokens // GRID_W
    row, col = jnp.meshgrid(jnp.arange(rows), jnp.arange(GRID_W), indexing="ij")
    pos = jnp.stack([row.reshape(-1), col.reshape(-1)], axis=-1).astype(F32)
    inv = ROPE_THETA ** (-jnp.arange(0, ROPE_AXIS_DIM, 2, dtype=F32) / ROPE_AXIS_DIM)
    ang = pos[:, :, None] * inv[None, None, :]
    cos, sin = jnp.cos(ang), jnp.sin(ang)
    cos_t = jnp.concatenate([cos[:, 0], cos[:, 0], cos[:, 1], cos[:, 1]], axis=-1)
    sin_t = jnp.concatenate([-sin[:, 0], sin[:, 0], -sin[:, 1], sin[:, 1]], axis=-1)
    return cos_t, sin_t


def kernel(x, c, ctx, c_ctx, norm1_g, w_mod, b_mod, w_in, q_norm_g, k_norm_g, w_attn_out, conv_dw_w, conv_dw_b, conv_ln_g, conv_ln_b, w_conv_out, w_out, norm2_g, w_router_group, b_router_group, w_router_expert, b_router_expert, w_exp_gate, w_exp_up, w_exp_down, norm_f_g):
    b, s, d = x.shape
    n_ctx = ctx.shape[1]
    assert w_in.shape[0] == 1, "single-layer stack"
    conv_width = conv_dw_w.shape[-1]
    k_off = ATTN_WIDTH
    glu_off = k_off + 2 * KV_WIDTH
    gate_off = glu_off + 2 * conv_width

    n_c = b + 1
    cvec = jnp.zeros((SUBLANES * ((n_c + SUBLANES - 1) // SUBLANES), d), F32).at[:b].set(c).at[b].set(c_ctx)
    mod = _mod_vectors(cvec, w_mod[0], b_mod.reshape(1, -1))
    sh1, sc1, ga1, sh2, sc2, ga2 = [mod[:b, i * d:(i + 1) * d].reshape(b, 1, d) for i in range(N_MOD)]
    csh1, csc1 = [mod[b:b + 1, i * d:(i + 1) * d].reshape(1, 1, d) for i in range(2)]

    g1 = norm1_g.reshape(1, d)
    h = _norm_mod(x, g1, sh1, sc1, tl=512)
    hc = _norm_mod(ctx, g1, csh1, csc1, tl=n_ctx)
    w_in_b = w_in[0].astype(BF16)
    cos_t, sin_t = _rope_tables(s)
    qg = q_norm_g.reshape(1, HEAD_DIM)
    kg = k_norm_g.reshape(1, HEAD_DIM)
    q = _q_proj(h, w_in_b, qg, cos_t, sin_t, 0, HEAD_DIM ** -0.5 * LOG2E, tm=1024)
    k, v = _kv_proj(h, w_in_b, kg, cos_t, sin_t, k_off, tm=1024)
    kc, vc = _kv_proj(hc, w_in_b, kg, None, None, k_off, tm=n_ctx)
    attn = _attention(q, k, v, kc, vc, tq=128, tk=1024)

    h2d = h.reshape(b * s, d)
    u = _glu_proj(h2d, w_in_b, glu_off, conv_width, tm=1024)
    conv = _conv_module(u.reshape(b, s, conv_width), conv_dw_w.reshape(CONV_TAPS, conv_width // LANES, LANES),
                        conv_dw_b.reshape(conv_width // LANES, LANES), conv_ln_g.reshape(1, -1),
                        conv_ln_b.reshape(1, -1), tl=256)
    mrg = _merge(h2d, attn.reshape(b * s, ATTN_WIDTH), conv.reshape(b * s, conv_width), w_in_b, gate_off,
                 w_attn_out[0].astype(BF16), w_conv_out[0].astype(BF16))
    x1 = _out_proj(mrg.reshape(b, s, d), w_out[0].astype(BF16), x, ga1)

    w_r = jnp.zeros((d, ROUTER_LANES), F32).at[:, :N_GROUPS].set(w_router_group[0]) \
        .at[:, N_GROUPS:N_GROUPS + N_EXPERTS].set(w_router_expert[0])
    b_r = jnp.zeros((1, ROUTER_LANES), F32).at[0, :N_GROUPS].set(b_router_group[0]) \
        .at[0, N_GROUPS:N_GROUPS + N_EXPERTS].set(b_router_expert[0])
    hp, logits = _norm2_router(x1, norm2_g.reshape(1, d), sh2, sc2, w_r.astype(BF16), b_r, tl=256)
    n = b * s
    nt = d // 2 // LANES
    tmb = 512
    meta, wts, cnt = _route(logits.reshape(n, ROUTER_LANES))
    counts = cnt[0, N_GROUPS:N_GROUPS + N_EXPERTS].astype(jnp.int32)
    nblk, pstart, blk_e, last_blk, n_used = _block_layout(counts, n * TOP_K, tmb)
    pst = jnp.zeros((1, LANES), jnp.int32).at[0, N_GROUPS:N_GROUPS + N_EXPERTS].set(pstart)
    dest = _dest_rows(meta, pst)[:, :TOP_K].reshape(-1)
    xs0 = _zero_tails(last_blk, nblk * tmb, tmb, nt)
    xs = _dispatch(hp.reshape(n, nt, LANES), dest, xs0.reshape(nblk * tmb, nt, LANES), rows=256)
    act = _expert_up(xs.reshape(nblk * tmb * nt, LANES), w_exp_gate[0], w_exp_up[0], blk_e, n_used, tmb)
    ys = _expert_down(act, w_exp_down[0], blk_e, n_used, tmb)
    return _combine(ys.reshape(nblk * tmb, nt, LANES), dest, wts, x1, ga2, norm_f_g.reshape(1, d), rows=256)
```

```python
import functools

import jax
import jax.numpy as jnp
from jax import lax
from jax.experimental import pallas as pl
from jax.experimental.pallas import tpu as pltpu

F32 = jnp.float32
BF16 = jnp.bfloat16

GRID_W = 64
HEAD_DIM = 128
N_Q_HEADS = 16
N_KV_HEADS = 4
Q_PER_KV = N_Q_HEADS // N_KV_HEADS
ATTN_WIDTH = N_Q_HEADS * HEAD_DIM
KV_WIDTH = N_KV_HEADS * HEAD_DIM
CONV_TAPS = 31
CONV_HALO = 16
ROPE_THETA = 10000.0
ROPE_AXIS_DIM = HEAD_DIM // 2
N_GROUPS = 4
EXPERTS_PER_GROUP = 8
N_EXPERTS = N_GROUPS * EXPERTS_PER_GROUP
TOP_K = 2
N_MOD = 6
EPS = 1e-6
LOG2E = 1.4426950408889634
LANES = 128
SUBLANES = 8
ROUTER_LANES = LANES
SLAB_PITCH = 24
DMA_LOOP_UNROLL = 4

V7X_VMEM_LIMIT = 56 * 1024 * 1024


def _params(*sem):
    return pltpu.CompilerParams(dimension_semantics=sem, vmem_limit_bytes=V7X_VMEM_LIMIT)


def _sigmoid(x):
    return 1.0 / (1.0 + jnp.exp(-x))


def _silu(x):
    return x * _sigmoid(x)


def _rms(x, g):
    return x * lax.rsqrt(jnp.mean(x * x, axis=-1, keepdims=True) + EPS) * g


def _mod_kernel(c_ref, w_ref, b_ref, o_ref):
    s = _silu(c_ref[...]).astype(BF16)
    o_ref[...] = jnp.dot(s, w_ref[...].astype(BF16), preferred_element_type=F32) + b_ref[...]


def _mod_vectors(cvec, w_mod, b_mod, tn=512):
    m, d = cvec.shape
    n = w_mod.shape[1]
    return pl.pallas_call(
        _mod_kernel,
        grid=(n // tn,),
        in_specs=[pl.BlockSpec((m, d), lambda j: (0, 0)),
                  pl.BlockSpec((d, tn), lambda j: (0, j)),
                  pl.BlockSpec((1, tn), lambda j: (0, j))],
        out_specs=pl.BlockSpec((m, tn), lambda j: (0, j)),
        out_shape=jax.ShapeDtypeStruct((m, n), F32),
        compiler_params=_params("arbitrary"),
        name="mod_vectors",
    )(cvec, w_mod, b_mod)


def _norm_mod_kernel(x_ref, g_ref, sh_ref, sc_ref, o_ref):
    y = _rms(x_ref[0], g_ref[...])
    o_ref[0] = (y * (1.0 + sc_ref[0]) + sh_ref[0]).astype(o_ref.dtype)


def _norm_mod(x, g, shift, scale, tl):
    b, l, d = x.shape
    per_batch = shift.shape[0] > 1
    mod_map = (lambda bi, li: (bi, 0, 0)) if per_batch else (lambda bi, li: (0, 0, 0))
    return pl.pallas_call(
        _norm_mod_kernel,
        grid=(b, l // tl),
        in_specs=[pl.BlockSpec((1, tl, d), lambda bi, li: (bi, li, 0)),
                  pl.BlockSpec((1, d), lambda bi, li: (0, 0)),
                  pl.BlockSpec((1, 1, d), mod_map),
                  pl.BlockSpec((1, 1, d), mod_map)],
        out_specs=pl.BlockSpec((1, tl, d), lambda bi, li: (bi, li, 0)),
        out_shape=jax.ShapeDtypeStruct((b, l, d), BF16),
        compiler_params=_params("parallel", "parallel"),
        name="norm_modulate",
    )(x, g, shift, scale)


def _head_norm_rope(a, g, cos, sin):
    y = _rms(a, g)
    if cos is None:
        return y
    lane = lax.broadcasted_iota(jnp.int32, y.shape, 1)
    quarter = ROPE_AXIS_DIM // 2
    partner = jnp.where((lane % ROPE_AXIS_DIM) < quarter,
                        pltpu.roll(y, HEAD_DIM - quarter, 1), pltpu.roll(y, quarter, 1))
    return y * cos + partner * sin


def _q_proj_kernel(h_ref, w_ref, g_ref, cos_ref, sin_ref, o_ref, *, scale):
    acc = jnp.dot(h_ref[0], w_ref[...], preferred_element_type=F32)
    for hh in range(o_ref.shape[1]):
        a = acc[:, hh * HEAD_DIM:(hh + 1) * HEAD_DIM]
        y = _head_norm_rope(a, g_ref[...], cos_ref[...], sin_ref[...])
        o_ref[0, hh] = (y * scale).astype(o_ref.dtype)


def _q_proj(h, w, g, cos_t, sin_t, col_off, scale, tm, tn=1024):
    b, l, d = h.shape
    jb = col_off // tn
    hpt = tn // HEAD_DIM
    return pl.pallas_call(
        functools.partial(_q_proj_kernel, scale=scale),
        grid=(b, l // tm, ATTN_WIDTH // tn),
        in_specs=[pl.BlockSpec((1, tm, d), lambda bi, i, j: (bi, i, 0)),
                  pl.BlockSpec((d, tn), lambda bi, i, j: (0, jb + j)),
                  pl.BlockSpec((1, HEAD_DIM), lambda bi, i, j: (0, 0)),
                  pl.BlockSpec((tm, HEAD_DIM), lambda bi, i, j: (i, 0)),
                  pl.BlockSpec((tm, HEAD_DIM), lambda bi, i, j: (i, 0))],
        out_specs=pl.BlockSpec((1, hpt, tm, HEAD_DIM), lambda bi, i, j: (bi, j, i, 0)),
        out_shape=jax.ShapeDtypeStruct((b, N_Q_HEADS, l, HEAD_DIM), BF16),
        compiler_params=_params("parallel", "parallel", "arbitrary"),
        name="q_proj",
    )(h, w, g, cos_t, sin_t)


def _kv_proj_kernel(h_ref, w_ref, g_ref, *rest, rope):
    if rope:
        cos_ref, sin_ref, k_ref, v_ref = rest
        cos, sin = cos_ref[...], sin_ref[...]
    else:
        k_ref, v_ref = rest
        cos = sin = None
    acc = jnp.dot(h_ref[0], w_ref[...], preferred_element_type=F32)
    for hh in range(N_KV_HEADS):
        a = acc[:, hh * HEAD_DIM:(hh + 1) * HEAD_DIM]
        k_ref[0, :, hh * HEAD_DIM:(hh + 1) * HEAD_DIM] = _head_norm_rope(a, g_ref[...], cos, sin).astype(k_ref.dtype)
    v_ref[0] = acc[:, KV_WIDTH:].astype(v_ref.dtype)


def _kv_proj(h, w, g, cos_t, sin_t, col_off, tm):
    b, l, d = h.shape
    tn = 2 * KV_WIDTH
    jb = col_off // tn
    rope = cos_t is not None
    in_specs = [pl.BlockSpec((1, tm, d), lambda bi, i: (bi, i, 0)),
                pl.BlockSpec((d, tn), lambda bi, i: (0, jb)),
                pl.BlockSpec((1, HEAD_DIM), lambda bi, i: (0, 0))]
    args = [h, w, g]
    if rope:
        in_specs += [pl.BlockSpec((tm, HEAD_DIM), lambda bi, i: (i, 0))] * 2
        args += [cos_t, sin_t]
    return pl.pallas_call(
        functools.partial(_kv_proj_kernel, rope=rope),
        grid=(b, l // tm),
        in_specs=in_specs,
        out_specs=[pl.BlockSpec((1, tm, KV_WIDTH), lambda bi, i: (bi, i, 0))] * 2,
        out_shape=[jax.ShapeDtypeStruct((b, l, KV_WIDTH), BF16)] * 2,
        compiler_params=_params("parallel", "parallel"),
        name="kv_proj_rope" if rope else "kv_proj_ctx",
    )(*args)


def _glu_proj_kernel(h_ref, wa_ref, wg_ref, o_ref):
    a = jnp.dot(h_ref[...], wa_ref[...], preferred_element_type=F32)
    gt = jnp.dot(h_ref[...], wg_ref[...], preferred_element_type=F32)
    o_ref[...] = (a * _sigmoid(gt)).astype(o_ref.dtype)


def _glu_proj(h2d, w, col_off, width, tm, tn=512):
    m, d = h2d.shape
    ja = col_off // tn
    jg = (col_off + width) // tn
    return pl.pallas_call(
        _glu_proj_kernel,
        grid=(m // tm, width // tn),
        in_specs=[pl.BlockSpec((tm, d), lambda i, j: (i, 0)),
                  pl.BlockSpec((d, tn), lambda i, j: (0, ja + j)),
                  pl.BlockSpec((d, tn), lambda i, j: (0, jg + j))],
        out_specs=pl.BlockSpec((tm, tn), lambda i, j: (i, j)),
        out_shape=jax.ShapeDtypeStruct((m, width), BF16),
        compiler_params=_params("parallel", "arbitrary"),
        name="glu_proj",
    )(h2d, w, w)


def _attn_kernel(q_ref, k_ref, v_ref, kc_ref, vc_ref, o_ref,
                 s0_ref, s1_ref, p0_ref, p1_ref, m_ref, al_ref, acc_ref, *, tk, rb):
    g, tq, dh = q_ref.shape[1:]
    rows = g * tq
    q = q_ref[0].reshape(rows, dh)
    chunks = [(k_ref, v_ref, c * tk, tk) for c in range(k_ref.shape[1] // tk)]
    chunks.append((kc_ref, vc_ref, 0, kc_ref.shape[1]))
    s_refs = (s0_ref, s1_ref)
    p_refs = (p0_ref, p1_ref)

    def scores(j):
        kr, _, st, n = chunks[j]
        s_refs[j % 2][:, :n] = lax.dot_general(q, kr[0, st:st + n, :], (((1,), (1,)), ((), ())),
                                               preferred_element_type=F32)

    def softmax(j):
        n = chunks[j][3]
        s_ref, p_ref = s_refs[j % 2], p_refs[j % 2]
        for r0 in range(0, rows, rb):
            sb = s_ref[r0:r0 + rb, :n]
            mn = jnp.max(sb, axis=-1, keepdims=True)
            if j > 0:
                mo = m_ref[r0:r0 + rb, :]
                mn = jnp.maximum(mo, mn)
                al_ref[r0:r0 + rb, :] = jnp.exp2(mo - mn)
            m_ref[r0:r0 + rb, :] = mn
            p_ref[r0:r0 + rb, :n] = jnp.exp2(sb - mn).astype(BF16)

    def weighted_values(j):
        _, vr, st, n = chunks[j]
        ones_col = (lax.broadcasted_iota(jnp.int32, (n, dh), 1) == 0).astype(BF16)
        v1 = jnp.concatenate([vr[0, st:st + n, :], ones_col], axis=1)
        upd = jnp.dot(p_refs[j % 2][:, :n], v1, preferred_element_type=F32)
        if j == 0:
            acc_ref[...] = upd
        else:
            acc_ref[...] = al_ref[...] * acc_ref[...] + upd

    scores(0)
    for j in range(len(chunks)):
        if j + 1 < len(chunks):
            scores(j + 1)
        softmax(j)
        weighted_values(j)

    acc = acc_ref[...]
    o = acc[:, :dh] / acc[:, dh:dh + 1]
    for gi in range(g):
        o_ref[0, :, gi * dh:(gi + 1) * dh] = o[gi * tq:(gi + 1) * tq].astype(o_ref.dtype)


def _attention(q, k, v, kc, vc, tq, tk, rb=16):
    b, _, l, dh = q.shape
    lc = kc.shape[1]
    rows = Q_PER_KV * tq
    gdh = Q_PER_KV * dh
    return pl.pallas_call(
        functools.partial(_attn_kernel, tk=tk, rb=rb),
        grid=(b, N_KV_HEADS, l // tq),
        in_specs=[pl.BlockSpec((1, Q_PER_KV, tq, dh), lambda bi, kh, qi: (bi, kh, qi, 0)),
                  pl.BlockSpec((1, l, dh), lambda bi, kh, qi: (bi, 0, kh)),
                  pl.BlockSpec((1, l, dh), lambda bi, kh, qi: (bi, 0, kh)),
                  pl.BlockSpec((1, lc, dh), lambda bi, kh, qi: (bi, 0, kh)),
                  pl.BlockSpec((1, lc, dh), lambda bi, kh, qi: (bi, 0, kh))],
        out_specs=pl.BlockSpec((1, tq, gdh), lambda bi, kh, qi: (bi, qi, kh)),
        out_shape=jax.ShapeDtypeStruct((b, l, ATTN_WIDTH), BF16),
        scratch_shapes=[pltpu.VMEM((rows, tk), F32), pltpu.VMEM((rows, tk), F32),
                        pltpu.VMEM((rows, tk), BF16), pltpu.VMEM((rows, tk), BF16),
                        pltpu.VMEM((rows, 1), F32), pltpu.VMEM((rows, 1), F32),
                        pltpu.VMEM((rows, 2 * dh), F32)],
        compiler_params=_params("parallel", "parallel", "arbitrary"),
        name="attention",
    )(q, k, v, kc, vc)


def _conv_kernel(prev_ref, cur_ref, next_ref, w_ref, b_ref, g_ref, beta_ref, o_ref, win_ref, y_ref, *, tc):
    li = pl.program_id(1)
    tl, c = cur_ref.shape[1:]
    halo = prev_ref.shape[1]
    nt = c // LANES

    def put_tokens(vals, tok0):
        for j in range(nt):
            win_ref[pl.ds(tok0 * nt + j, vals.shape[0], stride=nt), :] = vals[:, j * LANES:(j + 1) * LANES]

    put_tokens(jnp.where(li > 0, prev_ref[0].astype(F32), 0.0), 0)
    put_tokens(cur_ref[0].astype(F32), halo)
    put_tokens(jnp.where(li < pl.num_programs(1) - 1, next_ref[0].astype(F32), 0.0), halo + tl)

    first = halo - CONV_TAPS // 2
    bias = b_ref[...][None]

    def token_chunk(ci, carry):
        tok = ci * tc
        acc = jnp.zeros((tc, nt, LANES), F32) + bias
        for t in range(CONV_TAPS):
            r0 = pl.multiple_of((tok + first + t) * nt, nt)
            acc = acc + win_ref[pl.ds(r0, tc * nt), :].reshape(tc, nt, LANES) * w_ref[t][None]
        y_ref[pl.ds(pl.multiple_of(tok * nt, nt), tc * nt), :] = acc.reshape(tc * nt, LANES)
        return carry

    lax.fori_loop(0, tl // tc, token_chunk, 0)
    y = jnp.concatenate([y_ref[pl.ds(j, tl, stride=nt), :] for j in range(nt)], axis=1)
    mu = jnp.mean(y, axis=-1, keepdims=True)
    yc = y - mu
    var = jnp.mean(yc * yc, axis=-1, keepdims=True)
    z = yc * lax.rsqrt(var + EPS) * g_ref[...] + beta_ref[...]
    o_ref[0] = _silu(z).astype(o_ref.dtype)


def _conv_module(u, w_dw, b_dw, ln_g, ln_b, tl, tc=16):
    b, l, c = u.shape
    nt = c // LANES
    hb = tl // CONV_HALO
    n_halo = l // CONV_HALO
    return pl.pallas_call(
        functools.partial(_conv_kernel, tc=tc),
        grid=(b, l // tl),
        in_specs=[pl.BlockSpec((1, CONV_HALO, c), lambda bi, li: (bi, jnp.maximum(li * hb - 1, 0), 0)),
                  pl.BlockSpec((1, tl, c), lambda bi, li: (bi, li, 0)),
                  pl.BlockSpec((1, CONV_HALO, c), lambda bi, li: (bi, jnp.minimum((li + 1) * hb, n_halo - 1), 0)),
                  pl.BlockSpec((CONV_TAPS, nt, LANES), lambda bi, li: (0, 0, 0)),
                  pl.BlockSpec((nt, LANES), lambda bi, li: (0, 0)),
                  pl.BlockSpec((1, c), lambda bi, li: (0, 0)),
                  pl.BlockSpec((1, c), lambda bi, li: (0, 0))],
        out_specs=pl.BlockSpec((1, tl, c), lambda bi, li: (bi, li, 0)),
        out_shape=jax.ShapeDtypeStruct((b, l, c), BF16),
        scratch_shapes=[pltpu.VMEM(((tl + 2 * CONV_HALO) * nt, LANES), F32), pltpu.VMEM((tl * nt, LANES), F32)],
        compiler_params=_params("parallel", "arbitrary"),
        name="conv_module",
    )(u, u, u, w_dw, b_dw, ln_g, ln_b)


def _merge_kernel(h_ref, a_ref, c_ref, wga_ref, wgc_ref, wa_ref, wc_ref, o_ref):
    h = h_ref[...]
    g_a = _sigmoid(jnp.dot(h, wga_ref[...], preferred_element_type=F32))
    g_c = _sigmoid(jnp.dot(h, wgc_ref[...], preferred_element_type=F32))
    a = jnp.dot(a_ref[...], wa_ref[...], preferred_element_type=F32)
    cb = jnp.dot(c_ref[...], wc_ref[...], preferred_element_type=F32)
    o_ref[...] = (g_a * a + g_c * cb).astype(o_ref.dtype)


def _merge(h2d, attn, conv, w_in, gate_off, wa, wc, tm=512, tn=512):
    m, d = h2d.shape
    ka = attn.shape[1]
    kc = conv.shape[1]
    nj = d // tn
    ja = gate_off // tn
    jc = (gate_off + d) // tn
    return pl.pallas_call(
        _merge_kernel,
        grid=(nj, m // tm),
        in_specs=[pl.BlockSpec((tm, d), lambda j, i: (i, 0)),
                  pl.BlockSpec((tm, ka), lambda j, i: (i, 0)),
                  pl.BlockSpec((tm, kc), lambda j, i: (i, 0)),
                  pl.BlockSpec((d, tn), lambda j, i: (0, ja + j)),
                  pl.BlockSpec((d, tn), lambda j, i: (0, jc + j)),
                  pl.BlockSpec((ka, tn), lambda j, i: (0, j)),
                  pl.BlockSpec((kc, tn), lambda j, i: (0, j))],
        out_specs=pl.BlockSpec((tm, tn), lambda j, i: (i, j)),
        out_shape=jax.ShapeDtypeStruct((m, d), BF16),
        compiler_params=_params("parallel", "arbitrary"),
        name="merge_branches",
    )(h2d, attn, conv, w_in, w_in, wa, wc)


def _out_proj_kernel(m_ref, w_ref, x_ref, ga_ref, o_ref):
    acc = jnp.dot(m_ref[0], w_ref[...], preferred_element_type=F32)
    o_ref[0] = x_ref[0] + ga_ref[0] * acc


def _out_proj(mrg, w, x, gate, tm=1024, tn=1024):
    b, l, d = x.shape
    return pl.pallas_call(
        _out_proj_kernel,
        grid=(b, l // tm, d // tn),
        in_specs=[pl.BlockSpec((1, tm, d), lambda bi, i, j: (bi, i, 0)),
                  pl.BlockSpec((d, tn), lambda bi, i, j: (0, j)),
                  pl.BlockSpec((1, tm, tn), lambda bi, i, j: (bi, i, j)),
                  pl.BlockSpec((1, 1, tn), lambda bi, i, j: (bi, 0, j))],
        out_specs=pl.BlockSpec((1, tm, tn), lambda bi, i, j: (bi, i, j)),
        out_shape=jax.ShapeDtypeStruct((b, l, d), F32),
        compiler_params=_params("parallel", "parallel", "arbitrary"),
        name="out_proj_residual",
    )(mrg, w, x, gate)


def _pack_halves(y):
    n = y.shape[1] // 2
    return pltpu.pack_elementwise([y[:, :n], y[:, n:]], packed_dtype=BF16)


def _unpack_halves(p):
    lo = pltpu.unpack_elementwise(p, index=0, packed_dtype=BF16, unpacked_dtype=F32)
    hi = pltpu.unpack_elementwise(p, index=1, packed_dtype=BF16, unpacked_dtype=F32)
    return lo, hi


def _norm2_router_kernel(x_ref, g_ref, sh_ref, sc_ref, wr_ref, br_ref, hp_ref, lg_ref):
    y = _rms(x_ref[0], g_ref[...]) * (1.0 + sc_ref[0]) + sh_ref[0]
    packed = _pack_halves(y)
    tl = packed.shape[0]
    nt = packed.shape[1] // LANES
    for j in range(nt):
        hp_ref[pl.ds(j, tl, stride=nt), :] = packed[:, j * LANES:(j + 1) * LANES]
    lg_ref[0] = jnp.dot(y.astype(BF16), wr_ref[...], preferred_element_type=F32) + br_ref[...]


def _norm2_router(x, g, shift, scale, w_r, b_r, tl):
    b, l, d = x.shape
    nt = d // 2 // LANES
    lb = l // tl
    return pl.pallas_call(
        _norm2_router_kernel,
        grid=(b, lb),
        in_specs=[pl.BlockSpec((1, tl, d), lambda bi, li: (bi, li, 0)),
                  pl.BlockSpec((1, d), lambda bi, li: (0, 0)),
                  pl.BlockSpec((1, 1, d), lambda bi, li: (bi, 0, 0)),
                  pl.BlockSpec((1, 1, d), lambda bi, li: (bi, 0, 0)),
                  pl.BlockSpec((d, ROUTER_LANES), lambda bi, li: (0, 0)),
                  pl.BlockSpec((1, ROUTER_LANES), lambda bi, li: (0, 0))],
        out_specs=[pl.BlockSpec((tl * nt, LANES), lambda bi, li: (bi * lb + li, 0)),
                   pl.BlockSpec((1, tl, ROUTER_LANES), lambda bi, li: (bi, li, 0))],
        out_shape=[jax.ShapeDtypeStruct((b * l * nt, LANES), jnp.int32),
                   jax.ShapeDtypeStruct((b, l, ROUTER_LANES), F32)],
        compiler_params=_params("parallel", "parallel"),
        name="norm2_router",
    )(x, g, shift, scale, w_r, b_r)


def _first_lane(mask, lane):
    return jnp.min(jnp.where(mask, lane, LANES), axis=-1, keepdims=True)


def _route_kernel(lg_ref, meta_ref, wts_ref, cnt_ref, carry_ref):
    @pl.when(pl.program_id(0) == 0)
    def _():
        carry_ref[...] = jnp.zeros_like(carry_ref)

    lg = lg_ref[...]
    tb = lg.shape[0]
    lane = lax.broadcasted_iota(jnp.int32, lg.shape, 1)
    neg_inf = jnp.float32(-jnp.inf)
    is_group = lane < N_GROUPS
    gl = jnp.where(is_group, lg, neg_inf)
    g_max = jnp.max(gl, axis=-1, keepdims=True)
    g_sel = _first_lane(gl == g_max, lane)
    p_g = 1.0 / jnp.sum(jnp.where(is_group, jnp.exp(lg - g_max), 0.0), axis=-1, keepdims=True)

    e_idx = lane - N_GROUPS
    in_group = (e_idx >= g_sel * EXPERTS_PER_GROUP) & (e_idx < (g_sel + 1) * EXPERTS_PER_GROUP)
    ev = jnp.where(in_group, lg, neg_inf)
    v1 = jnp.max(ev, axis=-1, keepdims=True)
    i1 = _first_lane(ev == v1, lane)
    ev2 = jnp.where(lane == i1, neg_inf, ev)
    v2 = jnp.max(ev2, axis=-1, keepdims=True)
    i2 = _first_lane(ev2 == v2, lane)
    t = jnp.exp(v2 - v1)
    w1 = p_g / (1.0 + t)
    w2 = w1 * t

    oh1 = lane == i1
    oh2 = lane == i2
    oh = (oh1 | oh2).astype(BF16)
    earlier = (lax.broadcasted_iota(jnp.int32, (tb, tb), 0) > lax.broadcasted_iota(jnp.int32, (tb, tb), 1)).astype(BF16)
    before = jnp.dot(earlier, oh, preferred_element_type=F32) + carry_ref[...]
    r1 = jnp.sum(jnp.where(oh1, before, 0.0), axis=-1, keepdims=True).astype(jnp.int32)
    r2 = jnp.sum(jnp.where(oh2, before, 0.0), axis=-1, keepdims=True).astype(jnp.int32)
    carry_ref[...] += jnp.sum(oh.astype(F32), axis=0, keepdims=True)

    meta_ref[...] = jnp.where(lane == 0, i1 - N_GROUPS, jnp.where(lane == 1, i2 - N_GROUPS,
                              jnp.where(lane == 2, r1, jnp.where(lane == 3, r2, 0))))
    wts_ref[...] = jnp.where(lane == 0, w1, jnp.where(lane == 1, w2, 0.0))
    cnt_ref[...] = carry_ref[...]


def _route(logits, tb=512):
    n = logits.shape[0]
    return pl.pallas_call(
        _route_kernel,
        grid=(n // tb,),
        in_specs=[pl.BlockSpec((tb, LANES), lambda i: (i, 0))],
        out_specs=[pl.BlockSpec((tb, LANES), lambda i: (i, 0)),
                   pl.BlockSpec((tb, LANES), lambda i: (i, 0)),
                   pl.BlockSpec((1, LANES), lambda i: (0, 0))],
        out_shape=[jax.ShapeDtypeStruct((n, LANES), jnp.int32),
                   jax.ShapeDtypeStruct((n, LANES), F32),
                   jax.ShapeDtypeStruct((1, LANES), F32)],
        scratch_shapes=[pltpu.VMEM((1, LANES), F32)],
        compiler_params=_params("arbitrary"),
        name="moe_route",
    )(logits)


def _dest_kernel(meta_ref, pst_ref, o_ref):
    meta = meta_ref[...]
    lane = lax.broadcasted_iota(jnp.int32, meta.shape, 1)
    pst = pst_ref[...]

    def row_of(slot):
        e = meta[:, slot:slot + 1]
        start = jnp.sum(jnp.where(lane == e + N_GROUPS, pst, 0), axis=-1, keepdims=True)
        return start + meta[:, TOP_K + slot:TOP_K + slot + 1]

    o_ref[...] = jnp.where(lane == 0, row_of(0), jnp.where(lane == 1, row_of(1), 0))


def _dest_rows(meta, pst, tb=512):
    n = meta.shape[0]
    return pl.pallas_call(
        _dest_kernel,
        grid=(n // tb,),
        in_specs=[pl.BlockSpec((tb, LANES), lambda i: (i, 0)),
                  pl.BlockSpec((1, LANES), lambda i: (0, 0))],
        out_specs=pl.BlockSpec((tb, LANES), lambda i: (i, 0)),
        out_shape=jax.ShapeDtypeStruct((n, LANES), jnp.int32),
        compiler_params=_params("parallel"),
        name="moe_dest_rows",
    )(meta, pst)


def _zero_tail_kernel(lb_ref, o_ref):
    o_ref[...] = jnp.zeros_like(o_ref)


def _zero_tails(last_blk, total, tmb, nt):
    return pl.pallas_call(
        _zero_tail_kernel,
        grid_spec=pltpu.PrefetchScalarGridSpec(
            num_scalar_prefetch=1,
            grid=(last_blk.shape[0],),
            in_specs=[],
            out_specs=pl.BlockSpec((tmb * nt, LANES), lambda e, lb: (lb[e], 0))),
        out_shape=jax.ShapeDtypeStruct((total * nt, LANES), jnp.int32),
        compiler_params=_params("arbitrary"),
        name="moe_zero_tails",
    )(last_blk)


def _dispatch_kernel(dest_ref, hp_ref, xs_in_ref, xs_ref, sem):
    rows = hp_ref.shape[0]

    def slab_copy(r, k):
        return pltpu.make_async_copy(hp_ref.at[r], xs_ref.at[dest_ref[0, 0, r * TOP_K + k]], sem)

    def start(r, c):
        for k in range(TOP_K):
            slab_copy(r, k).start(priority=k % 2)
        return c

    def wait(r, c):
        for k in range(TOP_K):
            slab_copy(r, k).wait()
        return c

    lax.fori_loop(0, rows, start, 0, unroll=DMA_LOOP_UNROLL)
    lax.fori_loop(0, rows, wait, 0, unroll=DMA_LOOP_UNROLL)


def _dispatch(hp3, dest, xs0, rows):
    n, nt, _ = hp3.shape
    steps = n // rows
    return pl.pallas_call(
        _dispatch_kernel,
        grid=(steps,),
        in_specs=[pl.BlockSpec((1, 1, rows * TOP_K), lambda i: (i, 0, 0), memory_space=pltpu.SMEM),
                  pl.BlockSpec((rows, nt, LANES), lambda i: (i, 0, 0)),
                  pl.BlockSpec(memory_space=pl.ANY)],
        out_specs=pl.BlockSpec(memory_space=pl.ANY),
        out_shape=jax.ShapeDtypeStruct(xs0.shape, xs0.dtype),
        scratch_shapes=[pltpu.SemaphoreType.DMA(())],
        input_output_aliases={2: 0},
        compiler_params=_params("arbitrary"),
        name="moe_dispatch",
    )(dest.reshape(steps, 1, rows * TOP_K), hp3, xs0)


def _expert_in_kernel(be_ref, nu_ref, x_ref, w_ref, *rest, nt, tn):
    gate_ref = rest[0] if len(rest) == 2 else None
    o_ref = rest[-1]

    @pl.when(pl.program_id(0) < nu_ref[0])
    def _():
        tmb = o_ref.shape[0]
        halves = [_unpack_halves(x_ref[pl.ds(j, tmb, stride=nt), :]) for j in range(nt)]
        lo = jnp.concatenate([h[0].astype(BF16) for h in halves], axis=1)
        hi = jnp.concatenate([h[1].astype(BF16) for h in halves], axis=1)
        half = nt * LANES
        for c0 in range(0, o_ref.shape[1], tn):
            y = (jnp.dot(lo, w_ref[0, :half, c0:c0 + tn].astype(BF16), preferred_element_type=F32)
                 + jnp.dot(hi, w_ref[0, half:, c0:c0 + tn].astype(BF16), preferred_element_type=F32))
            if gate_ref is None:
                y = _silu(y)
            else:
                y = gate_ref[:, c0:c0 + tn].astype(F32) * y
            o_ref[:, c0:c0 + tn] = y.astype(o_ref.dtype)

    @pl.when(pl.program_id(0) >= nu_ref[0])
    def _():
        o_ref[...] = jnp.zeros_like(o_ref)


def _expert_in(xs, w, gate, blk_e, n_used, tmb, name, tn=256):
    _, d, ff = w.shape
    nt = d // 2 // LANES
    total = xs.shape[0] // nt
    nblk = total // tmb

    def blk(bi, nu):
        return jnp.minimum(bi, nu[0] - 1)

    in_specs = [pl.BlockSpec((tmb * nt, LANES), lambda bi, be, nu: (blk(bi, nu), 0)),
                pl.BlockSpec((1, d, ff), lambda bi, be, nu: (be[bi], 0, 0))]
    args = [xs, w]
    if gate is not None:
        in_specs.append(pl.BlockSpec((tmb, ff), lambda bi, be, nu: (blk(bi, nu), 0)))
        args.append(gate)
    return pl.pallas_call(
        functools.partial(_expert_in_kernel, nt=nt, tn=tn),
        grid_spec=pltpu.PrefetchScalarGridSpec(
            num_scalar_prefetch=2,
            grid=(nblk,),
            in_specs=in_specs,
            out_specs=pl.BlockSpec((tmb, ff), lambda bi, be, nu: (bi, 0))),
        out_shape=jax.ShapeDtypeStruct((total, ff), BF16),
        compiler_params=_params("arbitrary"),
        name=name,
    )(blk_e, n_used, *args)


def _expert_down_kernel(be_ref, nu_ref, a_ref, w_ref, o_ref, *, nt, tn):
    @pl.when(pl.program_id(0) < nu_ref[0])
    def _():
        a = a_ref[...]
        tmb = a.shape[0]
        half = nt * LANES
        for c0 in range(0, half, tn):
            ylo = jnp.dot(a, w_ref[0, :, c0:c0 + tn].astype(BF16), preferred_element_type=F32)
            yhi = jnp.dot(a, w_ref[0, :, half + c0:half + c0 + tn].astype(BF16), preferred_element_type=F32)
            packed = pltpu.pack_elementwise([ylo, yhi], packed_dtype=BF16)
            for j in range(tn // LANES):
                o_ref[pl.ds(c0 // LANES + j, tmb, stride=nt), :] = packed[:, j * LANES:(j + 1) * LANES]

    @pl.when(pl.program_id(0) >= nu_ref[0])
    def _():
        o_ref[...] = jnp.zeros_like(o_ref)


def _expert_down(act, w_down, blk_e, n_used, tmb, tn=512):
    total, ff = act.shape
    d = w_down.shape[2]
    nt = d // 2 // LANES
    nblk = total // tmb

    def blk(bi, nu):
        return jnp.minimum(bi, nu[0] - 1)

    return pl.pallas_call(
        functools.partial(_expert_down_kernel, nt=nt, tn=tn),
        grid_spec=pltpu.PrefetchScalarGridSpec(
            num_scalar_prefetch=2,
            grid=(nblk,),
            in_specs=[pl.BlockSpec((tmb, ff), lambda bi, be, nu: (blk(bi, nu), 0)),
                      pl.BlockSpec((1, ff, d), lambda bi, be, nu: (be[bi], 0, 0))],
            out_specs=pl.BlockSpec((tmb * nt, LANES), lambda bi, be, nu: (bi, 0))),
        out_shape=jax.ShapeDtypeStruct((total * nt, LANES), jnp.int32),
        compiler_params=_params("arbitrary"),
        name="expert_down",
    )(blk_e, n_used, act, w_down)


def _combine_kernel(pos_ref, nxt_ref, ys_ref, w_ref, x_ref, ga_ref, g_ref, o_ref, *scratch):
    n_buf = 2 * TOP_K
    slabs, sems = scratch[:n_buf], scratch[n_buf:]
    rows = x_ref.shape[1]
    hr = rows // 2
    nt = ys_ref.shape[1]
    half = nt * LANES
    step = pl.program_id(0) * pl.num_programs(1) + pl.program_id(1)
    n_steps = pl.num_programs(0) * pl.num_programs(1)

    def slab_copy(idx_ref, h, r, k):
        dst = slabs[h * TOP_K + k].at[pl.ds(pl.multiple_of(r * SLAB_PITCH, SUBLANES), nt)]
        return pltpu.make_async_copy(ys_ref.at[idx_ref[0, 0, (h * hr + r) * TOP_K + k]], dst, sems[h * TOP_K + k])

    def issue(idx_ref, h):
        def body(r, c):
            for k in range(TOP_K):
                slab_copy(idx_ref, h, r, k).start(priority=k % 2)
            return c
        lax.fori_loop(0, hr, body, 0, unroll=DMA_LOOP_UNROLL)

    def wait(idx_ref, h):
        def body(r, c):
            for k in range(TOP_K):
                slab_copy(idx_ref, h, r, k).wait()
            return c
        lax.fori_loop(0, hr, body, 0, unroll=DMA_LOOP_UNROLL)

    def compute(h):
        r0 = h * hr
        w0 = w_ref[r0:r0 + hr, 0:1]
        w1 = w_ref[r0:r0 + hr, 1:2]
        ss = jnp.zeros((hr, 1), F32)
        for j in range(nt):
            lo0, hi0 = _unpack_halves(slabs[h * TOP_K][pl.ds(j, hr, stride=SLAB_PITCH), :])
            lo1, hi1 = _unpack_halves(slabs[h * TOP_K + 1][pl.ds(j, hr, stride=SLAB_PITCH), :])
            for c0, y in ((j * LANES, w0 * lo0 + w1 * lo1), (half + j * LANES, w0 * hi0 + w1 * hi1)):
                z = x_ref[0, r0:r0 + hr, c0:c0 + LANES] + ga_ref[0, :, c0:c0 + LANES] * y
                ss = ss + jnp.sum(z * z, axis=-1, keepdims=True)
                o_ref[0, r0:r0 + hr, c0:c0 + LANES] = z
        inv = lax.rsqrt(ss / (2 * half) + EPS)
        o_ref[0, r0:r0 + hr, :] = o_ref[0, r0:r0 + hr, :] * inv * g_ref[...]

    @pl.when(step == 0)
    def _():
        issue(pos_ref, 0)

    issue(pos_ref, 1)
    wait(pos_ref, 0)
    compute(0)

    @pl.when(step + 1 < n_steps)
    def _():
        issue(nxt_ref, 0)

    wait(pos_ref, 1)
    compute(1)


def _combine(ys3, pos, wts, x, gate, g, rows):
    b, l, d = x.shape
    lb = l // rows
    n_steps = b * lb
    pos3 = pos.reshape(n_steps, 1, rows * TOP_K)
    slab = pltpu.VMEM((rows // 2 * SLAB_PITCH, LANES), ys3.dtype)
    return pl.pallas_call(
        _combine_kernel,
        grid=(b, lb),
        in_specs=[pl.BlockSpec((1, 1, rows * TOP_K), lambda bi, i: (bi * lb + i, 0, 0), memory_space=pltpu.SMEM),
                  pl.BlockSpec((1, 1, rows * TOP_K), lambda bi, i: (jnp.minimum(bi * lb + i + 1, n_steps - 1), 0, 0),
                               memory_space=pltpu.SMEM),
                  pl.BlockSpec(memory_space=pl.ANY),
                  pl.BlockSpec((rows, LANES), lambda bi, i: (bi * lb + i, 0)),
                  pl.BlockSpec((1, rows, d), lambda bi, i: (bi, i, 0)),
                  pl.BlockSpec((1, 1, d), lambda bi, i: (bi, 0, 0)),
                  pl.BlockSpec((1, d), lambda bi, i: (0, 0))],
        out_specs=pl.BlockSpec((1, rows, d), lambda bi, i: (bi, i, 0)),
        out_shape=jax.ShapeDtypeStruct((b, l, d), F32),
        scratch_shapes=[slab] * (2 * TOP_K) + [pltpu.SemaphoreType.DMA(())] * (2 * TOP_K),
        compiler_params=_params("arbitrary", "arbitrary"),
        name="moe_combine_norm",
    )(pos3, pos3, ys3, wts, x, gate, g)


def _block_layout(counts, n_pairs, tmb):
    nblk = (n_pairs + N_EXPERTS * (tmb - 1) + tmb - 1) // tmb
    blocks = (counts + tmb - 1) // tmb
    bend = jnp.cumsum(blocks)
    pstart = (bend - blocks) * tmb
    n_used = bend[-1]
    blk_ids = jnp.minimum(jnp.arange(nblk, dtype=jnp.int32), n_used - 1)
    blk_e = jnp.minimum(jnp.searchsorted(bend, blk_ids, side="right"), N_EXPERTS - 1).astype(jnp.int32)
    last_blk = jnp.maximum(bend - 1, 0).astype(jnp.int32)
    return nblk, pstart.astype(jnp.int32), blk_e, last_blk, n_used.astype(jnp.int32).reshape(1)


def _rope_tables(n_tokens):
    rows = n_tokens // GRID_W
    row, col = jnp.meshgrid(jnp.arange(rows), jnp.arange(GRID_W), indexing="ij")
    pos = jnp.stack([row.reshape(-1), col.reshape(-1)], axis=-1).astype(F32)
    inv = ROPE_THETA ** (-jnp.arange(0, ROPE_AXIS_DIM, 2, dtype=F32) / ROPE_AXIS_DIM)
    ang = pos[:, :, None] * inv[None, None, :]
    cos, sin = jnp.cos(ang), jnp.sin(ang)
    cos_t = jnp.concatenate([cos[:, 0], cos[:, 0], cos[:, 1], cos[:, 1]], axis=-1)
    sin_t = jnp.concatenate([-sin[:, 0], sin[:, 0], -sin[:, 1], sin[:, 1]], axis=-1)
    return cos_t, sin_t


def kernel(x, c, ctx, c_ctx, norm1_g, w_mod, b_mod, w_in, q_norm_g, k_norm_g, w_attn_out, conv_dw_w, conv_dw_b, conv_ln_g, conv_ln_b, w_conv_out, w_out, norm2_g, w_router_group, b_router_group, w_router_expert, b_router_expert, w_exp_gate, w_exp_up, w_exp_down, norm_f_g):
    b, s, d = x.shape
    n_ctx = ctx.shape[1]
    assert w_in.shape[0] == 1, "single-layer stack"
    conv_width = conv_dw_w.shape[-1]
    k_off = ATTN_WIDTH
    glu_off = k_off + 2 * KV_WIDTH
    gate_off = glu_off + 2 * conv_width

    n_c = b + 1
    cvec = jnp.zeros((SUBLANES * ((n_c + SUBLANES - 1) // SUBLANES), d), F32).at[:b].set(c).at[b].set(c_ctx)
    mod = _mod_vectors(cvec, w_mod[0], b_mod.reshape(1, -1))
    sh1, sc1, ga1, sh2, sc2, ga2 = [mod[:b, i * d:(i + 1) * d].reshape(b, 1, d) for i in range(N_MOD)]
    csh1, csc1 = [mod[b:b + 1, i * d:(i + 1) * d].reshape(1, 1, d) for i in range(2)]

    g1 = norm1_g.reshape(1, d)
    h = _norm_mod(x, g1, sh1, sc1, tl=512)
    hc = _norm_mod(ctx, g1, csh1, csc1, tl=n_ctx)
    w_in_b = w_in[0].astype(BF16)
    cos_t, sin_t = _rope_tables(s)
    qg = q_norm_g.reshape(1, HEAD_DIM)
    kg = k_norm_g.reshape(1, HEAD_DIM)
    q = _q_proj(h, w_in_b, qg, cos_t, sin_t, 0, HEAD_DIM ** -0.5 * LOG2E, tm=1024)
    k, v = _kv_proj(h, w_in_b, kg, cos_t, sin_t, k_off, tm=1024)
    kc, vc = _kv_proj(hc, w_in_b, kg, None, None, k_off, tm=n_ctx)
    attn = _attention(q, k, v, kc, vc, tq=128, tk=1024)

    h2d = h.reshape(b * s, d)
    u = _glu_proj(h2d, w_in_b, glu_off, conv_width, tm=1024)
    conv = _conv_module(u.reshape(b, s, conv_width), conv_dw_w.reshape(CONV_TAPS, conv_width // LANES, LANES),
                        conv_dw_b.reshape(conv_width // LANES, LANES), conv_ln_g.reshape(1, -1),
                        conv_ln_b.reshape(1, -1), tl=256)
    mrg = _merge(h2d, attn.reshape(b * s, ATTN_WIDTH), conv.reshape(b * s, conv_width), w_in_b, gate_off,
                 w_attn_out[0].astype(BF16), w_conv_out[0].astype(BF16))
    x1 = _out_proj(mrg.reshape(b, s, d), w_out[0].astype(BF16), x, ga1)

    w_r = jnp.zeros((d, ROUTER_LANES), F32).at[:, :N_GROUPS].set(w_router_group[0]) \
        .at[:, N_GROUPS:N_GROUPS + N_EXPERTS].set(w_router_expert[0])
    b_r = jnp.zeros((1, ROUTER_LANES), F32).at[0, :N_GROUPS].set(b_router_group[0]) \
        .at[0, N_GROUPS:N_GROUPS + N_EXPERTS].set(b_router_expert[0])
    hp, logits = _norm2_router(x1, norm2_g.reshape(1, d), sh2, sc2, w_r.astype(BF16), b_r, tl=256)
    n = b * s
    nt = d // 2 // LANES
    tmb = 512
    meta, wts, cnt = _route(logits.reshape(n, ROUTER_LANES))
    counts = cnt[0, N_GROUPS:N_GROUPS + N_EXPERTS].astype(jnp.int32)
    nblk, pstart, blk_e, last_blk, n_used = _block_layout(counts, n * TOP_K, tmb)
    pst = jnp.zeros((1, LANES), jnp.int32).at[0, N_GROUPS:N_GROUPS + N_EXPERTS].set(pstart)
    dest = _dest_rows(meta, pst)[:, :TOP_K].reshape(-1)
    xs0 = _zero_tails(last_blk, nblk * tmb, tmb, nt)
    xs = _dispatch(hp.reshape(n, nt, LANES), dest, xs0.reshape(nblk * tmb, nt, LANES), rows=256)
    xs2 = xs.reshape(nblk * tmb * nt, LANES)
    sg = _expert_in(xs2, w_exp_gate[0], None, blk_e, n_used, tmb, "expert_gate")
    act = _expert_in(xs2, w_exp_up[0], sg, blk_e, n_used, tmb, "expert_up")
    ys = _expert_down(act, w_exp_down[0], blk_e, n_used, tmb)
    return _combine(ys.reshape(nblk * tmb, nt, LANES), dest, wts, x1, ga2, norm_f_g.reshape(1, d), rows=256)
```

```python
import functools

import jax
import jax.numpy as jnp
from jax import lax
from jax.experimental import pallas as pl
from jax.experimental.pallas import tpu as pltpu

F32 = jnp.float32
BF16 = jnp.bfloat16

GRID_W = 64
HEAD_DIM = 128
N_Q_HEADS = 16
N_KV_HEADS = 4
Q_PER_KV = N_Q_HEADS // N_KV_HEADS
ATTN_WIDTH = N_Q_HEADS * HEAD_DIM
KV_WIDTH = N_KV_HEADS * HEAD_DIM
CONV_TAPS = 31
CONV_HALO = 16
ROPE_THETA = 10000.0
ROPE_AXIS_DIM = HEAD_DIM // 2
N_GROUPS = 4
EXPERTS_PER_GROUP = 8
N_EXPERTS = N_GROUPS * EXPERTS_PER_GROUP
TOP_K = 2
N_MOD = 6
EPS = 1e-6
LOG2E = 1.4426950408889634
LANES = 128
SUBLANES = 8
ROUTER_LANES = LANES
SLAB_PITCH = 24
DMA_LOOP_UNROLL = 4

V7X_VMEM_LIMIT = 56 * 1024 * 1024


def _params(*sem):
    return pltpu.CompilerParams(dimension_semantics=sem, vmem_limit_bytes=V7X_VMEM_LIMIT)


def _sigmoid(x):
    return 1.0 / (1.0 + jnp.exp(-x))


def _silu(x):
    return x * _sigmoid(x)


def _rms(x, g):
    return x * lax.rsqrt(jnp.mean(x * x, axis=-1, keepdims=True) + EPS) * g


def _mod_kernel(c_ref, w_ref, b_ref, o_ref):
    s = _silu(c_ref[...]).astype(BF16)
    o_ref[...] = jnp.dot(s, w_ref[...].astype(BF16), preferred_element_type=F32) + b_ref[...]


def _mod_vectors(cvec, w_mod, b_mod, tn=512):
    m, d = cvec.shape
    n = w_mod.shape[1]
    return pl.pallas_call(
        _mod_kernel,
        grid=(n // tn,),
        in_specs=[pl.BlockSpec((m, d), lambda j: (0, 0)),
                  pl.BlockSpec((d, tn), lambda j: (0, j)),
                  pl.BlockSpec((1, tn), lambda j: (0, j))],
        out_specs=pl.BlockSpec((m, tn), lambda j: (0, j)),
        out_shape=jax.ShapeDtypeStruct((m, n), F32),
        compiler_params=_params("arbitrary"),
        name="mod_vectors",
    )(cvec, w_mod, b_mod)


def _norm_mod_kernel(x_ref, g_ref, sh_ref, sc_ref, o_ref):
    y = _rms(x_ref[0], g_ref[...])
    o_ref[0] = (y * (1.0 + sc_ref[0]) + sh_ref[0]).astype(o_ref.dtype)


def _norm_mod(x, g, shift, scale, tl):
    b, l, d = x.shape
    per_batch = shift.shape[0] > 1
    mod_map = (lambda bi, li: (bi, 0, 0)) if per_batch else (lambda bi, li: (0, 0, 0))
    return pl.pallas_call(
        _norm_mod_kernel,
        grid=(b, l // tl),
        in_specs=[pl.BlockSpec((1, tl, d), lambda bi, li: (bi, li, 0)),
                  pl.BlockSpec((1, d), lambda bi, li: (0, 0)),
                  pl.BlockSpec((1, 1, d), mod_map),
                  pl.BlockSpec((1, 1, d), mod_map)],
        out_specs=pl.BlockSpec((1, tl, d), lambda bi, li: (bi, li, 0)),
        out_shape=jax.ShapeDtypeStruct((b, l, d), BF16),
        compiler_params=_params("parallel", "parallel"),
        name="norm_modulate",
    )(x, g, shift, scale)


def _head_norm_rope(a, g, cos, sin):
    y = _rms(a, g)
    if cos is None:
        return y
    return y * cos + pltpu.roll(y, HEAD_DIM // 2, 1) * sin


def _q_proj_kernel(h_ref, w_ref, g_ref, cos_ref, sin_ref, o_ref, *, scale):
    acc = jnp.dot(h_ref[0], w_ref[...], preferred_element_type=F32)
    for hh in range(o_ref.shape[1]):
        a = acc[:, hh * HEAD_DIM:(hh + 1) * HEAD_DIM]
        y = _head_norm_rope(a, g_ref[...], cos_ref[...], sin_ref[...])
        o_ref[0, hh] = (y * scale).astype(o_ref.dtype)


def _q_proj(h, w, g, cos_t, sin_t, col_off, scale, tm, tn=1024):
    b, l, d = h.shape
    jb = col_off // tn
    hpt = tn // HEAD_DIM
    return pl.pallas_call(
        functools.partial(_q_proj_kernel, scale=scale),
        grid=(b, l // tm, ATTN_WIDTH // tn),
        in_specs=[pl.BlockSpec((1, tm, d), lambda bi, i, j: (bi, i, 0)),
                  pl.BlockSpec((d, tn), lambda bi, i, j: (0, jb + j)),
                  pl.BlockSpec((1, HEAD_DIM), lambda bi, i, j: (0, 0)),
                  pl.BlockSpec((tm, HEAD_DIM), lambda bi, i, j: (i, 0)),
                  pl.BlockSpec((tm, HEAD_DIM), lambda bi, i, j: (i, 0))],
        out_specs=pl.BlockSpec((1, hpt, tm, HEAD_DIM), lambda bi, i, j: (bi, j, i, 0)),
        out_shape=jax.ShapeDtypeStruct((b, N_Q_HEADS, l, HEAD_DIM), BF16),
        compiler_params=_params("parallel", "parallel", "arbitrary"),
        name="q_proj",
    )(h, w, g, cos_t, sin_t)


def _kv_proj_kernel(h_ref, wk_ref, wv_ref, g_ref, *rest, rope):
    if rope:
        cos_ref, sin_ref, k_ref, v_ref = rest
        cos, sin = cos_ref[...], sin_ref[...]
    else:
        k_ref, v_ref = rest
        cos = sin = None
    h = h_ref[0]
    acc = jnp.dot(h, wk_ref[...], preferred_element_type=F32)
    for hh in range(N_KV_HEADS):
        a = acc[:, hh * HEAD_DIM:(hh + 1) * HEAD_DIM]
        k_ref[0, :, hh * HEAD_DIM:(hh + 1) * HEAD_DIM] = _head_norm_rope(a, g_ref[...], cos, sin).astype(k_ref.dtype)
    v_ref[0] = jnp.dot(h, wv_ref[...], preferred_element_type=F32).astype(v_ref.dtype)


def _kv_proj(h, w_qk, w, g, cos_t, sin_t, k_off, tm):
    b, l, d = h.shape
    rope = cos_t is not None
    jk = k_off // KV_WIDTH
    in_specs = [pl.BlockSpec((1, tm, d), lambda bi, i: (bi, i, 0)),
                pl.BlockSpec((d, KV_WIDTH), lambda bi, i: (0, jk)),
                pl.BlockSpec((d, KV_WIDTH), lambda bi, i: (0, jk + 1)),
                pl.BlockSpec((1, HEAD_DIM), lambda bi, i: (0, 0))]
    args = [h, w_qk, w, g]
    if rope:
        in_specs += [pl.BlockSpec((tm, HEAD_DIM), lambda bi, i: (i, 0))] * 2
        args += [cos_t, sin_t]
    return pl.pallas_call(
        functools.partial(_kv_proj_kernel, rope=rope),
        grid=(b, l // tm),
        in_specs=in_specs,
        out_specs=[pl.BlockSpec((1, tm, KV_WIDTH), lambda bi, i: (bi, i, 0))] * 2,
        out_shape=[jax.ShapeDtypeStruct((b, l, KV_WIDTH), BF16)] * 2,
        compiler_params=_params("parallel", "parallel"),
        name="kv_proj_rope" if rope else "kv_proj_ctx",
    )(*args)


def _glu_proj_kernel(h_ref, wa_ref, wg_ref, o_ref):
    a = jnp.dot(h_ref[...], wa_ref[...], preferred_element_type=F32)
    gt = jnp.dot(h_ref[...], wg_ref[...], preferred_element_type=F32)
    o_ref[...] = (a * _sigmoid(gt)).astype(o_ref.dtype)


def _glu_proj(h2d, w, col_off, width, tm, tn=512):
    m, d = h2d.shape
    ja = col_off // tn
    jg = (col_off + width) // tn
    return pl.pallas_call(
        _glu_proj_kernel,
        grid=(m // tm, width // tn),
        in_specs=[pl.BlockSpec((tm, d), lambda i, j: (i, 0)),
                  pl.BlockSpec((d, tn), lambda i, j: (0, ja + j)),
                  pl.BlockSpec((d, tn), lambda i, j: (0, jg + j))],
        out_specs=pl.BlockSpec((tm, tn), lambda i, j: (i, j)),
        out_shape=jax.ShapeDtypeStruct((m, width), BF16),
        compiler_params=_params("parallel", "arbitrary"),
        name="glu_proj",
    )(h2d, w, w)


def _attn_kernel(q_ref, k_ref, v_ref, kc_ref, vc_ref, o_ref,
                 s0_ref, s1_ref, p0_ref, p1_ref, m_ref, al_ref, acc_ref, *, tk, rb):
    g, tq, dh = q_ref.shape[1:]
    rows = s0_ref.shape[0]
    tqs = rows // g
    chunks = [(k_ref, v_ref, c * tk, tk) for c in range(k_ref.shape[1] // tk)]
    chunks.append((kc_ref, vc_ref, 0, kc_ref.shape[1]))
    units = [(sb, j) for sb in range(tq // tqs) for j in range(len(chunks))]
    s_refs = (s0_ref, s1_ref)
    p_refs = (p0_ref, p1_ref)

    def scores(u):
        sb, j = units[u]
        kr, _, st, n = chunks[j]
        q = q_ref[0, :, sb * tqs:(sb + 1) * tqs, :].reshape(rows, dh)
        s_refs[u % 2][:, :n] = lax.dot_general(q, kr[0, st:st + n, :], (((1,), (1,)), ((), ())),
                                               preferred_element_type=F32)

    def softmax(u):
        j = units[u][1]
        n = chunks[j][3]
        s_ref, p_ref = s_refs[u % 2], p_refs[u % 2]
        for r0 in range(0, rows, rb):
            sblk = s_ref[r0:r0 + rb, :n]
            mn = jnp.max(sblk, axis=-1, keepdims=True)
            if j > 0:
                mo = m_ref[r0:r0 + rb, :]
                mn = jnp.maximum(mo, mn)
                al_ref[r0:r0 + rb, :] = jnp.exp2(mo - mn)
            m_ref[r0:r0 + rb, :] = mn
            p_ref[r0:r0 + rb, :n] = jnp.exp2(sblk - mn).astype(BF16)

    def weighted_values(u):
        j = units[u][1]
        _, vr, st, n = chunks[j]
        ones_col = (lax.broadcasted_iota(jnp.int32, (n, dh), 1) == 0).astype(BF16)
        v1 = jnp.concatenate([vr[0, st:st + n, :], ones_col], axis=1)
        upd = jnp.dot(p_refs[u % 2][:, :n], v1, preferred_element_type=F32)
        if j == 0:
            acc_ref[...] = upd
        else:
            acc_ref[...] = al_ref[...] * acc_ref[...] + upd

    def finish(sb):
        acc = acc_ref[...]
        o = acc[:, :dh] / acc[:, dh:dh + 1]
        for gi in range(g):
            o_ref[0, sb * tqs:(sb + 1) * tqs, gi * dh:(gi + 1) * dh] = o[gi * tqs:(gi + 1) * tqs].astype(o_ref.dtype)

    scores(0)
    for u, (sb, j) in enumerate(units):
        if u + 1 < len(units):
            scores(u + 1)
        softmax(u)
        weighted_values(u)
        if j == len(chunks) - 1:
            finish(sb)


def _attention(q, k, v, kc, vc, tq, tqs, tk, rb=16):
    b, _, l, dh = q.shape
    lc = kc.shape[1]
    rows = Q_PER_KV * tqs
    gdh = Q_PER_KV * dh
    return pl.pallas_call(
        functools.partial(_attn_kernel, tk=tk, rb=rb),
        grid=(b, N_KV_HEADS, l // tq),
        in_specs=[pl.BlockSpec((1, Q_PER_KV, tq, dh), lambda bi, kh, qi: (bi, kh, qi, 0)),
                  pl.BlockSpec((1, l, dh), lambda bi, kh, qi: (bi, 0, kh)),
                  pl.BlockSpec((1, l, dh), lambda bi, kh, qi: (bi, 0, kh)),
                  pl.BlockSpec((1, lc, dh), lambda bi, kh, qi: (bi, 0, kh)),
                  pl.BlockSpec((1, lc, dh), lambda bi, kh, qi: (bi, 0, kh))],
        out_specs=pl.BlockSpec((1, tq, gdh), lambda bi, kh, qi: (bi, qi, kh)),
        out_shape=jax.ShapeDtypeStruct((b, l, ATTN_WIDTH), BF16),
        scratch_shapes=[pltpu.VMEM((rows, tk), F32), pltpu.VMEM((rows, tk), F32),
                        pltpu.VMEM((rows, tk), BF16), pltpu.VMEM((rows, tk), BF16),
                        pltpu.VMEM((rows, 1), F32), pltpu.VMEM((rows, 1), F32),
                        pltpu.VMEM((rows, 2 * dh), F32)],
        compiler_params=_params("parallel", "parallel", "arbitrary"),
        name="attention",
    )(q, k, v, kc, vc)


def _conv_kernel(prev_ref, cur_ref, next_ref, w_ref, b_ref, g_ref, beta_ref, o_ref, win_ref, y_ref, *, tc, rc):
    li = pl.program_id(1)
    tl, c = cur_ref.shape[1:]
    halo = prev_ref.shape[1]
    nt = c // LANES

    def put_tokens(vals, tok0):
        for j in range(nt):
            win_ref[pl.ds(tok0 * nt + j, vals.shape[0], stride=nt), :] = vals[:, j * LANES:(j + 1) * LANES]

    def put_chunk(ci, carry):
        r0 = pl.multiple_of(ci * rc, rc)
        put_tokens(cur_ref[0, pl.ds(r0, rc), :].astype(F32), halo + r0)
        return carry

    put_tokens(jnp.where(li > 0, prev_ref[0].astype(F32), 0.0), 0)
    lax.fori_loop(0, tl // rc, put_chunk, 0)
    put_tokens(jnp.where(li < pl.num_programs(1) - 1, next_ref[0].astype(F32), 0.0), halo + tl)

    first = halo - CONV_TAPS // 2
    bias = b_ref[...][None]

    def token_chunk(ci, carry):
        tok = ci * tc
        acc = jnp.zeros((tc, nt, LANES), F32) + bias
        for t in range(CONV_TAPS):
            r0 = pl.multiple_of((tok + first + t) * nt, nt)
            acc = acc + win_ref[pl.ds(r0, tc * nt), :].reshape(tc, nt, LANES) * w_ref[t][None]
        y_ref[pl.ds(pl.multiple_of(tok * nt, nt), tc * nt), :] = acc.reshape(tc * nt, LANES)
        return carry

    lax.fori_loop(0, tl // tc, token_chunk, 0)

    def norm_chunk(ci, carry):
        r0 = pl.multiple_of(ci * rc, rc)
        y = jnp.concatenate([y_ref[pl.ds(r0 * nt + j, rc, stride=nt), :] for j in range(nt)], axis=1)
        mu = jnp.mean(y, axis=-1, keepdims=True)
        yc = y - mu
        var = jnp.mean(yc * yc, axis=-1, keepdims=True)
        z = yc * lax.rsqrt(var + EPS) * g_ref[...] + beta_ref[...]
        o_ref[0, pl.ds(r0, rc), :] = _silu(z).astype(o_ref.dtype)
        return carry

    lax.fori_loop(0, tl // rc, norm_chunk, 0, unroll=2)


def _conv_module(u, w_dw, b_dw, ln_g, ln_b, tl, tc=16, rc=32):
    b, l, c = u.shape
    nt = c // LANES
    hb = tl // CONV_HALO
    n_halo = l // CONV_HALO
    return pl.pallas_call(
        functools.partial(_conv_kernel, tc=tc, rc=rc),
        grid=(b, l // tl),
        in_specs=[pl.BlockSpec((1, CONV_HALO, c), lambda bi, li: (bi, jnp.maximum(li * hb - 1, 0), 0)),
                  pl.BlockSpec((1, tl, c), lambda bi, li: (bi, li, 0)),
                  pl.BlockSpec((1, CONV_HALO, c), lambda bi, li: (bi, jnp.minimum((li + 1) * hb, n_halo - 1), 0)),
                  pl.BlockSpec((CONV_TAPS, nt, LANES), lambda bi, li: (0, 0, 0)),
                  pl.BlockSpec((nt, LANES), lambda bi, li: (0, 0)),
                  pl.BlockSpec((1, c), lambda bi, li: (0, 0)),
                  pl.BlockSpec((1, c), lambda bi, li: (0, 0))],
        out_specs=pl.BlockSpec((1, tl, c), lambda bi, li: (bi, li, 0)),
        out_shape=jax.ShapeDtypeStruct((b, l, c), BF16),
        scratch_shapes=[pltpu.VMEM(((tl + 2 * CONV_HALO) * nt, LANES), F32), pltpu.VMEM((tl * nt, LANES), F32)],
        compiler_params=_params("parallel", "arbitrary"),
        name="conv_module",
    )(u, u, u, w_dw, b_dw, ln_g, ln_b)


def _merge_kernel(h_ref, a_ref, c_ref, wga_ref, wgc_ref, wa_ref, wc_ref, o_ref):
    h = h_ref[...]
    g_a = _sigmoid(jnp.dot(h, wga_ref[...], preferred_element_type=F32))
    g_c = _sigmoid(jnp.dot(h, wgc_ref[...], preferred_element_type=F32))
    a = jnp.dot(a_ref[...], wa_ref[...], preferred_element_type=F32)
    cb = jnp.dot(c_ref[...], wc_ref[...], preferred_element_type=F32)
    o_ref[...] = (g_a * a + g_c * cb).astype(o_ref.dtype)


def _merge(h2d, attn, conv, w_in, gate_off, wa, wc, tm=512, tn=512):
    m, d = h2d.shape
    ka = attn.shape[1]
    kc = conv.shape[1]
    nj = d // tn
    ja = gate_off // tn
    jc = (gate_off + d) // tn
    return pl.pallas_call(
        _merge_kernel,
        grid=(nj, m // tm),
        in_specs=[pl.BlockSpec((tm, d), lambda j, i: (i, 0)),
                  pl.BlockSpec((tm, ka), lambda j, i: (i, 0)),
                  pl.BlockSpec((tm, kc), lambda j, i: (i, 0)),
                  pl.BlockSpec((d, tn), lambda j, i: (0, ja + j)),
                  pl.BlockSpec((d, tn), lambda j, i: (0, jc + j)),
                  pl.BlockSpec((ka, tn), lambda j, i: (0, j)),
                  pl.BlockSpec((kc, tn), lambda j, i: (0, j))],
        out_specs=pl.BlockSpec((tm, tn), lambda j, i: (i, j)),
        out_shape=jax.ShapeDtypeStruct((m, d), BF16),
        compiler_params=_params("parallel", "arbitrary"),
        name="merge_branches",
    )(h2d, attn, conv, w_in, w_in, wa, wc)


def _out_proj_kernel(m_ref, w_ref, x_ref, ga_ref, o_ref):
    acc = jnp.dot(m_ref[0], w_ref[...], preferred_element_type=F32)
    o_ref[0] = x_ref[0] + ga_ref[0] * acc


def _out_proj(mrg, w, x, gate, tm=1024, tn=1024):
    b, l, d = x.shape
    return pl.pallas_call(
        _out_proj_kernel,
        grid=(b, l // tm, d // tn),
        in_specs=[pl.BlockSpec((1, tm, d), lambda bi, i, j: (bi, i, 0)),
                  pl.BlockSpec((d, tn), lambda bi, i, j: (0, j)),
                  pl.BlockSpec((1, tm, tn), lambda bi, i, j: (bi, i, j)),
                  pl.BlockSpec((1, 1, tn), lambda bi, i, j: (bi, 0, j))],
        out_specs=pl.BlockSpec((1, tm, tn), lambda bi, i, j: (bi, i, j)),
        out_shape=jax.ShapeDtypeStruct((b, l, d), F32),
        compiler_params=_params("parallel", "parallel", "arbitrary"),
        name="out_proj_residual",
    )(mrg, w, x, gate)


def _pack_halves(y):
    n = y.shape[1] // 2
    return pltpu.pack_elementwise([y[:, :n], y[:, n:]], packed_dtype=BF16)


def _unpack_halves(p):
    lo = pltpu.unpack_elementwise(p, index=0, packed_dtype=BF16, unpacked_dtype=F32)
    hi = pltpu.unpack_elementwise(p, index=1, packed_dtype=BF16, unpacked_dtype=F32)
    return lo, hi


def _norm2_router_kernel(x_ref, g_ref, sh_ref, sc_ref, wr_ref, br_ref, hp_ref, lg_ref):
    y = _rms(x_ref[0], g_ref[...]) * (1.0 + sc_ref[0]) + sh_ref[0]
    packed = _pack_halves(y)
    tl = packed.shape[0]
    nt = packed.shape[1] // LANES
    for j in range(nt):
        hp_ref[pl.ds(j, tl, stride=nt), :] = packed[:, j * LANES:(j + 1) * LANES]
    lg_ref[0] = jnp.dot(y.astype(BF16), wr_ref[...], preferred_element_type=F32) + br_ref[...]


def _norm2_router(x, g, shift, scale, w_r, b_r, tl):
    b, l, d = x.shape
    nt = d // 2 // LANES
    lb = l // tl
    return pl.pallas_call(
        _norm2_router_kernel,
        grid=(b, lb),
        in_specs=[pl.BlockSpec((1, tl, d), lambda bi, li: (bi, li, 0)),
                  pl.BlockSpec((1, d), lambda bi, li: (0, 0)),
                  pl.BlockSpec((1, 1, d), lambda bi, li: (bi, 0, 0)),
                  pl.BlockSpec((1, 1, d), lambda bi, li: (bi, 0, 0)),
                  pl.BlockSpec((d, ROUTER_LANES), lambda bi, li: (0, 0)),
                  pl.BlockSpec((1, ROUTER_LANES), lambda bi, li: (0, 0))],
        out_specs=[pl.BlockSpec((tl * nt, LANES), lambda bi, li: (bi * lb + li, 0)),
                   pl.BlockSpec((1, tl, ROUTER_LANES), lambda bi, li: (bi, li, 0))],
        out_shape=[jax.ShapeDtypeStruct((b * l * nt, LANES), jnp.int32),
                   jax.ShapeDtypeStruct((b, l, ROUTER_LANES), F32)],
        compiler_params=_params("parallel", "parallel"),
        name="norm2_router",
    )(x, g, shift, scale, w_r, b_r)


def _first_lane(mask, lane):
    return jnp.min(jnp.where(mask, lane, LANES), axis=-1, keepdims=True)


def _route_kernel(lg_ref, meta_ref, wts_ref, cnt_ref, carry_ref):
    @pl.when(pl.program_id(0) == 0)
    def _():
        carry_ref[...] = jnp.zeros_like(carry_ref)

    lg = lg_ref[...]
    tb = lg.shape[0]
    lane = lax.broadcasted_iota(jnp.int32, lg.shape, 1)
    neg_inf = jnp.float32(-jnp.inf)
    is_group = lane < N_GROUPS
    gl = jnp.where(is_group, lg, neg_inf)
    g_max = jnp.max(gl, axis=-1, keepdims=True)
    g_sel = _first_lane(gl == g_max, lane)
    p_g = 1.0 / jnp.sum(jnp.where(is_group, jnp.exp(lg - g_max), 0.0), axis=-1, keepdims=True)

    e_idx = lane - N_GROUPS
    in_group = (e_idx >= g_sel * EXPERTS_PER_GROUP) & (e_idx < (g_sel + 1) * EXPERTS_PER_GROUP)
    ev = jnp.where(in_group, lg, neg_inf)
    v1 = jnp.max(ev, axis=-1, keepdims=True)
    i1 = _first_lane(ev == v1, lane)
    ev2 = jnp.where(lane == i1, neg_inf, ev)
    v2 = jnp.max(ev2, axis=-1, keepdims=True)
    i2 = _first_lane(ev2 == v2, lane)
    t = jnp.exp(v2 - v1)
    w1 = p_g / (1.0 + t)
    w2 = w1 * t

    oh1 = lane == i1
    oh2 = lane == i2
    oh = (oh1 | oh2).astype(BF16)
    earlier = (lax.broadcasted_iota(jnp.int32, (tb, tb), 0) > lax.broadcasted_iota(jnp.int32, (tb, tb), 1)).astype(BF16)
    before = jnp.dot(earlier, oh, preferred_element_type=F32) + carry_ref[...]
    r1 = jnp.sum(jnp.where(oh1, before, 0.0), axis=-1, keepdims=True).astype(jnp.int32)
    r2 = jnp.sum(jnp.where(oh2, before, 0.0), axis=-1, keepdims=True).astype(jnp.int32)
    carry_ref[...] += jnp.sum(oh.astype(F32), axis=0, keepdims=True)

    meta_ref[...] = jnp.where(lane == 0, i1 - N_GROUPS, jnp.where(lane == 1, i2 - N_GROUPS,
                              jnp.where(lane == 2, r1, jnp.where(lane == 3, r2, 0))))
    wts_ref[...] = jnp.where(lane == 0, w1, jnp.where(lane == 1, w2, 0.0))
    cnt_ref[...] = carry_ref[...]


def _route(logits, tb=512):
    n = logits.shape[0]
    return pl.pallas_call(
        _route_kernel,
        grid=(n // tb,),
        in_specs=[pl.BlockSpec((tb, LANES), lambda i: (i, 0))],
        out_specs=[pl.BlockSpec((tb, LANES), lambda i: (i, 0)),
                   pl.BlockSpec((tb, LANES), lambda i: (i, 0)),
                   pl.BlockSpec((1, LANES), lambda i: (0, 0))],
        out_shape=[jax.ShapeDtypeStruct((n, LANES), jnp.int32),
                   jax.ShapeDtypeStruct((n, LANES), F32),
                   jax.ShapeDtypeStruct((1, LANES), F32)],
        scratch_shapes=[pltpu.VMEM((1, LANES), F32)],
        compiler_params=_params("arbitrary"),
        name="moe_route",
    )(logits)


def _dest_kernel(meta_ref, pst_ref, o_ref):
    meta = meta_ref[...]
    lane = lax.broadcasted_iota(jnp.int32, meta.shape, 1)
    pst = pst_ref[...]

    def row_of(slot):
        e = meta[:, slot:slot + 1]
        start = jnp.sum(jnp.where(lane == e + N_GROUPS, pst, 0), axis=-1, keepdims=True)
        return start + meta[:, TOP_K + slot:TOP_K + slot + 1]

    o_ref[...] = jnp.where(lane == 0, row_of(0), jnp.where(lane == 1, row_of(1), 0))


def _dest_rows(meta, pst, tb=512):
    n = meta.shape[0]
    return pl.pallas_call(
        _dest_kernel,
        grid=(n // tb,),
        in_specs=[pl.BlockSpec((tb, LANES), lambda i: (i, 0)),
                  pl.BlockSpec((1, LANES), lambda i: (0, 0))],
        out_specs=pl.BlockSpec((tb, LANES), lambda i: (i, 0)),
        out_shape=jax.ShapeDtypeStruct((n, LANES), jnp.int32),
        compiler_params=_params("parallel"),
        name="moe_dest_rows",
    )(meta, pst)


def _zero_tail_kernel(lb_ref, o_ref):
    o_ref[...] = jnp.zeros_like(o_ref)


def _zero_tails(last_blk, total, tmb, nt):
    return pl.pallas_call(
        _zero_tail_kernel,
        grid_spec=pltpu.PrefetchScalarGridSpec(
            num_scalar_prefetch=1,
            grid=(last_blk.shape[0],),
            in_specs=[],
            out_specs=pl.BlockSpec((tmb * nt, LANES), lambda e, lb: (lb[e], 0))),
        out_shape=jax.ShapeDtypeStruct((total * nt, LANES), jnp.int32),
        compiler_params=_params("arbitrary"),
        name="moe_zero_tails",
    )(last_blk)


def _dispatch_kernel(dest_ref, hp_ref, xs_in_ref, xs_ref, sem):
    rows = hp_ref.shape[0]

    def slab_copy(r, k):
        return pltpu.make_async_copy(hp_ref.at[r], xs_ref.at[dest_ref[0, 0, r * TOP_K + k]], sem)

    def start(r, c):
        for k in range(TOP_K):
            slab_copy(r, k).start(priority=k % 2)
        return c

    def wait(r, c):
        for k in range(TOP_K):
            slab_copy(r, k).wait()
        return c

    lax.fori_loop(0, rows, start, 0, unroll=DMA_LOOP_UNROLL)
    lax.fori_loop(0, rows, wait, 0, unroll=DMA_LOOP_UNROLL)


def _dispatch(hp3, dest, xs0, rows):
    n, nt, _ = hp3.shape
    steps = n // rows
    return pl.pallas_call(
        _dispatch_kernel,
        grid=(steps,),
        in_specs=[pl.BlockSpec((1, 1, rows * TOP_K), lambda i: (i, 0, 0), memory_space=pltpu.SMEM),
                  pl.BlockSpec((rows, nt, LANES), lambda i: (i, 0, 0)),
                  pl.BlockSpec(memory_space=pl.ANY)],
        out_specs=pl.BlockSpec(memory_space=pl.ANY),
        out_shape=jax.ShapeDtypeStruct(xs0.shape, xs0.dtype),
        scratch_shapes=[pltpu.SemaphoreType.DMA(())],
        input_output_aliases={2: 0},
        compiler_params=_params("arbitrary"),
        name="moe_dispatch",
    )(dest.reshape(steps, 1, rows * TOP_K), hp3, xs0)


def _expert_in_kernel(be_ref, nu_ref, x_ref, w_ref, *rest, nt, tn):
    gate_ref = rest[0] if len(rest) == 2 else None
    o_ref = rest[-1]

    @pl.when(pl.program_id(0) < nu_ref[0])
    def _():
        tmb = o_ref.shape[0]
        halves = [_unpack_halves(x_ref[pl.ds(j, tmb, stride=nt), :]) for j in range(nt)]
        lo = jnp.concatenate([h[0].astype(BF16) for h in halves], axis=1)
        hi = jnp.concatenate([h[1].astype(BF16) for h in halves], axis=1)
        half = nt * LANES
        for c0 in range(0, o_ref.shape[1], tn):
            y = (jnp.dot(lo, w_ref[0, :half, c0:c0 + tn].astype(BF16), preferred_element_type=F32)
                 + jnp.dot(hi, w_ref[0, half:, c0:c0 + tn].astype(BF16), preferred_element_type=F32))
            if gate_ref is None:
                y = _silu(y)
            else:
                y = gate_ref[:, c0:c0 + tn].astype(F32) * y
            o_ref[:, c0:c0 + tn] = y.astype(o_ref.dtype)

    @pl.when(pl.program_id(0) >= nu_ref[0])
    def _():
        o_ref[...] = jnp.zeros_like(o_ref)


def _expert_in(xs, w, gate, blk_e, n_used, tmb, name, tn=256):
    _, d, ff = w.shape
    nt = d // 2 // LANES
    total = xs.shape[0] // nt
    nblk = total // tmb

    def blk(bi, nu):
        return jnp.minimum(bi, nu[0] - 1)

    in_specs = [pl.BlockSpec((tmb * nt, LANES), lambda bi, be, nu: (blk(bi, nu), 0)),
                pl.BlockSpec((1, d, ff), lambda bi, be, nu: (be[bi], 0, 0))]
    args = [xs, w]
    if gate is not None:
        in_specs.append(pl.BlockSpec((tmb, ff), lambda bi, be, nu: (blk(bi, nu), 0)))
        args.append(gate)
    return pl.pallas_call(
        functools.partial(_expert_in_kernel, nt=nt, tn=tn),
        grid_spec=pltpu.PrefetchScalarGridSpec(
            num_scalar_prefetch=2,
            grid=(nblk,),
            in_specs=in_specs,
            out_specs=pl.BlockSpec((tmb, ff), lambda bi, be, nu: (bi, 0))),
        out_shape=jax.ShapeDtypeStruct((total, ff), BF16),
        compiler_params=_params("arbitrary"),
        name=name,
    )(blk_e, n_used, *args)


def _expert_down_kernel(be_ref, nu_ref, a_ref, w_ref, o_ref, *, nt, tn):
    @pl.when(pl.program_id(0) < nu_ref[0])
    def _():
        a = a_ref[...]
        tmb = a.shape[0]
        half = nt * LANES
        for c0 in range(0, half, tn):
            ylo = jnp.dot(a, w_ref[0, :, c0:c0 + tn].astype(BF16), preferred_element_type=F32)
            yhi = jnp.dot(a, w_ref[0, :, half + c0:half + c0 + tn].astype(BF16), preferred_element_type=F32)
            packed = pltpu.pack_elementwise([ylo, yhi], packed_dtype=BF16)
            for j in range(tn // LANES):
                o_ref[pl.ds(c0 // LANES + j, tmb, stride=nt), :] = packed[:, j * LANES:(j + 1) * LANES]

    @pl.when(pl.program_id(0) >= nu_ref[0])
    def _():
        o_ref[...] = jnp.zeros_like(o_ref)


def _expert_down(act, w_down, blk_e, n_used, tmb, tn=512):
    total, ff = act.shape
    d = w_down.shape[2]
    nt = d // 2 // LANES
    nblk = total // tmb

    def blk(bi, nu):
        return jnp.minimum(bi, nu[0] - 1)

    return pl.pallas_call(
        functools.partial(_expert_down_kernel, nt=nt, tn=tn),
        grid_spec=pltpu.PrefetchScalarGridSpec(
            num_scalar_prefetch=2,
            grid=(nblk,),
            in_specs=[pl.BlockSpec((tmb, ff), lambda bi, be, nu: (blk(bi, nu), 0)),
                      pl.BlockSpec((1, ff, d), lambda bi, be, nu: (be[bi], 0, 0))],
            out_specs=pl.BlockSpec((tmb * nt, LANES), lambda bi, be, nu: (bi, 0))),
        out_shape=jax.ShapeDtypeStruct((total * nt, LANES), jnp.int32),
        compiler_params=_params("arbitrary"),
        name="expert_down",
    )(blk_e, n_used, act, w_down)


def _combine_kernel(pos_ref, nxt_ref, ys_ref, w_ref, x_ref, ga_ref, g_ref, o_ref, *scratch):
    n_buf = 2 * TOP_K
    slabs, sems = scratch[:n_buf], scratch[n_buf:]
    rows = x_ref.shape[1]
    hr = rows // 2
    nt = ys_ref.shape[1]
    half = nt * LANES
    step = pl.program_id(0) * pl.num_programs(1) + pl.program_id(1)
    n_steps = pl.num_programs(0) * pl.num_programs(1)

    def slab_copy(idx_ref, h, r, k):
        dst = slabs[h * TOP_K + k].at[pl.ds(pl.multiple_of(r * SLAB_PITCH, SUBLANES), nt)]
        return pltpu.make_async_copy(ys_ref.at[idx_ref[0, 0, (h * hr + r) * TOP_K + k]], dst, sems[h * TOP_K + k])

    def issue(idx_ref, h):
        def body(r, c):
            for k in range(TOP_K):
                slab_copy(idx_ref, h, r, k).start(priority=k % 2)
            return c
        lax.fori_loop(0, hr, body, 0, unroll=DMA_LOOP_UNROLL)

    def wait(idx_ref, h):
        def body(r, c):
            for k in range(TOP_K):
                slab_copy(idx_ref, h, r, k).wait()
            return c
        lax.fori_loop(0, hr, body, 0, unroll=DMA_LOOP_UNROLL)

    def compute(h):
        r0 = h * hr
        w0 = w_ref[r0:r0 + hr, 0:1]
        w1 = w_ref[r0:r0 + hr, 1:2]
        ss = jnp.zeros((hr, 1), F32)
        for j in range(nt):
            lo0, hi0 = _unpack_halves(slabs[h * TOP_K][pl.ds(j, hr, stride=SLAB_PITCH), :])
            lo1, hi1 = _unpack_halves(slabs[h * TOP_K + 1][pl.ds(j, hr, stride=SLAB_PITCH), :])
            for c0, y in ((j * LANES, w0 * lo0 + w1 * lo1), (half + j * LANES, w0 * hi0 + w1 * hi1)):
                z = x_ref[0, r0:r0 + hr, c0:c0 + LANES] + ga_ref[0, :, c0:c0 + LANES] * y
                ss = ss + jnp.sum(z * z, axis=-1, keepdims=True)
                o_ref[0, r0:r0 + hr, c0:c0 + LANES] = z
        inv = lax.rsqrt(ss / (2 * half) + EPS)
        o_ref[0, r0:r0 + hr, :] = o_ref[0, r0:r0 + hr, :] * inv * g_ref[...]

    @pl.when(step == 0)
    def _():
        issue(pos_ref, 0)

    issue(pos_ref, 1)
    wait(pos_ref, 0)
    compute(0)

    @pl.when(step + 1 < n_steps)
    def _():
        issue(nxt_ref, 0)

    wait(pos_ref, 1)
    compute(1)


def _combine(ys3, pos, wts, x, gate, g, rows):
    b, l, d = x.shape
    lb = l // rows
    n_steps = b * lb
    pos3 = pos.reshape(n_steps, 1, rows * TOP_K)
    slab = pltpu.VMEM((rows // 2 * SLAB_PITCH, LANES), ys3.dtype)
    return pl.pallas_call(
        _combine_kernel,
        grid=(b, lb),
        in_specs=[pl.BlockSpec((1, 1, rows * TOP_K), lambda bi, i: (bi * lb + i, 0, 0), memory_space=pltpu.SMEM),
                  pl.BlockSpec((1, 1, rows * TOP_K), lambda bi, i: (jnp.minimum(bi * lb + i + 1, n_steps - 1), 0, 0),
                               memory_space=pltpu.SMEM),
                  pl.BlockSpec(memory_space=pl.ANY),
                  pl.BlockSpec((rows, LANES), lambda bi, i: (bi * lb + i, 0)),
                  pl.BlockSpec((1, rows, d), lambda bi, i: (bi, i, 0)),
                  pl.BlockSpec((1, 1, d), lambda bi, i: (bi, 0, 0)),
                  pl.BlockSpec((1, d), lambda bi, i: (0, 0))],
        out_specs=pl.BlockSpec((1, rows, d), lambda bi, i: (bi, i, 0)),
        out_shape=jax.ShapeDtypeStruct((b, l, d), F32),
        scratch_shapes=[slab] * (2 * TOP_K) + [pltpu.SemaphoreType.DMA(())] * (2 * TOP_K),
        compiler_params=_params("arbitrary", "arbitrary"),
        name="moe_combine_norm",
    )(pos3, pos3, ys3, wts, x, gate, g)


def _block_layout(counts, n_pairs, tmb):
    nblk = (n_pairs + N_EXPERTS * (tmb - 1) + tmb - 1) // tmb
    blocks = (counts + tmb - 1) // tmb
    bend = jnp.cumsum(blocks)
    pstart = (bend - blocks) * tmb
    n_used = bend[-1]
    blk_ids = jnp.minimum(jnp.arange(nblk, dtype=jnp.int32), n_used - 1)
    blk_e = jnp.minimum(jnp.searchsorted(bend, blk_ids, side="right"), N_EXPERTS - 1).astype(jnp.int32)
    last_blk = jnp.maximum(bend - 1, 0).astype(jnp.int32)
    return nblk, pstart.astype(jnp.int32), blk_e, last_blk, n_used.astype(jnp.int32).reshape(1)


def _rope_dim_order(t):
    quarter = ROPE_AXIS_DIM // 2
    lead = t.shape[:-1]
    t = t.reshape(*lead, -1, 2, 2, quarter)
    return jnp.swapaxes(t, -3, -2).reshape(*lead, -1)


def _rope_tables(n_tokens):
    rows = n_tokens // GRID_W
    row, col = jnp.meshgrid(jnp.arange(rows), jnp.arange(GRID_W), indexing="ij")
    pos = jnp.stack([row.reshape(-1), col.reshape(-1)], axis=-1).astype(F32)
    inv = ROPE_THETA ** (-jnp.arange(0, ROPE_AXIS_DIM, 2, dtype=F32) / ROPE_AXIS_DIM)
    ang = pos[:, :, None] * inv[None, None, :]
    cos, sin = jnp.cos(ang), jnp.sin(ang)
    cos_t = jnp.concatenate([cos[:, 0], cos[:, 1], cos[:, 0], cos[:, 1]], axis=-1)
    sin_t = jnp.concatenate([-sin[:, 0], -sin[:, 1], sin[:, 0], sin[:, 1]], axis=-1)
    return cos_t, sin_t


def kernel(x, c, ctx, c_ctx, norm1_g, w_mod, b_mod, w_in, q_norm_g, k_norm_g, w_attn_out, conv_dw_w, conv_dw_b, conv_ln_g, conv_ln_b, w_conv_out, w_out, norm2_g, w_router_group, b_router_group, w_router_expert, b_router_expert, w_exp_gate, w_exp_up, w_exp_down, norm_f_g):
    b, s, d = x.shape
    n_ctx = ctx.shape[1]
    assert w_in.shape[0] == 1, "single-layer stack"
    conv_width = conv_dw_w.shape[-1]
    k_off = ATTN_WIDTH
    glu_off = k_off + 2 * KV_WIDTH
    gate_off = glu_off + 2 * conv_width

    n_c = b + 1
    cvec = jnp.zeros((SUBLANES * ((n_c + SUBLANES - 1) // SUBLANES), d), F32).at[:b].set(c).at[b].set(c_ctx)
    mod = _mod_vectors(cvec, w_mod[0], b_mod.reshape(1, -1))
    sh1, sc1, ga1, sh2, sc2, ga2 = [mod[:b, i * d:(i + 1) * d].reshape(b, 1, d) for i in range(N_MOD)]
    csh1, csc1 = [mod[b:b + 1, i * d:(i + 1) * d].reshape(1, 1, d) for i in range(2)]

    g1 = norm1_g.reshape(1, d)
    h = _norm_mod(x, g1, sh1, sc1, tl=512)
    hc = _norm_mod(ctx, g1, csh1, csc1, tl=n_ctx)
    w_in_b = w_in[0].astype(BF16)
    w_qk = _rope_dim_order(w_in[0, :, :k_off + KV_WIDTH]).astype(BF16)
    cos_t, sin_t = _rope_tables(s)
    qg = _rope_dim_order(q_norm_g.reshape(1, HEAD_DIM))
    kg = _rope_dim_order(k_norm_g.reshape(1, HEAD_DIM))
    q = _q_proj(h, w_qk, qg, cos_t, sin_t, 0, HEAD_DIM ** -0.5 * LOG2E, tm=1024)
    k, v = _kv_proj(h, w_qk, w_in_b, kg, cos_t, sin_t, k_off, tm=1024)
    kc, vc = _kv_proj(hc, w_qk, w_in_b, kg, None, None, k_off, tm=n_ctx)
    attn = _attention(q, k, v, kc, vc, tq=256, tqs=128, tk=1024, rb=8)

    h2d = h.reshape(b * s, d)
    u = _glu_proj(h2d, w_in_b, glu_off, conv_width, tm=1024)
    conv = _conv_module(u.reshape(b, s, conv_width), conv_dw_w.reshape(CONV_TAPS, conv_width // LANES, LANES),
                        conv_dw_b.reshape(conv_width // LANES, LANES), conv_ln_g.reshape(1, -1),
                        conv_ln_b.reshape(1, -1), tl=256)
    mrg = _merge(h2d, attn.reshape(b * s, ATTN_WIDTH), conv.reshape(b * s, conv_width), w_in_b, gate_off,
                 w_attn_out[0].astype(BF16), w_conv_out[0].astype(BF16))
    x1 = _out_proj(mrg.reshape(b, s, d), w_out[0].astype(BF16), x, ga1)

    w_r = jnp.zeros((d, ROUTER_LANES), F32).at[:, :N_GROUPS].set(w_router_group[0]) \
        .at[:, N_GROUPS:N_GROUPS + N_EXPERTS].set(w_router_expert[0])
    b_r = jnp.zeros((1, ROUTER_LANES), F32).at[0, :N_GROUPS].set(b_router_group[0]) \
        .at[0, N_GROUPS:N_GROUPS + N_EXPERTS].set(b_router_expert[0])
    hp, logits = _norm2_router(x1, norm2_g.reshape(1, d), sh2, sc2, w_r.astype(BF16), b_r, tl=256)
    n = b * s
    nt = d // 2 // LANES
    tmb = 512
    meta, wts, cnt = _route(logits.reshape(n, ROUTER_LANES))
    counts = cnt[0, N_GROUPS:N_GROUPS + N_EXPERTS].astype(jnp.int32)
    nblk, pstart, blk_e, last_blk, n_used = _block_layout(counts, n * TOP_K, tmb)
    pst = jnp.zeros((1, LANES), jnp.int32).at[0, N_GROUPS:N_GROUPS + N_EXPERTS].set(pstart)
    dest = _dest_rows(meta, pst)[:, :TOP_K].reshape(-1)
    xs0 = _zero_tails(last_blk, nblk * tmb, tmb, nt)
    xs = _dispatch(hp.reshape(n, nt, LANES), dest, xs0.reshape(nblk * tmb, nt, LANES), rows=256)
    xs2 = xs.reshape(nblk * tmb * nt, LANES)
    sg = _expert_in(xs2, w_exp_gate[0], None, blk_e, n_used, tmb, "expert_gate")
    act = _expert_in(xs2, w_exp_up[0], sg, blk_e, n_used, tmb, "expert_up")
    ys = _expert_down(act, w_exp_down[0], blk_e, n_used, tmb)
    return _combine(ys.reshape(nblk * tmb, nt, LANES), dest, wts, x1, ga2, norm_f_g.reshape(1, d), rows=256)
```

```python
import functools

import jax
import jax.numpy as jnp
from jax import lax
from jax.experimental import pallas as pl
from jax.experimental.pallas import tpu as pltpu

F32 = jnp.float32
BF16 = jnp.bfloat16

GRID_W = 64
HEAD_DIM = 128
N_Q_HEADS = 16
N_KV_HEADS = 4
Q_PER_KV = N_Q_HEADS // N_KV_HEADS
ATTN_WIDTH = N_Q_HEADS * HEAD_DIM
KV_WIDTH = N_KV_HEADS * HEAD_DIM
CONV_TAPS = 31
CONV_HALO = 16
ROPE_THETA = 10000.0
ROPE_AXIS_DIM = HEAD_DIM // 2
N_GROUPS = 4
EXPERTS_PER_GROUP = 8
N_EXPERTS = N_GROUPS * EXPERTS_PER_GROUP
TOP_K = 2
N_MOD = 6
EPS = 1e-6
LOG2E = 1.4426950408889634
LANES = 128
SUBLANES = 8
ROUTER_LANES = LANES
SLAB_PITCH = 24
DMA_LOOP_UNROLL = 4

V7X_VMEM_LIMIT = 56 * 1024 * 1024


def _params(*sem):
    return pltpu.CompilerParams(dimension_semantics=sem, vmem_limit_bytes=V7X_VMEM_LIMIT)


def _sigmoid(x):
    return 1.0 / (1.0 + jnp.exp(-x))


def _silu(x):
    return x * _sigmoid(x)


def _rms(x, g):
    return x * lax.rsqrt(jnp.mean(x * x, axis=-1, keepdims=True) + EPS) * g


def _mod_kernel(c_ref, w_ref, b_ref, o_ref):
    s = _silu(c_ref[...]).astype(BF16)
    o_ref[...] = jnp.dot(s, w_ref[...].astype(BF16), preferred_element_type=F32) + b_ref[...]


def _mod_vectors(cvec, w_mod, b_mod, tn=512):
    m, d = cvec.shape
    n = w_mod.shape[1]
    return pl.pallas_call(
        _mod_kernel,
        grid=(n // tn,),
        in_specs=[pl.BlockSpec((m, d), lambda j: (0, 0)),
                  pl.BlockSpec((d, tn), lambda j: (0, j)),
                  pl.BlockSpec((1, tn), lambda j: (0, j))],
        out_specs=pl.BlockSpec((m, tn), lambda j: (0, j)),
        out_shape=jax.ShapeDtypeStruct((m, n), F32),
        compiler_params=_params("arbitrary"),
        name="mod_vectors",
    )(cvec, w_mod, b_mod)


def _norm_mod_kernel(x_ref, g_ref, sh_ref, sc_ref, o_ref):
    y = _rms(x_ref[0], g_ref[...])
    o_ref[0] = (y * (1.0 + sc_ref[0]) + sh_ref[0]).astype(o_ref.dtype)


def _norm_mod(x, g, shift, scale, tl):
    b, l, d = x.shape
    per_batch = shift.shape[0] > 1
    mod_map = (lambda bi, li: (bi, 0, 0)) if per_batch else (lambda bi, li: (0, 0, 0))
    return pl.pallas_call(
        _norm_mod_kernel,
        grid=(b, l // tl),
        in_specs=[pl.BlockSpec((1, tl, d), lambda bi, li: (bi, li, 0)),
                  pl.BlockSpec((1, d), lambda bi, li: (0, 0)),
                  pl.BlockSpec((1, 1, d), mod_map),
                  pl.BlockSpec((1, 1, d), mod_map)],
        out_specs=pl.BlockSpec((1, tl, d), lambda bi, li: (bi, li, 0)),
        out_shape=jax.ShapeDtypeStruct((b, l, d), BF16),
        compiler_params=_params("parallel", "parallel"),
        name="norm_modulate",
    )(x, g, shift, scale)


def _head_norm_rope(a, g, cos, sin):
    y = _rms(a, g)
    if cos is None:
        return y
    return y * cos + pltpu.roll(y, HEAD_DIM // 2, 1) * sin


def _q_proj_kernel(h_ref, w_ref, g_ref, cos_ref, sin_ref, o_ref, *, scale):
    acc = jnp.dot(h_ref[0], w_ref[...], preferred_element_type=F32)
    for hh in range(o_ref.shape[1]):
        a = acc[:, hh * HEAD_DIM:(hh + 1) * HEAD_DIM]
        y = _head_norm_rope(a, g_ref[...], cos_ref[...], sin_ref[...])
        o_ref[0, hh] = (y * scale).astype(o_ref.dtype)


def _q_proj(h, w, g, cos_t, sin_t, col_off, scale, tm, tn=1024):
    b, l, d = h.shape
    jb = col_off // tn
    hpt = tn // HEAD_DIM
    return pl.pallas_call(
        functools.partial(_q_proj_kernel, scale=scale),
        grid=(b, l // tm, ATTN_WIDTH // tn),
        in_specs=[pl.BlockSpec((1, tm, d), lambda bi, i, j: (bi, i, 0)),
                  pl.BlockSpec((d, tn), lambda bi, i, j: (0, jb + j)),
                  pl.BlockSpec((1, HEAD_DIM), lambda bi, i, j: (0, 0)),
                  pl.BlockSpec((tm, HEAD_DIM), lambda bi, i, j: (i, 0)),
                  pl.BlockSpec((tm, HEAD_DIM), lambda bi, i, j: (i, 0))],
        out_specs=pl.BlockSpec((1, hpt, tm, HEAD_DIM), lambda bi, i, j: (bi, j, i, 0)),
        out_shape=jax.ShapeDtypeStruct((b, N_Q_HEADS, l, HEAD_DIM), BF16),
        compiler_params=_params("parallel", "parallel", "arbitrary"),
        name="q_proj",
    )(h, w, g, cos_t, sin_t)


def _kv_proj_kernel(h_ref, wk_ref, wv_ref, g_ref, *rest, rope):
    if rope:
        cos_ref, sin_ref, k_ref, v_ref = rest
        cos, sin = cos_ref[...], sin_ref[...]
    else:
        k_ref, v_ref = rest
        cos = sin = None
    h = h_ref[0]
    acc = jnp.dot(h, wk_ref[...], preferred_element_type=F32)
    for hh in range(N_KV_HEADS):
        a = acc[:, hh * HEAD_DIM:(hh + 1) * HEAD_DIM]
        k_ref[0, :, hh * HEAD_DIM:(hh + 1) * HEAD_DIM] = _head_norm_rope(a, g_ref[...], cos, sin).astype(k_ref.dtype)
    v_ref[0] = jnp.dot(h, wv_ref[...], preferred_element_type=F32).astype(v_ref.dtype)


def _kv_proj(h, w_qk, w, g, cos_t, sin_t, k_off, tm):
    b, l, d = h.shape
    rope = cos_t is not None
    jk = k_off // KV_WIDTH
    in_specs = [pl.BlockSpec((1, tm, d), lambda bi, i: (bi, i, 0)),
                pl.BlockSpec((d, KV_WIDTH), lambda bi, i: (0, jk)),
                pl.BlockSpec((d, KV_WIDTH), lambda bi, i: (0, jk + 1)),
                pl.BlockSpec((1, HEAD_DIM), lambda bi, i: (0, 0))]
    args = [h, w_qk, w, g]
    if rope:
        in_specs += [pl.BlockSpec((tm, HEAD_DIM), lambda bi, i: (i, 0))] * 2
        args += [cos_t, sin_t]
    return pl.pallas_call(
        functools.partial(_kv_proj_kernel, rope=rope),
        grid=(b, l // tm),
        in_specs=in_specs,
        out_specs=[pl.BlockSpec((1, tm, KV_WIDTH), lambda bi, i: (bi, i, 0))] * 2,
        out_shape=[jax.ShapeDtypeStruct((b, l, KV_WIDTH), BF16)] * 2,
        compiler_params=_params("parallel", "parallel"),
        name="kv_proj_rope" if rope else "kv_proj_ctx",
    )(*args)


def _glu_proj_kernel(h_ref, wa_ref, wg_ref, o_ref):
    a = jnp.dot(h_ref[...], wa_ref[...], preferred_element_type=F32)
    gt = jnp.dot(h_ref[...], wg_ref[...], preferred_element_type=F32)
    o_ref[...] = (a * _sigmoid(gt)).astype(o_ref.dtype)


def _glu_proj(h2d, w, col_off, width, tm, tn=512):
    m, d = h2d.shape
    ja = col_off // tn
    jg = (col_off + width) // tn
    return pl.pallas_call(
        _glu_proj_kernel,
        grid=(m // tm, width // tn),
        in_specs=[pl.BlockSpec((tm, d), lambda i, j: (i, 0)),
                  pl.BlockSpec((d, tn), lambda i, j: (0, ja + j)),
                  pl.BlockSpec((d, tn), lambda i, j: (0, jg + j))],
        out_specs=pl.BlockSpec((tm, tn), lambda i, j: (i, j)),
        out_shape=jax.ShapeDtypeStruct((m, width), BF16),
        compiler_params=_params("parallel", "arbitrary"),
        name="glu_proj",
    )(h2d, w, w)


def _attn_kernel(q_ref, k_ref, v_ref, kc_ref, vc_ref, o_ref,
                 s0_ref, s1_ref, p0_ref, p1_ref, m_ref, al_ref, acc_ref, *, tk, rb):
    g, tq, dh = q_ref.shape[1:]
    rows = s0_ref.shape[0]
    tqs = rows // g
    chunks = [(k_ref, v_ref, c * tk, tk) for c in range(k_ref.shape[1] // tk)]
    chunks.append((kc_ref, vc_ref, 0, kc_ref.shape[1]))
    units = [(sb, j) for sb in range(tq // tqs) for j in range(len(chunks))]
    s_refs = (s0_ref, s1_ref)
    p_refs = (p0_ref, p1_ref)

    def scores(u):
        sb, j = units[u]
        kr, _, st, n = chunks[j]
        q = q_ref[0, :, sb * tqs:(sb + 1) * tqs, :].reshape(rows, dh)
        s_refs[u % 2][:, :n] = lax.dot_general(q, kr[0, st:st + n, :], (((1,), (1,)), ((), ())),
                                               preferred_element_type=F32)

    def softmax(u):
        j = units[u][1]
        n = chunks[j][3]
        s_ref, p_ref = s_refs[u % 2], p_refs[u % 2]
        for r0 in range(0, rows, rb):
            sblk = s_ref[r0:r0 + rb, :n]
            mn = jnp.max(sblk, axis=-1, keepdims=True)
            if j > 0:
                mo = m_ref[r0:r0 + rb, :]
                mn = jnp.maximum(mo, mn)
                al_ref[r0:r0 + rb, :] = jnp.exp2(mo - mn)
            m_ref[r0:r0 + rb, :] = mn
            p_ref[r0:r0 + rb, :n] = jnp.exp2(sblk - mn).astype(BF16)

    def weighted_values(u):
        j = units[u][1]
        _, vr, st, n = chunks[j]
        ones_col = (lax.broadcasted_iota(jnp.int32, (n, dh), 1) == 0).astype(BF16)
        v1 = jnp.concatenate([vr[0, st:st + n, :], ones_col], axis=1)
        upd = jnp.dot(p_refs[u % 2][:, :n], v1, preferred_element_type=F32)
        if j == 0:
            acc_ref[...] = upd
        else:
            acc_ref[...] = al_ref[...] * acc_ref[...] + upd

    def finish(sb):
        acc = acc_ref[...]
        o = acc[:, :dh] / acc[:, dh:dh + 1]
        for gi in range(g):
            o_ref[0, sb * tqs:(sb + 1) * tqs, gi * dh:(gi + 1) * dh] = o[gi * tqs:(gi + 1) * tqs].astype(o_ref.dtype)

    scores(0)
    for u, (sb, j) in enumerate(units):
        if u + 1 < len(units):
            scores(u + 1)
        softmax(u)
        weighted_values(u)
        if j == len(chunks) - 1:
            finish(sb)


def _attention(q, k, v, kc, vc, tq, tqs, tk, rb=16):
    b, _, l, dh = q.shape
    lc = kc.shape[1]
    rows = Q_PER_KV * tqs
    gdh = Q_PER_KV * dh
    return pl.pallas_call(
        functools.partial(_attn_kernel, tk=tk, rb=rb),
        grid=(b, N_KV_HEADS, l // tq),
        in_specs=[pl.BlockSpec((1, Q_PER_KV, tq, dh), lambda bi, kh, qi: (bi, kh, qi, 0)),
                  pl.BlockSpec((1, l, dh), lambda bi, kh, qi: (bi, 0, kh)),
                  pl.BlockSpec((1, l, dh), lambda bi, kh, qi: (bi, 0, kh)),
                  pl.BlockSpec((1, lc, dh), lambda bi, kh, qi: (bi, 0, kh)),
                  pl.BlockSpec((1, lc, dh), lambda bi, kh, qi: (bi, 0, kh))],
        out_specs=pl.BlockSpec((1, tq, gdh), lambda bi, kh, qi: (bi, qi, kh)),
        out_shape=jax.ShapeDtypeStruct((b, l, ATTN_WIDTH), BF16),
        scratch_shapes=[pltpu.VMEM((rows, tk), F32), pltpu.VMEM((rows, tk), F32),
                        pltpu.VMEM((rows, tk), BF16), pltpu.VMEM((rows, tk), BF16),
                        pltpu.VMEM((rows, 1), F32), pltpu.VMEM((rows, 1), F32),
                        pltpu.VMEM((rows, 2 * dh), F32)],
        compiler_params=_params("parallel", "parallel", "arbitrary"),
        name="attention",
    )(q, k, v, kc, vc)


def _conv_kernel(prev_ref, cur_ref, next_ref, w_ref, b_ref, g_ref, beta_ref, o_ref, win_ref, y_ref, *, tc, rc):
    li = pl.program_id(1)
    tl, c = cur_ref.shape[1:]
    halo = prev_ref.shape[1]
    nt = c // LANES

    def put_tokens(vals, tok0):
        for j in range(nt):
            win_ref[pl.ds(tok0 * nt + j, vals.shape[0], stride=nt), :] = vals[:, j * LANES:(j + 1) * LANES]

    def put_chunk(ci, carry):
        r0 = pl.multiple_of(ci * rc, rc)
        put_tokens(cur_ref[0, pl.ds(r0, rc), :].astype(F32), halo + r0)
        return carry

    put_tokens(jnp.where(li > 0, prev_ref[0].astype(F32), 0.0), 0)
    lax.fori_loop(0, tl // rc, put_chunk, 0)
    put_tokens(jnp.where(li < pl.num_programs(1) - 1, next_ref[0].astype(F32), 0.0), halo + tl)

    first = halo - CONV_TAPS // 2
    bias = b_ref[...][None]

    def token_chunk(ci, carry):
        tok = ci * tc
        acc = jnp.zeros((tc, nt, LANES), F32) + bias
        for t in range(CONV_TAPS):
            r0 = pl.multiple_of((tok + first + t) * nt, nt)
            acc = acc + win_ref[pl.ds(r0, tc * nt), :].reshape(tc, nt, LANES) * w_ref[t][None]
        y_ref[pl.ds(pl.multiple_of(tok * nt, nt), tc * nt), :] = acc.reshape(tc * nt, LANES)
        return carry

    lax.fori_loop(0, tl // tc, token_chunk, 0)

    def norm_chunk(ci, carry):
        r0 = pl.multiple_of(ci * rc, rc)
        y = jnp.concatenate([y_ref[pl.ds(r0 * nt + j, rc, stride=nt), :] for j in range(nt)], axis=1)
        mu = jnp.mean(y, axis=-1, keepdims=True)
        yc = y - mu
        var = jnp.mean(yc * yc, axis=-1, keepdims=True)
        z = yc * lax.rsqrt(var + EPS) * g_ref[...] + beta_ref[...]
        o_ref[0, pl.ds(r0, rc), :] = _silu(z).astype(o_ref.dtype)
        return carry

    lax.fori_loop(0, tl // rc, norm_chunk, 0, unroll=2)


def _conv_module(u, w_dw, b_dw, ln_g, ln_b, tl, tc=16, rc=32):
    b, l, c = u.shape
    nt = c // LANES
    hb = tl // CONV_HALO
    n_halo = l // CONV_HALO
    return pl.pallas_call(
        functools.partial(_conv_kernel, tc=tc, rc=rc),
        grid=(b, l // tl),
        in_specs=[pl.BlockSpec((1, CONV_HALO, c), lambda bi, li: (bi, jnp.maximum(li * hb - 1, 0), 0)),
                  pl.BlockSpec((1, tl, c), lambda bi, li: (bi, li, 0)),
                  pl.BlockSpec((1, CONV_HALO, c), lambda bi, li: (bi, jnp.minimum((li + 1) * hb, n_halo - 1), 0)),
                  pl.BlockSpec((CONV_TAPS, nt, LANES), lambda bi, li: (0, 0, 0)),
                  pl.BlockSpec((nt, LANES), lambda bi, li: (0, 0)),
                  pl.BlockSpec((1, c), lambda bi, li: (0, 0)),
                  pl.BlockSpec((1, c), lambda bi, li: (0, 0))],
        out_specs=pl.BlockSpec((1, tl, c), lambda bi, li: (bi, li, 0)),
        out_shape=jax.ShapeDtypeStruct((b, l, c), BF16),
        scratch_shapes=[pltpu.VMEM(((tl + 2 * CONV_HALO) * nt, LANES), F32), pltpu.VMEM((tl * nt, LANES), F32)],
        compiler_params=_params("parallel", "arbitrary"),
        name="conv_module",
    )(u, u, u, w_dw, b_dw, ln_g, ln_b)


def _merge_kernel(h_ref, a_ref, c_ref, wga_ref, wgc_ref, wa_ref, wc_ref, o_ref):
    h = h_ref[...]
    g_a = _sigmoid(jnp.dot(h, wga_ref[...], preferred_element_type=F32))
    g_c = _sigmoid(jnp.dot(h, wgc_ref[...], preferred_element_type=F32))
    a = jnp.dot(a_ref[...], wa_ref[...].astype(BF16), preferred_element_type=F32)
    cb = jnp.dot(c_ref[...], wc_ref[...].astype(BF16), preferred_element_type=F32)
    o_ref[...] = (g_a * a + g_c * cb).astype(o_ref.dtype)


def _merge(h2d, attn, conv, w_in, gate_off, wa, wc, tm=512, tn=512):
    m, d = h2d.shape
    ka = attn.shape[1]
    kc = conv.shape[1]
    nj = d // tn
    ja = gate_off // tn
    jc = (gate_off + d) // tn
    return pl.pallas_call(
        _merge_kernel,
        grid=(nj, m // tm),
        in_specs=[pl.BlockSpec((tm, d), lambda j, i: (i, 0)),
                  pl.BlockSpec((tm, ka), lambda j, i: (i, 0)),
                  pl.BlockSpec((tm, kc), lambda j, i: (i, 0)),
                  pl.BlockSpec((d, tn), lambda j, i: (0, ja + j)),
                  pl.BlockSpec((d, tn), lambda j, i: (0, jc + j)),
                  pl.BlockSpec((ka, tn), lambda j, i: (0, j)),
                  pl.BlockSpec((kc, tn), lambda j, i: (0, j))],
        out_specs=pl.BlockSpec((tm, tn), lambda j, i: (i, j)),
        out_shape=jax.ShapeDtypeStruct((m, d), BF16),
        compiler_params=_params("parallel", "arbitrary"),
        name="merge_branches",
    )(h2d, attn, conv, w_in, w_in, wa, wc)


def _out_proj_kernel(m_ref, w_ref, x_ref, ga_ref, o_ref):
    acc = jnp.dot(m_ref[0], w_ref[...], preferred_element_type=F32)
    o_ref[0] = x_ref[0] + ga_ref[0] * acc


def _out_proj(mrg, w, x, gate, tm=1024, tn=1024):
    b, l, d = x.shape
    return pl.pallas_call(
        _out_proj_kernel,
        grid=(b, l // tm, d // tn),
        in_specs=[pl.BlockSpec((1, tm, d), lambda bi, i, j: (bi, i, 0)),
                  pl.BlockSpec((d, tn), lambda bi, i, j: (0, j)),
                  pl.BlockSpec((1, tm, tn), lambda bi, i, j: (bi, i, j)),
                  pl.BlockSpec((1, 1, tn), lambda bi, i, j: (bi, 0, j))],
        out_specs=pl.BlockSpec((1, tm, tn), lambda bi, i, j: (bi, i, j)),
        out_shape=jax.ShapeDtypeStruct((b, l, d), F32),
        compiler_params=_params("parallel", "parallel", "arbitrary"),
        name="out_proj_residual",
    )(mrg, w, x, gate)


def _pack_halves(y):
    n = y.shape[1] // 2
    return pltpu.pack_elementwise([y[:, :n], y[:, n:]], packed_dtype=BF16)


def _unpack_halves(p):
    lo = pltpu.unpack_elementwise(p, index=0, packed_dtype=BF16, unpacked_dtype=F32)
    hi = pltpu.unpack_elementwise(p, index=1, packed_dtype=BF16, unpacked_dtype=F32)
    return lo, hi


def _norm2_router_kernel(x_ref, g_ref, sh_ref, sc_ref, wr_ref, br_ref, hp_ref, lg_ref):
    y = _rms(x_ref[0], g_ref[...]) * (1.0 + sc_ref[0]) + sh_ref[0]
    packed = _pack_halves(y)
    tl = packed.shape[0]
    nt = packed.shape[1] // LANES
    for j in range(nt):
        hp_ref[pl.ds(j, tl, stride=nt), :] = packed[:, j * LANES:(j + 1) * LANES]
    lg_ref[0] = jnp.dot(y.astype(BF16), wr_ref[...], preferred_element_type=F32) + br_ref[...]


def _norm2_router(x, g, shift, scale, w_r, b_r, tl):
    b, l, d = x.shape
    nt = d // 2 // LANES
    lb = l // tl
    return pl.pallas_call(
        _norm2_router_kernel,
        grid=(b, lb),
        in_specs=[pl.BlockSpec((1, tl, d), lambda bi, li: (bi, li, 0)),
                  pl.BlockSpec((1, d), lambda bi, li: (0, 0)),
                  pl.BlockSpec((1, 1, d), lambda bi, li: (bi, 0, 0)),
                  pl.BlockSpec((1, 1, d), lambda bi, li: (bi, 0, 0)),
                  pl.BlockSpec((d, ROUTER_LANES), lambda bi, li: (0, 0)),
                  pl.BlockSpec((1, ROUTER_LANES), lambda bi, li: (0, 0))],
        out_specs=[pl.BlockSpec((tl * nt, LANES), lambda bi, li: (bi * lb + li, 0)),
                   pl.BlockSpec((1, tl, ROUTER_LANES), lambda bi, li: (bi, li, 0))],
        out_shape=[jax.ShapeDtypeStruct((b * l * nt, LANES), jnp.int32),
                   jax.ShapeDtypeStruct((b, l, ROUTER_LANES), F32)],
        compiler_params=_params("parallel", "parallel"),
        name="norm2_router",
    )(x, g, shift, scale, w_r, b_r)


def _first_lane(mask, lane):
    return jnp.min(jnp.where(mask, lane, LANES), axis=-1, keepdims=True)


def _route_kernel(lg_ref, meta_ref, wts_ref, cnt_ref, carry_ref):
    @pl.when(pl.program_id(0) == 0)
    def _():
        carry_ref[...] = jnp.zeros_like(carry_ref)

    lg = lg_ref[...]
    tb = lg.shape[0]
    lane = lax.broadcasted_iota(jnp.int32, lg.shape, 1)
    neg_inf = jnp.float32(-jnp.inf)
    is_group = lane < N_GROUPS
    gl = jnp.where(is_group, lg, neg_inf)
    g_max = jnp.max(gl, axis=-1, keepdims=True)
    g_sel = _first_lane(gl == g_max, lane)
    p_g = 1.0 / jnp.sum(jnp.where(is_group, jnp.exp(lg - g_max), 0.0), axis=-1, keepdims=True)

    e_idx = lane - N_GROUPS
    in_group = (e_idx >= g_sel * EXPERTS_PER_GROUP) & (e_idx < (g_sel + 1) * EXPERTS_PER_GROUP)
    ev = jnp.where(in_group, lg, neg_inf)
    v1 = jnp.max(ev, axis=-1, keepdims=True)
    i1 = _first_lane(ev == v1, lane)
    ev2 = jnp.where(lane == i1, neg_inf, ev)
    v2 = jnp.max(ev2, axis=-1, keepdims=True)
    i2 = _first_lane(ev2 == v2, lane)
    t = jnp.exp(v2 - v1)
    w1 = p_g / (1.0 + t)
    w2 = w1 * t

    oh1 = lane == i1
    oh2 = lane == i2
    oh = (oh1 | oh2).astype(BF16)
    earlier = (lax.broadcasted_iota(jnp.int32, (tb, tb), 0) > lax.broadcasted_iota(jnp.int32, (tb, tb), 1)).astype(BF16)
    before = jnp.dot(earlier, oh, preferred_element_type=F32) + carry_ref[...]
    r1 = jnp.sum(jnp.where(oh1, before, 0.0), axis=-1, keepdims=True).astype(jnp.int32)
    r2 = jnp.sum(jnp.where(oh2, before, 0.0), axis=-1, keepdims=True).astype(jnp.int32)
    carry_ref[...] += jnp.sum(oh.astype(F32), axis=0, keepdims=True)

    meta_ref[...] = jnp.where(lane == 0, i1 - N_GROUPS, jnp.where(lane == 1, i2 - N_GROUPS,
                              jnp.where(lane == 2, r1, jnp.where(lane == 3, r2, 0))))
    wts_ref[...] = jnp.where(lane == 0, w1, jnp.where(lane == 1, w2, 0.0))
    cnt_ref[...] = carry_ref[...]


def _route(logits, tb=512):
    n = logits.shape[0]
    return pl.pallas_call(
        _route_kernel,
        grid=(n // tb,),
        in_specs=[pl.BlockSpec((tb, LANES), lambda i: (i, 0))],
        out_specs=[pl.BlockSpec((tb, LANES), lambda i: (i, 0)),
                   pl.BlockSpec((tb, LANES), lambda i: (i, 0)),
                   pl.BlockSpec((1, LANES), lambda i: (0, 0))],
        out_shape=[jax.ShapeDtypeStruct((n, LANES), jnp.int32),
                   jax.ShapeDtypeStruct((n, LANES), F32),
                   jax.ShapeDtypeStruct((1, LANES), F32)],
        scratch_shapes=[pltpu.VMEM((1, LANES), F32)],
        compiler_params=_params("arbitrary"),
        name="moe_route",
    )(logits)


def _dest_kernel(meta_ref, pst_ref, o_ref):
    meta = meta_ref[...]
    lane = lax.broadcasted_iota(jnp.int32, meta.shape, 1)
    pst = pst_ref[...]

    def row_of(slot):
        e = meta[:, slot:slot + 1]
        start = jnp.sum(jnp.where(lane == e + N_GROUPS, pst, 0), axis=-1, keepdims=True)
        return start + meta[:, TOP_K + slot:TOP_K + slot + 1]

    o_ref[...] = jnp.where(lane == 0, row_of(0), jnp.where(lane == 1, row_of(1), 0))


def _dest_rows(meta, pst, tb=512):
    n = meta.shape[0]
    return pl.pallas_call(
        _dest_kernel,
        grid=(n // tb,),
        in_specs=[pl.BlockSpec((tb, LANES), lambda i: (i, 0)),
                  pl.BlockSpec((1, LANES), lambda i: (0, 0))],
        out_specs=pl.BlockSpec((tb, LANES), lambda i: (i, 0)),
        out_shape=jax.ShapeDtypeStruct((n, LANES), jnp.int32),
        compiler_params=_params("parallel"),
        name="moe_dest_rows",
    )(meta, pst)


def _zero_tail_kernel(lb_ref, o_ref):
    o_ref[...] = jnp.zeros_like(o_ref)


def _zero_tails(last_blk, total, tmb, nt):
    return pl.pallas_call(
        _zero_tail_kernel,
        grid_spec=pltpu.PrefetchScalarGridSpec(
            num_scalar_prefetch=1,
            grid=(last_blk.shape[0],),
            in_specs=[],
            out_specs=pl.BlockSpec((tmb * nt, LANES), lambda e, lb: (lb[e], 0))),
        out_shape=jax.ShapeDtypeStruct((total * nt, LANES), jnp.int32),
        compiler_params=_params("arbitrary"),
        name="moe_zero_tails",
    )(last_blk)


def _dispatch_kernel(dest_ref, hp_ref, xs_in_ref, xs_ref, sem):
    rows = hp_ref.shape[0]

    def slab_copy(r, k):
        return pltpu.make_async_copy(hp_ref.at[r], xs_ref.at[dest_ref[0, 0, r * TOP_K + k]], sem)

    def start(r, c):
        for k in range(TOP_K):
            slab_copy(r, k).start(priority=k % 2)
        return c

    def wait(r, c):
        for k in range(TOP_K):
            slab_copy(r, k).wait()
        return c

    lax.fori_loop(0, rows, start, 0, unroll=DMA_LOOP_UNROLL)
    lax.fori_loop(0, rows, wait, 0, unroll=DMA_LOOP_UNROLL)


def _dispatch(hp3, dest, xs0, rows):
    n, nt, _ = hp3.shape
    steps = n // rows
    return pl.pallas_call(
        _dispatch_kernel,
        grid=(steps,),
        in_specs=[pl.BlockSpec((1, 1, rows * TOP_K), lambda i: (i, 0, 0), memory_space=pltpu.SMEM),
                  pl.BlockSpec((rows, nt, LANES), lambda i: (i, 0, 0)),
                  pl.BlockSpec(memory_space=pl.ANY)],
        out_specs=pl.BlockSpec(memory_space=pl.ANY),
        out_shape=jax.ShapeDtypeStruct(xs0.shape, xs0.dtype),
        scratch_shapes=[pltpu.SemaphoreType.DMA(())],
        input_output_aliases={2: 0},
        compiler_params=_params("arbitrary"),
        name="moe_dispatch",
    )(dest.reshape(steps, 1, rows * TOP_K), hp3, xs0)


def _expert_in_kernel(be_ref, nu_ref, x_ref, w_ref, *rest, nt, tn):
    gate_ref = rest[0] if len(rest) == 2 else None
    o_ref = rest[-1]

    @pl.when(pl.program_id(0) < nu_ref[0])
    def _():
        tmb = o_ref.shape[0]
        halves = [_unpack_halves(x_ref[pl.ds(j, tmb, stride=nt), :]) for j in range(nt)]
        lo = jnp.concatenate([h[0].astype(BF16) for h in halves], axis=1)
        hi = jnp.concatenate([h[1].astype(BF16) for h in halves], axis=1)
        half = nt * LANES
        for c0 in range(0, o_ref.shape[1], tn):
            y = (jnp.dot(lo, w_ref[0, :half, c0:c0 + tn].astype(BF16), preferred_element_type=F32)
                 + jnp.dot(hi, w_ref[0, half:, c0:c0 + tn].astype(BF16), preferred_element_type=F32))
            if gate_ref is None:
                y = _silu(y)
            else:
                y = gate_ref[:, c0:c0 + tn].astype(F32) * y
            o_ref[:, c0:c0 + tn] = y.astype(o_ref.dtype)

    @pl.when(pl.program_id(0) >= nu_ref[0])
    def _():
        o_ref[...] = jnp.zeros_like(o_ref)


def _expert_in(xs, w, gate, blk_e, n_used, tmb, name, tn=256):
    _, d, ff = w.shape
    nt = d // 2 // LANES
    total = xs.shape[0] // nt
    nblk = total // tmb

    def blk(bi, nu):
        return jnp.minimum(bi, nu[0] - 1)

    in_specs = [pl.BlockSpec((tmb * nt, LANES), lambda bi, be, nu: (blk(bi, nu), 0)),
                pl.BlockSpec((1, d, ff), lambda bi, be, nu: (be[bi], 0, 0))]
    args = [xs, w]
    if gate is not None:
        in_specs.append(pl.BlockSpec((tmb, ff), lambda bi, be, nu: (blk(bi, nu), 0)))
        args.append(gate)
    return pl.pallas_call(
        functools.partial(_expert_in_kernel, nt=nt, tn=tn),
        grid_spec=pltpu.PrefetchScalarGridSpec(
            num_scalar_prefetch=2,
            grid=(nblk,),
            in_specs=in_specs,
            out_specs=pl.BlockSpec((tmb, ff), lambda bi, be, nu: (bi, 0))),
        out_shape=jax.ShapeDtypeStruct((total, ff), BF16),
        compiler_params=_params("arbitrary"),
        name=name,
    )(blk_e, n_used, *args)


def _expert_down_kernel(be_ref, nu_ref, a_ref, w_ref, o_ref, *, nt, tn):
    @pl.when(pl.program_id(0) < nu_ref[0])
    def _():
        a = a_ref[...]
        tmb = a.shape[0]
        half = nt * LANES
        for c0 in range(0, half, tn):
            ylo = jnp.dot(a, w_ref[0, :, c0:c0 + tn].astype(BF16), preferred_element_type=F32)
            yhi = jnp.dot(a, w_ref[0, :, half + c0:half + c0 + tn].astype(BF16), preferred_element_type=F32)
            packed = pltpu.pack_elementwise([ylo, yhi], packed_dtype=BF16)
            for j in range(tn // LANES):
                o_ref[pl.ds(c0 // LANES + j, tmb, stride=nt), :] = packed[:, j * LANES:(j + 1) * LANES]

    @pl.when(pl.program_id(0) >= nu_ref[0])
    def _():
        o_ref[...] = jnp.zeros_like(o_ref)


def _expert_down(act, w_down, blk_e, n_used, tmb, tn=512):
    total, ff = act.shape
    d = w_down.shape[2]
    nt = d // 2 // LANES
    nblk = total // tmb

    def blk(bi, nu):
        return jnp.minimum(bi, nu[0] - 1)

    return pl.pallas_call(
        functools.partial(_expert_down_kernel, nt=nt, tn=tn),
        grid_spec=pltpu.PrefetchScalarGridSpec(
            num_scalar_prefetch=2,
            grid=(nblk,),
            in_specs=[pl.BlockSpec((tmb, ff), lambda bi, be, nu: (blk(bi, nu), 0)),
                      pl.BlockSpec((1, ff, d), lambda bi, be, nu: (be[bi], 0, 0))],
            out_specs=pl.BlockSpec((tmb * nt, LANES), lambda bi, be, nu: (bi, 0))),
        out_shape=jax.ShapeDtypeStruct((total * nt, LANES), jnp.int32),
        compiler_params=_params("arbitrary"),
        name="expert_down",
    )(blk_e, n_used, act, w_down)


def _combine_kernel(pos_ref, nxt_ref, ys_ref, w_ref, x_ref, ga_ref, g_ref, o_ref, *scratch):
    n_buf = 2 * TOP_K
    slabs, sems = scratch[:n_buf], scratch[n_buf:]
    rows = x_ref.shape[1]
    hr = rows // 2
    nt = ys_ref.shape[1]
    half = nt * LANES
    step = pl.program_id(0) * pl.num_programs(1) + pl.program_id(1)
    n_steps = pl.num_programs(0) * pl.num_programs(1)

    def slab_copy(idx_ref, h, r, k):
        dst = slabs[h * TOP_K + k].at[pl.ds(pl.multiple_of(r * SLAB_PITCH, SUBLANES), nt)]
        return pltpu.make_async_copy(ys_ref.at[idx_ref[0, 0, (h * hr + r) * TOP_K + k]], dst, sems[h * TOP_K + k])

    def issue(idx_ref, h):
        def body(r, c):
            for k in range(TOP_K):
                slab_copy(idx_ref, h, r, k).start(priority=k % 2)
            return c
        lax.fori_loop(0, hr, body, 0, unroll=DMA_LOOP_UNROLL)

    def wait(idx_ref, h):
        def body(r, c):
            for k in range(TOP_K):
                slab_copy(idx_ref, h, r, k).wait()
            return c
        lax.fori_loop(0, hr, body, 0, unroll=DMA_LOOP_UNROLL)

    def compute(h):
        r0 = h * hr
        w0 = w_ref[r0:r0 + hr, 0:1]
        w1 = w_ref[r0:r0 + hr, 1:2]
        ss = jnp.zeros((hr, 1), F32)
        for j in range(nt):
            lo0, hi0 = _unpack_halves(slabs[h * TOP_K][pl.ds(j, hr, stride=SLAB_PITCH), :])
            lo1, hi1 = _unpack_halves(slabs[h * TOP_K + 1][pl.ds(j, hr, stride=SLAB_PITCH), :])
            for c0, y in ((j * LANES, w0 * lo0 + w1 * lo1), (half + j * LANES, w0 * hi0 + w1 * hi1)):
                z = x_ref[0, r0:r0 + hr, c0:c0 + LANES] + ga_ref[0, :, c0:c0 + LANES] * y
                ss = ss + jnp.sum(z * z, axis=-1, keepdims=True)
                o_ref[0, r0:r0 + hr, c0:c0 + LANES] = z
        inv = lax.rsqrt(ss / (2 * half) + EPS)
        o_ref[0, r0:r0 + hr, :] = o_ref[0, r0:r0 + hr, :] * inv * g_ref[...]

    @pl.when(step == 0)
    def _():
        issue(pos_ref, 0)

    issue(pos_ref, 1)
    wait(pos_ref, 0)
    compute(0)

    @pl.when(step + 1 < n_steps)
    def _():
        issue(nxt_ref, 0)

    wait(pos_ref, 1)
    compute(1)


def _combine(ys3, pos, wts, x, gate, g, rows):
    b, l, d = x.shape
    lb = l // rows
    n_steps = b * lb
    pos3 = pos.reshape(n_steps, 1, rows * TOP_K)
    slab = pltpu.VMEM((rows // 2 * SLAB_PITCH, LANES), ys3.dtype)
    return pl.pallas_call(
        _combine_kernel,
        grid=(b, lb),
        in_specs=[pl.BlockSpec((1, 1, rows * TOP_K), lambda bi, i: (bi * lb + i, 0, 0), memory_space=pltpu.SMEM),
                  pl.BlockSpec((1, 1, rows * TOP_K), lambda bi, i: (jnp.minimum(bi * lb + i + 1, n_steps - 1), 0, 0),
                               memory_space=pltpu.SMEM),
                  pl.BlockSpec(memory_space=pl.ANY),
                  pl.BlockSpec((rows, LANES), lambda bi, i: (bi * lb + i, 0)),
                  pl.BlockSpec((1, rows, d), lambda bi, i: (bi, i, 0)),
                  pl.BlockSpec((1, 1, d), lambda bi, i: (bi, 0, 0)),
                  pl.BlockSpec((1, d), lambda bi, i: (0, 0))],
        out_specs=pl.BlockSpec((1, rows, d), lambda bi, i: (bi, i, 0)),
        out_shape=jax.ShapeDtypeStruct((b, l, d), F32),
        scratch_shapes=[slab] * (2 * TOP_K) + [pltpu.SemaphoreType.DMA(())] * (2 * TOP_K),
        compiler_params=_params("arbitrary", "arbitrary"),
        name="moe_combine_norm",
    )(pos3, pos3, ys3, wts, x, gate, g)


def _block_layout(counts, n_pairs, tmb):
    nblk = (n_pairs + N_EXPERTS * (tmb - 1) + tmb - 1) // tmb
    blocks = (counts + tmb - 1) // tmb
    bend = jnp.cumsum(blocks)
    pstart = (bend - blocks) * tmb
    n_used = bend[-1]
    blk_ids = jnp.minimum(jnp.arange(nblk, dtype=jnp.int32), n_used - 1)
    blk_e = jnp.minimum(jnp.searchsorted(bend, blk_ids, side="right"), N_EXPERTS - 1).astype(jnp.int32)
    last_blk = jnp.maximum(bend - 1, 0).astype(jnp.int32)
    return nblk, pstart.astype(jnp.int32), blk_e, last_blk, n_used.astype(jnp.int32).reshape(1)


def _rope_dim_order(t):
    quarter = ROPE_AXIS_DIM // 2
    lead = t.shape[:-1]
    t = t.reshape(*lead, -1, 2, 2, quarter)
    return jnp.swapaxes(t, -3, -2).reshape(*lead, -1)


def _rope_tables(n_tokens):
    rows = n_tokens // GRID_W
    row, col = jnp.meshgrid(jnp.arange(rows), jnp.arange(GRID_W), indexing="ij")
    pos = jnp.stack([row.reshape(-1), col.reshape(-1)], axis=-1).astype(F32)
    inv = ROPE_THETA ** (-jnp.arange(0, ROPE_AXIS_DIM, 2, dtype=F32) / ROPE_AXIS_DIM)
    ang = pos[:, :, None] * inv[None, None, :]
    cos, sin = jnp.cos(ang), jnp.sin(ang)
    cos_t = jnp.concatenate([cos[:, 0], cos[:, 1], cos[:, 0], cos[:, 1]], axis=-1)
    sin_t = jnp.concatenate([-sin[:, 0], -sin[:, 1], sin[:, 0], sin[:, 1]], axis=-1)
    return cos_t, sin_t


def kernel(x, c, ctx, c_ctx, norm1_g, w_mod, b_mod, w_in, q_norm_g, k_norm_g, w_attn_out, conv_dw_w, conv_dw_b, conv_ln_g, conv_ln_b, w_conv_out, w_out, norm2_g, w_router_group, b_router_group, w_router_expert, b_router_expert, w_exp_gate, w_exp_up, w_exp_down, norm_f_g):
    b, s, d = x.shape
    n_ctx = ctx.shape[1]
    assert w_in.shape[0] == 1, "single-layer stack"
    conv_width = conv_dw_w.shape[-1]
    k_off = ATTN_WIDTH
    glu_off = k_off + 2 * KV_WIDTH
    gate_off = glu_off + 2 * conv_width

    n_c = b + 1
    cvec = jnp.zeros((SUBLANES * ((n_c + SUBLANES - 1) // SUBLANES), d), F32).at[:b].set(c).at[b].set(c_ctx)
    mod = _mod_vectors(cvec, w_mod[0], b_mod.reshape(1, -1))
    sh1, sc1, ga1, sh2, sc2, ga2 = [mod[:b, i * d:(i + 1) * d].reshape(b, 1, d) for i in range(N_MOD)]
    csh1, csc1 = [mod[b:b + 1, i * d:(i + 1) * d].reshape(1, 1, d) for i in range(2)]

    g1 = norm1_g.reshape(1, d)
    h = _norm_mod(x, g1, sh1, sc1, tl=512)
    hc = _norm_mod(ctx, g1, csh1, csc1, tl=n_ctx)
    w_in_b = w_in[0].astype(BF16)
    w_qk = _rope_dim_order(w_in_b[:, :k_off + KV_WIDTH])
    cos_t, sin_t = _rope_tables(s)
    qg = _rope_dim_order(q_norm_g.reshape(1, HEAD_DIM))
    kg = _rope_dim_order(k_norm_g.reshape(1, HEAD_DIM))
    q = _q_proj(h, w_qk, qg, cos_t, sin_t, 0, HEAD_DIM ** -0.5 * LOG2E, tm=1024)
    k, v = _kv_proj(h, w_qk, w_in_b, kg, cos_t, sin_t, k_off, tm=1024)
    kc, vc = _kv_proj(hc, w_qk, w_in_b, kg, None, None, k_off, tm=n_ctx)
    attn = _attention(q, k, v, kc, vc, tq=256, tqs=128, tk=1024, rb=8)

    h2d = h.reshape(b * s, d)
    u = _glu_proj(h2d, w_in_b, glu_off, conv_width, tm=1024)
    conv = _conv_module(u.reshape(b, s, conv_width), conv_dw_w.reshape(CONV_TAPS, conv_width // LANES, LANES),
                        conv_dw_b.reshape(conv_width // LANES, LANES), conv_ln_g.reshape(1, -1),
                        conv_ln_b.reshape(1, -1), tl=256)
    mrg = _merge(h2d, attn.reshape(b * s, ATTN_WIDTH), conv.reshape(b * s, conv_width), w_in_b, gate_off,
                 w_attn_out[0], w_conv_out[0])
    x1 = _out_proj(mrg.reshape(b, s, d), w_out[0].astype(BF16), x, ga1)

    w_r = jnp.zeros((d, ROUTER_LANES), F32).at[:, :N_GROUPS].set(w_router_group[0]) \
        .at[:, N_GROUPS:N_GROUPS + N_EXPERTS].set(w_router_expert[0])
    b_r = jnp.zeros((1, ROUTER_LANES), F32).at[0, :N_GROUPS].set(b_router_group[0]) \
        .at[0, N_GROUPS:N_GROUPS + N_EXPERTS].set(b_router_expert[0])
    hp, logits = _norm2_router(x1, norm2_g.reshape(1, d), sh2, sc2, w_r.astype(BF16), b_r, tl=256)
    n = b * s
    nt = d // 2 // LANES
    tmb = 256
    meta, wts, cnt = _route(logits.reshape(n, ROUTER_LANES))
    counts = cnt[0, N_GROUPS:N_GROUPS + N_EXPERTS].astype(jnp.int32)
    nblk, pstart, blk_e, last_blk, n_used = _block_layout(counts, n * TOP_K, tmb)
    pst = jnp.zeros((1, LANES), jnp.int32).at[0, N_GROUPS:N_GROUPS + N_EXPERTS].set(pstart)
    dest = _dest_rows(meta, pst)[:, :TOP_K].reshape(-1)
    xs0 = _zero_tails(last_blk, nblk * tmb, tmb, nt)
    xs = _dispatch(hp.reshape(n, nt, LANES), dest, xs0.reshape(nblk * tmb, nt, LANES), rows=256)
    xs2 = xs.reshape(nblk * tmb * nt, LANES)
    sg = _expert_in(xs2, w_exp_gate[0], None, blk_e, n_used, tmb, "expert_gate")
    act = _expert_in(xs2, w_exp_up[0], sg, blk_e, n_used, tmb, "expert_up")
    ys = _expert_down(act, w_exp_down[0], blk_e, n_used, tmb)
    return _combine(ys.reshape(nblk * tmb, nt, LANES), dest, wts, x1, ga2, norm_f_g.reshape(1, d), rows=256)
```

```python
import functools

import jax
import jax.numpy as jnp
from jax import lax
from jax.experimental import pallas as pl
from jax.experimental.pallas import tpu as pltpu

F32 = jnp.float32
BF16 = jnp.bfloat16

GRID_W = 64
HEAD_DIM = 128
N_Q_HEADS = 16
N_KV_HEADS = 4
Q_PER_KV = N_Q_HEADS // N_KV_HEADS
ATTN_WIDTH = N_Q_HEADS * HEAD_DIM
KV_WIDTH = N_KV_HEADS * HEAD_DIM
CONV_TAPS = 31
CONV_HALO = 16
ROPE_THETA = 10000.0
ROPE_AXIS_DIM = HEAD_DIM // 2
N_GROUPS = 4
EXPERTS_PER_GROUP = 8
N_EXPERTS = N_GROUPS * EXPERTS_PER_GROUP
TOP_K = 2
N_MOD = 6
EPS = 1e-6
LOG2E = 1.4426950408889634
LANES = 128
SUBLANES = 8
ROUTER_LANES = LANES
SLAB_PITCH = 24
DMA_LOOP_UNROLL = 4

V7X_VMEM_LIMIT = 56 * 1024 * 1024


def _params(*sem):
    return pltpu.CompilerParams(dimension_semantics=sem, vmem_limit_bytes=V7X_VMEM_LIMIT)


def _sigmoid(x):
    return 1.0 / (1.0 + jnp.exp(-x))


def _silu(x):
    return x * _sigmoid(x)


def _rms(x, g):
    return x * lax.rsqrt(jnp.mean(x * x, axis=-1, keepdims=True) + EPS) * g


def _mod_kernel(c_ref, w_ref, b_ref, o_ref):
    s = _silu(c_ref[...]).astype(BF16)
    o_ref[...] = jnp.dot(s, w_ref[...].astype(BF16), preferred_element_type=F32) + b_ref[...]


def _mod_vectors(cvec, w_mod, b_mod, tn=512):
    m, d = cvec.shape
    n = w_mod.shape[1]
    return pl.pallas_call(
        _mod_kernel,
        grid=(n // tn,),
        in_specs=[pl.BlockSpec((m, d), lambda j: (0, 0)),
                  pl.BlockSpec((d, tn), lambda j: (0, j)),
                  pl.BlockSpec((1, tn), lambda j: (0, j))],
        out_specs=pl.BlockSpec((m, tn), lambda j: (0, j)),
        out_shape=jax.ShapeDtypeStruct((m, n), F32),
        compiler_params=_params("arbitrary"),
        name="mod_vectors",
    )(cvec, w_mod, b_mod)


def _norm_mod_kernel(x_ref, g_ref, sh_ref, sc_ref, o_ref):
    y = _rms(x_ref[0], g_ref[...])
    o_ref[0] = (y * (1.0 + sc_ref[0]) + sh_ref[0]).astype(o_ref.dtype)


def _norm_mod(x, g, shift, scale, tl):
    b, l, d = x.shape
    per_batch = shift.shape[0] > 1
    mod_map = (lambda bi, li: (bi, 0, 0)) if per_batch else (lambda bi, li: (0, 0, 0))
    return pl.pallas_call(
        _norm_mod_kernel,
        grid=(b, l // tl),
        in_specs=[pl.BlockSpec((1, tl, d), lambda bi, li: (bi, li, 0)),
                  pl.BlockSpec((1, d), lambda bi, li: (0, 0)),
                  pl.BlockSpec((1, 1, d), mod_map),
                  pl.BlockSpec((1, 1, d), mod_map)],
        out_specs=pl.BlockSpec((1, tl, d), lambda bi, li: (bi, li, 0)),
        out_shape=jax.ShapeDtypeStruct((b, l, d), BF16),
        compiler_params=_params("parallel", "parallel"),
        name="norm_modulate",
    )(x, g, shift, scale)


def _head_norm_rope(a, g, cos, sin):
    y = _rms(a, g)
    if cos is None:
        return y
    lane = lax.broadcasted_iota(jnp.int32, y.shape, 1)
    quarter = ROPE_AXIS_DIM // 2
    partner = jnp.where((lane % ROPE_AXIS_DIM) < quarter,
                        pltpu.roll(y, HEAD_DIM - quarter, 1), pltpu.roll(y, quarter, 1))
    return y * cos + partner * sin


def _q_proj_kernel(h_ref, w_ref, g_ref, cos_ref, sin_ref, o_ref, *, scale):
    acc = jnp.dot(h_ref[0], w_ref[...], preferred_element_type=F32)
    for hh in range(o_ref.shape[1]):
        a = acc[:, hh * HEAD_DIM:(hh + 1) * HEAD_DIM]
        y = _head_norm_rope(a, g_ref[...], cos_ref[...], sin_ref[...])
        o_ref[0, hh] = (y * scale).astype(o_ref.dtype)


def _q_proj(h, w, g, cos_t, sin_t, col_off, scale, tm, tn=1024):
    b, l, d = h.shape
    jb = col_off // tn
    hpt = tn // HEAD_DIM
    return pl.pallas_call(
        functools.partial(_q_proj_kernel, scale=scale),
        grid=(b, l // tm, ATTN_WIDTH // tn),
        in_specs=[pl.BlockSpec((1, tm, d), lambda bi, i, j: (bi, i, 0)),
                  pl.BlockSpec((d, tn), lambda bi, i, j: (0, jb + j)),
                  pl.BlockSpec((1, HEAD_DIM), lambda bi, i, j: (0, 0)),
                  pl.BlockSpec((tm, HEAD_DIM), lambda bi, i, j: (i, 0)),
                  pl.BlockSpec((tm, HEAD_DIM), lambda bi, i, j: (i, 0))],
        out_specs=pl.BlockSpec((1, hpt, tm, HEAD_DIM), lambda bi, i, j: (bi, j, i, 0)),
        out_shape=jax.ShapeDtypeStruct((b, N_Q_HEADS, l, HEAD_DIM), BF16),
        compiler_params=_params("parallel", "parallel", "arbitrary"),
        name="q_proj",
    )(h, w, g, cos_t, sin_t)


def _kv_proj_kernel(h_ref, wk_ref, wv_ref, g_ref, *rest, rope):
    if rope:
        cos_ref, sin_ref, k_ref, v_ref = rest
        cos, sin = cos_ref[...], sin_ref[...]
    else:
        k_ref, v_ref = rest
        cos = sin = None
    h = h_ref[0]
    acc = jnp.dot(h, wk_ref[...], preferred_element_type=F32)
    for hh in range(N_KV_HEADS):
        a = acc[:, hh * HEAD_DIM:(hh + 1) * HEAD_DIM]
        k_ref[0, :, hh * HEAD_DIM:(hh + 1) * HEAD_DIM] = _head_norm_rope(a, g_ref[...], cos, sin).astype(k_ref.dtype)
    v_ref[0] = jnp.dot(h, wv_ref[...], preferred_element_type=F32).astype(v_ref.dtype)


def _kv_proj(h, w, g, cos_t, sin_t, k_off, tm):
    b, l, d = h.shape
    rope = cos_t is not None
    jk = k_off // KV_WIDTH
    in_specs = [pl.BlockSpec((1, tm, d), lambda bi, i: (bi, i, 0)),
                pl.BlockSpec((d, KV_WIDTH), lambda bi, i: (0, jk)),
                pl.BlockSpec((d, KV_WIDTH), lambda bi, i: (0, jk + 1)),
                pl.BlockSpec((1, HEAD_DIM), lambda bi, i: (0, 0))]
    args = [h, w, w, g]
    if rope:
        in_specs += [pl.BlockSpec((tm, HEAD_DIM), lambda bi, i: (i, 0))] * 2
        args += [cos_t, sin_t]
    return pl.pallas_call(
        functools.partial(_kv_proj_kernel, rope=rope),
        grid=(b, l // tm),
        in_specs=in_specs,
        out_specs=[pl.BlockSpec((1, tm, KV_WIDTH), lambda bi, i: (bi, i, 0))] * 2,
        out_shape=[jax.ShapeDtypeStruct((b, l, KV_WIDTH), BF16)] * 2,
        compiler_params=_params("parallel", "parallel"),
        name="kv_proj_rope" if rope else "kv_proj_ctx",
    )(*args)


def _glu_proj_kernel(h_ref, wa_ref, wg_ref, o_ref):
    a = jnp.dot(h_ref[...], wa_ref[...], preferred_element_type=F32)
    gt = jnp.dot(h_ref[...], wg_ref[...], preferred_element_type=F32)
    o_ref[...] = (a * _sigmoid(gt)).astype(o_ref.dtype)


def _glu_proj(h2d, w, col_off, width, tm, tn=512):
    m, d = h2d.shape
    ja = col_off // tn
    jg = (col_off + width) // tn
    return pl.pallas_call(
        _glu_proj_kernel,
        grid=(m // tm, width // tn),
        in_specs=[pl.BlockSpec((tm, d), lambda i, j: (i, 0)),
                  pl.BlockSpec((d, tn), lambda i, j: (0, ja + j)),
                  pl.BlockSpec((d, tn), lambda i, j: (0, jg + j))],
        out_specs=pl.BlockSpec((tm, tn), lambda i, j: (i, j)),
        out_shape=jax.ShapeDtypeStruct((m, width), BF16),
        compiler_params=_params("parallel", "arbitrary"),
        name="glu_proj",
    )(h2d, w, w)


def _attn_kernel(q_ref, k_ref, v_ref, kc_ref, vc_ref, o_ref,
                 s0_ref, s1_ref, p0_ref, p1_ref, m_ref, al_ref, acc_ref, *, tk, rb):
    g, tq, dh = q_ref.shape[1:]
    rows = s0_ref.shape[0]
    tqs = rows // g
    chunks = [(k_ref, v_ref, c * tk, tk) for c in range(k_ref.shape[1] // tk)]
    chunks.append((kc_ref, vc_ref, 0, kc_ref.shape[1]))
    units = [(sb, j) for sb in range(tq // tqs) for j in range(len(chunks))]
    s_refs = (s0_ref, s1_ref)
    p_refs = (p0_ref, p1_ref)

    def scores(u):
        sb, j = units[u]
        kr, _, st, n = chunks[j]
        q = q_ref[0, :, sb * tqs:(sb + 1) * tqs, :].reshape(rows, dh)
        s_refs[u % 2][:, :n] = lax.dot_general(q, kr[0, st:st + n, :], (((1,), (1,)), ((), ())),
                                               preferred_element_type=F32)

    def softmax(u):
        j = units[u][1]
        n = chunks[j][3]
        s_ref, p_ref = s_refs[u % 2], p_refs[u % 2]
        for r0 in range(0, rows, rb):
            sblk = s_ref[r0:r0 + rb, :n]
            mn = jnp.max(sblk, axis=-1, keepdims=True)
            if j > 0:
                mo = m_ref[r0:r0 + rb, :]
                mn = jnp.maximum(mo, mn)
                al_ref[r0:r0 + rb, :] = jnp.exp2(mo - mn)
            m_ref[r0:r0 + rb, :] = mn
            p_ref[r0:r0 + rb, :n] = jnp.exp2(sblk - mn).astype(BF16)

    def weighted_values(u):
        j = units[u][1]
        _, vr, st, n = chunks[j]
        ones_col = (lax.broadcasted_iota(jnp.int32, (n, dh), 1) == 0).astype(BF16)
        v1 = jnp.concatenate([vr[0, st:st + n, :], ones_col], axis=1)
        upd = jnp.dot(p_refs[u % 2][:, :n], v1, preferred_element_type=F32)
        if j == 0:
            acc_ref[...] = upd
        else:
            acc_ref[...] = al_ref[...] * acc_ref[...] + upd

    def finish(sb):
        acc = acc_ref[...]
        o = acc[:, :dh] / acc[:, dh:dh + 1]
        for gi in range(g):
            o_ref[0, sb * tqs:(sb + 1) * tqs, gi * dh:(gi + 1) * dh] = o[gi * tqs:(gi + 1) * tqs].astype(o_ref.dtype)

    scores(0)
    for u, (sb, j) in enumerate(units):
        if u + 1 < len(units):
            scores(u + 1)
        softmax(u)
        weighted_values(u)
        if j == len(chunks) - 1:
            finish(sb)


def _attention(q, k, v, kc, vc, tq, tqs, tk, rb=16):
    b, _, l, dh = q.shape
    lc = kc.shape[1]
    rows = Q_PER_KV * tqs
    gdh = Q_PER_KV * dh
    return pl.pallas_call(
        functools.partial(_attn_kernel, tk=tk, rb=rb),
        grid=(b, N_KV_HEADS, l // tq),
        in_specs=[pl.BlockSpec((1, Q_PER_KV, tq, dh), lambda bi, kh, qi: (bi, kh, qi, 0)),
                  pl.BlockSpec((1, l, dh), lambda bi, kh, qi: (bi, 0, kh)),
                  pl.BlockSpec((1, l, dh), lambda bi, kh, qi: (bi, 0, kh)),
                  pl.BlockSpec((1, lc, dh), lambda bi, kh, qi: (bi, 0, kh)),
                  pl.BlockSpec((1, lc, dh), lambda bi, kh, qi: (bi, 0, kh))],
        out_specs=pl.BlockSpec((1, tq, gdh), lambda bi, kh, qi: (bi, qi, kh)),
        out_shape=jax.ShapeDtypeStruct((b, l, ATTN_WIDTH), BF16),
        scratch_shapes=[pltpu.VMEM((rows, tk), F32), pltpu.VMEM((rows, tk), F32),
                        pltpu.VMEM((rows, tk), BF16), pltpu.VMEM((rows, tk), BF16),
                        pltpu.VMEM((rows, 1), F32), pltpu.VMEM((rows, 1), F32),
                        pltpu.VMEM((rows, 2 * dh), F32)],
        compiler_params=_params("parallel", "parallel", "arbitrary"),
        name="attention",
    )(q, k, v, kc, vc)


def _conv_kernel(prev_ref, cur_ref, next_ref, w_ref, b_ref, g_ref, beta_ref, o_ref, win_ref, y_ref, *, tc, rc):
    li = pl.program_id(1)
    tl, c = cur_ref.shape[1:]
    halo = prev_ref.shape[1]
    nt = c // LANES

    def put_tokens(vals, tok0):
        for j in range(nt):
            win_ref[pl.ds(tok0 * nt + j, vals.shape[0], stride=nt), :] = vals[:, j * LANES:(j + 1) * LANES]

    def put_chunk(ci, carry):
        r0 = pl.multiple_of(ci * rc, rc)
        put_tokens(cur_ref[0, pl.ds(r0, rc), :].astype(F32), halo + r0)
        return carry

    put_tokens(jnp.where(li > 0, prev_ref[0].astype(F32), 0.0), 0)
    lax.fori_loop(0, tl // rc, put_chunk, 0)
    put_tokens(jnp.where(li < pl.num_programs(1) - 1, next_ref[0].astype(F32), 0.0), halo + tl)

    first = halo - CONV_TAPS // 2
    bias = b_ref[...][None]

    def token_chunk(ci, carry):
        tok = ci * tc
        acc = jnp.zeros((tc, nt, LANES), F32) + bias
        for t in range(CONV_TAPS):
            r0 = pl.multiple_of((tok + first + t) * nt, nt)
            acc = acc + win_ref[pl.ds(r0, tc * nt), :].reshape(tc, nt, LANES) * w_ref[t][None]
        y_ref[pl.ds(pl.multiple_of(tok * nt, nt), tc * nt), :] = acc.reshape(tc * nt, LANES)
        return carry

    lax.fori_loop(0, tl // tc, token_chunk, 0)

    def norm_chunk(ci, carry):
        r0 = pl.multiple_of(ci * rc, rc)
        y = jnp.concatenate([y_ref[pl.ds(r0 * nt + j, rc, stride=nt), :] for j in range(nt)], axis=1)
        mu = jnp.mean(y, axis=-1, keepdims=True)
        yc = y - mu
        var = jnp.mean(yc * yc, axis=-1, keepdims=True)
        z = yc * lax.rsqrt(var + EPS) * g_ref[...] + beta_ref[...]
        o_ref[0, pl.ds(r0, rc), :] = _silu(z).astype(o_ref.dtype)
        return carry

    lax.fori_loop(0, tl // rc, norm_chunk, 0, unroll=2)


def _conv_module(u, w_dw, b_dw, ln_g, ln_b, tl, tc=16, rc=32):
    b, l, c = u.shape
    nt = c // LANES
    hb = tl // CONV_HALO
    n_halo = l // CONV_HALO
    return pl.pallas_call(
        functools.partial(_conv_kernel, tc=tc, rc=rc),
        grid=(b, l // tl),
        in_specs=[pl.BlockSpec((1, CONV_HALO, c), lambda bi, li: (bi, jnp.maximum(li * hb - 1, 0), 0)),
                  pl.BlockSpec((1, tl, c), lambda bi, li: (bi, li, 0)),
                  pl.BlockSpec((1, CONV_HALO, c), lambda bi, li: (bi, jnp.minimum((li + 1) * hb, n_halo - 1), 0)),
                  pl.BlockSpec((CONV_TAPS, nt, LANES), lambda bi, li: (0, 0, 0)),
                  pl.BlockSpec((nt, LANES), lambda bi, li: (0, 0)),
                  pl.BlockSpec((1, c), lambda bi, li: (0, 0)),
                  pl.BlockSpec((1, c), lambda bi, li: (0, 0))],
        out_specs=pl.BlockSpec((1, tl, c), lambda bi, li: (bi, li, 0)),
        out_shape=jax.ShapeDtypeStruct((b, l, c), BF16),
        scratch_shapes=[pltpu.VMEM(((tl + 2 * CONV_HALO) * nt, LANES), F32), pltpu.VMEM((tl * nt, LANES), F32)],
        compiler_params=_params("parallel", "arbitrary"),
        name="conv_module",
    )(u, u, u, w_dw, b_dw, ln_g, ln_b)


def _merge_kernel(h_ref, a_ref, c_ref, wga_ref, wgc_ref, wa_ref, wc_ref, o_ref):
    h = h_ref[...]
    g_a = _sigmoid(jnp.dot(h, wga_ref[...], preferred_element_type=F32))
    g_c = _sigmoid(jnp.dot(h, wgc_ref[...], preferred_element_type=F32))
    a = jnp.dot(a_ref[...], wa_ref[...].astype(BF16), preferred_element_type=F32)
    cb = jnp.dot(c_ref[...], wc_ref[...].astype(BF16), preferred_element_type=F32)
    o_ref[...] = (g_a * a + g_c * cb).astype(o_ref.dtype)


def _merge(h2d, attn, conv, w_in, gate_off, wa, wc, tm=512, tn=512):
    m, d = h2d.shape
    ka = attn.shape[1]
    kc = conv.shape[1]
    nj = d // tn
    ja = gate_off // tn
    jc = (gate_off + d) // tn
    return pl.pallas_call(
        _merge_kernel,
        grid=(nj, m // tm),
        in_specs=[pl.BlockSpec((tm, d), lambda j, i: (i, 0)),
                  pl.BlockSpec((tm, ka), lambda j, i: (i, 0)),
                  pl.BlockSpec((tm, kc), lambda j, i: (i, 0)),
                  pl.BlockSpec((d, tn), lambda j, i: (0, ja + j)),
                  pl.BlockSpec((d, tn), lambda j, i: (0, jc + j)),
                  pl.BlockSpec((ka, tn), lambda j, i: (0, j)),
                  pl.BlockSpec((kc, tn), lambda j, i: (0, j))],
        out_specs=pl.BlockSpec((tm, tn), lambda j, i: (i, j)),
        out_shape=jax.ShapeDtypeStruct((m, d), BF16),
        compiler_params=_params("parallel", "arbitrary"),
        name="merge_branches",
    )(h2d, attn, conv, w_in, w_in, wa, wc)


def _out_proj_kernel(m_ref, w_ref, x_ref, ga_ref, o_ref):
    acc = jnp.dot(m_ref[0], w_ref[...], preferred_element_type=F32)
    o_ref[0] = x_ref[0] + ga_ref[0] * acc


def _out_proj(mrg, w, x, gate, tm=1024, tn=1024):
    b, l, d = x.shape
    return pl.pallas_call(
        _out_proj_kernel,
        grid=(b, l // tm, d // tn),
        in_specs=[pl.BlockSpec((1, tm, d), lambda bi, i, j: (bi, i, 0)),
                  pl.BlockSpec((d, tn), lambda bi, i, j: (0, j)),
                  pl.BlockSpec((1, tm, tn), lambda bi, i, j: (bi, i, j)),
                  pl.BlockSpec((1, 1, tn), lambda bi, i, j: (bi, 0, j))],
        out_specs=pl.BlockSpec((1, tm, tn), lambda bi, i, j: (bi, i, j)),
        out_shape=jax.ShapeDtypeStruct((b, l, d), F32),
        compiler_params=_params("parallel", "parallel", "arbitrary"),
        name="out_proj_residual",
    )(mrg, w, x, gate)


def _pack_halves(y):
    n = y.shape[1] // 2
    return pltpu.pack_elementwise([y[:, :n], y[:, n:]], packed_dtype=BF16)


def _unpack_halves(p):
    lo = pltpu.unpack_elementwise(p, index=0, packed_dtype=BF16, unpacked_dtype=F32)
    hi = pltpu.unpack_elementwise(p, index=1, packed_dtype=BF16, unpacked_dtype=F32)
    return lo, hi


def _norm2_router_kernel(x_ref, g_ref, sh_ref, sc_ref, wr_ref, br_ref, hp_ref, lg_ref):
    y = _rms(x_ref[0], g_ref[...]) * (1.0 + sc_ref[0]) + sh_ref[0]
    packed = _pack_halves(y)
    tl = packed.shape[0]
    nt = packed.shape[1] // LANES
    for j in range(nt):
        hp_ref[pl.ds(j, tl, stride=nt), :] = packed[:, j * LANES:(j + 1) * LANES]
    lg_ref[0] = jnp.dot(y.astype(BF16), wr_ref[...], preferred_element_type=F32) + br_ref[...]


def _norm2_router(x, g, shift, scale, w_r, b_r, tl):
    b, l, d = x.shape
    nt = d // 2 // LANES
    lb = l // tl
    return pl.pallas_call(
        _norm2_router_kernel,
        grid=(b, lb),
        in_specs=[pl.BlockSpec((1, tl, d), lambda bi, li: (bi, li, 0)),
                  pl.BlockSpec((1, d), lambda bi, li: (0, 0)),
                  pl.BlockSpec((1, 1, d), lambda bi, li: (bi, 0, 0)),
                  pl.BlockSpec((1, 1, d), lambda bi, li: (bi, 0, 0)),
                  pl.BlockSpec((d, ROUTER_LANES), lambda bi, li: (0, 0)),
                  pl.BlockSpec((1, ROUTER_LANES), lambda bi, li: (0, 0))],
        out_specs=[pl.BlockSpec((tl * nt, LANES), lambda bi, li: (bi * lb + li, 0)),
                   pl.BlockSpec((1, tl, ROUTER_LANES), lambda bi, li: (bi, li, 0))],
        out_shape=[jax.ShapeDtypeStruct((b * l * nt, LANES), jnp.int32),
                   jax.ShapeDtypeStruct((b, l, ROUTER_LANES), F32)],
        compiler_params=_params("parallel", "parallel"),
        name="norm2_router",
    )(x, g, shift, scale, w_r, b_r)


def _first_lane(mask, lane):
    return jnp.min(jnp.where(mask, lane, LANES), axis=-1, keepdims=True)


def _route_kernel(lg_ref, meta_ref, wts_ref, cnt_ref, carry_ref):
    @pl.when(pl.program_id(0) == 0)
    def _():
        carry_ref[...] = jnp.zeros_like(carry_ref)

    lg = lg_ref[...]
    tb = lg.shape[0]
    lane = lax.broadcasted_iota(jnp.int32, lg.shape, 1)
    neg_inf = jnp.float32(-jnp.inf)
    is_group = lane < N_GROUPS
    gl = jnp.where(is_group, lg, neg_inf)
    g_max = jnp.max(gl, axis=-1, keepdims=True)
    g_sel = _first_lane(gl == g_max, lane)
    p_g = 1.0 / jnp.sum(jnp.where(is_group, jnp.exp(lg - g_max), 0.0), axis=-1, keepdims=True)

    e_idx = lane - N_GROUPS
    in_group = (e_idx >= g_sel * EXPERTS_PER_GROUP) & (e_idx < (g_sel + 1) * EXPERTS_PER_GROUP)
    ev = jnp.where(in_group, lg, neg_inf)
    v1 = jnp.max(ev, axis=-1, keepdims=True)
    i1 = _first_lane(ev == v1, lane)
    ev2 = jnp.where(lane == i1, neg_inf, ev)
    v2 = jnp.max(ev2, axis=-1, keepdims=True)
    i2 = _first_lane(ev2 == v2, lane)
    t = jnp.exp(v2 - v1)
    w1 = p_g / (1.0 + t)
    w2 = w1 * t

    oh1 = lane == i1
    oh2 = lane == i2
    oh = (oh1 | oh2).astype(BF16)
    earlier = (lax.broadcasted_iota(jnp.int32, (tb, tb), 0) > lax.broadcasted_iota(jnp.int32, (tb, tb), 1)).astype(BF16)
    before = jnp.dot(earlier, oh, preferred_element_type=F32) + carry_ref[...]
    r1 = jnp.sum(jnp.where(oh1, before, 0.0), axis=-1, keepdims=True).astype(jnp.int32)
    r2 = jnp.sum(jnp.where(oh2, before, 0.0), axis=-1, keepdims=True).astype(jnp.int32)
    carry_ref[...] += jnp.sum(oh.astype(F32), axis=0, keepdims=True)

    meta_ref[...] = jnp.where(lane == 0, i1 - N_GROUPS, jnp.where(lane == 1, i2 - N_GROUPS,
                              jnp.where(lane == 2, r1, jnp.where(lane == 3, r2, 0))))
    wts_ref[...] = jnp.where(lane == 0, w1, jnp.where(lane == 1, w2, 0.0))
    cnt_ref[...] = carry_ref[...]


def _route(logits, tb=512):
    n = logits.shape[0]
    return pl.pallas_call(
        _route_kernel,
        grid=(n // tb,),
        in_specs=[pl.BlockSpec((tb, LANES), lambda i: (i, 0))],
        out_specs=[pl.BlockSpec((tb, LANES), lambda i: (i, 0)),
                   pl.BlockSpec((tb, LANES), lambda i: (i, 0)),
                   pl.BlockSpec((1, LANES), lambda i: (0, 0))],
        out_shape=[jax.ShapeDtypeStruct((n, LANES), jnp.int32),
                   jax.ShapeDtypeStruct((n, LANES), F32),
                   jax.ShapeDtypeStruct((1, LANES), F32)],
        scratch_shapes=[pltpu.VMEM((1, LANES), F32)],
        compiler_params=_params("arbitrary"),
        name="moe_route",
    )(logits)


def _dest_kernel(meta_ref, pst_ref, o_ref):
    meta = meta_ref[...]
    lane = lax.broadcasted_iota(jnp.int32, meta.shape, 1)
    pst = pst_ref[...]

    def row_of(slot):
        e = meta[:, slot:slot + 1]
        start = jnp.sum(jnp.where(lane == e + N_GROUPS, pst, 0), axis=-1, keepdims=True)
        return start + meta[:, TOP_K + slot:TOP_K + slot + 1]

    o_ref[...] = jnp.where(lane == 0, row_of(0), jnp.where(lane == 1, row_of(1), 0))


def _dest_rows(meta, pst, tb=512):
    n = meta.shape[0]
    return pl.pallas_call(
        _dest_kernel,
        grid=(n // tb,),
        in_specs=[pl.BlockSpec((tb, LANES), lambda i: (i, 0)),
                  pl.BlockSpec((1, LANES), lambda i: (0, 0))],
        out_specs=pl.BlockSpec((tb, LANES), lambda i: (i, 0)),
        out_shape=jax.ShapeDtypeStruct((n, LANES), jnp.int32),
        compiler_params=_params("parallel"),
        name="moe_dest_rows",
    )(meta, pst)


def _zero_tail_kernel(lb_ref, o_ref):
    o_ref[...] = jnp.zeros_like(o_ref)


def _zero_tails(last_blk, total, tmb, nt):
    return pl.pallas_call(
        _zero_tail_kernel,
        grid_spec=pltpu.PrefetchScalarGridSpec(
            num_scalar_prefetch=1,
            grid=(last_blk.shape[0],),
            in_specs=[],
            out_specs=pl.BlockSpec((tmb * nt, LANES), lambda e, lb: (lb[e], 0))),
        out_shape=jax.ShapeDtypeStruct((total * nt, LANES), jnp.int32),
        compiler_params=_params("arbitrary"),
        name="moe_zero_tails",
    )(last_blk)


def _dispatch_kernel(dest_ref, hp_ref, xs_in_ref, xs_ref, sem):
    rows = hp_ref.shape[0]

    def slab_copy(r, k):
        return pltpu.make_async_copy(hp_ref.at[r], xs_ref.at[dest_ref[0, 0, r * TOP_K + k]], sem)

    def start(r, c):
        for k in range(TOP_K):
            slab_copy(r, k).start(priority=k % 2)
        return c

    def wait(r, c):
        for k in range(TOP_K):
            slab_copy(r, k).wait()
        return c

    lax.fori_loop(0, rows, start, 0, unroll=DMA_LOOP_UNROLL)
    lax.fori_loop(0, rows, wait, 0, unroll=DMA_LOOP_UNROLL)


def _dispatch(hp3, dest, xs0, rows):
    n, nt, _ = hp3.shape
    steps = n // rows
    return pl.pallas_call(
        _dispatch_kernel,
        grid=(steps,),
        in_specs=[pl.BlockSpec((1, 1, rows * TOP_K), lambda i: (i, 0, 0), memory_space=pltpu.SMEM),
                  pl.BlockSpec((rows, nt, LANES), lambda i: (i, 0, 0)),
                  pl.BlockSpec(memory_space=pl.ANY)],
        out_specs=pl.BlockSpec(memory_space=pl.ANY),
        out_shape=jax.ShapeDtypeStruct(xs0.shape, xs0.dtype),
        scratch_shapes=[pltpu.SemaphoreType.DMA(())],
        input_output_aliases={2: 0},
        compiler_params=_params("arbitrary"),
        name="moe_dispatch",
    )(dest.reshape(steps, 1, rows * TOP_K), hp3, xs0)


def _expert_in_kernel(be_ref, nu_ref, x_ref, w_ref, *rest, nt, tn):
    gate_ref = rest[0] if len(rest) == 2 else None
    o_ref = rest[-1]

    @pl.when(pl.program_id(0) < nu_ref[0])
    def _():
        tmb = o_ref.shape[0]
        halves = [_unpack_halves(x_ref[pl.ds(j, tmb, stride=nt), :]) for j in range(nt)]
        lo = jnp.concatenate([h[0].astype(BF16) for h in halves], axis=1)
        hi = jnp.concatenate([h[1].astype(BF16) for h in halves], axis=1)
        half = nt * LANES
        for c0 in range(0, o_ref.shape[1], tn):
            y = (jnp.dot(lo, w_ref[0, :half, c0:c0 + tn].astype(BF16), preferred_element_type=F32)
                 + jnp.dot(hi, w_ref[0, half:, c0:c0 + tn].astype(BF16), preferred_element_type=F32))
            if gate_ref is None:
                y = _silu(y)
            else:
                y = gate_ref[:, c0:c0 + tn].astype(F32) * y
            o_ref[:, c0:c0 + tn] = y.astype(o_ref.dtype)

    @pl.when(pl.program_id(0) >= nu_ref[0])
    def _():
        o_ref[...] = jnp.zeros_like(o_ref)


def _expert_in(xs, w, gate, blk_e, n_used, tmb, name, tn=256):
    _, d, ff = w.shape
    nt = d // 2 // LANES
    total = xs.shape[0] // nt
    nblk = total // tmb

    def blk(bi, nu):
        return jnp.minimum(bi, nu[0] - 1)

    in_specs = [pl.BlockSpec((tmb * nt, LANES), lambda bi, be, nu: (blk(bi, nu), 0)),
                pl.BlockSpec((1, d, ff), lambda bi, be, nu: (be[bi], 0, 0))]
    args = [xs, w]
    if gate is not None:
        in_specs.append(pl.BlockSpec((tmb, ff), lambda bi, be, nu: (blk(bi, nu), 0)))
        args.append(gate)
    return pl.pallas_call(
        functools.partial(_expert_in_kernel, nt=nt, tn=tn),
        grid_spec=pltpu.PrefetchScalarGridSpec(
            num_scalar_prefetch=2,
            grid=(nblk,),
            in_specs=in_specs,
            out_specs=pl.BlockSpec((tmb, ff), lambda bi, be, nu: (bi, 0))),
        out_shape=jax.ShapeDtypeStruct((total, ff), BF16),
        compiler_params=_params("arbitrary"),
        name=name,
    )(blk_e, n_used, *args)


def _expert_down_kernel(be_ref, nu_ref, a_ref, w_ref, o_ref, *, nt, tn):
    @pl.when(pl.program_id(0) < nu_ref[0])
    def _():
        a = a_ref[...]
        tmb = a.shape[0]
        half = nt * LANES
        for c0 in range(0, half, tn):
            ylo = jnp.dot(a, w_ref[0, :, c0:c0 + tn].astype(BF16), preferred_element_type=F32)
            yhi = jnp.dot(a, w_ref[0, :, half + c0:half + c0 + tn].astype(BF16), preferred_element_type=F32)
            packed = pltpu.pack_elementwise([ylo, yhi], packed_dtype=BF16)
            for j in range(tn // LANES):
                o_ref[pl.ds(c0 // LANES + j, tmb, stride=nt), :] = packed[:, j * LANES:(j + 1) * LANES]

    @pl.when(pl.program_id(0) >= nu_ref[0])
    def _():
        o_ref[...] = jnp.zeros_like(o_ref)


def _expert_down(act, w_down, blk_e, n_used, tmb, tn=512):
    total, ff = act.shape
    d = w_down.shape[2]
    nt = d // 2 // LANES
    nblk = total // tmb

    def blk(bi, nu):
        return jnp.minimum(bi, nu[0] - 1)

    return pl.pallas_call(
        functools.partial(_expert_down_kernel, nt=nt, tn=tn),
        grid_spec=pltpu.PrefetchScalarGridSpec(
            num_scalar_prefetch=2,
            grid=(nblk,),
            in_specs=[pl.BlockSpec((tmb, ff), lambda bi, be, nu: (blk(bi, nu), 0)),
                      pl.BlockSpec((1, ff, d), lambda bi, be, nu: (be[bi], 0, 0))],
            out_specs=pl.BlockSpec((tmb * nt, LANES), lambda bi, be, nu: (bi, 0))),
        out_shape=jax.ShapeDtypeStruct((total * nt, LANES), jnp.int32),
        compiler_params=_params("arbitrary"),
        name="expert_down",
    )(blk_e, n_used, act, w_down)


def _combine_kernel(pos_ref, nxt_ref, ys_ref, w_ref, x_ref, ga_ref, g_ref, o_ref, *scratch):
    n_buf = 2 * TOP_K
    slabs, sems = scratch[:n_buf], scratch[n_buf:]
    rows = x_ref.shape[1]
    hr = rows // 2
    nt = ys_ref.shape[1]
    half = nt * LANES
    step = pl.program_id(0) * pl.num_programs(1) + pl.program_id(1)
    n_steps = pl.num_programs(0) * pl.num_programs(1)

    def slab_copy(idx_ref, h, r, k):
        dst = slabs[h * TOP_K + k].at[pl.ds(pl.multiple_of(r * SLAB_PITCH, SUBLANES), nt)]
        return pltpu.make_async_copy(ys_ref.at[idx_ref[0, 0, (h * hr + r) * TOP_K + k]], dst, sems[h * TOP_K + k])

    def issue(idx_ref, h):
        def body(r, c):
            for k in range(TOP_K):
                slab_copy(idx_ref, h, r, k).start(priority=k % 2)
            return c
        lax.fori_loop(0, hr, body, 0, unroll=DMA_LOOP_UNROLL)

    def wait(idx_ref, h):
        def body(r, c):
            for k in range(TOP_K):
                slab_copy(idx_ref, h, r, k).wait()
            return c
        lax.fori_loop(0, hr, body, 0, unroll=DMA_LOOP_UNROLL)

    def compute(h):
        r0 = h * hr
        w0 = w_ref[r0:r0 + hr, 0:1]
        w1 = w_ref[r0:r0 + hr, 1:2]
        ss = jnp.zeros((hr, 1), F32)
        for j in range(nt):
            lo0, hi0 = _unpack_halves(slabs[h * TOP_K][pl.ds(j, hr, stride=SLAB_PITCH), :])
            lo1, hi1 = _unpack_halves(slabs[h * TOP_K + 1][pl.ds(j, hr, stride=SLAB_PITCH), :])
            for c0, y in ((j * LANES, w0 * lo0 + w1 * lo1), (half + j * LANES, w0 * hi0 + w1 * hi1)):
                z = x_ref[0, r0:r0 + hr, c0:c0 + LANES] + ga_ref[0, :, c0:c0 + LANES] * y
                ss = ss + jnp.sum(z * z, axis=-1, keepdims=True)
                o_ref[0, r0:r0 + hr, c0:c0 + LANES] = z
        inv = lax.rsqrt(ss / (2 * half) + EPS)
        o_ref[0, r0:r0 + hr, :] = o_ref[0, r0:r0 + hr, :] * inv * g_ref[...]

    @pl.when(step == 0)
    def _():
        issue(pos_ref, 0)

    issue(pos_ref, 1)
    wait(pos_ref, 0)
    compute(0)

    @pl.when(step + 1 < n_steps)
    def _():
        issue(nxt_ref, 0)

    wait(pos_ref, 1)
    compute(1)


def _combine(ys3, pos, wts, x, gate, g, rows):
    b, l, d = x.shape
    lb = l // rows
    n_steps = b * lb
    pos3 = pos.reshape(n_steps, 1, rows * TOP_K)
    slab = pltpu.VMEM((rows // 2 * SLAB_PITCH, LANES), ys3.dtype)
    return pl.pallas_call(
        _combine_kernel,
        grid=(b, lb),
        in_specs=[pl.BlockSpec((1, 1, rows * TOP_K), lambda bi, i: (bi * lb + i, 0, 0), memory_space=pltpu.SMEM),
                  pl.BlockSpec((1, 1, rows * TOP_K), lambda bi, i: (jnp.minimum(bi * lb + i + 1, n_steps - 1), 0, 0),
                               memory_space=pltpu.SMEM),
                  pl.BlockSpec(memory_space=pl.ANY),
                  pl.BlockSpec((rows, LANES), lambda bi, i: (bi * lb + i, 0)),
                  pl.BlockSpec((1, rows, d), lambda bi, i: (bi, i, 0)),
                  pl.BlockSpec((1, 1, d), lambda bi, i: (bi, 0, 0)),
                  pl.BlockSpec((1, d), lambda bi, i: (0, 0))],
        out_specs=pl.BlockSpec((1, rows, d), lambda bi, i: (bi, i, 0)),
        out_shape=jax.ShapeDtypeStruct((b, l, d), F32),
        scratch_shapes=[slab] * (2 * TOP_K) + [pltpu.SemaphoreType.DMA(())] * (2 * TOP_K),
        compiler_params=_params("arbitrary", "arbitrary"),
        name="moe_combine_norm",
    )(pos3, pos3, ys3, wts, x, gate, g)


def _block_layout(counts, n_pairs, tmb):
    nblk = (n_pairs + N_EXPERTS * (tmb - 1) + tmb - 1) // tmb
    blocks = (counts + tmb - 1) // tmb
    bend = jnp.cumsum(blocks)
    pstart = (bend - blocks) * tmb
    n_used = bend[-1]
    blk_ids = jnp.minimum(jnp.arange(nblk, dtype=jnp.int32), n_used - 1)
    blk_e = jnp.minimum(jnp.searchsorted(bend, blk_ids, side="right"), N_EXPERTS - 1).astype(jnp.int32)
    last_blk = jnp.maximum(bend - 1, 0).astype(jnp.int32)
    return nblk, pstart.astype(jnp.int32), blk_e, last_blk, n_used.astype(jnp.int32).reshape(1)


def _rope_tables(n_tokens):
    rows = n_tokens // GRID_W
    row, col = jnp.meshgrid(jnp.arange(rows), jnp.arange(GRID_W), indexing="ij")
    pos = jnp.stack([row.reshape(-1), col.reshape(-1)], axis=-1).astype(F32)
    inv = ROPE_THETA ** (-jnp.arange(0, ROPE_AXIS_DIM, 2, dtype=F32) / ROPE_AXIS_DIM)
    ang = pos[:, :, None] * inv[None, None, :]
    cos, sin = jnp.cos(ang), jnp.sin(ang)
    cos_t = jnp.concatenate([cos[:, 0], cos[:, 0], cos[:, 1], cos[:, 1]], axis=-1)
    sin_t = jnp.concatenate([-sin[:, 0], sin[:, 0], -sin[:, 1], sin[:, 1]], axis=-1)
    return cos_t, sin_t


def kernel(x, c, ctx, c_ctx, norm1_g, w_mod, b_mod, w_in, q_norm_g, k_norm_g, w_attn_out, conv_dw_w, conv_dw_b, conv_ln_g, conv_ln_b, w_conv_out, w_out, norm2_g, w_router_group, b_router_group, w_router_expert, b_router_expert, w_exp_gate, w_exp_up, w_exp_down, norm_f_g):
    b, s, d = x.shape
    n_ctx = ctx.shape[1]
    assert w_in.shape[0] == 1, "single-layer stack"
    conv_width = conv_dw_w.shape[-1]
    k_off = ATTN_WIDTH
    glu_off = k_off + 2 * KV_WIDTH
    gate_off = glu_off + 2 * conv_width

    n_c = b + 1
    cvec = jnp.zeros((SUBLANES * ((n_c + SUBLANES - 1) // SUBLANES), d), F32).at[:b].set(c).at[b].set(c_ctx)
    mod = _mod_vectors(cvec, w_mod[0], b_mod.reshape(1, -1))
    sh1, sc1, ga1, sh2, sc2, ga2 = [mod[:b, i * d:(i + 1) * d].reshape(b, 1, d) for i in range(N_MOD)]
    csh1, csc1 = [mod[b:b + 1, i * d:(i + 1) * d].reshape(1, 1, d) for i in range(2)]

    g1 = norm1_g.reshape(1, d)
    h = _norm_mod(x, g1, sh1, sc1, tl=512)
    hc = _norm_mod(ctx, g1, csh1, csc1, tl=n_ctx)
    w_in_b = w_in[0].astype(BF16)
    cos_t, sin_t = _rope_tables(s)
    qg = q_norm_g.reshape(1, HEAD_DIM)
    kg = k_norm_g.reshape(1, HEAD_DIM)
    q = _q_proj(h, w_in_b, qg, cos_t, sin_t, 0, HEAD_DIM ** -0.5 * LOG2E, tm=1024)
    k, v = _kv_proj(h, w_in_b, kg, cos_t, sin_t, k_off, tm=1024)
    kc, vc = _kv_proj(hc, w_in_b, kg, None, None, k_off, tm=n_ctx)
    attn = _attention(q, k, v, kc, vc, tq=512, tqs=128, tk=1024, rb=8)

    h2d = h.reshape(b * s, d)
    u = _glu_proj(h2d, w_in_b, glu_off, conv_width, tm=1024)
    conv = _conv_module(u.reshape(b, s, conv_width), conv_dw_w.reshape(CONV_TAPS, conv_width // LANES, LANES),
                        conv_dw_b.reshape(conv_width // LANES, LANES), conv_ln_g.reshape(1, -1),
                        conv_ln_b.reshape(1, -1), tl=256)
    mrg = _merge(h2d, attn.reshape(b * s, ATTN_WIDTH), conv.reshape(b * s, conv_width), w_in_b, gate_off,
                 w_attn_out[0], w_conv_out[0])
    x1 = _out_proj(mrg.reshape(b, s, d), w_out[0].astype(BF16), x, ga1)

    w_r = jnp.zeros((d, ROUTER_LANES), F32).at[:, :N_GROUPS].set(w_router_group[0]) \
        .at[:, N_GROUPS:N_GROUPS + N_EXPERTS].set(w_router_expert[0])
    b_r = jnp.zeros((1, ROUTER_LANES), F32).at[0, :N_GROUPS].set(b_router_group[0]) \
        .at[0, N_GROUPS:N_GROUPS + N_EXPERTS].set(b_router_expert[0])
    hp, logits = _norm2_router(x1, norm2_g.reshape(1, d), sh2, sc2, w_r.astype(BF16), b_r, tl=256)
    n = b * s
    nt = d // 2 // LANES
    tmb = 512
    meta, wts, cnt = _route(logits.reshape(n, ROUTER_LANES))
    counts = cnt[0, N_GROUPS:N_GROUPS + N_EXPERTS].astype(jnp.int32)
    nblk, pstart, blk_e, last_blk, n_used = _block_layout(counts, n * TOP_K, tmb)
    pst = jnp.zeros((1, LANES), jnp.int32).at[0, N_GROUPS:N_GROUPS + N_EXPERTS].set(pstart)
    dest = _dest_rows(meta, pst)[:, :TOP_K].reshape(-1)
    xs0 = _zero_tails(last_blk, nblk * tmb, tmb, nt)
    xs = _dispatch(hp.reshape(n, nt, LANES), dest, xs0.reshape(nblk * tmb, nt, LANES), rows=256)
    xs2 = xs.reshape(nblk * tmb * nt, LANES)
    sg = _expert_in(xs2, w_exp_gate[0], None, blk_e, n_used, tmb, "expert_gate")
    act = _expert_in(xs2, w_exp_up[0], sg, blk_e, n_used, tmb, "expert_up")
    ys = _expert_down(act, w_exp_down[0], blk_e, n_used, tmb)
    return _combine(ys.reshape(nblk * tmb, nt, LANES), dest, wts, x1, ga2, norm_f_g.reshape(1, d), rows=256)
```

```python
import functools

import jax
import jax.numpy as jnp
from jax import lax
from jax.experimental import pallas as pl
from jax.experimental.pallas import tpu as pltpu

F32 = jnp.float32
BF16 = jnp.bfloat16

GRID_W = 64
HEAD_DIM = 128
N_Q_HEADS = 16
N_KV_HEADS = 4
Q_PER_KV = N_Q_HEADS // N_KV_HEADS
ATTN_WIDTH = N_Q_HEADS * HEAD_DIM
KV_WIDTH = N_KV_HEADS * HEAD_DIM
CONV_TAPS = 31
CONV_HALO = 16
ROPE_THETA = 10000.0
ROPE_AXIS_DIM = HEAD_DIM // 2
N_GROUPS = 4
EXPERTS_PER_GROUP = 8
N_EXPERTS = N_GROUPS * EXPERTS_PER_GROUP
TOP_K = 2
N_MOD = 6
EPS = 1e-6
LOG2E = 1.4426950408889634
LANES = 128
SUBLANES = 8
ROUTER_LANES = LANES
SLAB_PITCH = 24
DMA_LOOP_UNROLL = 4

V7X_VMEM_LIMIT = 56 * 1024 * 1024


def _params(*sem):
    return pltpu.CompilerParams(dimension_semantics=sem, vmem_limit_bytes=V7X_VMEM_LIMIT)


def _sigmoid(x):
    return 1.0 / (1.0 + jnp.exp(-x))


def _silu(x):
    return x * _sigmoid(x)


def _rms(x, g):
    return x * lax.rsqrt(jnp.mean(x * x, axis=-1, keepdims=True) + EPS) * g


def _mod_kernel(c_ref, w_ref, b_ref, o_ref):
    s = _silu(c_ref[...]).astype(BF16)
    o_ref[...] = jnp.dot(s, w_ref[...].astype(BF16), preferred_element_type=F32) + b_ref[...]


def _mod_vectors(cvec, w_mod, b_mod, tn=1024):
    m, d = cvec.shape
    n = w_mod.shape[1]
    return pl.pallas_call(
        _mod_kernel,
        grid=(n // tn,),
        in_specs=[pl.BlockSpec((m, d), lambda j: (0, 0)),
                  pl.BlockSpec((d, tn), lambda j: (0, j)),
                  pl.BlockSpec((1, tn), lambda j: (0, j))],
        out_specs=pl.BlockSpec((m, tn), lambda j: (0, j)),
        out_shape=jax.ShapeDtypeStruct((m, n), F32),
        compiler_params=_params("arbitrary"),
        name="mod_vectors",
    )(cvec, w_mod, b_mod)


def _norm_mod_kernel(x_ref, g_ref, sh_ref, sc_ref, o_ref):
    y = _rms(x_ref[0], g_ref[...])
    o_ref[0] = (y * (1.0 + sc_ref[0]) + sh_ref[0]).astype(o_ref.dtype)


def _norm_mod(x, g, shift, scale, tl):
    b, l, d = x.shape
    per_batch = shift.shape[0] > 1
    mod_map = (lambda bi, li: (bi, 0, 0)) if per_batch else (lambda bi, li: (0, 0, 0))
    return pl.pallas_call(
        _norm_mod_kernel,
        grid=(b, l // tl),
        in_specs=[pl.BlockSpec((1, tl, d), lambda bi, li: (bi, li, 0)),
                  pl.BlockSpec((1, d), lambda bi, li: (0, 0)),
                  pl.BlockSpec((1, 1, d), mod_map),
                  pl.BlockSpec((1, 1, d), mod_map)],
        out_specs=pl.BlockSpec((1, tl, d), lambda bi, li: (bi, li, 0)),
        out_shape=jax.ShapeDtypeStruct((b, l, d), BF16),
        compiler_params=_params("parallel", "parallel"),
        name="norm_modulate",
    )(x, g, shift, scale)


def _head_norm_rope(a, g, cos, sin):
    y = _rms(a, g)
    if cos is None:
        return y
    lane = lax.broadcasted_iota(jnp.int32, y.shape, 1)
    quarter = ROPE_AXIS_DIM // 2
    partner = jnp.where((lane % ROPE_AXIS_DIM) < quarter,
                        pltpu.roll(y, HEAD_DIM - quarter, 1), pltpu.roll(y, quarter, 1))
    return y * cos + partner * sin


def _q_proj_kernel(h_ref, w_ref, g_ref, cos_ref, sin_ref, o_ref, *, scale):
    acc = jnp.dot(h_ref[0], w_ref[...], preferred_element_type=F32)
    for hh in range(o_ref.shape[1]):
        a = acc[:, hh * HEAD_DIM:(hh + 1) * HEAD_DIM]
        y = _head_norm_rope(a, g_ref[...], cos_ref[...], sin_ref[...])
        o_ref[0, hh] = (y * scale).astype(o_ref.dtype)


def _q_proj(h, w, g, cos_t, sin_t, col_off, scale, tm, tn=1024):
    b, l, d = h.shape
    jb = col_off // tn
    hpt = tn // HEAD_DIM
    return pl.pallas_call(
        functools.partial(_q_proj_kernel, scale=scale),
        grid=(b, l // tm, ATTN_WIDTH // tn),
        in_specs=[pl.BlockSpec((1, tm, d), lambda bi, i, j: (bi, i, 0)),
                  pl.BlockSpec((d, tn), lambda bi, i, j: (0, jb + j)),
                  pl.BlockSpec((1, HEAD_DIM), lambda bi, i, j: (0, 0)),
                  pl.BlockSpec((tm, HEAD_DIM), lambda bi, i, j: (i, 0)),
                  pl.BlockSpec((tm, HEAD_DIM), lambda bi, i, j: (i, 0))],
        out_specs=pl.BlockSpec((1, hpt, tm, HEAD_DIM), lambda bi, i, j: (bi, j, i, 0)),
        out_shape=jax.ShapeDtypeStruct((b, N_Q_HEADS, l, HEAD_DIM), BF16),
        compiler_params=_params("parallel", "parallel", "arbitrary"),
        name="q_proj",
    )(h, w, g, cos_t, sin_t)


def _kv_proj_kernel(h_ref, wk_ref, wv_ref, g_ref, *rest, rope):
    if rope:
        cos_ref, sin_ref, k_ref, v_ref = rest
        cos, sin = cos_ref[...], sin_ref[...]
    else:
        k_ref, v_ref = rest
        cos = sin = None
    h = h_ref[0]
    acc = jnp.dot(h, wk_ref[...], preferred_element_type=F32)
    for hh in range(N_KV_HEADS):
        a = acc[:, hh * HEAD_DIM:(hh + 1) * HEAD_DIM]
        k_ref[0, :, hh * HEAD_DIM:(hh + 1) * HEAD_DIM] = _head_norm_rope(a, g_ref[...], cos, sin).astype(k_ref.dtype)
    v_ref[0] = jnp.dot(h, wv_ref[...], preferred_element_type=F32).astype(v_ref.dtype)


def _kv_proj(h, w, g, cos_t, sin_t, k_off, tm):
    b, l, d = h.shape
    rope = cos_t is not None
    jk = k_off // KV_WIDTH
    in_specs = [pl.BlockSpec((1, tm, d), lambda bi, i: (bi, i, 0)),
                pl.BlockSpec((d, KV_WIDTH), lambda bi, i: (0, jk)),
                pl.BlockSpec((d, KV_WIDTH), lambda bi, i: (0, jk + 1)),
                pl.BlockSpec((1, HEAD_DIM), lambda bi, i: (0, 0))]
    args = [h, w, w, g]
    if rope:
        in_specs += [pl.BlockSpec((tm, HEAD_DIM), lambda bi, i: (i, 0))] * 2
        args += [cos_t, sin_t]
    return pl.pallas_call(
        functools.partial(_kv_proj_kernel, rope=rope),
        grid=(b, l // tm),
        in_specs=in_specs,
        out_specs=[pl.BlockSpec((1, tm, KV_WIDTH), lambda bi, i: (bi, i, 0))] * 2,
        out_shape=[jax.ShapeDtypeStruct((b, l, KV_WIDTH), BF16)] * 2,
        compiler_params=_params("parallel", "parallel"),
        name="kv_proj_rope" if rope else "kv_proj_ctx",
    )(*args)


def _glu_proj_kernel(h_ref, wa_ref, wg_ref, o_ref):
    a = jnp.dot(h_ref[...], wa_ref[...], preferred_element_type=F32)
    gt = jnp.dot(h_ref[...], wg_ref[...], preferred_element_type=F32)
    o_ref[...] = (a * _sigmoid(gt)).astype(o_ref.dtype)


def _glu_proj(h2d, w, col_off, width, tm, tn=512):
    m, d = h2d.shape
    ja = col_off // tn
    jg = (col_off + width) // tn
    return pl.pallas_call(
        _glu_proj_kernel,
        grid=(m // tm, width // tn),
        in_specs=[pl.BlockSpec((tm, d), lambda i, j: (i, 0)),
                  pl.BlockSpec((d, tn), lambda i, j: (0, ja + j)),
                  pl.BlockSpec((d, tn), lambda i, j: (0, jg + j))],
        out_specs=pl.BlockSpec((tm, tn), lambda i, j: (i, j)),
        out_shape=jax.ShapeDtypeStruct((m, width), BF16),
        compiler_params=_params("parallel", "arbitrary"),
        name="glu_proj",
    )(h2d, w, w)


def _attn_kernel(q_ref, k_ref, v_ref, kc_ref, vc_ref, o_ref,
                 s0_ref, s1_ref, p0_ref, p1_ref, m_ref, al_ref, acc_ref, *, tk, rb):
    g, tq, dh = q_ref.shape[1:]
    rows = s0_ref.shape[0]
    tqs = rows // g
    chunks = [(k_ref, v_ref, c * tk, tk) for c in range(k_ref.shape[1] // tk)]
    chunks.append((kc_ref, vc_ref, 0, kc_ref.shape[1]))
    units = [(sb, j) for sb in range(tq // tqs) for j in range(len(chunks))]
    s_refs = (s0_ref, s1_ref)
    p_refs = (p0_ref, p1_ref)

    def scores(u):
        sb, j = units[u]
        kr, _, st, n = chunks[j]
        q = q_ref[0, :, sb * tqs:(sb + 1) * tqs, :].reshape(rows, dh)
        s_refs[u % 2][:, :n] = lax.dot_general(q, kr[0, st:st + n, :], (((1,), (1,)), ((), ())),
                                               preferred_element_type=F32)

    def softmax(u):
        j = units[u][1]
        n = chunks[j][3]
        s_ref, p_ref = s_refs[u % 2], p_refs[u % 2]
        for r0 in range(0, rows, rb):
            sblk = s_ref[r0:r0 + rb, :n]
            mn = jnp.max(sblk, axis=-1, keepdims=True)
            if j > 0:
                mo = m_ref[r0:r0 + rb, :]
                mn = jnp.maximum(mo, mn)
                al_ref[r0:r0 + rb, :] = jnp.exp2(mo - mn)
            m_ref[r0:r0 + rb, :] = mn
            p_ref[r0:r0 + rb, :n] = jnp.exp2(sblk - mn).astype(BF16)

    def weighted_values(u):
        j = units[u][1]
        _, vr, st, n = chunks[j]
        ones_col = (lax.broadcasted_iota(jnp.int32, (n, dh), 1) == 0).astype(BF16)
        v1 = jnp.concatenate([vr[0, st:st + n, :], ones_col], axis=1)
        upd = jnp.dot(p_refs[u % 2][:, :n], v1, preferred_element_type=F32)
        if j == 0:
            acc_ref[...] = upd
        else:
            acc_ref[...] = al_ref[...] * acc_ref[...] + upd

    def finish(sb):
        acc = acc_ref[...]
        o = acc[:, :dh] / acc[:, dh:dh + 1]
        for gi in range(g):
            o_ref[0, sb * tqs:(sb + 1) * tqs, gi * dh:(gi + 1) * dh] = o[gi * tqs:(gi + 1) * tqs].astype(o_ref.dtype)

    scores(0)
    for u, (sb, j) in enumerate(units):
        if u + 1 < len(units):
            scores(u + 1)
        softmax(u)
        weighted_values(u)
        if j == len(chunks) - 1:
            finish(sb)


def _attention(q, k, v, kc, vc, tq, tqs, tk, rb=16):
    b, _, l, dh = q.shape
    lc = kc.shape[1]
    rows = Q_PER_KV * tqs
    gdh = Q_PER_KV * dh
    return pl.pallas_call(
        functools.partial(_attn_kernel, tk=tk, rb=rb),
        grid=(b, N_KV_HEADS, l // tq),
        in_specs=[pl.BlockSpec((1, Q_PER_KV, tq, dh), lambda bi, kh, qi: (bi, kh, qi, 0)),
                  pl.BlockSpec((1, l, dh), lambda bi, kh, qi: (bi, 0, kh)),
                  pl.BlockSpec((1, l, dh), lambda bi, kh, qi: (bi, 0, kh)),
                  pl.BlockSpec((1, lc, dh), lambda bi, kh, qi: (bi, 0, kh)),
                  pl.BlockSpec((1, lc, dh), lambda bi, kh, qi: (bi, 0, kh))],
        out_specs=pl.BlockSpec((1, tq, gdh), lambda bi, kh, qi: (bi, qi, kh)),
        out_shape=jax.ShapeDtypeStruct((b, l, ATTN_WIDTH), BF16),
        scratch_shapes=[pltpu.VMEM((rows, tk), F32), pltpu.VMEM((rows, tk), F32),
                        pltpu.VMEM((rows, tk), BF16), pltpu.VMEM((rows, tk), BF16),
                        pltpu.VMEM((rows, 1), F32), pltpu.VMEM((rows, 1), F32),
                        pltpu.VMEM((rows, 2 * dh), F32)],
        compiler_params=_params("parallel", "parallel", "arbitrary"),
        name="attention",
    )(q, k, v, kc, vc)


def _conv_kernel(prev_ref, cur_ref, next_ref, w_ref, b_ref, g_ref, beta_ref, o_ref, win_ref, y_ref, *, tc, rc):
    li = pl.program_id(1)
    tl, c = cur_ref.shape[1:]
    halo = prev_ref.shape[1]
    nt = c // LANES

    def put_tokens(vals, tok0):
        for j in range(nt):
            win_ref[pl.ds(tok0 * nt + j, vals.shape[0], stride=nt), :] = vals[:, j * LANES:(j + 1) * LANES]

    def put_chunk(ci, carry):
        r0 = pl.multiple_of(ci * rc, rc)
        put_tokens(cur_ref[0, pl.ds(r0, rc), :].astype(F32), halo + r0)
        return carry

    put_tokens(jnp.where(li > 0, prev_ref[0].astype(F32), 0.0), 0)
    lax.fori_loop(0, tl // rc, put_chunk, 0)
    put_tokens(jnp.where(li < pl.num_programs(1) - 1, next_ref[0].astype(F32), 0.0), halo + tl)

    first = halo - CONV_TAPS // 2
    bias = b_ref[...][None]

    def token_chunk(ci, carry):
        tok = ci * tc
        acc = jnp.zeros((tc, nt, LANES), F32) + bias
        for t in range(CONV_TAPS):
            r0 = pl.multiple_of((tok + first + t) * nt, nt)
            acc = acc + win_ref[pl.ds(r0, tc * nt), :].reshape(tc, nt, LANES) * w_ref[t][None]
        y_ref[pl.ds(pl.multiple_of(tok * nt, nt), tc * nt), :] = acc.reshape(tc * nt, LANES)
        return carry

    lax.fori_loop(0, tl // tc, token_chunk, 0)

    def norm_chunk(ci, carry):
        r0 = pl.multiple_of(ci * rc, rc)
        y = jnp.concatenate([y_ref[pl.ds(r0 * nt + j, rc, stride=nt), :] for j in range(nt)], axis=1)
        mu = jnp.mean(y, axis=-1, keepdims=True)
        yc = y - mu
        var = jnp.mean(yc * yc, axis=-1, keepdims=True)
        z = yc * lax.rsqrt(var + EPS) * g_ref[...] + beta_ref[...]
        o_ref[0, pl.ds(r0, rc), :] = _silu(z).astype(o_ref.dtype)
        return carry

    lax.fori_loop(0, tl // rc, norm_chunk, 0, unroll=2)


def _conv_module(u, w_dw, b_dw, ln_g, ln_b, tl, tc=16, rc=32):
    b, l, c = u.shape
    nt = c // LANES
    hb = tl // CONV_HALO
    n_halo = l // CONV_HALO
    return pl.pallas_call(
        functools.partial(_conv_kernel, tc=tc, rc=rc),
        grid=(b, l // tl),
        in_specs=[pl.BlockSpec((1, CONV_HALO, c), lambda bi, li: (bi, jnp.maximum(li * hb - 1, 0), 0)),
                  pl.BlockSpec((1, tl, c), lambda bi, li: (bi, li, 0)),
                  pl.BlockSpec((1, CONV_HALO, c), lambda bi, li: (bi, jnp.minimum((li + 1) * hb, n_halo - 1), 0)),
                  pl.BlockSpec((CONV_TAPS, nt, LANES), lambda bi, li: (0, 0, 0)),
                  pl.BlockSpec((nt, LANES), lambda bi, li: (0, 0)),
                  pl.BlockSpec((1, c), lambda bi, li: (0, 0)),
                  pl.BlockSpec((1, c), lambda bi, li: (0, 0))],
        out_specs=pl.BlockSpec((1, tl, c), lambda bi, li: (bi, li, 0)),
        out_shape=jax.ShapeDtypeStruct((b, l, c), BF16),
        scratch_shapes=[pltpu.VMEM(((tl + 2 * CONV_HALO) * nt, LANES), F32), pltpu.VMEM((tl * nt, LANES), F32)],
        compiler_params=_params("parallel", "arbitrary"),
        name="conv_module",
    )(u, u, u, w_dw, b_dw, ln_g, ln_b)


def _merge_kernel(h_ref, a_ref, c_ref, wga_ref, wgc_ref, wa_ref, wc_ref, o_ref):
    h = h_ref[...]
    g_a = _sigmoid(jnp.dot(h, wga_ref[...], preferred_element_type=F32))
    g_c = _sigmoid(jnp.dot(h, wgc_ref[...], preferred_element_type=F32))
    a = jnp.dot(a_ref[...], wa_ref[...].astype(BF16), preferred_element_type=F32)
    cb = jnp.dot(c_ref[...], wc_ref[...].astype(BF16), preferred_element_type=F32)
    o_ref[...] = (g_a * a + g_c * cb).astype(o_ref.dtype)


def _merge(h2d, attn, conv, w_in, gate_off, wa, wc, tm=512, tn=512):
    m, d = h2d.shape
    ka = attn.shape[1]
    kc = conv.shape[1]
    nj = d // tn
    ja = gate_off // tn
    jc = (gate_off + d) // tn
    return pl.pallas_call(
        _merge_kernel,
        grid=(nj, m // tm),
        in_specs=[pl.BlockSpec((tm, d), lambda j, i: (i, 0)),
                  pl.BlockSpec((tm, ka), lambda j, i: (i, 0)),
                  pl.BlockSpec((tm, kc), lambda j, i: (i, 0)),
                  pl.BlockSpec((d, tn), lambda j, i: (0, ja + j)),
                  pl.BlockSpec((d, tn), lambda j, i: (0, jc + j)),
                  pl.BlockSpec((ka, tn), lambda j, i: (0, j)),
                  pl.BlockSpec((kc, tn), lambda j, i: (0, j))],
        out_specs=pl.BlockSpec((tm, tn), lambda j, i: (i, j)),
        out_shape=jax.ShapeDtypeStruct((m, d), BF16),
        compiler_params=_params("parallel", "arbitrary"),
        name="merge_branches",
    )(h2d, attn, conv, w_in, w_in, wa, wc)


def _out_proj_kernel(m_ref, w_ref, x_ref, ga_ref, o_ref):
    acc = jnp.dot(m_ref[0], w_ref[...], preferred_element_type=F32)
    o_ref[0] = x_ref[0] + ga_ref[0] * acc


def _out_proj(mrg, w, x, gate, tm=1024, tn=1024):
    b, l, d = x.shape
    return pl.pallas_call(
        _out_proj_kernel,
        grid=(b, l // tm, d // tn),
        in_specs=[pl.BlockSpec((1, tm, d), lambda bi, i, j: (bi, i, 0)),
                  pl.BlockSpec((d, tn), lambda bi, i, j: (0, j)),
                  pl.BlockSpec((1, tm, tn), lambda bi, i, j: (bi, i, j)),
                  pl.BlockSpec((1, 1, tn), lambda bi, i, j: (bi, 0, j))],
        out_specs=pl.BlockSpec((1, tm, tn), lambda bi, i, j: (bi, i, j)),
        out_shape=jax.ShapeDtypeStruct((b, l, d), F32),
        compiler_params=_params("parallel", "parallel", "arbitrary"),
        name="out_proj_residual",
    )(mrg, w, x, gate)


def _pack_halves(y):
    n = y.shape[1] // 2
    return pltpu.pack_elementwise([y[:, :n], y[:, n:]], packed_dtype=BF16)


def _unpack_halves(p):
    lo = pltpu.unpack_elementwise(p, index=0, packed_dtype=BF16, unpacked_dtype=F32)
    hi = pltpu.unpack_elementwise(p, index=1, packed_dtype=BF16, unpacked_dtype=F32)
    return lo, hi


def _norm2_router_kernel(x_ref, g_ref, sh_ref, sc_ref, wr_ref, br_ref, hp_ref, lg_ref):
    y = _rms(x_ref[0], g_ref[...]) * (1.0 + sc_ref[0]) + sh_ref[0]
    packed = _pack_halves(y)
    tl = packed.shape[0]
    nt = packed.shape[1] // LANES
    for j in range(nt):
        hp_ref[pl.ds(j, tl, stride=nt), :] = packed[:, j * LANES:(j + 1) * LANES]
    lg_ref[0] = jnp.dot(y.astype(BF16), wr_ref[...], preferred_element_type=F32) + br_ref[...]


def _norm2_router(x, g, shift, scale, w_r, b_r, tl):
    b, l, d = x.shape
    nt = d // 2 // LANES
    lb = l // tl
    return pl.pallas_call(
        _norm2_router_kernel,
        grid=(b, lb),
        in_specs=[pl.BlockSpec((1, tl, d), lambda bi, li: (bi, li, 0)),
                  pl.BlockSpec((1, d), lambda bi, li: (0, 0)),
                  pl.BlockSpec((1, 1, d), lambda bi, li: (bi, 0, 0)),
                  pl.BlockSpec((1, 1, d), lambda bi, li: (bi, 0, 0)),
                  pl.BlockSpec((d, ROUTER_LANES), lambda bi, li: (0, 0)),
                  pl.BlockSpec((1, ROUTER_LANES), lambda bi, li: (0, 0))],
        out_specs=[pl.BlockSpec((tl * nt, LANES), lambda bi, li: (bi * lb + li, 0)),
                   pl.BlockSpec((1, tl, ROUTER_LANES), lambda bi, li: (bi, li, 0))],
        out_shape=[jax.ShapeDtypeStruct((b * l * nt, LANES), jnp.int32),
                   jax.ShapeDtypeStruct((b, l, ROUTER_LANES), F32)],
        compiler_params=_params("parallel", "parallel"),
        name="norm2_router",
    )(x, g, shift, scale, w_r, b_r)


def _first_lane(mask, lane):
    return jnp.min(jnp.where(mask, lane, LANES), axis=-1, keepdims=True)


def _route_kernel(lg_ref, meta_ref, wts_ref, cnt_ref, carry_ref):
    @pl.when(pl.program_id(0) == 0)
    def _():
        carry_ref[...] = jnp.zeros_like(carry_ref)

    lg = lg_ref[...]
    tb = lg.shape[0]
    lane = lax.broadcasted_iota(jnp.int32, lg.shape, 1)
    neg_inf = jnp.float32(-jnp.inf)
    is_group = lane < N_GROUPS
    gl = jnp.where(is_group, lg, neg_inf)
    g_max = jnp.max(gl, axis=-1, keepdims=True)
    g_sel = _first_lane(gl == g_max, lane)
    p_g = 1.0 / jnp.sum(jnp.where(is_group, jnp.exp(lg - g_max), 0.0), axis=-1, keepdims=True)

    e_idx = lane - N_GROUPS
    in_group = (e_idx >= g_sel * EXPERTS_PER_GROUP) & (e_idx < (g_sel + 1) * EXPERTS_PER_GROUP)
    ev = jnp.where(in_group, lg, neg_inf)
    v1 = jnp.max(ev, axis=-1, keepdims=True)
    i1 = _first_lane(ev == v1, lane)
    ev2 = jnp.where(lane == i1, neg_inf, ev)
    v2 = jnp.max(ev2, axis=-1, keepdims=True)
    i2 = _first_lane(ev2 == v2, lane)
    t = jnp.exp(v2 - v1)
    w1 = p_g / (1.0 + t)
    w2 = w1 * t

    oh1 = lane == i1
    oh2 = lane == i2
    oh = (oh1 | oh2).astype(BF16)
    earlier = (lax.broadcasted_iota(jnp.int32, (tb, tb), 0) > lax.broadcasted_iota(jnp.int32, (tb, tb), 1)).astype(BF16)
    before = jnp.dot(earlier, oh, preferred_element_type=F32) + carry_ref[...]
    r1 = jnp.sum(jnp.where(oh1, before, 0.0), axis=-1, keepdims=True).astype(jnp.int32)
    r2 = jnp.sum(jnp.where(oh2, before, 0.0), axis=-1, keepdims=True).astype(jnp.int32)
    carry_ref[...] += jnp.sum(oh.astype(F32), axis=0, keepdims=True)

    meta_ref[...] = jnp.where(lane == 0, i1 - N_GROUPS, jnp.where(lane == 1, i2 - N_GROUPS,
                              jnp.where(lane == 2, r1, jnp.where(lane == 3, r2, 0))))
    wts_ref[...] = jnp.where(lane == 0, w1, jnp.where(lane == 1, w2, 0.0))
    cnt_ref[...] = carry_ref[...]


def _route(logits, tb=512):
    n = logits.shape[0]
    return pl.pallas_call(
        _route_kernel,
        grid=(n // tb,),
        in_specs=[pl.BlockSpec((tb, LANES), lambda i: (i, 0))],
        out_specs=[pl.BlockSpec((tb, LANES), lambda i: (i, 0)),
                   pl.BlockSpec((tb, LANES), lambda i: (i, 0)),
                   pl.BlockSpec((1, LANES), lambda i: (0, 0))],
        out_shape=[jax.ShapeDtypeStruct((n, LANES), jnp.int32),
                   jax.ShapeDtypeStruct((n, LANES), F32),
                   jax.ShapeDtypeStruct((1, LANES), F32)],
        scratch_shapes=[pltpu.VMEM((1, LANES), F32)],
        compiler_params=_params("arbitrary"),
        name="moe_route",
    )(logits)


def _dest_kernel(meta_ref, pst_ref, o_ref):
    meta = meta_ref[...]
    lane = lax.broadcasted_iota(jnp.int32, meta.shape, 1)
    pst = pst_ref[...]

    def row_of(slot):
        e = meta[:, slot:slot + 1]
        start = jnp.sum(jnp.where(lane == e + N_GROUPS, pst, 0), axis=-1, keepdims=True)
        return start + meta[:, TOP_K + slot:TOP_K + slot + 1]

    o_ref[...] = jnp.where(lane == 0, row_of(0), jnp.where(lane == 1, row_of(1), 0))


def _dest_rows(meta, pst, tb=512):
    n = meta.shape[0]
    return pl.pallas_call(
        _dest_kernel,
        grid=(n // tb,),
        in_specs=[pl.BlockSpec((tb, LANES), lambda i: (i, 0)),
                  pl.BlockSpec((1, LANES), lambda i: (0, 0))],
        out_specs=pl.BlockSpec((tb, LANES), lambda i: (i, 0)),
        out_shape=jax.ShapeDtypeStruct((n, LANES), jnp.int32),
        compiler_params=_params("parallel"),
        name="moe_dest_rows",
    )(meta, pst)


def _zero_tail_kernel(lb_ref, o_ref):
    o_ref[...] = jnp.zeros_like(o_ref)


def _zero_tails(last_blk, total, tmb, nt):
    return pl.pallas_call(
        _zero_tail_kernel,
        grid_spec=pltpu.PrefetchScalarGridSpec(
            num_scalar_prefetch=1,
            grid=(last_blk.shape[0],),
            in_specs=[],
            out_specs=pl.BlockSpec((tmb * nt, LANES), lambda e, lb: (lb[e], 0))),
        out_shape=jax.ShapeDtypeStruct((total * nt, LANES), jnp.int32),
        compiler_params=_params("arbitrary"),
        name="moe_zero_tails",
    )(last_blk)


def _dispatch_kernel(dest_ref, hp_ref, xs_in_ref, xs_ref, sem):
    rows = hp_ref.shape[0]

    def slab_copy(r, k):
        return pltpu.make_async_copy(hp_ref.at[r], xs_ref.at[dest_ref[0, 0, r * TOP_K + k]], sem)

    def start(r, c):
        for k in range(TOP_K):
            slab_copy(r, k).start(priority=k % 2)
        return c

    def wait(r, c):
        for k in range(TOP_K):
            slab_copy(r, k).wait()
        return c

    lax.fori_loop(0, rows, start, 0, unroll=DMA_LOOP_UNROLL)
    lax.fori_loop(0, rows, wait, 0, unroll=DMA_LOOP_UNROLL)


def _dispatch(hp3, dest, xs0, rows):
    n, nt, _ = hp3.shape
    steps = n // rows
    return pl.pallas_call(
        _dispatch_kernel,
        grid=(steps,),
        in_specs=[pl.BlockSpec((1, 1, rows * TOP_K), lambda i: (i, 0, 0), memory_space=pltpu.SMEM),
                  pl.BlockSpec((rows, nt, LANES), lambda i: (i, 0, 0)),
                  pl.BlockSpec(memory_space=pl.ANY)],
        out_specs=pl.BlockSpec(memory_space=pl.ANY),
        out_shape=jax.ShapeDtypeStruct(xs0.shape, xs0.dtype),
        scratch_shapes=[pltpu.SemaphoreType.DMA(())],
        input_output_aliases={2: 0},
        compiler_params=_params("arbitrary"),
        name="moe_dispatch",
    )(dest.reshape(steps, 1, rows * TOP_K), hp3, xs0)


def _by_valid_rows(nv, tmb, compute, o_ref):
    hm = tmb // 2
    tail = o_ref.shape[0] // 2

    @pl.when(nv > hm)
    def _():
        compute(tmb)

    @pl.when((nv > 0) & (nv <= hm))
    def _():
        compute(hm)
        o_ref[tail:, :] = jnp.zeros((tail, o_ref.shape[1]), o_ref.dtype)

    @pl.when(nv == 0)
    def _():
        o_ref[...] = jnp.zeros_like(o_ref)


def _expert_in_kernel(be_ref, nu_ref, nv_ref, x_ref, w_ref, *rest, nt, tn):
    gate_ref = rest[0] if len(rest) == 2 else None
    o_ref = rest[-1]

    def compute(rows):
        halves = [_unpack_halves(x_ref[pl.ds(j, rows, stride=nt), :]) for j in range(nt)]
        lo = jnp.concatenate([h[0].astype(BF16) for h in halves], axis=1)
        hi = jnp.concatenate([h[1].astype(BF16) for h in halves], axis=1)
        half = nt * LANES
        for c0 in range(0, o_ref.shape[1], tn):
            y = (jnp.dot(lo, w_ref[0, :half, c0:c0 + tn].astype(BF16), preferred_element_type=F32)
                 + jnp.dot(hi, w_ref[0, half:, c0:c0 + tn].astype(BF16), preferred_element_type=F32))
            if gate_ref is None:
                y = _silu(y)
            else:
                y = gate_ref[:rows, c0:c0 + tn].astype(F32) * y
            o_ref[:rows, c0:c0 + tn] = y.astype(o_ref.dtype)

    _by_valid_rows(nv_ref[pl.program_id(0)], o_ref.shape[0], compute, o_ref)


def _expert_in(xs, w, gate, blk_e, n_used, n_valid, tmb, name, tn=256):
    _, d, ff = w.shape
    nt = d // 2 // LANES
    total = xs.shape[0] // nt
    nblk = total // tmb

    def blk(bi, nu):
        return jnp.minimum(bi, nu[0] - 1)

    in_specs = [pl.BlockSpec((tmb * nt, LANES), lambda bi, be, nu, nv: (blk(bi, nu), 0)),
                pl.BlockSpec((1, d, ff), lambda bi, be, nu, nv: (be[bi], 0, 0))]
    args = [xs, w]
    if gate is not None:
        in_specs.append(pl.BlockSpec((tmb, ff), lambda bi, be, nu, nv: (blk(bi, nu), 0)))
        args.append(gate)
    return pl.pallas_call(
        functools.partial(_expert_in_kernel, nt=nt, tn=tn),
        grid_spec=pltpu.PrefetchScalarGridSpec(
            num_scalar_prefetch=3,
            grid=(nblk,),
            in_specs=in_specs,
            out_specs=pl.BlockSpec((tmb, ff), lambda bi, be, nu, nv: (bi, 0))),
        out_shape=jax.ShapeDtypeStruct((total, ff), BF16),
        compiler_params=_params("arbitrary"),
        name=name,
    )(blk_e, n_used, n_valid, *args)


def _expert_down_kernel(be_ref, nu_ref, nv_ref, a_ref, w_ref, o_ref, *, nt, tn):
    tmb = a_ref.shape[0]

    def compute(rows):
        a = a_ref[:rows, :]
        half = nt * LANES
        for c0 in range(0, half, tn):
            ylo = jnp.dot(a, w_ref[0, :, c0:c0 + tn].astype(BF16), preferred_element_type=F32)
            yhi = jnp.dot(a, w_ref[0, :, half + c0:half + c0 + tn].astype(BF16), preferred_element_type=F32)
            packed = pltpu.pack_elementwise([ylo, yhi], packed_dtype=BF16)
            for j in range(tn // LANES):
                o_ref[pl.ds(c0 // LANES + j, rows, stride=nt), :] = packed[:, j * LANES:(j + 1) * LANES]

    _by_valid_rows(nv_ref[pl.program_id(0)], tmb, compute, o_ref)


def _expert_down(act, w_down, blk_e, n_used, n_valid, tmb, tn=512):
    total, ff = act.shape
    d = w_down.shape[2]
    nt = d // 2 // LANES
    nblk = total // tmb

    def blk(bi, nu):
        return jnp.minimum(bi, nu[0] - 1)

    return pl.pallas_call(
        functools.partial(_expert_down_kernel, nt=nt, tn=tn),
        grid_spec=pltpu.PrefetchScalarGridSpec(
            num_scalar_prefetch=3,
            grid=(nblk,),
            in_specs=[pl.BlockSpec((tmb, ff), lambda bi, be, nu, nv: (blk(bi, nu), 0)),
                      pl.BlockSpec((1, ff, d), lambda bi, be, nu, nv: (be[bi], 0, 0))],
            out_specs=pl.BlockSpec((tmb * nt, LANES), lambda bi, be, nu, nv: (bi, 0))),
        out_shape=jax.ShapeDtypeStruct((total * nt, LANES), jnp.int32),
        compiler_params=_params("arbitrary"),
        name="expert_down",
    )(blk_e, n_used, n_valid, act, w_down)


def _combine_kernel(pos_ref, nxt_ref, ys_ref, w_ref, x_ref, ga_ref, g_ref, o_ref, *scratch):
    n_buf = 2 * TOP_K
    slabs, sems = scratch[:n_buf], scratch[n_buf:]
    rows = x_ref.shape[1]
    hr = rows // 2
    nt = ys_ref.shape[1]
    half = nt * LANES
    step = pl.program_id(0) * pl.num_programs(1) + pl.program_id(1)
    n_steps = pl.num_programs(0) * pl.num_programs(1)

    def slab_copy(idx_ref, h, r, k):
        dst = slabs[h * TOP_K + k].at[pl.ds(pl.multiple_of(r * SLAB_PITCH, SUBLANES), nt)]
        return pltpu.make_async_copy(ys_ref.at[idx_ref[0, 0, (h * hr + r) * TOP_K + k]], dst, sems[h * TOP_K + k])

    def issue(idx_ref, h):
        def body(r, c):
            for k in range(TOP_K):
                slab_copy(idx_ref, h, r, k).start(priority=k % 2)
            return c
        lax.fori_loop(0, hr, body, 0, unroll=DMA_LOOP_UNROLL)

    def wait(idx_ref, h):
        def body(r, c):
            for k in range(TOP_K):
                slab_copy(idx_ref, h, r, k).wait()
            return c
        lax.fori_loop(0, hr, body, 0, unroll=DMA_LOOP_UNROLL)

    def compute(h):
        r0 = h * hr
        w0 = w_ref[r0:r0 + hr, 0:1]
        w1 = w_ref[r0:r0 + hr, 1:2]
        ss = jnp.zeros((hr, 1), F32)
        for j in range(nt):
            lo0, hi0 = _unpack_halves(slabs[h * TOP_K][pl.ds(j, hr, stride=SLAB_PITCH), :])
            lo1, hi1 = _unpack_halves(slabs[h * TOP_K + 1][pl.ds(j, hr, stride=SLAB_PITCH), :])
            for c0, y in ((j * LANES, w0 * lo0 + w1 * lo1), (half + j * LANES, w0 * hi0 + w1 * hi1)):
                z = x_ref[0, r0:r0 + hr, c0:c0 + LANES] + ga_ref[0, :, c0:c0 + LANES] * y
                ss = ss + jnp.sum(z * z, axis=-1, keepdims=True)
                o_ref[0, r0:r0 + hr, c0:c0 + LANES] = z
        inv = lax.rsqrt(ss / (2 * half) + EPS)
        o_ref[0, r0:r0 + hr, :] = o_ref[0, r0:r0 + hr, :] * inv * g_ref[...]

    @pl.when(step == 0)
    def _():
        issue(pos_ref, 0)

    issue(pos_ref, 1)
    wait(pos_ref, 0)
    compute(0)

    @pl.when(step + 1 < n_steps)
    def _():
        issue(nxt_ref, 0)

    wait(pos_ref, 1)
    compute(1)


def _combine(ys3, pos, wts, x, gate, g, rows):
    b, l, d = x.shape
    lb = l // rows
    n_steps = b * lb
    pos3 = pos.reshape(n_steps, 1, rows * TOP_K)
    slab = pltpu.VMEM((rows // 2 * SLAB_PITCH, LANES), ys3.dtype)
    return pl.pallas_call(
        _combine_kernel,
        grid=(b, lb),
        in_specs=[pl.BlockSpec((1, 1, rows * TOP_K), lambda bi, i: (bi * lb + i, 0, 0), memory_space=pltpu.SMEM),
                  pl.BlockSpec((1, 1, rows * TOP_K), lambda bi, i: (jnp.minimum(bi * lb + i + 1, n_steps - 1), 0, 0),
                               memory_space=pltpu.SMEM),
                  pl.BlockSpec(memory_space=pl.ANY),
                  pl.BlockSpec((rows, LANES), lambda bi, i: (bi * lb + i, 0)),
                  pl.BlockSpec((1, rows, d), lambda bi, i: (bi, i, 0)),
                  pl.BlockSpec((1, 1, d), lambda bi, i: (bi, 0, 0)),
                  pl.BlockSpec((1, d), lambda bi, i: (0, 0))],
        out_specs=pl.BlockSpec((1, rows, d), lambda bi, i: (bi, i, 0)),
        out_shape=jax.ShapeDtypeStruct((b, l, d), F32),
        scratch_shapes=[slab] * (2 * TOP_K) + [pltpu.SemaphoreType.DMA(())] * (2 * TOP_K),
        compiler_params=_params("arbitrary", "arbitrary"),
        name="moe_combine_norm",
    )(pos3, pos3, ys3, wts, x, gate, g)


def _block_layout(counts, n_pairs, tmb):
    nblk = (n_pairs + N_EXPERTS * (tmb - 1) + tmb - 1) // tmb
    blocks = (counts + tmb - 1) // tmb
    bend = jnp.cumsum(blocks)
    bstart = bend - blocks
    pstart = bstart * tmb
    n_used = bend[-1]
    ids = jnp.arange(nblk, dtype=jnp.int32)
    blk_e = jnp.minimum(jnp.searchsorted(bend, jnp.minimum(ids, n_used - 1), side="right"),
                        N_EXPERTS - 1).astype(jnp.int32)
    last_blk = jnp.maximum(bend - 1, 0).astype(jnp.int32)
    n_valid = jnp.where(ids < n_used, jnp.clip(counts[blk_e] - (ids - bstart[blk_e]) * tmb, 0, tmb), 0)
    return (nblk, pstart.astype(jnp.int32), blk_e, last_blk, n_used.astype(jnp.int32).reshape(1),
            n_valid.astype(jnp.int32))


def _rope_tables(n_tokens):
    rows = n_tokens // GRID_W
    row, col = jnp.meshgrid(jnp.arange(rows), jnp.arange(GRID_W), indexing="ij")
    pos = jnp.stack([row.reshape(-1), col.reshape(-1)], axis=-1).astype(F32)
    inv = ROPE_THETA ** (-jnp.arange(0, ROPE_AXIS_DIM, 2, dtype=F32) / ROPE_AXIS_DIM)
    ang = pos[:, :, None] * inv[None, None, :]
    cos, sin = jnp.cos(ang), jnp.sin(ang)
    cos_t = jnp.concatenate([cos[:, 0], cos[:, 0], cos[:, 1], cos[:, 1]], axis=-1)
    sin_t = jnp.concatenate([-sin[:, 0], sin[:, 0], -sin[:, 1], sin[:, 1]], axis=-1)
    return cos_t, sin_t


def kernel(x, c, ctx, c_ctx, norm1_g, w_mod, b_mod, w_in, q_norm_g, k_norm_g, w_attn_out, conv_dw_w, conv_dw_b, conv_ln_g, conv_ln_b, w_conv_out, w_out, norm2_g, w_router_group, b_router_group, w_router_expert, b_router_expert, w_exp_gate, w_exp_up, w_exp_down, norm_f_g):
    b, s, d = x.shape
    n_ctx = ctx.shape[1]
    assert w_in.shape[0] == 1, "single-layer stack"
    conv_width = conv_dw_w.shape[-1]
    k_off = ATTN_WIDTH
    glu_off = k_off + 2 * KV_WIDTH
    gate_off = glu_off + 2 * conv_width

    n_c = b + 1
    cvec = jnp.zeros((SUBLANES * ((n_c + SUBLANES - 1) // SUBLANES), d), F32).at[:b].set(c).at[b].set(c_ctx)
    mod = _mod_vectors(cvec, w_mod[0], b_mod.reshape(1, -1))
    sh1, sc1, ga1, sh2, sc2, ga2 = [mod[:b, i * d:(i + 1) * d].reshape(b, 1, d) for i in range(N_MOD)]
    csh1, csc1 = [mod[b:b + 1, i * d:(i + 1) * d].reshape(1, 1, d) for i in range(2)]

    g1 = norm1_g.reshape(1, d)
    h = _norm_mod(x, g1, sh1, sc1, tl=512)
    hc = _norm_mod(ctx, g1, csh1, csc1, tl=n_ctx)
    w_in_b = w_in[0].astype(BF16)
    cos_t, sin_t = _rope_tables(s)
    qg = q_norm_g.reshape(1, HEAD_DIM)
    kg = k_norm_g.reshape(1, HEAD_DIM)
    q = _q_proj(h, w_in_b, qg, cos_t, sin_t, 0, HEAD_DIM ** -0.5 * LOG2E, tm=1024)
    k, v = _kv_proj(h, w_in_b, kg, cos_t, sin_t, k_off, tm=1024)
    kc, vc = _kv_proj(hc, w_in_b, kg, None, None, k_off, tm=n_ctx)
    attn = _attention(q, k, v, kc, vc, tq=128, tqs=128, tk=1024, rb=8)

    h2d = h.reshape(b * s, d)
    u = _glu_proj(h2d, w_in_b, glu_off, conv_width, tm=1024)
    conv = _conv_module(u.reshape(b, s, conv_width), conv_dw_w.reshape(CONV_TAPS, conv_width // LANES, LANES),
                        conv_dw_b.reshape(conv_width // LANES, LANES), conv_ln_g.reshape(1, -1),
                        conv_ln_b.reshape(1, -1), tl=256)
    mrg = _merge(h2d, attn.reshape(b * s, ATTN_WIDTH), conv.reshape(b * s, conv_width), w_in_b, gate_off,
                 w_attn_out[0], w_conv_out[0])
    x1 = _out_proj(mrg.reshape(b, s, d), w_out[0].astype(BF16), x, ga1)

    w_r = jnp.zeros((d, ROUTER_LANES), F32).at[:, :N_GROUPS].set(w_router_group[0]) \
        .at[:, N_GROUPS:N_GROUPS + N_EXPERTS].set(w_router_expert[0])
    b_r = jnp.zeros((1, ROUTER_LANES), F32).at[0, :N_GROUPS].set(b_router_group[0]) \
        .at[0, N_GROUPS:N_GROUPS + N_EXPERTS].set(b_router_expert[0])
    hp, logits = _norm2_router(x1, norm2_g.reshape(1, d), sh2, sc2, w_r.astype(BF16), b_r, tl=256)
    n = b * s
    nt = d // 2 // LANES
    tmb = 512
    meta, wts, cnt = _route(logits.reshape(n, ROUTER_LANES))
    counts = cnt[0, N_GROUPS:N_GROUPS + N_EXPERTS].astype(jnp.int32)
    nblk, pstart, blk_e, last_blk, n_used, n_valid = _block_layout(counts, n * TOP_K, tmb)
    pst = jnp.zeros((1, LANES), jnp.int32).at[0, N_GROUPS:N_GROUPS + N_EXPERTS].set(pstart)
    dest = _dest_rows(meta, pst)[:, :TOP_K].reshape(-1)
    xs0 = _zero_tails(last_blk, nblk * tmb, tmb, nt)
    xs = _dispatch(hp.reshape(n, nt, LANES), dest, xs0.reshape(nblk * tmb, nt, LANES), rows=256)
    xs2 = xs.reshape(nblk * tmb * nt, LANES)
    sg = _expert_in(xs2, w_exp_gate[0], None, blk_e, n_used, n_valid, tmb, "expert_gate")
    act = _expert_in(xs2, w_exp_up[0], sg, blk_e, n_used, n_valid, tmb, "expert_up")
    ys = _expert_down(act, w_exp_down[0], blk_e, n_used, n_valid, tmb)
    return _combine(ys.reshape(nblk * tmb, nt, LANES), dest, wts, x1, ga2, norm_f_g.reshape(1, d), rows=256)
```

```python
import functools

import jax
import jax.numpy as jnp
from jax import lax
from jax.experimental import pallas as pl
from jax.experimental.pallas import tpu as pltpu

F32 = jnp.float32
BF16 = jnp.bfloat16

GRID_W = 64
HEAD_DIM = 128
N_Q_HEADS = 16
N_KV_HEADS = 4
Q_PER_KV = N_Q_HEADS // N_KV_HEADS
ATTN_WIDTH = N_Q_HEADS * HEAD_DIM
KV_WIDTH = N_KV_HEADS * HEAD_DIM
CONV_TAPS = 31
CONV_HALO = 16
ROPE_THETA = 10000.0
ROPE_AXIS_DIM = HEAD_DIM // 2
N_GROUPS = 4
EXPERTS_PER_GROUP = 8
N_EXPERTS = N_GROUPS * EXPERTS_PER_GROUP
TOP_K = 2
N_MOD = 6
EPS = 1e-6
LOG2E = 1.4426950408889634
LANES = 128
SUBLANES = 8
ROUTER_LANES = LANES
SLAB_PITCH = 24
DMA_LOOP_UNROLL = 4

V7X_VMEM_LIMIT = 56 * 1024 * 1024


def _params(*sem):
    return pltpu.CompilerParams(dimension_semantics=sem, vmem_limit_bytes=V7X_VMEM_LIMIT)


def _sigmoid(x):
    return 1.0 / (1.0 + jnp.exp(-x))


def _silu(x):
    return x * _sigmoid(x)


def _rms(x, g):
    return x * lax.rsqrt(jnp.mean(x * x, axis=-1, keepdims=True) + EPS) * g


def _mod_kernel(c_ref, w_ref, b_ref, o_ref):
    s = _silu(c_ref[...]).astype(BF16)
    o_ref[...] = jnp.dot(s, w_ref[...].astype(BF16), preferred_element_type=F32) + b_ref[...]


def _mod_vectors(cvec, w_mod, b_mod, tn=1024):
    m, d = cvec.shape
    n = w_mod.shape[1]
    return pl.pallas_call(
        _mod_kernel,
        grid=(n // tn,),
        in_specs=[pl.BlockSpec((m, d), lambda j: (0, 0)),
                  pl.BlockSpec((d, tn), lambda j: (0, j)),
                  pl.BlockSpec((1, tn), lambda j: (0, j))],
        out_specs=pl.BlockSpec((m, tn), lambda j: (0, j)),
        out_shape=jax.ShapeDtypeStruct((m, n), F32),
        compiler_params=_params("arbitrary"),
        name="mod_vectors",
    )(cvec, w_mod, b_mod)


def _norm_mod_kernel(x_ref, g_ref, sh_ref, sc_ref, o_ref):
    y = _rms(x_ref[0], g_ref[...])
    o_ref[0] = (y * (1.0 + sc_ref[0]) + sh_ref[0]).astype(o_ref.dtype)


def _norm_mod(x, g, shift, scale, tl):
    b, l, d = x.shape
    per_batch = shift.shape[0] > 1
    mod_map = (lambda bi, li: (bi, 0, 0)) if per_batch else (lambda bi, li: (0, 0, 0))
    return pl.pallas_call(
        _norm_mod_kernel,
        grid=(b, l // tl),
        in_specs=[pl.BlockSpec((1, tl, d), lambda bi, li: (bi, li, 0)),
                  pl.BlockSpec((1, d), lambda bi, li: (0, 0)),
                  pl.BlockSpec((1, 1, d), mod_map),
                  pl.BlockSpec((1, 1, d), mod_map)],
        out_specs=pl.BlockSpec((1, tl, d), lambda bi, li: (bi, li, 0)),
        out_shape=jax.ShapeDtypeStruct((b, l, d), BF16),
        compiler_params=_params("parallel", "parallel"),
        name="norm_modulate",
    )(x, g, shift, scale)


def _head_norm_rope(a, g, cos, sin):
    y = _rms(a, g)
    if cos is None:
        return y
    lane = lax.broadcasted_iota(jnp.int32, y.shape, 1)
    quarter = ROPE_AXIS_DIM // 2
    partner = jnp.where((lane % ROPE_AXIS_DIM) < quarter,
                        pltpu.roll(y, HEAD_DIM - quarter, 1), pltpu.roll(y, quarter, 1))
    return y * cos + partner * sin


def _q_proj_kernel(h_ref, w_ref, g_ref, cos_ref, sin_ref, o_ref, *, scale):
    acc = jnp.dot(h_ref[0], w_ref[...], preferred_element_type=F32)
    for hh in range(o_ref.shape[1]):
        a = acc[:, hh * HEAD_DIM:(hh + 1) * HEAD_DIM]
        y = _head_norm_rope(a, g_ref[...], cos_ref[...], sin_ref[...])
        o_ref[0, hh] = (y * scale).astype(o_ref.dtype)


def _q_proj(h, w, g, cos_t, sin_t, col_off, scale, tm, tn=1024):
    b, l, d = h.shape
    jb = col_off // tn
    hpt = tn // HEAD_DIM
    return pl.pallas_call(
        functools.partial(_q_proj_kernel, scale=scale),
        grid=(b, l // tm, ATTN_WIDTH // tn),
        in_specs=[pl.BlockSpec((1, tm, d), lambda bi, i, j: (bi, i, 0)),
                  pl.BlockSpec((d, tn), lambda bi, i, j: (0, jb + j)),
                  pl.BlockSpec((1, HEAD_DIM), lambda bi, i, j: (0, 0)),
                  pl.BlockSpec((tm, HEAD_DIM), lambda bi, i, j: (i, 0)),
                  pl.BlockSpec((tm, HEAD_DIM), lambda bi, i, j: (i, 0))],
        out_specs=pl.BlockSpec((1, hpt, tm, HEAD_DIM), lambda bi, i, j: (bi, j, i, 0)),
        out_shape=jax.ShapeDtypeStruct((b, N_Q_HEADS, l, HEAD_DIM), BF16),
        compiler_params=_params("parallel", "parallel", "arbitrary"),
        name="q_proj",
    )(h, w, g, cos_t, sin_t)


def _kv_proj_kernel(h_ref, wk_ref, wv_ref, g_ref, *rest, rope):
    if rope:
        cos_ref, sin_ref, k_ref, v_ref = rest
        cos, sin = cos_ref[...], sin_ref[...]
    else:
        k_ref, v_ref = rest
        cos = sin = None
    h = h_ref[0]
    acc = jnp.dot(h, wk_ref[...], preferred_element_type=F32)
    for hh in range(N_KV_HEADS):
        a = acc[:, hh * HEAD_DIM:(hh + 1) * HEAD_DIM]
        k_ref[0, :, hh * HEAD_DIM:(hh + 1) * HEAD_DIM] = _head_norm_rope(a, g_ref[...], cos, sin).astype(k_ref.dtype)
    v_ref[0] = jnp.dot(h, wv_ref[...], preferred_element_type=F32).astype(v_ref.dtype)


def _kv_proj(h, w, g, cos_t, sin_t, k_off, tm):
    b, l, d = h.shape
    rope = cos_t is not None
    jk = k_off // KV_WIDTH
    in_specs = [pl.BlockSpec((1, tm, d), lambda bi, i: (bi, i, 0)),
                pl.BlockSpec((d, KV_WIDTH), lambda bi, i: (0, jk)),
                pl.BlockSpec((d, KV_WIDTH), lambda bi, i: (0, jk + 1)),
                pl.BlockSpec((1, HEAD_DIM), lambda bi, i: (0, 0))]
    args = [h, w, w, g]
    if rope:
        in_specs += [pl.BlockSpec((tm, HEAD_DIM), lambda bi, i: (i, 0))] * 2
        args += [cos_t, sin_t]
    return pl.pallas_call(
        functools.partial(_kv_proj_kernel, rope=rope),
        grid=(b, l // tm),
        in_specs=in_specs,
        out_specs=[pl.BlockSpec((1, tm, KV_WIDTH), lambda bi, i: (bi, i, 0))] * 2,
        out_shape=[jax.ShapeDtypeStruct((b, l, KV_WIDTH), BF16)] * 2,
        compiler_params=_params("parallel", "parallel"),
        name="kv_proj_rope" if rope else "kv_proj_ctx",
    )(*args)


def _glu_proj_kernel(h_ref, wa_ref, wg_ref, o_ref):
    a = jnp.dot(h_ref[...], wa_ref[...], preferred_element_type=F32)
    gt = jnp.dot(h_ref[...], wg_ref[...], preferred_element_type=F32)
    o_ref[...] = (a * _sigmoid(gt)).astype(o_ref.dtype)


def _glu_proj(h2d, w, col_off, width, tm, tn=512):
    m, d = h2d.shape
    ja = col_off // tn
    jg = (col_off + width) // tn
    return pl.pallas_call(
        _glu_proj_kernel,
        grid=(m // tm, width // tn),
        in_specs=[pl.BlockSpec((tm, d), lambda i, j: (i, 0)),
                  pl.BlockSpec((d, tn), lambda i, j: (0, ja + j)),
                  pl.BlockSpec((d, tn), lambda i, j: (0, jg + j))],
        out_specs=pl.BlockSpec((tm, tn), lambda i, j: (i, j)),
        out_shape=jax.ShapeDtypeStruct((m, width), BF16),
        compiler_params=_params("parallel", "arbitrary"),
        name="glu_proj",
    )(h2d, w, w)


def _attn_kernel(q_ref, k_ref, v_ref, kc_ref, vc_ref, o_ref,
                 s0_ref, s1_ref, p0_ref, p1_ref, m_ref, al_ref, acc_ref, *, tk, rb):
    g, tq, dh = q_ref.shape[1:]
    rows = s0_ref.shape[0]
    tqs = rows // g
    chunks = [(k_ref, v_ref, c * tk, tk) for c in range(k_ref.shape[1] // tk)]
    chunks.append((kc_ref, vc_ref, 0, kc_ref.shape[1]))
    units = [(sb, j) for sb in range(tq // tqs) for j in range(len(chunks))]
    s_refs = (s0_ref, s1_ref)
    p_refs = (p0_ref, p1_ref)

    def scores(u):
        sb, j = units[u]
        kr, _, st, n = chunks[j]
        q = q_ref[0, :, sb * tqs:(sb + 1) * tqs, :].reshape(rows, dh)
        s_refs[u % 2][:, :n] = lax.dot_general(q, kr[0, st:st + n, :], (((1,), (1,)), ((), ())),
                                               preferred_element_type=F32)

    def softmax(u):
        j = units[u][1]
        n = chunks[j][3]
        s_ref, p_ref = s_refs[u % 2], p_refs[u % 2]
        for r0 in range(0, rows, rb):
            sblk = s_ref[r0:r0 + rb, :n]
            mn = jnp.max(sblk, axis=-1, keepdims=True)
            if j > 0:
                mo = m_ref[r0:r0 + rb, :]
                mn = jnp.maximum(mo, mn)
                al_ref[r0:r0 + rb, :] = jnp.exp2(mo - mn)
            m_ref[r0:r0 + rb, :] = mn
            p_ref[r0:r0 + rb, :n] = jnp.exp2(sblk - mn).astype(BF16)

    def weighted_values(u):
        j = units[u][1]
        _, vr, st, n = chunks[j]
        ones_col = (lax.broadcasted_iota(jnp.int32, (n, dh), 1) == 0).astype(BF16)
        v1 = jnp.concatenate([vr[0, st:st + n, :], ones_col], axis=1)
        upd = jnp.dot(p_refs[u % 2][:, :n], v1, preferred_element_type=F32)
        if j == 0:
            acc_ref[...] = upd
        else:
            acc_ref[...] = al_ref[...] * acc_ref[...] + upd

    def finish(sb):
        acc = acc_ref[...]
        o = acc[:, :dh] / acc[:, dh:dh + 1]
        for gi in range(g):
            o_ref[0, sb * tqs:(sb + 1) * tqs, gi * dh:(gi + 1) * dh] = o[gi * tqs:(gi + 1) * tqs].astype(o_ref.dtype)

    scores(0)
    for u, (sb, j) in enumerate(units):
        if u + 1 < len(units):
            scores(u + 1)
        softmax(u)
        weighted_values(u)
        if j == len(chunks) - 1:
            finish(sb)


def _attention(q, k, v, kc, vc, tq, tqs, tk, rb=16):
    b, _, l, dh = q.shape
    lc = kc.shape[1]
    rows = Q_PER_KV * tqs
    gdh = Q_PER_KV * dh
    return pl.pallas_call(
        functools.partial(_attn_kernel, tk=tk, rb=rb),
        grid=(b, N_KV_HEADS, l // tq),
        in_specs=[pl.BlockSpec((1, Q_PER_KV, tq, dh), lambda bi, kh, qi: (bi, kh, qi, 0)),
                  pl.BlockSpec((1, l, dh), lambda bi, kh, qi: (bi, 0, kh)),
                  pl.BlockSpec((1, l, dh), lambda bi, kh, qi: (bi, 0, kh)),
                  pl.BlockSpec((1, lc, dh), lambda bi, kh, qi: (bi, 0, kh)),
                  pl.BlockSpec((1, lc, dh), lambda bi, kh, qi: (bi, 0, kh))],
        out_specs=pl.BlockSpec((1, tq, gdh), lambda bi, kh, qi: (bi, qi, kh)),
        out_shape=jax.ShapeDtypeStruct((b, l, ATTN_WIDTH), BF16),
        scratch_shapes=[pltpu.VMEM((rows, tk), F32), pltpu.VMEM((rows, tk), F32),
                        pltpu.VMEM((rows, tk), BF16), pltpu.VMEM((rows, tk), BF16),
                        pltpu.VMEM((rows, 1), F32), pltpu.VMEM((rows, 1), F32),
                        pltpu.VMEM((rows, 2 * dh), F32)],
        compiler_params=_params("parallel", "parallel", "arbitrary"),
        name="attention",
    )(q, k, v, kc, vc)


def _conv_kernel(prev_ref, cur_ref, next_ref, w_ref, b_ref, g_ref, beta_ref, o_ref, win_ref, y_ref, *, tc, rc):
    li = pl.program_id(1)
    tl, c = cur_ref.shape[1:]
    halo = prev_ref.shape[1]
    nt = c // LANES

    def put_tokens(vals, tok0):
        for j in range(nt):
            win_ref[pl.ds(tok0 * nt + j, vals.shape[0], stride=nt), :] = vals[:, j * LANES:(j + 1) * LANES]

    def put_chunk(ci, carry):
        r0 = pl.multiple_of(ci * rc, rc)
        put_tokens(cur_ref[0, pl.ds(r0, rc), :].astype(F32), halo + r0)
        return carry

    put_tokens(jnp.where(li > 0, prev_ref[0].astype(F32), 0.0), 0)
    lax.fori_loop(0, tl // rc, put_chunk, 0)
    put_tokens(jnp.where(li < pl.num_programs(1) - 1, next_ref[0].astype(F32), 0.0), halo + tl)

    first = halo - CONV_TAPS // 2
    bias = b_ref[...][None]

    def token_chunk(ci, carry):
        tok = ci * tc
        acc = jnp.zeros((tc, nt, LANES), F32) + bias
        for t in range(CONV_TAPS):
            r0 = pl.multiple_of((tok + first + t) * nt, nt)
            acc = acc + win_ref[pl.ds(r0, tc * nt), :].reshape(tc, nt, LANES) * w_ref[t][None]
        y_ref[pl.ds(pl.multiple_of(tok * nt, nt), tc * nt), :] = acc.reshape(tc * nt, LANES)
        return carry

    lax.fori_loop(0, tl // tc, token_chunk, 0)

    def norm_chunk(ci, carry):
        r0 = pl.multiple_of(ci * rc, rc)
        y = jnp.concatenate([y_ref[pl.ds(r0 * nt + j, rc, stride=nt), :] for j in range(nt)], axis=1)
        mu = jnp.mean(y, axis=-1, keepdims=True)
        yc = y - mu
        var = jnp.mean(yc * yc, axis=-1, keepdims=True)
        z = yc * lax.rsqrt(var + EPS) * g_ref[...] + beta_ref[...]
        o_ref[0, pl.ds(r0, rc), :] = _silu(z).astype(o_ref.dtype)
        return carry

    lax.fori_loop(0, tl // rc, norm_chunk, 0, unroll=2)


def _conv_module(u, w_dw, b_dw, ln_g, ln_b, tl, tc=16, rc=32):
    b, l, c = u.shape
    nt = c // LANES
    hb = tl // CONV_HALO
    n_halo = l // CONV_HALO
    return pl.pallas_call(
        functools.partial(_conv_kernel, tc=tc, rc=rc),
        grid=(b, l // tl),
        in_specs=[pl.BlockSpec((1, CONV_HALO, c), lambda bi, li: (bi, jnp.maximum(li * hb - 1, 0), 0)),
                  pl.BlockSpec((1, tl, c), lambda bi, li: (bi, li, 0)),
                  pl.BlockSpec((1, CONV_HALO, c), lambda bi, li: (bi, jnp.minimum((li + 1) * hb, n_halo - 1), 0)),
                  pl.BlockSpec((CONV_TAPS, nt, LANES), lambda bi, li: (0, 0, 0)),
                  pl.BlockSpec((nt, LANES), lambda bi, li: (0, 0)),
                  pl.BlockSpec((1, c), lambda bi, li: (0, 0)),
                  pl.BlockSpec((1, c), lambda bi, li: (0, 0))],
        out_specs=pl.BlockSpec((1, tl, c), lambda bi, li: (bi, li, 0)),
        out_shape=jax.ShapeDtypeStruct((b, l, c), BF16),
        scratch_shapes=[pltpu.VMEM(((tl + 2 * CONV_HALO) * nt, LANES), F32), pltpu.VMEM((tl * nt, LANES), F32)],
        compiler_params=_params("parallel", "arbitrary"),
        name="conv_module",
    )(u, u, u, w_dw, b_dw, ln_g, ln_b)


def _merge_kernel(h_ref, a_ref, c_ref, wga_ref, wgc_ref, wa_ref, wc_ref, o_ref):
    h = h_ref[...]
    g_a = _sigmoid(jnp.dot(h, wga_ref[...], preferred_element_type=F32))
    g_c = _sigmoid(jnp.dot(h, wgc_ref[...], preferred_element_type=F32))
    a = jnp.dot(a_ref[...], wa_ref[...].astype(BF16), preferred_element_type=F32)
    cb = jnp.dot(c_ref[...], wc_ref[...].astype(BF16), preferred_element_type=F32)
    o_ref[...] = (g_a * a + g_c * cb).astype(o_ref.dtype)


def _merge(h2d, attn, conv, w_in, gate_off, wa, wc, tm=512, tn=512):
    m, d = h2d.shape
    ka = attn.shape[1]
    kc = conv.shape[1]
    nj = d // tn
    ja = gate_off // tn
    jc = (gate_off + d) // tn
    return pl.pallas_call(
        _merge_kernel,
        grid=(nj, m // tm),
        in_specs=[pl.BlockSpec((tm, d), lambda j, i: (i, 0)),
                  pl.BlockSpec((tm, ka), lambda j, i: (i, 0)),
                  pl.BlockSpec((tm, kc), lambda j, i: (i, 0)),
                  pl.BlockSpec((d, tn), lambda j, i: (0, ja + j)),
                  pl.BlockSpec((d, tn), lambda j, i: (0, jc + j)),
                  pl.BlockSpec((ka, tn), lambda j, i: (0, j)),
                  pl.BlockSpec((kc, tn), lambda j, i: (0, j))],
        out_specs=pl.BlockSpec((tm, tn), lambda j, i: (i, j)),
        out_shape=jax.ShapeDtypeStruct((m, d), BF16),
        compiler_params=_params("parallel", "arbitrary"),
        name="merge_branches",
    )(h2d, attn, conv, w_in, w_in, wa, wc)


def _out_proj_kernel(m_ref, w_ref, x_ref, ga_ref, o_ref):
    acc = jnp.dot(m_ref[0], w_ref[...], preferred_element_type=F32)
    o_ref[0] = x_ref[0] + ga_ref[0] * acc


def _out_proj(mrg, w, x, gate, tm=1024, tn=1024):
    b, l, d = x.shape
    return pl.pallas_call(
        _out_proj_kernel,
        grid=(b, l // tm, d // tn),
        in_specs=[pl.BlockSpec((1, tm, d), lambda bi, i, j: (bi, i, 0)),
                  pl.BlockSpec((d, tn), lambda bi, i, j: (0, j)),
                  pl.BlockSpec((1, tm, tn), lambda bi, i, j: (bi, i, j)),
                  pl.BlockSpec((1, 1, tn), lambda bi, i, j: (bi, 0, j))],
        out_specs=pl.BlockSpec((1, tm, tn), lambda bi, i, j: (bi, i, j)),
        out_shape=jax.ShapeDtypeStruct((b, l, d), F32),
        compiler_params=_params("parallel", "parallel", "arbitrary"),
        name="out_proj_residual",
    )(mrg, w, x, gate)


def _pack_halves(y):
    n = y.shape[1] // 2
    return pltpu.pack_elementwise([y[:, :n], y[:, n:]], packed_dtype=BF16)


def _unpack_halves(p):
    lo = pltpu.unpack_elementwise(p, index=0, packed_dtype=BF16, unpacked_dtype=F32)
    hi = pltpu.unpack_elementwise(p, index=1, packed_dtype=BF16, unpacked_dtype=F32)
    return lo, hi


def _norm2_router_kernel(x_ref, g_ref, sh_ref, sc_ref, wr_ref, br_ref, hp_ref, lg_ref):
    y = _rms(x_ref[0], g_ref[...]) * (1.0 + sc_ref[0]) + sh_ref[0]
    packed = _pack_halves(y)
    tl = packed.shape[0]
    nt = packed.shape[1] // LANES
    for j in range(nt):
        hp_ref[pl.ds(j, tl, stride=nt), :] = packed[:, j * LANES:(j + 1) * LANES]
    lg_ref[0] = jnp.dot(y.astype(BF16), wr_ref[...], preferred_element_type=F32) + br_ref[...]


def _norm2_router(x, g, shift, scale, w_r, b_r, tl):
    b, l, d = x.shape
    nt = d // 2 // LANES
    lb = l // tl
    return pl.pallas_call(
        _norm2_router_kernel,
        grid=(b, lb),
        in_specs=[pl.BlockSpec((1, tl, d), lambda bi, li: (bi, li, 0)),
                  pl.BlockSpec((1, d), lambda bi, li: (0, 0)),
                  pl.BlockSpec((1, 1, d), lambda bi, li: (bi, 0, 0)),
                  pl.BlockSpec((1, 1, d), lambda bi, li: (bi, 0, 0)),
                  pl.BlockSpec((d, ROUTER_LANES), lambda bi, li: (0, 0)),
                  pl.BlockSpec((1, ROUTER_LANES), lambda bi, li: (0, 0))],
        out_specs=[pl.BlockSpec((tl * nt, LANES), lambda bi, li: (bi * lb + li, 0)),
                   pl.BlockSpec((1, tl, ROUTER_LANES), lambda bi, li: (bi, li, 0))],
        out_shape=[jax.ShapeDtypeStruct((b * l * nt, LANES), jnp.int32),
                   jax.ShapeDtypeStruct((b, l, ROUTER_LANES), F32)],
        compiler_params=_params("parallel", "parallel"),
        name="norm2_router",
    )(x, g, shift, scale, w_r, b_r)


def _first_lane(mask, lane):
    return jnp.min(jnp.where(mask, lane, LANES), axis=-1, keepdims=True)


def _route_kernel(lg_ref, meta_ref, wts_ref, cnt_ref, carry_ref):
    @pl.when(pl.program_id(0) == 0)
    def _():
        carry_ref[...] = jnp.zeros_like(carry_ref)

    lg = lg_ref[...]
    tb = lg.shape[0]
    lane = lax.broadcasted_iota(jnp.int32, lg.shape, 1)
    neg_inf = jnp.float32(-jnp.inf)
    is_group = lane < N_GROUPS
    gl = jnp.where(is_group, lg, neg_inf)
    g_max = jnp.max(gl, axis=-1, keepdims=True)
    g_sel = _first_lane(gl == g_max, lane)
    p_g = 1.0 / jnp.sum(jnp.where(is_group, jnp.exp(lg - g_max), 0.0), axis=-1, keepdims=True)

    e_idx = lane - N_GROUPS
    in_group = (e_idx >= g_sel * EXPERTS_PER_GROUP) & (e_idx < (g_sel + 1) * EXPERTS_PER_GROUP)
    ev = jnp.where(in_group, lg, neg_inf)
    v1 = jnp.max(ev, axis=-1, keepdims=True)
    i1 = _first_lane(ev == v1, lane)
    ev2 = jnp.where(lane == i1, neg_inf, ev)
    v2 = jnp.max(ev2, axis=-1, keepdims=True)
    i2 = _first_lane(ev2 == v2, lane)
    t = jnp.exp(v2 - v1)
    w1 = p_g / (1.0 + t)
    w2 = w1 * t

    oh1 = lane == i1
    oh2 = lane == i2
    oh = (oh1 | oh2).astype(BF16)
    earlier = (lax.broadcasted_iota(jnp.int32, (tb, tb), 0) > lax.broadcasted_iota(jnp.int32, (tb, tb), 1)).astype(BF16)
    before = jnp.dot(earlier, oh, preferred_element_type=F32) + carry_ref[...]
    r1 = jnp.sum(jnp.where(oh1, before, 0.0), axis=-1, keepdims=True).astype(jnp.int32)
    r2 = jnp.sum(jnp.where(oh2, before, 0.0), axis=-1, keepdims=True).astype(jnp.int32)
    carry_ref[...] += jnp.sum(oh.astype(F32), axis=0, keepdims=True)

    meta_ref[...] = jnp.where(lane == 0, i1 - N_GROUPS, jnp.where(lane == 1, i2 - N_GROUPS,
                              jnp.where(lane == 2, r1, jnp.where(lane == 3, r2, 0))))
    wts_ref[...] = jnp.where(lane == 0, w1, jnp.where(lane == 1, w2, 0.0))
    cnt_ref[...] = carry_ref[...]


def _route(logits, tb=512):
    n = logits.shape[0]
    return pl.pallas_call(
        _route_kernel,
        grid=(n // tb,),
        in_specs=[pl.BlockSpec((tb, LANES), lambda i: (i, 0))],
        out_specs=[pl.BlockSpec((tb, LANES), lambda i: (i, 0)),
                   pl.BlockSpec((tb, LANES), lambda i: (i, 0)),
                   pl.BlockSpec((1, LANES), lambda i: (0, 0))],
        out_shape=[jax.ShapeDtypeStruct((n, LANES), jnp.int32),
                   jax.ShapeDtypeStruct((n, LANES), F32),
                   jax.ShapeDtypeStruct((1, LANES), F32)],
        scratch_shapes=[pltpu.VMEM((1, LANES), F32)],
        compiler_params=_params("arbitrary"),
        name="moe_route",
    )(logits)


def _dest_kernel(meta_ref, pst_ref, o_ref):
    meta = meta_ref[...]
    lane = lax.broadcasted_iota(jnp.int32, meta.shape, 1)
    pst = pst_ref[...]

    def row_of(slot):
        e = meta[:, slot:slot + 1]
        start = jnp.sum(jnp.where(lane == e + N_GROUPS, pst, 0), axis=-1, keepdims=True)
        return start + meta[:, TOP_K + slot:TOP_K + slot + 1]

    o_ref[...] = jnp.where(lane == 0, row_of(0), jnp.where(lane == 1, row_of(1), 0))


def _dest_rows(meta, pst, tb=512):
    n = meta.shape[0]
    return pl.pallas_call(
        _dest_kernel,
        grid=(n // tb,),
        in_specs=[pl.BlockSpec((tb, LANES), lambda i: (i, 0)),
                  pl.BlockSpec((1, LANES), lambda i: (0, 0))],
        out_specs=pl.BlockSpec((tb, LANES), lambda i: (i, 0)),
        out_shape=jax.ShapeDtypeStruct((n, LANES), jnp.int32),
        compiler_params=_params("parallel"),
        name="moe_dest_rows",
    )(meta, pst)


def _zero_tail_kernel(lb_ref, o_ref):
    o_ref[...] = jnp.zeros_like(o_ref)


def _zero_tails(last_blk, total, tmb, nt):
    return pl.pallas_call(
        _zero_tail_kernel,
        grid_spec=pltpu.PrefetchScalarGridSpec(
            num_scalar_prefetch=1,
            grid=(last_blk.shape[0],),
            in_specs=[],
            out_specs=pl.BlockSpec((tmb * nt, LANES), lambda e, lb: (lb[e], 0))),
        out_shape=jax.ShapeDtypeStruct((total * nt, LANES), jnp.int32),
        compiler_params=_params("arbitrary"),
        name="moe_zero_tails",
    )(last_blk)


def _dispatch_kernel(dest_ref, hp_ref, xs_in_ref, xs_ref, sem):
    rows = hp_ref.shape[0]

    def slab_copy(r, k):
        return pltpu.make_async_copy(hp_ref.at[r], xs_ref.at[dest_ref[0, 0, r * TOP_K + k]], sem)

    def start(r, c):
        for k in range(TOP_K):
            slab_copy(r, k).start(priority=k % 2)
        return c

    def wait(r, c):
        for k in range(TOP_K):
            slab_copy(r, k).wait()
        return c

    lax.fori_loop(0, rows, start, 0, unroll=DMA_LOOP_UNROLL)
    lax.fori_loop(0, rows, wait, 0, unroll=DMA_LOOP_UNROLL)


def _dispatch(hp3, dest, xs0, rows):
    n, nt, _ = hp3.shape
    steps = n // rows
    return pl.pallas_call(
        _dispatch_kernel,
        grid=(steps,),
        in_specs=[pl.BlockSpec((1, 1, rows * TOP_K), lambda i: (i, 0, 0), memory_space=pltpu.SMEM),
                  pl.BlockSpec((rows, nt, LANES), lambda i: (i, 0, 0)),
                  pl.BlockSpec(memory_space=pl.ANY)],
        out_specs=pl.BlockSpec(memory_space=pl.ANY),
        out_shape=jax.ShapeDtypeStruct(xs0.shape, xs0.dtype),
        scratch_shapes=[pltpu.SemaphoreType.DMA(())],
        input_output_aliases={2: 0},
        compiler_params=_params("arbitrary"),
        name="moe_dispatch",
    )(dest.reshape(steps, 1, rows * TOP_K), hp3, xs0)


def _by_valid_rows(nv, tmb, compute, o_ref):
    hm = tmb // 2
    tail = o_ref.shape[0] // 2

    @pl.when(nv > hm)
    def _():
        compute(tmb)

    @pl.when((nv > 0) & (nv <= hm))
    def _():
        compute(hm)
        o_ref[tail:, :] = jnp.zeros((tail, o_ref.shape[1]), o_ref.dtype)

    @pl.when(nv == 0)
    def _():
        o_ref[...] = jnp.zeros_like(o_ref)


def _expert_weights(plan, w_hbm, wbuf, sem):
    be_ref, first_ref, slot_ref, nxt_ref = plan
    bi = pl.program_id(0)
    slot = slot_ref[bi]

    def copy(e, s):
        return pltpu.make_async_copy(w_hbm.at[e], wbuf.at[s], sem.at[s])

    @pl.when(bi == 0)
    def _():
        copy(be_ref[0], 0).start()

    @pl.when((first_ref[bi] == 1) & (nxt_ref[bi] >= 0))
    def _():
        copy(nxt_ref[bi], 1 - slot).start()

    @pl.when(first_ref[bi] == 1)
    def _():
        copy(be_ref[bi], slot).wait()

    return wbuf.at[slot]


def _expert_in_kernel(be_ref, first_ref, slot_ref, nxt_ref, nu_ref, nv_ref, x_ref, w_hbm, *rest, nt, tn):
    gate_ref = rest[0] if len(rest) == 4 else None
    o_ref, wbuf, sem = rest[-3:]
    w_ref = _expert_weights((be_ref, first_ref, slot_ref, nxt_ref), w_hbm, wbuf, sem)

    def compute(rows):
        halves = [_unpack_halves(x_ref[pl.ds(j, rows, stride=nt), :]) for j in range(nt)]
        lo = jnp.concatenate([h[0].astype(BF16) for h in halves], axis=1)
        hi = jnp.concatenate([h[1].astype(BF16) for h in halves], axis=1)
        half = nt * LANES
        for c0 in range(0, o_ref.shape[1], tn):
            y = (jnp.dot(lo, w_ref[:half, c0:c0 + tn].astype(BF16), preferred_element_type=F32)
                 + jnp.dot(hi, w_ref[half:, c0:c0 + tn].astype(BF16), preferred_element_type=F32))
            if gate_ref is None:
                y = _silu(y)
            else:
                y = gate_ref[:rows, c0:c0 + tn].astype(F32) * y
            o_ref[:rows, c0:c0 + tn] = y.astype(o_ref.dtype)

    _by_valid_rows(nv_ref[pl.program_id(0)], o_ref.shape[0], compute, o_ref)


N_PLAN = 6


def _blk_clamped(bi, *plan):
    return jnp.minimum(bi, plan[4][0] - 1)


def _expert_in(xs, w, gate, plan, tmb, name, tn=256):
    _, d, ff = w.shape
    nt = d // 2 // LANES
    total = xs.shape[0] // nt
    nblk = total // tmb
    in_specs = [pl.BlockSpec((tmb * nt, LANES), lambda bi, *p: (_blk_clamped(bi, *p), 0)),
                pl.BlockSpec(memory_space=pl.ANY)]
    args = [xs, w]
    if gate is not None:
        in_specs.append(pl.BlockSpec((tmb, ff), lambda bi, *p: (_blk_clamped(bi, *p), 0)))
        args.append(gate)
    return pl.pallas_call(
        functools.partial(_expert_in_kernel, nt=nt, tn=tn),
        grid_spec=pltpu.PrefetchScalarGridSpec(
            num_scalar_prefetch=N_PLAN,
            grid=(nblk,),
            in_specs=in_specs,
            out_specs=pl.BlockSpec((tmb, ff), lambda bi, *p: (bi, 0)),
            scratch_shapes=[pltpu.VMEM((2, d, ff), w.dtype), pltpu.SemaphoreType.DMA((2,))]),
        out_shape=jax.ShapeDtypeStruct((total, ff), BF16),
        compiler_params=_params("arbitrary"),
        name=name,
    )(*plan, *args)


def _expert_down_kernel(be_ref, first_ref, slot_ref, nxt_ref, nu_ref, nv_ref, a_ref, w_hbm, o_ref, wbuf, sem,
                        *, nt, tn):
    tmb = a_ref.shape[0]
    w_ref = _expert_weights((be_ref, first_ref, slot_ref, nxt_ref), w_hbm, wbuf, sem)

    def compute(rows):
        a = a_ref[:rows, :]
        half = nt * LANES
        for c0 in range(0, half, tn):
            ylo = jnp.dot(a, w_ref[:, c0:c0 + tn].astype(BF16), preferred_element_type=F32)
            yhi = jnp.dot(a, w_ref[:, half + c0:half + c0 + tn].astype(BF16), preferred_element_type=F32)
            packed = pltpu.pack_elementwise([ylo, yhi], packed_dtype=BF16)
            for j in range(tn // LANES):
                o_ref[pl.ds(c0 // LANES + j, rows, stride=nt), :] = packed[:, j * LANES:(j + 1) * LANES]

    _by_valid_rows(nv_ref[pl.program_id(0)], tmb, compute, o_ref)


def _expert_down(act, w_down, plan, tmb, tn=512):
    total, ff = act.shape
    d = w_down.shape[2]
    nt = d // 2 // LANES
    nblk = total // tmb
    return pl.pallas_call(
        functools.partial(_expert_down_kernel, nt=nt, tn=tn),
        grid_spec=pltpu.PrefetchScalarGridSpec(
            num_scalar_prefetch=N_PLAN,
            grid=(nblk,),
            in_specs=[pl.BlockSpec((tmb, ff), lambda bi, *p: (_blk_clamped(bi, *p), 0)),
                      pl.BlockSpec(memory_space=pl.ANY)],
            out_specs=pl.BlockSpec((tmb * nt, LANES), lambda bi, *p: (bi, 0)),
            scratch_shapes=[pltpu.VMEM((2, ff, d), w_down.dtype), pltpu.SemaphoreType.DMA((2,))]),
        out_shape=jax.ShapeDtypeStruct((total * nt, LANES), jnp.int32),
        compiler_params=_params("arbitrary"),
        name="expert_down",
    )(*plan, act, w_down)


def _combine_kernel(pos_ref, nxt_ref, ys_ref, w_ref, x_ref, ga_ref, g_ref, o_ref, *scratch):
    n_buf = 2 * TOP_K
    slabs, sems = scratch[:n_buf], scratch[n_buf:]
    rows = x_ref.shape[1]
    hr = rows // 2
    nt = ys_ref.shape[1]
    half = nt * LANES
    step = pl.program_id(0) * pl.num_programs(1) + pl.program_id(1)
    n_steps = pl.num_programs(0) * pl.num_programs(1)

    def slab_copy(idx_ref, h, r, k):
        dst = slabs[h * TOP_K + k].at[pl.ds(pl.multiple_of(r * SLAB_PITCH, SUBLANES), nt)]
        return pltpu.make_async_copy(ys_ref.at[idx_ref[0, 0, (h * hr + r) * TOP_K + k]], dst, sems[h * TOP_K + k])

    def issue(idx_ref, h):
        def body(r, c):
            for k in range(TOP_K):
                slab_copy(idx_ref, h, r, k).start(priority=k % 2)
            return c
        lax.fori_loop(0, hr, body, 0, unroll=DMA_LOOP_UNROLL)

    def wait(idx_ref, h):
        def body(r, c):
            for k in range(TOP_K):
                slab_copy(idx_ref, h, r, k).wait()
            return c
        lax.fori_loop(0, hr, body, 0, unroll=DMA_LOOP_UNROLL)

    def compute(h):
        r0 = h * hr
        w0 = w_ref[r0:r0 + hr, 0:1]
        w1 = w_ref[r0:r0 + hr, 1:2]
        ss = jnp.zeros((hr, 1), F32)
        for j in range(nt):
            lo0, hi0 = _unpack_halves(slabs[h * TOP_K][pl.ds(j, hr, stride=SLAB_PITCH), :])
            lo1, hi1 = _unpack_halves(slabs[h * TOP_K + 1][pl.ds(j, hr, stride=SLAB_PITCH), :])
            for c0, y in ((j * LANES, w0 * lo0 + w1 * lo1), (half + j * LANES, w0 * hi0 + w1 * hi1)):
                z = x_ref[0, r0:r0 + hr, c0:c0 + LANES] + ga_ref[0, :, c0:c0 + LANES] * y
                ss = ss + jnp.sum(z * z, axis=-1, keepdims=True)
                o_ref[0, r0:r0 + hr, c0:c0 + LANES] = z
        inv = lax.rsqrt(ss / (2 * half) + EPS)
        o_ref[0, r0:r0 + hr, :] = o_ref[0, r0:r0 + hr, :] * inv * g_ref[...]

    @pl.when(step == 0)
    def _():
        issue(pos_ref, 0)

    issue(pos_ref, 1)
    wait(pos_ref, 0)
    compute(0)

    @pl.when(step + 1 < n_steps)
    def _():
        issue(nxt_ref, 0)

    wait(pos_ref, 1)
    compute(1)


def _combine(ys3, pos, wts, x, gate, g, rows):
    b, l, d = x.shape
    lb = l // rows
    n_steps = b * lb
    pos3 = pos.reshape(n_steps, 1, rows * TOP_K)
    slab = pltpu.VMEM((rows // 2 * SLAB_PITCH, LANES), ys3.dtype)
    return pl.pallas_call(
        _combine_kernel,
        grid=(b, lb),
        in_specs=[pl.BlockSpec((1, 1, rows * TOP_K), lambda bi, i: (bi * lb + i, 0, 0), memory_space=pltpu.SMEM),
                  pl.BlockSpec((1, 1, rows * TOP_K), lambda bi, i: (jnp.minimum(bi * lb + i + 1, n_steps - 1), 0, 0),
                               memory_space=pltpu.SMEM),
                  pl.BlockSpec(memory_space=pl.ANY),
                  pl.BlockSpec((rows, LANES), lambda bi, i: (bi * lb + i, 0)),
                  pl.BlockSpec((1, rows, d), lambda bi, i: (bi, i, 0)),
                  pl.BlockSpec((1, 1, d), lambda bi, i: (bi, 0, 0)),
                  pl.BlockSpec((1, d), lambda bi, i: (0, 0))],
        out_specs=pl.BlockSpec((1, rows, d), lambda bi, i: (bi, i, 0)),
        out_shape=jax.ShapeDtypeStruct((b, l, d), F32),
        scratch_shapes=[slab] * (2 * TOP_K) + [pltpu.SemaphoreType.DMA(())] * (2 * TOP_K),
        compiler_params=_params("arbitrary", "arbitrary"),
        name="moe_combine_norm",
    )(pos3, pos3, ys3, wts, x, gate, g)


def _block_layout(counts, n_pairs, tmb):
    nblk = (n_pairs + N_EXPERTS * (tmb - 1) + tmb - 1) // tmb
    blocks = (counts + tmb - 1) // tmb
    bend = jnp.cumsum(blocks)
    bstart = bend - blocks
    pstart = bstart * tmb
    n_used = bend[-1]
    ids = jnp.arange(nblk, dtype=jnp.int32)
    blk_e = jnp.minimum(jnp.searchsorted(bend, jnp.minimum(ids, n_used - 1), side="right"),
                        N_EXPERTS - 1).astype(jnp.int32)
    last_blk = jnp.maximum(bend - 1, 0).astype(jnp.int32)
    used = ids < n_used
    n_valid = jnp.where(used, jnp.clip(counts[blk_e] - (ids - bstart[blk_e]) * tmb, 0, tmb), 0)
    first = used & ((ids == 0) | (blk_e != jnp.roll(blk_e, 1)))
    slot = (jnp.cumsum(first.astype(jnp.int32)) - 1) % 2
    experts = jnp.arange(N_EXPERTS, dtype=jnp.int32)
    later = jnp.where((blocks > 0)[None, :] & (experts[None, :] > experts[:, None]), experts[None, :], N_EXPERTS)
    nxt_e = jnp.min(later, axis=1)
    nxt = jnp.where(nxt_e[blk_e] < N_EXPERTS, nxt_e[blk_e], -1)
    i32 = lambda a: a.astype(jnp.int32)
    plan = (blk_e, i32(first), i32(slot), i32(nxt), i32(n_used).reshape(1), i32(n_valid))
    return nblk, i32(pstart), last_blk, plan


def _rope_tables(n_tokens):
    rows = n_tokens // GRID_W
    row, col = jnp.meshgrid(jnp.arange(rows), jnp.arange(GRID_W), indexing="ij")
    pos = jnp.stack([row.reshape(-1), col.reshape(-1)], axis=-1).astype(F32)
    inv = ROPE_THETA ** (-jnp.arange(0, ROPE_AXIS_DIM, 2, dtype=F32) / ROPE_AXIS_DIM)
    ang = pos[:, :, None] * inv[None, None, :]
    cos, sin = jnp.cos(ang), jnp.sin(ang)
    cos_t = jnp.concatenate([cos[:, 0], cos[:, 0], cos[:, 1], cos[:, 1]], axis=-1)
    sin_t = jnp.concatenate([-sin[:, 0], sin[:, 0], -sin[:, 1], sin[:, 1]], axis=-1)
    return cos_t, sin_t


def kernel(x, c, ctx, c_ctx, norm1_g, w_mod, b_mod, w_in, q_norm_g, k_norm_g, w_attn_out, conv_dw_w, conv_dw_b, conv_ln_g, conv_ln_b, w_conv_out, w_out, norm2_g, w_router_group, b_router_group, w_router_expert, b_router_expert, w_exp_gate, w_exp_up, w_exp_down, norm_f_g):
    b, s, d = x.shape
    n_ctx = ctx.shape[1]
    assert w_in.shape[0] == 1, "single-layer stack"
    conv_width = conv_dw_w.shape[-1]
    k_off = ATTN_WIDTH
    glu_off = k_off + 2 * KV_WIDTH
    gate_off = glu_off + 2 * conv_width

    n_c = b + 1
    cvec = jnp.zeros((SUBLANES * ((n_c + SUBLANES - 1) // SUBLANES), d), F32).at[:b].set(c).at[b].set(c_ctx)
    mod = _mod_vectors(cvec, w_mod[0], b_mod.reshape(1, -1))
    sh1, sc1, ga1, sh2, sc2, ga2 = [mod[:b, i * d:(i + 1) * d].reshape(b, 1, d) for i in range(N_MOD)]
    csh1, csc1 = [mod[b:b + 1, i * d:(i + 1) * d].reshape(1, 1, d) for i in range(2)]

    g1 = norm1_g.reshape(1, d)
    h = _norm_mod(x, g1, sh1, sc1, tl=512)
    hc = _norm_mod(ctx, g1, csh1, csc1, tl=n_ctx)
    w_in_b = w_in[0].astype(BF16)
    cos_t, sin_t = _rope_tables(s)
    qg = q_norm_g.reshape(1, HEAD_DIM)
    kg = k_norm_g.reshape(1, HEAD_DIM)
    q = _q_proj(h, w_in_b, qg, cos_t, sin_t, 0, HEAD_DIM ** -0.5 * LOG2E, tm=1024)
    k, v = _kv_proj(h, w_in_b, kg, cos_t, sin_t, k_off, tm=1024)
    kc, vc = _kv_proj(hc, w_in_b, kg, None, None, k_off, tm=n_ctx)
    attn = _attention(q, k, v, kc, vc, tq=128, tqs=128, tk=1024, rb=8)

    h2d = h.reshape(b * s, d)
    u = _glu_proj(h2d, w_in_b, glu_off, conv_width, tm=1024)
    conv = _conv_module(u.reshape(b, s, conv_width), conv_dw_w.reshape(CONV_TAPS, conv_width // LANES, LANES),
                        conv_dw_b.reshape(conv_width // LANES, LANES), conv_ln_g.reshape(1, -1),
                        conv_ln_b.reshape(1, -1), tl=256)
    mrg = _merge(h2d, attn.reshape(b * s, ATTN_WIDTH), conv.reshape(b * s, conv_width), w_in_b, gate_off,
                 w_attn_out[0], w_conv_out[0])
    x1 = _out_proj(mrg.reshape(b, s, d), w_out[0].astype(BF16), x, ga1)

    w_r = jnp.zeros((d, ROUTER_LANES), F32).at[:, :N_GROUPS].set(w_router_group[0]) \
        .at[:, N_GROUPS:N_GROUPS + N_EXPERTS].set(w_router_expert[0])
    b_r = jnp.zeros((1, ROUTER_LANES), F32).at[0, :N_GROUPS].set(b_router_group[0]) \
        .at[0, N_GROUPS:N_GROUPS + N_EXPERTS].set(b_router_expert[0])
    hp, logits = _norm2_router(x1, norm2_g.reshape(1, d), sh2, sc2, w_r.astype(BF16), b_r, tl=256)
    n = b * s
    nt = d // 2 // LANES
    tmb = 512
    meta, wts, cnt = _route(logits.reshape(n, ROUTER_LANES))
    counts = cnt[0, N_GROUPS:N_GROUPS + N_EXPERTS].astype(jnp.int32)
    nblk, pstart, last_blk, plan = _block_layout(counts, n * TOP_K, tmb)
    pst = jnp.zeros((1, LANES), jnp.int32).at[0, N_GROUPS:N_GROUPS + N_EXPERTS].set(pstart)
    dest = _dest_rows(meta, pst)[:, :TOP_K].reshape(-1)
    xs0 = _zero_tails(last_blk, nblk * tmb, tmb, nt)
    xs = _dispatch(hp.reshape(n, nt, LANES), dest, xs0.reshape(nblk * tmb, nt, LANES), rows=256)
    xs2 = xs.reshape(nblk * tmb * nt, LANES)
    sg = _expert_in(xs2, w_exp_gate[0], None, plan, tmb, "expert_gate")
    act = _expert_in(xs2, w_exp_up[0], sg, plan, tmb, "expert_up")
    ys = _expert_down(act, w_exp_down[0], plan, tmb)
    return _combine(ys.reshape(nblk * tmb, nt, LANES), dest, wts, x1, ga2, norm_f_g.reshape(1, d), rows=256)
```

```python
import functools

import jax
import jax.numpy as jnp
from jax import lax
from jax.experimental import pallas as pl
from jax.experimental.pallas import tpu as pltpu

F32 = jnp.float32
BF16 = jnp.bfloat16

GRID_W = 64
HEAD_DIM = 128
N_Q_HEADS = 16
N_KV_HEADS = 4
Q_PER_KV = N_Q_HEADS // N_KV_HEADS
ATTN_WIDTH = N_Q_HEADS * HEAD_DIM
KV_WIDTH = N_KV_HEADS * HEAD_DIM
CONV_TAPS = 31
CONV_HALO = 16
ROPE_THETA = 10000.0
ROPE_AXIS_DIM = HEAD_DIM // 2
N_GROUPS = 4
EXPERTS_PER_GROUP = 8
N_EXPERTS = N_GROUPS * EXPERTS_PER_GROUP
TOP_K = 2
N_MOD = 6
EPS = 1e-6
LOG2E = 1.4426950408889634
LANES = 128
SUBLANES = 8
ROUTER_LANES = LANES
SLAB_PITCH = 24
DMA_LOOP_UNROLL = 4

V7X_VMEM_LIMIT = 56 * 1024 * 1024


def _params(*sem):
    return pltpu.CompilerParams(dimension_semantics=sem, vmem_limit_bytes=V7X_VMEM_LIMIT)


def _sigmoid(x):
    return 1.0 / (1.0 + jnp.exp(-x))


def _silu(x):
    return x * _sigmoid(x)


def _rms(x, g):
    return x * lax.rsqrt(jnp.mean(x * x, axis=-1, keepdims=True) + EPS) * g


def _mod_kernel(c_ref, w_ref, b_ref, o_ref):
    s = _silu(c_ref[...]).astype(BF16)
    o_ref[...] = jnp.dot(s, w_ref[...].astype(BF16), preferred_element_type=F32) + b_ref[...]


def _mod_vectors(cvec, w_mod, b_mod, tn=1024):
    m, d = cvec.shape
    n = w_mod.shape[1]
    return pl.pallas_call(
        _mod_kernel,
        grid=(n // tn,),
        in_specs=[pl.BlockSpec((m, d), lambda j: (0, 0)),
                  pl.BlockSpec((d, tn), lambda j: (0, j)),
                  pl.BlockSpec((1, tn), lambda j: (0, j))],
        out_specs=pl.BlockSpec((m, tn), lambda j: (0, j)),
        out_shape=jax.ShapeDtypeStruct((m, n), F32),
        compiler_params=_params("arbitrary"),
        name="mod_vectors",
    )(cvec, w_mod, b_mod)


def _norm_mod_kernel(x_ref, g_ref, sh_ref, sc_ref, o_ref):
    y = _rms(x_ref[0], g_ref[...])
    o_ref[0] = (y * (1.0 + sc_ref[0]) + sh_ref[0]).astype(o_ref.dtype)


def _norm_mod(x, g, shift, scale, tl):
    b, l, d = x.shape
    per_batch = shift.shape[0] > 1
    mod_map = (lambda bi, li: (bi, 0, 0)) if per_batch else (lambda bi, li: (0, 0, 0))
    return pl.pallas_call(
        _norm_mod_kernel,
        grid=(b, l // tl),
        in_specs=[pl.BlockSpec((1, tl, d), lambda bi, li: (bi, li, 0)),
                  pl.BlockSpec((1, d), lambda bi, li: (0, 0)),
                  pl.BlockSpec((1, 1, d), mod_map),
                  pl.BlockSpec((1, 1, d), mod_map)],
        out_specs=pl.BlockSpec((1, tl, d), lambda bi, li: (bi, li, 0)),
        out_shape=jax.ShapeDtypeStruct((b, l, d), BF16),
        compiler_params=_params("parallel", "parallel"),
        name="norm_modulate",
    )(x, g, shift, scale)


def _head_norm_rope(a, g, cos, sin):
    y = _rms(a, g)
    if cos is None:
        return y
    lane = lax.broadcasted_iota(jnp.int32, y.shape, 1)
    quarter = ROPE_AXIS_DIM // 2
    partner = jnp.where((lane % ROPE_AXIS_DIM) < quarter,
                        pltpu.roll(y, HEAD_DIM - quarter, 1), pltpu.roll(y, quarter, 1))
    return y * cos + partner * sin


def _q_proj_kernel(h_ref, w_ref, g_ref, cos_ref, sin_ref, o_ref, *, scale):
    tm = h_ref.shape[1]
    for r0 in range(0, tm, tm // 4):
        rows = slice(r0, r0 + tm // 4)
        acc = jnp.dot(h_ref[0, rows, :], w_ref[...], preferred_element_type=F32)
        for hh in range(o_ref.shape[1]):
            a = acc[:, hh * HEAD_DIM:(hh + 1) * HEAD_DIM]
            y = _head_norm_rope(a, g_ref[...], cos_ref[rows, :], sin_ref[rows, :])
            o_ref[0, hh, rows, :] = (y * scale).astype(o_ref.dtype)


def _q_proj(h, w, g, cos_t, sin_t, col_off, scale, tm, tn=1024):
    b, l, d = h.shape
    jb = col_off // tn
    hpt = tn // HEAD_DIM
    return pl.pallas_call(
        functools.partial(_q_proj_kernel, scale=scale),
        grid=(b, l // tm, ATTN_WIDTH // tn),
        in_specs=[pl.BlockSpec((1, tm, d), lambda bi, i, j: (bi, i, 0)),
                  pl.BlockSpec((d, tn), lambda bi, i, j: (0, jb + j)),
                  pl.BlockSpec((1, HEAD_DIM), lambda bi, i, j: (0, 0)),
                  pl.BlockSpec((tm, HEAD_DIM), lambda bi, i, j: (i, 0)),
                  pl.BlockSpec((tm, HEAD_DIM), lambda bi, i, j: (i, 0))],
        out_specs=pl.BlockSpec((1, hpt, tm, HEAD_DIM), lambda bi, i, j: (bi, j, i, 0)),
        out_shape=jax.ShapeDtypeStruct((b, N_Q_HEADS, l, HEAD_DIM), BF16),
        compiler_params=_params("parallel", "parallel", "arbitrary"),
        name="q_proj",
    )(h, w, g, cos_t, sin_t)


def _kv_proj_kernel(h_ref, wk_ref, wv_ref, g_ref, *rest, rope):
    if rope:
        cos_ref, sin_ref, k_ref, v_ref = rest
        cos, sin = cos_ref[...], sin_ref[...]
    else:
        k_ref, v_ref = rest
        cos = sin = None
    h = h_ref[0]
    acc = jnp.dot(h, wk_ref[...], preferred_element_type=F32)
    for hh in range(N_KV_HEADS):
        a = acc[:, hh * HEAD_DIM:(hh + 1) * HEAD_DIM]
        k_ref[0, :, hh * HEAD_DIM:(hh + 1) * HEAD_DIM] = _head_norm_rope(a, g_ref[...], cos, sin).astype(k_ref.dtype)
    v_ref[0] = jnp.dot(h, wv_ref[...], preferred_element_type=F32).astype(v_ref.dtype)


def _kv_proj(h, w, g, cos_t, sin_t, k_off, tm):
    b, l, d = h.shape
    rope = cos_t is not None
    jk = k_off // KV_WIDTH
    in_specs = [pl.BlockSpec((1, tm, d), lambda bi, i: (bi, i, 0)),
                pl.BlockSpec((d, KV_WIDTH), lambda bi, i: (0, jk)),
                pl.BlockSpec((d, KV_WIDTH), lambda bi, i: (0, jk + 1)),
                pl.BlockSpec((1, HEAD_DIM), lambda bi, i: (0, 0))]
    args = [h, w, w, g]
    if rope:
        in_specs += [pl.BlockSpec((tm, HEAD_DIM), lambda bi, i: (i, 0))] * 2
        args += [cos_t, sin_t]
    return pl.pallas_call(
        functools.partial(_kv_proj_kernel, rope=rope),
        grid=(b, l // tm),
        in_specs=in_specs,
        out_specs=[pl.BlockSpec((1, tm, KV_WIDTH), lambda bi, i: (bi, i, 0))] * 2,
        out_shape=[jax.ShapeDtypeStruct((b, l, KV_WIDTH), BF16)] * 2,
        compiler_params=_params("parallel", "parallel"),
        name="kv_proj_rope" if rope else "kv_proj_ctx",
    )(*args)


def _glu_proj_kernel(h_ref, wa_ref, wg_ref, o_ref):
    a = jnp.dot(h_ref[...], wa_ref[...], preferred_element_type=F32)
    gt = jnp.dot(h_ref[...], wg_ref[...], preferred_element_type=F32)
    o_ref[...] = (a * _sigmoid(gt)).astype(o_ref.dtype)


def _glu_proj(h2d, w, col_off, width, tm, tn=512):
    m, d = h2d.shape
    ja = col_off // tn
    jg = (col_off + width) // tn
    return pl.pallas_call(
        _glu_proj_kernel,
        grid=(m // tm, width // tn),
        in_specs=[pl.BlockSpec((tm, d), lambda i, j: (i, 0)),
                  pl.BlockSpec((d, tn), lambda i, j: (0, ja + j)),
                  pl.BlockSpec((d, tn), lambda i, j: (0, jg + j))],
        out_specs=pl.BlockSpec((tm, tn), lambda i, j: (i, j)),
        out_shape=jax.ShapeDtypeStruct((m, width), BF16),
        compiler_params=_params("parallel", "arbitrary"),
        name="glu_proj",
    )(h2d, w, w)


def _attn_kernel(q_ref, k_ref, v_ref, kc_ref, vc_ref, o_ref,
                 s0_ref, s1_ref, p0_ref, p1_ref, m_ref, al_ref, acc_ref, *, tk, rb):
    g, tq, dh = q_ref.shape[1:]
    rows = s0_ref.shape[0]
    tqs = rows // g
    chunks = [(k_ref, v_ref, c * tk, tk) for c in range(k_ref.shape[1] // tk)]
    chunks.append((kc_ref, vc_ref, 0, kc_ref.shape[1]))
    units = [(sb, j) for sb in range(tq // tqs) for j in range(len(chunks))]
    s_refs = (s0_ref, s1_ref)
    p_refs = (p0_ref, p1_ref)

    def scores(u):
        sb, j = units[u]
        kr, _, st, n = chunks[j]
        q = q_ref[0, :, sb * tqs:(sb + 1) * tqs, :].reshape(rows, dh)
        s_refs[u % 2][:, :n] = lax.dot_general(q, kr[0, st:st + n, :], (((1,), (1,)), ((), ())),
                                               preferred_element_type=F32)

    def softmax(u):
        j = units[u][1]
        n = chunks[j][3]
        s_ref, p_ref = s_refs[u % 2], p_refs[u % 2]
        for r0 in range(0, rows, rb):
            sblk = s_ref[r0:r0 + rb, :n]
            mn = jnp.max(sblk, axis=-1, keepdims=True)
            if j > 0:
                mo = m_ref[r0:r0 + rb, :]
                mn = jnp.maximum(mo, mn)
                al_ref[r0:r0 + rb, :] = jnp.exp2(mo - mn)
            m_ref[r0:r0 + rb, :] = mn
            p_ref[r0:r0 + rb, :n] = jnp.exp2(sblk - mn).astype(BF16)

    def weighted_values(u):
        j = units[u][1]
        _, vr, st, n = chunks[j]
        ones_col = (lax.broadcasted_iota(jnp.int32, (n, dh), 1) == 0).astype(BF16)
        v1 = jnp.concatenate([vr[0, st:st + n, :], ones_col], axis=1)
        upd = jnp.dot(p_refs[u % 2][:, :n], v1, preferred_element_type=F32)
        if j == 0:
            acc_ref[...] = upd
        else:
            acc_ref[...] = al_ref[...] * acc_ref[...] + upd

    def finish(sb):
        acc = acc_ref[...]
        o = acc[:, :dh] / acc[:, dh:dh + 1]
        for gi in range(g):
            o_ref[0, sb * tqs:(sb + 1) * tqs, gi * dh:(gi + 1) * dh] = o[gi * tqs:(gi + 1) * tqs].astype(o_ref.dtype)

    scores(0)
    for u, (sb, j) in enumerate(units):
        if u + 1 < len(units):
            scores(u + 1)
        softmax(u)
        weighted_values(u)
        if j == len(chunks) - 1:
            finish(sb)


def _attention(q, k, v, kc, vc, tq, tqs, tk, rb=16):
    b, _, l, dh = q.shape
    lc = kc.shape[1]
    rows = Q_PER_KV * tqs
    gdh = Q_PER_KV * dh
    return pl.pallas_call(
        functools.partial(_attn_kernel, tk=tk, rb=rb),
        grid=(b, N_KV_HEADS, l // tq),
        in_specs=[pl.BlockSpec((1, Q_PER_KV, tq, dh), lambda bi, kh, qi: (bi, kh, qi, 0)),
                  pl.BlockSpec((1, l, dh), lambda bi, kh, qi: (bi, 0, kh)),
                  pl.BlockSpec((1, l, dh), lambda bi, kh, qi: (bi, 0, kh)),
                  pl.BlockSpec((1, lc, dh), lambda bi, kh, qi: (bi, 0, kh)),
                  pl.BlockSpec((1, lc, dh), lambda bi, kh, qi: (bi, 0, kh))],
        out_specs=pl.BlockSpec((1, tq, gdh), lambda bi, kh, qi: (bi, qi, kh)),
        out_shape=jax.ShapeDtypeStruct((b, l, ATTN_WIDTH), BF16),
        scratch_shapes=[pltpu.VMEM((rows, tk), F32), pltpu.VMEM((rows, tk), F32),
                        pltpu.VMEM((rows, tk), BF16), pltpu.VMEM((rows, tk), BF16),
                        pltpu.VMEM((rows, 1), F32), pltpu.VMEM((rows, 1), F32),
                        pltpu.VMEM((rows, 2 * dh), F32)],
        compiler_params=_params("parallel", "parallel", "arbitrary"),
        name="attention",
    )(q, k, v, kc, vc)


def _conv_kernel(prev_ref, cur_ref, next_ref, w_ref, b_ref, g_ref, beta_ref, o_ref, win_ref, y_ref, *, tc, rc):
    li = pl.program_id(1)
    tl, c = cur_ref.shape[1:]
    halo = prev_ref.shape[1]
    nt = c // LANES

    def put_tokens(vals, tok0):
        for j in range(nt):
            win_ref[pl.ds(tok0 * nt + j, vals.shape[0], stride=nt), :] = vals[:, j * LANES:(j + 1) * LANES]

    def put_chunk(ci, carry):
        r0 = pl.multiple_of(ci * rc, rc)
        put_tokens(cur_ref[0, pl.ds(r0, rc), :].astype(F32), halo + r0)
        return carry

    put_tokens(jnp.where(li > 0, prev_ref[0].astype(F32), 0.0), 0)
    lax.fori_loop(0, tl // rc, put_chunk, 0)
    put_tokens(jnp.where(li < pl.num_programs(1) - 1, next_ref[0].astype(F32), 0.0), halo + tl)

    first = halo - CONV_TAPS // 2
    bias = b_ref[...][None]

    def token_chunk(ci, carry):
        tok = ci * tc
        acc = jnp.zeros((tc, nt, LANES), F32) + bias
        for t in range(CONV_TAPS):
            r0 = pl.multiple_of((tok + first + t) * nt, nt)
            acc = acc + win_ref[pl.ds(r0, tc * nt), :].reshape(tc, nt, LANES) * w_ref[t][None]
        y_ref[pl.ds(pl.multiple_of(tok * nt, nt), tc * nt), :] = acc.reshape(tc * nt, LANES)
        return carry

    lax.fori_loop(0, tl // tc, token_chunk, 0)

    def norm_chunk(ci, carry):
        r0 = pl.multiple_of(ci * rc, rc)
        y = jnp.concatenate([y_ref[pl.ds(r0 * nt + j, rc, stride=nt), :] for j in range(nt)], axis=1)
        mu = jnp.mean(y, axis=-1, keepdims=True)
        yc = y - mu
        var = jnp.mean(yc * yc, axis=-1, keepdims=True)
        z = yc * lax.rsqrt(var + EPS) * g_ref[...] + beta_ref[...]
        o_ref[0, pl.ds(r0, rc), :] = _silu(z).astype(o_ref.dtype)
        return carry

    lax.fori_loop(0, tl // rc, norm_chunk, 0, unroll=2)


def _conv_module(u, w_dw, b_dw, ln_g, ln_b, tl, tc=16, rc=32):
    b, l, c = u.shape
    nt = c // LANES
    hb = tl // CONV_HALO
    n_halo = l // CONV_HALO
    return pl.pallas_call(
        functools.partial(_conv_kernel, tc=tc, rc=rc),
        grid=(b, l // tl),
        in_specs=[pl.BlockSpec((1, CONV_HALO, c), lambda bi, li: (bi, jnp.maximum(li * hb - 1, 0), 0)),
                  pl.BlockSpec((1, tl, c), lambda bi, li: (bi, li, 0)),
                  pl.BlockSpec((1, CONV_HALO, c), lambda bi, li: (bi, jnp.minimum((li + 1) * hb, n_halo - 1), 0)),
                  pl.BlockSpec((CONV_TAPS, nt, LANES), lambda bi, li: (0, 0, 0)),
                  pl.BlockSpec((nt, LANES), lambda bi, li: (0, 0)),
                  pl.BlockSpec((1, c), lambda bi, li: (0, 0)),
                  pl.BlockSpec((1, c), lambda bi, li: (0, 0))],
        out_specs=pl.BlockSpec((1, tl, c), lambda bi, li: (bi, li, 0)),
        out_shape=jax.ShapeDtypeStruct((b, l, c), BF16),
        scratch_shapes=[pltpu.VMEM(((tl + 2 * CONV_HALO) * nt, LANES), F32), pltpu.VMEM((tl * nt, LANES), F32)],
        compiler_params=_params("parallel", "arbitrary"),
        name="conv_module",
    )(u, u, u, w_dw, b_dw, ln_g, ln_b)


def _merge_kernel(h_ref, a_ref, c_ref, wga_ref, wgc_ref, wa_ref, wc_ref, o_ref):
    h = h_ref[...]
    g_a = _sigmoid(jnp.dot(h, wga_ref[...], preferred_element_type=F32))
    g_c = _sigmoid(jnp.dot(h, wgc_ref[...], preferred_element_type=F32))
    a = jnp.dot(a_ref[...], wa_ref[...].astype(BF16), preferred_element_type=F32)
    cb = jnp.dot(c_ref[...], wc_ref[...].astype(BF16), preferred_element_type=F32)
    o_ref[...] = (g_a * a + g_c * cb).astype(o_ref.dtype)


def _merge(h2d, attn, conv, w_in, gate_off, wa, wc, tm=512, tn=512):
    m, d = h2d.shape
    ka = attn.shape[1]
    kc = conv.shape[1]
    nj = d // tn
    ja = gate_off // tn
    jc = (gate_off + d) // tn
    return pl.pallas_call(
        _merge_kernel,
        grid=(nj, m // tm),
        in_specs=[pl.BlockSpec((tm, d), lambda j, i: (i, 0)),
                  pl.BlockSpec((tm, ka), lambda j, i: (i, 0)),
                  pl.BlockSpec((tm, kc), lambda j, i: (i, 0)),
                  pl.BlockSpec((d, tn), lambda j, i: (0, ja + j)),
                  pl.BlockSpec((d, tn), lambda j, i: (0, jc + j)),
                  pl.BlockSpec((ka, tn), lambda j, i: (0, j)),
                  pl.BlockSpec((kc, tn), lambda j, i: (0, j))],
        out_specs=pl.BlockSpec((tm, tn), lambda j, i: (i, j)),
        out_shape=jax.ShapeDtypeStruct((m, d), BF16),
        compiler_params=_params("parallel", "arbitrary"),
        name="merge_branches",
    )(h2d, attn, conv, w_in, w_in, wa, wc)


def _out_proj_kernel(m_ref, w_ref, x_ref, ga_ref, o_ref):
    acc = jnp.dot(m_ref[0], w_ref[...], preferred_element_type=F32)
    o_ref[0] = x_ref[0] + ga_ref[0] * acc


def _out_proj(mrg, w, x, gate, tm=1024, tn=1024):
    b, l, d = x.shape
    return pl.pallas_call(
        _out_proj_kernel,
        grid=(b, l // tm, d // tn),
        in_specs=[pl.BlockSpec((1, tm, d), lambda bi, i, j: (bi, i, 0)),
                  pl.BlockSpec((d, tn), lambda bi, i, j: (0, j)),
                  pl.BlockSpec((1, tm, tn), lambda bi, i, j: (bi, i, j)),
                  pl.BlockSpec((1, 1, tn), lambda bi, i, j: (bi, 0, j))],
        out_specs=pl.BlockSpec((1, tm, tn), lambda bi, i, j: (bi, i, j)),
        out_shape=jax.ShapeDtypeStruct((b, l, d), F32),
        compiler_params=_params("parallel", "parallel", "arbitrary"),
        name="out_proj_residual",
    )(mrg, w, x, gate)


def _pack_halves(y):
    n = y.shape[1] // 2
    return pltpu.pack_elementwise([y[:, :n], y[:, n:]], packed_dtype=BF16)


def _unpack_halves(p):
    lo = pltpu.unpack_elementwise(p, index=0, packed_dtype=BF16, unpacked_dtype=F32)
    hi = pltpu.unpack_elementwise(p, index=1, packed_dtype=BF16, unpacked_dtype=F32)
    return lo, hi


def _norm2_router_kernel(x_ref, g_ref, sh_ref, sc_ref, wr_ref, br_ref, hp_ref, lg_ref):
    y = _rms(x_ref[0], g_ref[...]) * (1.0 + sc_ref[0]) + sh_ref[0]
    packed = _pack_halves(y)
    tl = packed.shape[0]
    nt = packed.shape[1] // LANES
    for j in range(nt):
        hp_ref[pl.ds(j, tl, stride=nt), :] = packed[:, j * LANES:(j + 1) * LANES]
    lg_ref[0] = jnp.dot(y.astype(BF16), wr_ref[...], preferred_element_type=F32) + br_ref[...]


def _norm2_router(x, g, shift, scale, w_r, b_r, tl):
    b, l, d = x.shape
    nt = d // 2 // LANES
    lb = l // tl
    return pl.pallas_call(
        _norm2_router_kernel,
        grid=(b, lb),
        in_specs=[pl.BlockSpec((1, tl, d), lambda bi, li: (bi, li, 0)),
                  pl.BlockSpec((1, d), lambda bi, li: (0, 0)),
                  pl.BlockSpec((1, 1, d), lambda bi, li: (bi, 0, 0)),
                  pl.BlockSpec((1, 1, d), lambda bi, li: (bi, 0, 0)),
                  pl.BlockSpec((d, ROUTER_LANES), lambda bi, li: (0, 0)),
                  pl.BlockSpec((1, ROUTER_LANES), lambda bi, li: (0, 0))],
        out_specs=[pl.BlockSpec((tl * nt, LANES), lambda bi, li: (bi * lb + li, 0)),
                   pl.BlockSpec((1, tl, ROUTER_LANES), lambda bi, li: (bi, li, 0))],
        out_shape=[jax.ShapeDtypeStruct((b * l * nt, LANES), jnp.int32),
                   jax.ShapeDtypeStruct((b, l, ROUTER_LANES), F32)],
        compiler_params=_params("parallel", "parallel"),
        name="norm2_router",
    )(x, g, shift, scale, w_r, b_r)


def _first_lane(mask, lane):
    return jnp.min(jnp.where(mask, lane, LANES), axis=-1, keepdims=True)


def _route_kernel(lg_ref, meta_ref, wts_ref, cnt_ref, carry_ref):
    @pl.when(pl.program_id(0) == 0)
    def _():
        carry_ref[...] = jnp.zeros_like(carry_ref)

    lg = lg_ref[...]
    tb = lg.shape[0]
    lane = lax.broadcasted_iota(jnp.int32, lg.shape, 1)
    neg_inf = jnp.float32(-jnp.inf)
    is_group = lane < N_GROUPS
    gl = jnp.where(is_group, lg, neg_inf)
    g_max = jnp.max(gl, axis=-1, keepdims=True)
    g_sel = _first_lane(gl == g_max, lane)
    p_g = 1.0 / jnp.sum(jnp.where(is_group, jnp.exp(lg - g_max), 0.0), axis=-1, keepdims=True)

    e_idx = lane - N_GROUPS
    in_group = (e_idx >= g_sel * EXPERTS_PER_GROUP) & (e_idx < (g_sel + 1) * EXPERTS_PER_GROUP)
    ev = jnp.where(in_group, lg, neg_inf)
    v1 = jnp.max(ev, axis=-1, keepdims=True)
    i1 = _first_lane(ev == v1, lane)
    ev2 = jnp.where(lane == i1, neg_inf, ev)
    v2 = jnp.max(ev2, axis=-1, keepdims=True)
    i2 = _first_lane(ev2 == v2, lane)
    t = jnp.exp(v2 - v1)
    w1 = p_g / (1.0 + t)
    w2 = w1 * t

    oh1 = lane == i1
    oh2 = lane == i2
    oh = (oh1 | oh2).astype(BF16)
    earlier = (lax.broadcasted_iota(jnp.int32, (tb, tb), 0) > lax.broadcasted_iota(jnp.int32, (tb, tb), 1)).astype(BF16)
    before = jnp.dot(earlier, oh, preferred_element_type=F32) + carry_ref[...]
    r1 = jnp.sum(jnp.where(oh1, before, 0.0), axis=-1, keepdims=True).astype(jnp.int32)
    r2 = jnp.sum(jnp.where(oh2, before, 0.0), axis=-1, keepdims=True).astype(jnp.int32)
    carry_ref[...] += jnp.sum(oh.astype(F32), axis=0, keepdims=True)

    meta_ref[...] = jnp.where(lane == 0, i1 - N_GROUPS, jnp.where(lane == 1, i2 - N_GROUPS,
                              jnp.where(lane == 2, r1, jnp.where(lane == 3, r2, 0))))
    wts_ref[...] = jnp.where(lane == 0, w1, jnp.where(lane == 1, w2, 0.0))
    cnt_ref[...] = carry_ref[...]


def _route(logits, tb=512):
    n = logits.shape[0]
    return pl.pallas_call(
        _route_kernel,
        grid=(n // tb,),
        in_specs=[pl.BlockSpec((tb, LANES), lambda i: (i, 0))],
        out_specs=[pl.BlockSpec((tb, LANES), lambda i: (i, 0)),
                   pl.BlockSpec((tb, LANES), lambda i: (i, 0)),
                   pl.BlockSpec((1, LANES), lambda i: (0, 0))],
        out_shape=[jax.ShapeDtypeStruct((n, LANES), jnp.int32),
                   jax.ShapeDtypeStruct((n, LANES), F32),
                   jax.ShapeDtypeStruct((1, LANES), F32)],
        scratch_shapes=[pltpu.VMEM((1, LANES), F32)],
        compiler_params=_params("arbitrary"),
        name="moe_route",
    )(logits)


def _dest_kernel(meta_ref, pst_ref, o_ref):
    meta = meta_ref[...]
    lane = lax.broadcasted_iota(jnp.int32, meta.shape, 1)
    pst = pst_ref[...]

    def row_of(slot):
        e = meta[:, slot:slot + 1]
        start = jnp.sum(jnp.where(lane == e + N_GROUPS, pst, 0), axis=-1, keepdims=True)
        return start + meta[:, TOP_K + slot:TOP_K + slot + 1]

    o_ref[...] = jnp.where(lane == 0, row_of(0), jnp.where(lane == 1, row_of(1), 0))


def _dest_rows(meta, pst, tb=512):
    n = meta.shape[0]
    return pl.pallas_call(
        _dest_kernel,
        grid=(n // tb,),
        in_specs=[pl.BlockSpec((tb, LANES), lambda i: (i, 0)),
                  pl.BlockSpec((1, LANES), lambda i: (0, 0))],
        out_specs=pl.BlockSpec((tb, LANES), lambda i: (i, 0)),
        out_shape=jax.ShapeDtypeStruct((n, LANES), jnp.int32),
        compiler_params=_params("parallel"),
        name="moe_dest_rows",
    )(meta, pst)


def _zero_tail_kernel(lb_ref, o_ref):
    o_ref[...] = jnp.zeros_like(o_ref)


def _zero_tails(last_blk, total, tmb, nt):
    return pl.pallas_call(
        _zero_tail_kernel,
        grid_spec=pltpu.PrefetchScalarGridSpec(
            num_scalar_prefetch=1,
            grid=(last_blk.shape[0],),
            in_specs=[],
            out_specs=pl.BlockSpec((tmb * nt, LANES), lambda e, lb: (lb[e], 0))),
        out_shape=jax.ShapeDtypeStruct((total * nt, LANES), jnp.int32),
        compiler_params=_params("arbitrary"),
        name="moe_zero_tails",
    )(last_blk)


def _dispatch_kernel(dest_ref, hp_ref, xs_in_ref, xs_ref, sem):
    rows = hp_ref.shape[0]

    def slab_copy(r, k):
        return pltpu.make_async_copy(hp_ref.at[r], xs_ref.at[dest_ref[0, 0, r * TOP_K + k]], sem)

    def start(r, c):
        for k in range(TOP_K):
            slab_copy(r, k).start(priority=k % 2)
        return c

    def wait(r, c):
        for k in range(TOP_K):
            slab_copy(r, k).wait()
        return c

    lax.fori_loop(0, rows, start, 0, unroll=DMA_LOOP_UNROLL)
    lax.fori_loop(0, rows, wait, 0, unroll=DMA_LOOP_UNROLL)


def _dispatch(hp3, dest, xs0, rows):
    n, nt, _ = hp3.shape
    steps = n // rows
    return pl.pallas_call(
        _dispatch_kernel,
        grid=(steps,),
        in_specs=[pl.BlockSpec((1, 1, rows * TOP_K), lambda i: (i, 0, 0), memory_space=pltpu.SMEM),
                  pl.BlockSpec((rows, nt, LANES), lambda i: (i, 0, 0)),
                  pl.BlockSpec(memory_space=pl.ANY)],
        out_specs=pl.BlockSpec(memory_space=pl.ANY),
        out_shape=jax.ShapeDtypeStruct(xs0.shape, xs0.dtype),
        scratch_shapes=[pltpu.SemaphoreType.DMA(())],
        input_output_aliases={2: 0},
        compiler_params=_params("arbitrary"),
        name="moe_dispatch",
    )(dest.reshape(steps, 1, rows * TOP_K), hp3, xs0)


def _by_valid_rows(nv, tmb, compute, o_ref):
    hm = tmb // 2
    tail = o_ref.shape[0] // 2

    @pl.when(nv > hm)
    def _():
        compute(tmb)

    @pl.when((nv > 0) & (nv <= hm))
    def _():
        compute(hm)
        o_ref[tail:, :] = jnp.zeros((tail, o_ref.shape[1]), o_ref.dtype)

    @pl.when(nv == 0)
    def _():
        o_ref[...] = jnp.zeros_like(o_ref)


def _expert_weights(plan, w_hbm, wbuf, sem):
    be_ref, first_ref, slot_ref, nxt_ref = plan
    bi = pl.program_id(0)
    slot = slot_ref[bi]

    def copy(e, s):
        return pltpu.make_async_copy(w_hbm.at[e], wbuf.at[s], sem.at[s])

    @pl.when(bi == 0)
    def _():
        copy(be_ref[0], 0).start()

    @pl.when((first_ref[bi] == 1) & (nxt_ref[bi] >= 0))
    def _():
        copy(nxt_ref[bi], 1 - slot).start()

    @pl.when(first_ref[bi] == 1)
    def _():
        copy(be_ref[bi], slot).wait()

    return wbuf.at[slot]


def _expert_in_kernel(be_ref, first_ref, slot_ref, nxt_ref, nu_ref, nv_ref, x_ref, w_hbm, *rest, nt, tn):
    gate_ref = rest[0] if len(rest) == 4 else None
    o_ref, wbuf, sem = rest[-3:]
    w_ref = _expert_weights((be_ref, first_ref, slot_ref, nxt_ref), w_hbm, wbuf, sem)

    def compute(rows):
        halves = [_unpack_halves(x_ref[pl.ds(j, rows, stride=nt), :]) for j in range(nt)]
        lo = jnp.concatenate([h[0].astype(BF16) for h in halves], axis=1)
        hi = jnp.concatenate([h[1].astype(BF16) for h in halves], axis=1)
        half = nt * LANES
        for c0 in range(0, o_ref.shape[1], tn):
            y = (jnp.dot(lo, w_ref[:half, c0:c0 + tn].astype(BF16), preferred_element_type=F32)
                 + jnp.dot(hi, w_ref[half:, c0:c0 + tn].astype(BF16), preferred_element_type=F32))
            if gate_ref is None:
                y = _silu(y)
            else:
                y = gate_ref[:rows, c0:c0 + tn].astype(F32) * y
            o_ref[:rows, c0:c0 + tn] = y.astype(o_ref.dtype)

    _by_valid_rows(nv_ref[pl.program_id(0)], o_ref.shape[0], compute, o_ref)


N_PLAN = 6


def _blk_clamped(bi, *plan):
    return jnp.minimum(bi, plan[4][0] - 1)


def _expert_in(xs, w, gate, plan, tmb, name, tn=256):
    _, d, ff = w.shape
    nt = d // 2 // LANES
    total = xs.shape[0] // nt
    nblk = total // tmb
    in_specs = [pl.BlockSpec((tmb * nt, LANES), lambda bi, *p: (_blk_clamped(bi, *p), 0)),
                pl.BlockSpec(memory_space=pl.ANY)]
    args = [xs, w]
    if gate is not None:
        in_specs.append(pl.BlockSpec((tmb, ff), lambda bi, *p: (_blk_clamped(bi, *p), 0)))
        args.append(gate)
    return pl.pallas_call(
        functools.partial(_expert_in_kernel, nt=nt, tn=tn),
        grid_spec=pltpu.PrefetchScalarGridSpec(
            num_scalar_prefetch=N_PLAN,
            grid=(nblk,),
            in_specs=in_specs,
            out_specs=pl.BlockSpec((tmb, ff), lambda bi, *p: (bi, 0)),
            scratch_shapes=[pltpu.VMEM((2, d, ff), w.dtype), pltpu.SemaphoreType.DMA((2,))]),
        out_shape=jax.ShapeDtypeStruct((total, ff), BF16),
        compiler_params=_params("arbitrary"),
        name=name,
    )(*plan, *args)


def _expert_down_kernel(be_ref, first_ref, slot_ref, nxt_ref, nu_ref, nv_ref, a_ref, w_hbm, o_ref, wbuf, sem,
                        *, nt, tn):
    tmb = a_ref.shape[0]
    w_ref = _expert_weights((be_ref, first_ref, slot_ref, nxt_ref), w_hbm, wbuf, sem)

    def compute(rows):
        a = a_ref[:rows, :]
        half = nt * LANES
        for c0 in range(0, half, tn):
            ylo = jnp.dot(a, w_ref[:, c0:c0 + tn].astype(BF16), preferred_element_type=F32)
            yhi = jnp.dot(a, w_ref[:, half + c0:half + c0 + tn].astype(BF16), preferred_element_type=F32)
            packed = pltpu.pack_elementwise([ylo, yhi], packed_dtype=BF16)
            for j in range(tn // LANES):
                o_ref[pl.ds(c0 // LANES + j, rows, stride=nt), :] = packed[:, j * LANES:(j + 1) * LANES]

    _by_valid_rows(nv_ref[pl.program_id(0)], tmb, compute, o_ref)


def _expert_down(act, w_down, plan, tmb, tn=512):
    total, ff = act.shape
    d = w_down.shape[2]
    nt = d // 2 // LANES
    nblk = total // tmb
    return pl.pallas_call(
        functools.partial(_expert_down_kernel, nt=nt, tn=tn),
        grid_spec=pltpu.PrefetchScalarGridSpec(
            num_scalar_prefetch=N_PLAN,
            grid=(nblk,),
            in_specs=[pl.BlockSpec((tmb, ff), lambda bi, *p: (_blk_clamped(bi, *p), 0)),
                      pl.BlockSpec(memory_space=pl.ANY)],
            out_specs=pl.BlockSpec((tmb * nt, LANES), lambda bi, *p: (bi, 0)),
            scratch_shapes=[pltpu.VMEM((2, ff, d), w_down.dtype), pltpu.SemaphoreType.DMA((2,))]),
        out_shape=jax.ShapeDtypeStruct((total * nt, LANES), jnp.int32),
        compiler_params=_params("arbitrary"),
        name="expert_down",
    )(*plan, act, w_down)


def _combine_kernel(pos_ref, nxt_ref, ys_ref, w_ref, x_ref, ga_ref, g_ref, o_ref, *scratch):
    n_buf = 2 * TOP_K
    slabs, sems = scratch[:n_buf], scratch[n_buf:]
    rows = x_ref.shape[1]
    hr = rows // 2
    nt = ys_ref.shape[1]
    half = nt * LANES
    step = pl.program_id(0) * pl.num_programs(1) + pl.program_id(1)
    n_steps = pl.num_programs(0) * pl.num_programs(1)

    def slab_copy(idx_ref, h, r, k):
        dst = slabs[h * TOP_K + k].at[pl.ds(pl.multiple_of(r * SLAB_PITCH, SUBLANES), nt)]
        return pltpu.make_async_copy(ys_ref.at[idx_ref[0, 0, (h * hr + r) * TOP_K + k]], dst, sems[h * TOP_K + k])

    def issue(idx_ref, h):
        def body(r, c):
            for k in range(TOP_K):
                slab_copy(idx_ref, h, r, k).start(priority=k % 2)
            return c
        lax.fori_loop(0, hr, body, 0, unroll=DMA_LOOP_UNROLL)

    def wait(idx_ref, h):
        def body(r, c):
            for k in range(TOP_K):
                slab_copy(idx_ref, h, r, k).wait()
            return c
        lax.fori_loop(0, hr, body, 0, unroll=DMA_LOOP_UNROLL)

    def compute(h):
        r0 = h * hr
        w0 = w_ref[r0:r0 + hr, 0:1]
        w1 = w_ref[r0:r0 + hr, 1:2]
        ss = jnp.zeros((hr, 1), F32)
        for j in range(nt):
            lo0, hi0 = _unpack_halves(slabs[h * TOP_K][pl.ds(j, hr, stride=SLAB_PITCH), :])
            lo1, hi1 = _unpack_halves(slabs[h * TOP_K + 1][pl.ds(j, hr, stride=SLAB_PITCH), :])
            for c0, y in ((j * LANES, w0 * lo0 + w1 * lo1), (half + j * LANES, w0 * hi0 + w1 * hi1)):
                z = x_ref[0, r0:r0 + hr, c0:c0 + LANES] + ga_ref[0, :, c0:c0 + LANES] * y
                ss = ss + jnp.sum(z * z, axis=-1, keepdims=True)
                o_ref[0, r0:r0 + hr, c0:c0 + LANES] = z
        inv = lax.rsqrt(ss / (2 * half) + EPS)
        o_ref[0, r0:r0 + hr, :] = o_ref[0, r0:r0 + hr, :] * inv * g_ref[...]

    @pl.when(step == 0)
    def _():
        issue(pos_ref, 0)

    issue(pos_ref, 1)
    wait(pos_ref, 0)
    compute(0)

    @pl.when(step + 1 < n_steps)
    def _():
        issue(nxt_ref, 0)

    wait(pos_ref, 1)
    compute(1)


def _combine(ys3, pos, wts, x, gate, g, rows):
    b, l, d = x.shape
    lb = l // rows
    n_steps = b * lb
    pos3 = pos.reshape(n_steps, 1, rows * TOP_K)
    slab = pltpu.VMEM((rows // 2 * SLAB_PITCH, LANES), ys3.dtype)
    return pl.pallas_call(
        _combine_kernel,
        grid=(b, lb),
        in_specs=[pl.BlockSpec((1, 1, rows * TOP_K), lambda bi, i: (bi * lb + i, 0, 0), memory_space=pltpu.SMEM),
                  pl.BlockSpec((1, 1, rows * TOP_K), lambda bi, i: (jnp.minimum(bi * lb + i + 1, n_steps - 1), 0, 0),
                               memory_space=pltpu.SMEM),
                  pl.BlockSpec(memory_space=pl.ANY),
                  pl.BlockSpec((rows, LANES), lambda bi, i: (bi * lb + i, 0)),
                  pl.BlockSpec((1, rows, d), lambda bi, i: (bi, i, 0)),
                  pl.BlockSpec((1, 1, d), lambda bi, i: (bi, 0, 0)),
                  pl.BlockSpec((1, d), lambda bi, i: (0, 0))],
        out_specs=pl.BlockSpec((1, rows, d), lambda bi, i: (bi, i, 0)),
        out_shape=jax.ShapeDtypeStruct((b, l, d), F32),
        scratch_shapes=[slab] * (2 * TOP_K) + [pltpu.SemaphoreType.DMA(())] * (2 * TOP_K),
        compiler_params=_params("arbitrary", "arbitrary"),
        name="moe_combine_norm",
    )(pos3, pos3, ys3, wts, x, gate, g)


def _block_layout(counts, n_pairs, tmb):
    nblk = (n_pairs + N_EXPERTS * (tmb - 1) + tmb - 1) // tmb
    blocks = (counts + tmb - 1) // tmb
    bend = jnp.cumsum(blocks)
    bstart = bend - blocks
    pstart = bstart * tmb
    n_used = bend[-1]
    ids = jnp.arange(nblk, dtype=jnp.int32)
    blk_e = jnp.minimum(jnp.searchsorted(bend, jnp.minimum(ids, n_used - 1), side="right"),
                        N_EXPERTS - 1).astype(jnp.int32)
    last_blk = jnp.maximum(bend - 1, 0).astype(jnp.int32)
    used = ids < n_used
    n_valid = jnp.where(used, jnp.clip(counts[blk_e] - (ids - bstart[blk_e]) * tmb, 0, tmb), 0)
    first = used & ((ids == 0) | (blk_e != jnp.roll(blk_e, 1)))
    slot = (jnp.cumsum(first.astype(jnp.int32)) - 1) % 2
    experts = jnp.arange(N_EXPERTS, dtype=jnp.int32)
    later = jnp.where((blocks > 0)[None, :] & (experts[None, :] > experts[:, None]), experts[None, :], N_EXPERTS)
    nxt_e = jnp.min(later, axis=1)
    nxt = jnp.where(nxt_e[blk_e] < N_EXPERTS, nxt_e[blk_e], -1)
    i32 = lambda a: a.astype(jnp.int32)
    plan = (blk_e, i32(first), i32(slot), i32(nxt), i32(n_used).reshape(1), i32(n_valid))
    return nblk, i32(pstart), last_blk, plan


def _rope_tables(n_tokens):
    rows = n_tokens // GRID_W
    row, col = jnp.meshgrid(jnp.arange(rows), jnp.arange(GRID_W), indexing="ij")
    pos = jnp.stack([row.reshape(-1), col.reshape(-1)], axis=-1).astype(F32)
    inv = ROPE_THETA ** (-jnp.arange(0, ROPE_AXIS_DIM, 2, dtype=F32) / ROPE_AXIS_DIM)
    ang = pos[:, :, None] * inv[None, None, :]
    cos, sin = jnp.cos(ang), jnp.sin(ang)
    cos_t = jnp.concatenate([cos[:, 0], cos[:, 0], cos[:, 1], cos[:, 1]], axis=-1)
    sin_t = jnp.concatenate([-sin[:, 0], sin[:, 0], -sin[:, 1], sin[:, 1]], axis=-1)
    return cos_t, sin_t


def kernel(x, c, ctx, c_ctx, norm1_g, w_mod, b_mod, w_in, q_norm_g, k_norm_g, w_attn_out, conv_dw_w, conv_dw_b, conv_ln_g, conv_ln_b, w_conv_out, w_out, norm2_g, w_router_group, b_router_group, w_router_expert, b_router_expert, w_exp_gate, w_exp_up, w_exp_down, norm_f_g):
    b, s, d = x.shape
    n_ctx = ctx.shape[1]
    assert w_in.shape[0] == 1, "single-layer stack"
    conv_width = conv_dw_w.shape[-1]
    k_off = ATTN_WIDTH
    glu_off = k_off + 2 * KV_WIDTH
    gate_off = glu_off + 2 * conv_width

    n_c = b + 1
    cvec = jnp.zeros((SUBLANES * ((n_c + SUBLANES - 1) // SUBLANES), d), F32).at[:b].set(c).at[b].set(c_ctx)
    mod = _mod_vectors(cvec, w_mod[0], b_mod.reshape(1, -1))
    sh1, sc1, ga1, sh2, sc2, ga2 = [mod[:b, i * d:(i + 1) * d].reshape(b, 1, d) for i in range(N_MOD)]
    csh1, csc1 = [mod[b:b + 1, i * d:(i + 1) * d].reshape(1, 1, d) for i in range(2)]

    g1 = norm1_g.reshape(1, d)
    h = _norm_mod(x, g1, sh1, sc1, tl=512)
    hc = _norm_mod(ctx, g1, csh1, csc1, tl=n_ctx)
    w_in_b = w_in[0].astype(BF16)
    cos_t, sin_t = _rope_tables(s)
    qg = q_norm_g.reshape(1, HEAD_DIM)
    kg = k_norm_g.reshape(1, HEAD_DIM)
    q = _q_proj(h, w_in_b, qg, cos_t, sin_t, 0, HEAD_DIM ** -0.5 * LOG2E, tm=1024)
    k, v = _kv_proj(h, w_in_b, kg, cos_t, sin_t, k_off, tm=1024)
    kc, vc = _kv_proj(hc, w_in_b, kg, None, None, k_off, tm=n_ctx)
    attn = _attention(q, k, v, kc, vc, tq=128, tqs=128, tk=1024, rb=8)

    h2d = h.reshape(b * s, d)
    u = _glu_proj(h2d, w_in_b, glu_off, conv_width, tm=1024)
    conv = _conv_module(u.reshape(b, s, conv_width), conv_dw_w.reshape(CONV_TAPS, conv_width // LANES, LANES),
                        conv_dw_b.reshape(conv_width // LANES, LANES), conv_ln_g.reshape(1, -1),
                        conv_ln_b.reshape(1, -1), tl=256)
    mrg = _merge(h2d, attn.reshape(b * s, ATTN_WIDTH), conv.reshape(b * s, conv_width), w_in_b, gate_off,
                 w_attn_out[0], w_conv_out[0])
    x1 = _out_proj(mrg.reshape(b, s, d), w_out[0].astype(BF16), x, ga1)

    w_r = jnp.zeros((d, ROUTER_LANES), F32).at[:, :N_GROUPS].set(w_router_group[0]) \
        .at[:, N_GROUPS:N_GROUPS + N_EXPERTS].set(w_router_expert[0])
    b_r = jnp.zeros((1, ROUTER_LANES), F32).at[0, :N_GROUPS].set(b_router_group[0]) \
        .at[0, N_GROUPS:N_GROUPS + N_EXPERTS].set(b_router_expert[0])
    hp, logits = _norm2_router(x1, norm2_g.reshape(1, d), sh2, sc2, w_r.astype(BF16), b_r, tl=256)
    n = b * s
    nt = d // 2 // LANES
    tmb = 512
    meta, wts, cnt = _route(logits.reshape(n, ROUTER_LANES))
    counts = cnt[0, N_GROUPS:N_GROUPS + N_EXPERTS].astype(jnp.int32)
    nblk, pstart, last_blk, plan = _block_layout(counts, n * TOP_K, tmb)
    pst = jnp.zeros((1, LANES), jnp.int32).at[0, N_GROUPS:N_GROUPS + N_EXPERTS].set(pstart)
    dest = _dest_rows(meta, pst)[:, :TOP_K].reshape(-1)
    xs0 = _zero_tails(last_blk, nblk * tmb, tmb, nt)
    xs = _dispatch(hp.reshape(n, nt, LANES), dest, xs0.reshape(nblk * tmb, nt, LANES), rows=256)
    xs2 = xs.reshape(nblk * tmb * nt, LANES)
    sg = _expert_in(xs2, w_exp_gate[0], None, plan, tmb, "expert_gate")
    act = _expert_in(xs2, w_exp_up[0], sg, plan, tmb, "expert_up")
    ys = _expert_down(act, w_exp_down[0], plan, tmb)
    return _combine(ys.reshape(nblk * tmb, nt, LANES), dest, wts, x1, ga2, norm_f_g.reshape(1, d), rows=256)
```

```python
import functools

import jax
import jax.numpy as jnp
from jax import lax
from jax.experimental import pallas as pl
from jax.experimental.pallas import tpu as pltpu

F32 = jnp.float32
BF16 = jnp.bfloat16

GRID_W = 64
HEAD_DIM = 128
N_Q_HEADS = 16
N_KV_HEADS = 4
Q_PER_KV = N_Q_HEADS // N_KV_HEADS
ATTN_WIDTH = N_Q_HEADS * HEAD_DIM
KV_WIDTH = N_KV_HEADS * HEAD_DIM
CONV_TAPS = 31
CONV_HALO = 16
ROPE_THETA = 10000.0
ROPE_AXIS_DIM = HEAD_DIM // 2
N_GROUPS = 4
EXPERTS_PER_GROUP = 8
N_EXPERTS = N_GROUPS * EXPERTS_PER_GROUP
TOP_K = 2
N_MOD = 6
EPS = 1e-6
LOG2E = 1.4426950408889634
LANES = 128
SUBLANES = 8
ROUTER_LANES = LANES
SLAB_PITCH = 24
DMA_LOOP_UNROLL = 4

V7X_VMEM_LIMIT = 56 * 1024 * 1024


def _params(*sem):
    return pltpu.CompilerParams(dimension_semantics=sem, vmem_limit_bytes=V7X_VMEM_LIMIT)


def _sigmoid(x):
    return 1.0 / (1.0 + jnp.exp(-x))


def _silu(x):
    return x * _sigmoid(x)


def _rms(x, g):
    return x * lax.rsqrt(jnp.mean(x * x, axis=-1, keepdims=True) + EPS) * g


def _mod_kernel(c_ref, w_ref, b_ref, o_ref):
    s = _silu(c_ref[...]).astype(BF16)
    o_ref[...] = jnp.dot(s, w_ref[...].astype(BF16), preferred_element_type=F32) + b_ref[...]


def _mod_vectors(cvec, w_mod, b_mod, tn=1024):
    m, d = cvec.shape
    n = w_mod.shape[1]
    return pl.pallas_call(
        _mod_kernel,
        grid=(n // tn,),
        in_specs=[pl.BlockSpec((m, d), lambda j: (0, 0)),
                  pl.BlockSpec((d, tn), lambda j: (0, j)),
                  pl.BlockSpec((1, tn), lambda j: (0, j))],
        out_specs=pl.BlockSpec((m, tn), lambda j: (0, j)),
        out_shape=jax.ShapeDtypeStruct((m, n), F32),
        compiler_params=_params("arbitrary"),
        name="mod_vectors",
    )(cvec, w_mod, b_mod)


def _norm_mod_kernel(x_ref, g_ref, sh_ref, sc_ref, o_ref):
    y = _rms(x_ref[0], g_ref[...])
    o_ref[0] = (y * (1.0 + sc_ref[0]) + sh_ref[0]).astype(o_ref.dtype)


def _norm_mod(x, g, shift, scale, tl):
    b, l, d = x.shape
    per_batch = shift.shape[0] > 1
    mod_map = (lambda bi, li: (bi, 0, 0)) if per_batch else (lambda bi, li: (0, 0, 0))
    return pl.pallas_call(
        _norm_mod_kernel,
        grid=(b, l // tl),
        in_specs=[pl.BlockSpec((1, tl, d), lambda bi, li: (bi, li, 0)),
                  pl.BlockSpec((1, d), lambda bi, li: (0, 0)),
                  pl.BlockSpec((1, 1, d), mod_map),
                  pl.BlockSpec((1, 1, d), mod_map)],
        out_specs=pl.BlockSpec((1, tl, d), lambda bi, li: (bi, li, 0)),
        out_shape=jax.ShapeDtypeStruct((b, l, d), BF16),
        compiler_params=_params("parallel", "parallel"),
        name="norm_modulate",
    )(x, g, shift, scale)


def _head_norm_rope(a, g, cos, sin):
    y = _rms(a, g)
    if cos is None:
        return y
    lane = lax.broadcasted_iota(jnp.int32, y.shape, 1)
    quarter = ROPE_AXIS_DIM // 2
    partner = jnp.where((lane % ROPE_AXIS_DIM) < quarter,
                        pltpu.roll(y, HEAD_DIM - quarter, 1), pltpu.roll(y, quarter, 1))
    return y * cos + partner * sin


def _q_proj_kernel(h_ref, w_ref, g_ref, cos_ref, sin_ref, o_ref, *, scale):
    tm = h_ref.shape[1]
    for r0 in range(0, tm, tm // 4):
        rows = slice(r0, r0 + tm // 4)
        acc = jnp.dot(h_ref[0, rows, :], w_ref[...], preferred_element_type=F32)
        for hh in range(o_ref.shape[1]):
            a = acc[:, hh * HEAD_DIM:(hh + 1) * HEAD_DIM]
            y = _head_norm_rope(a, g_ref[...], cos_ref[rows, :], sin_ref[rows, :])
            o_ref[0, hh, rows, :] = (y * scale).astype(o_ref.dtype)


def _q_proj(h, w, g, cos_t, sin_t, col_off, scale, tm, tn=1024):
    b, l, d = h.shape
    jb = col_off // tn
    hpt = tn // HEAD_DIM
    return pl.pallas_call(
        functools.partial(_q_proj_kernel, scale=scale),
        grid=(b, l // tm, ATTN_WIDTH // tn),
        in_specs=[pl.BlockSpec((1, tm, d), lambda bi, i, j: (bi, i, 0)),
                  pl.BlockSpec((d, tn), lambda bi, i, j: (0, jb + j)),
                  pl.BlockSpec((1, HEAD_DIM), lambda bi, i, j: (0, 0)),
                  pl.BlockSpec((tm, HEAD_DIM), lambda bi, i, j: (i, 0)),
                  pl.BlockSpec((tm, HEAD_DIM), lambda bi, i, j: (i, 0))],
        out_specs=pl.BlockSpec((1, hpt, tm, HEAD_DIM), lambda bi, i, j: (bi, j, i, 0)),
        out_shape=jax.ShapeDtypeStruct((b, N_Q_HEADS, l, HEAD_DIM), BF16),
        compiler_params=_params("parallel", "parallel", "arbitrary"),
        name="q_proj",
    )(h, w, g, cos_t, sin_t)


def _kv_proj_kernel(h_ref, wk_ref, wv_ref, g_ref, *rest, rope):
    if rope:
        cos_ref, sin_ref, k_ref, v_ref = rest
        cos, sin = cos_ref[...], sin_ref[...]
    else:
        k_ref, v_ref = rest
        cos = sin = None
    h = h_ref[0]
    acc = jnp.dot(h, wk_ref[...], preferred_element_type=F32)
    for hh in range(N_KV_HEADS):
        a = acc[:, hh * HEAD_DIM:(hh + 1) * HEAD_DIM]
        k_ref[0, :, hh * HEAD_DIM:(hh + 1) * HEAD_DIM] = _head_norm_rope(a, g_ref[...], cos, sin).astype(k_ref.dtype)
    v_ref[0] = jnp.dot(h, wv_ref[...], preferred_element_type=F32).astype(v_ref.dtype)


def _kv_proj(h, w, g, cos_t, sin_t, k_off, tm):
    b, l, d = h.shape
    rope = cos_t is not None
    jk = k_off // KV_WIDTH
    in_specs = [pl.BlockSpec((1, tm, d), lambda bi, i: (bi, i, 0)),
                pl.BlockSpec((d, KV_WIDTH), lambda bi, i: (0, jk)),
                pl.BlockSpec((d, KV_WIDTH), lambda bi, i: (0, jk + 1)),
                pl.BlockSpec((1, HEAD_DIM), lambda bi, i: (0, 0))]
    args = [h, w, w, g]
    if rope:
        in_specs += [pl.BlockSpec((tm, HEAD_DIM), lambda bi, i: (i, 0))] * 2
        args += [cos_t, sin_t]
    return pl.pallas_call(
        functools.partial(_kv_proj_kernel, rope=rope),
        grid=(b, l // tm),
        in_specs=in_specs,
        out_specs=[pl.BlockSpec((1, tm, KV_WIDTH), lambda bi, i: (bi, i, 0))] * 2,
        out_shape=[jax.ShapeDtypeStruct((b, l, KV_WIDTH), BF16)] * 2,
        compiler_params=_params("parallel", "parallel"),
        name="kv_proj_rope" if rope else "kv_proj_ctx",
    )(*args)


def _glu_proj_kernel(h_ref, wa_ref, wg_ref, o_ref):
    a = jnp.dot(h_ref[...], wa_ref[...], preferred_element_type=F32)
    gt = jnp.dot(h_ref[...], wg_ref[...], preferred_element_type=F32)
    o_ref[...] = (a * _sigmoid(gt)).astype(o_ref.dtype)


def _glu_proj(h2d, w, col_off, width, tm, tn=512):
    m, d = h2d.shape
    ja = col_off // tn
    jg = (col_off + width) // tn
    return pl.pallas_call(
        _glu_proj_kernel,
        grid=(m // tm, width // tn),
        in_specs=[pl.BlockSpec((tm, d), lambda i, j: (i, 0)),
                  pl.BlockSpec((d, tn), lambda i, j: (0, ja + j)),
                  pl.BlockSpec((d, tn), lambda i, j: (0, jg + j))],
        out_specs=pl.BlockSpec((tm, tn), lambda i, j: (i, j)),
        out_shape=jax.ShapeDtypeStruct((m, width), BF16),
        compiler_params=_params("parallel", "arbitrary"),
        name="glu_proj",
    )(h2d, w, w)


def _attn_kernel(q_ref, k_ref, v_ref, kc_ref, vc_ref, o_ref, *scratch, tk, rb, n_groups):
    g, tq, dh = q_ref.shape[1:]
    per_group = len(scratch) // n_groups
    tqs = tq // n_groups
    rows = g * tqs
    chunks = [(k_ref, v_ref, c * tk, tk) for c in range(k_ref.shape[1] // tk)]
    chunks.append((kc_ref, vc_ref, 0, kc_ref.shape[1]))

    def bufs(gi):
        s0, s1, p0, p1, m_ref, al_ref, acc_ref = scratch[gi * per_group:(gi + 1) * per_group]
        return (s0, s1), (p0, p1), m_ref, al_ref, acc_ref

    def scores(j, gi):
        kr, _, st, n = chunks[j]
        q = q_ref[0, :, gi * tqs:(gi + 1) * tqs, :].reshape(rows, dh)
        bufs(gi)[0][j % 2][:, :n] = lax.dot_general(q, kr[0, st:st + n, :], (((1,), (1,)), ((), ())),
                                                    preferred_element_type=F32)

    def softmax(j, gi):
        n = chunks[j][3]
        s_refs, p_refs, m_ref, al_ref, _ = bufs(gi)
        s_ref, p_ref = s_refs[j % 2], p_refs[j % 2]
        for r0 in range(0, rows, rb):
            sblk = s_ref[r0:r0 + rb, :n]
            mn = jnp.max(sblk, axis=-1, keepdims=True)
            if j > 0:
                mo = m_ref[r0:r0 + rb, :]
                mn = jnp.maximum(mo, mn)
                al_ref[r0:r0 + rb, :] = jnp.exp2(mo - mn)
            m_ref[r0:r0 + rb, :] = mn
            p_ref[r0:r0 + rb, :n] = jnp.exp2(sblk - mn).astype(BF16)

    def weighted_values(j, gi):
        _, vr, st, n = chunks[j]
        _, p_refs, _, al_ref, acc_ref = bufs(gi)
        ones_col = (lax.broadcasted_iota(jnp.int32, (n, dh), 1) == 0).astype(BF16)
        v1 = jnp.concatenate([vr[0, st:st + n, :], ones_col], axis=1)
        upd = jnp.dot(p_refs[j % 2][:, :n], v1, preferred_element_type=F32)
        if j == 0:
            acc_ref[...] = upd
        else:
            acc_ref[...] = al_ref[...] * acc_ref[...] + upd

    def finish(gi):
        acc = bufs(gi)[4][...]
        o = acc[:, :dh] / acc[:, dh:dh + 1]
        for hi in range(g):
            o_ref[0, gi * tqs:(gi + 1) * tqs, hi * dh:(hi + 1) * dh] = o[hi * tqs:(hi + 1) * tqs].astype(o_ref.dtype)

    for gi in range(n_groups):
        scores(0, gi)
    for j in range(len(chunks)):
        for gi in range(n_groups):
            if j + 1 < len(chunks):
                scores(j + 1, gi)
            softmax(j, gi)
            weighted_values(j, gi)
    for gi in range(n_groups):
        finish(gi)


def _attention(q, k, v, kc, vc, tq, n_groups, tk, rb=8):
    b, _, l, dh = q.shape
    lc = kc.shape[1]
    rows = Q_PER_KV * tq // n_groups
    gdh = Q_PER_KV * dh
    group_scratch = [pltpu.VMEM((rows, tk), F32), pltpu.VMEM((rows, tk), F32),
                     pltpu.VMEM((rows, tk), BF16), pltpu.VMEM((rows, tk), BF16),
                     pltpu.VMEM((rows, 1), F32), pltpu.VMEM((rows, 1), F32),
                     pltpu.VMEM((rows, 2 * dh), F32)]
    return pl.pallas_call(
        functools.partial(_attn_kernel, tk=tk, rb=rb, n_groups=n_groups),
        grid=(b, N_KV_HEADS, l // tq),
        in_specs=[pl.BlockSpec((1, Q_PER_KV, tq, dh), lambda bi, kh, qi: (bi, kh, qi, 0)),
                  pl.BlockSpec((1, l, dh), lambda bi, kh, qi: (bi, 0, kh)),
                  pl.BlockSpec((1, l, dh), lambda bi, kh, qi: (bi, 0, kh)),
                  pl.BlockSpec((1, lc, dh), lambda bi, kh, qi: (bi, 0, kh)),
                  pl.BlockSpec((1, lc, dh), lambda bi, kh, qi: (bi, 0, kh))],
        out_specs=pl.BlockSpec((1, tq, gdh), lambda bi, kh, qi: (bi, qi, kh)),
        out_shape=jax.ShapeDtypeStruct((b, l, ATTN_WIDTH), BF16),
        scratch_shapes=group_scratch * n_groups,
        compiler_params=_params("parallel", "parallel", "arbitrary"),
        name="attention",
    )(q, k, v, kc, vc)


def _conv_kernel(prev_ref, cur_ref, next_ref, w_ref, b_ref, g_ref, beta_ref, o_ref, win_ref, y_ref, *, tc, rc):
    li = pl.program_id(1)
    tl, c = cur_ref.shape[1:]
    halo = prev_ref.shape[1]
    nt = c // LANES

    def put_tokens(vals, tok0):
        for j in range(nt):
            win_ref[pl.ds(tok0 * nt + j, vals.shape[0], stride=nt), :] = vals[:, j * LANES:(j + 1) * LANES]

    def put_chunk(ci, carry):
        r0 = pl.multiple_of(ci * rc, rc)
        put_tokens(cur_ref[0, pl.ds(r0, rc), :].astype(F32), halo + r0)
        return carry

    put_tokens(jnp.where(li > 0, prev_ref[0].astype(F32), 0.0), 0)
    lax.fori_loop(0, tl // rc, put_chunk, 0)
    put_tokens(jnp.where(li < pl.num_programs(1) - 1, next_ref[0].astype(F32), 0.0), halo + tl)

    first = halo - CONV_TAPS // 2
    bias = b_ref[...][None]

    def token_chunk(ci, carry):
        tok = ci * tc
        acc = jnp.zeros((tc, nt, LANES), F32) + bias
        for t in range(CONV_TAPS):
            r0 = pl.multiple_of((tok + first + t) * nt, nt)
            acc = acc + win_ref[pl.ds(r0, tc * nt), :].reshape(tc, nt, LANES) * w_ref[t][None]
        y_ref[pl.ds(pl.multiple_of(tok * nt, nt), tc * nt), :] = acc.reshape(tc * nt, LANES)
        return carry

    lax.fori_loop(0, tl // tc, token_chunk, 0)

    def norm_chunk(ci, carry):
        r0 = pl.multiple_of(ci * rc, rc)
        y = jnp.concatenate([y_ref[pl.ds(r0 * nt + j, rc, stride=nt), :] for j in range(nt)], axis=1)
        mu = jnp.mean(y, axis=-1, keepdims=True)
        yc = y - mu
        var = jnp.mean(yc * yc, axis=-1, keepdims=True)
        z = yc * lax.rsqrt(var + EPS) * g_ref[...] + beta_ref[...]
        o_ref[0, pl.ds(r0, rc), :] = _silu(z).astype(o_ref.dtype)
        return carry

    lax.fori_loop(0, tl // rc, norm_chunk, 0, unroll=2)


def _conv_module(u, w_dw, b_dw, ln_g, ln_b, tl, tc=16, rc=32):
    b, l, c = u.shape
    nt = c // LANES
    hb = tl // CONV_HALO
    n_halo = l // CONV_HALO
    return pl.pallas_call(
        functools.partial(_conv_kernel, tc=tc, rc=rc),
        grid=(b, l // tl),
        in_specs=[pl.BlockSpec((1, CONV_HALO, c), lambda bi, li: (bi, jnp.maximum(li * hb - 1, 0), 0)),
                  pl.BlockSpec((1, tl, c), lambda bi, li: (bi, li, 0)),
                  pl.BlockSpec((1, CONV_HALO, c), lambda bi, li: (bi, jnp.minimum((li + 1) * hb, n_halo - 1), 0)),
                  pl.BlockSpec((CONV_TAPS, nt, LANES), lambda bi, li: (0, 0, 0)),
                  pl.BlockSpec((nt, LANES), lambda bi, li: (0, 0)),
                  pl.BlockSpec((1, c), lambda bi, li: (0, 0)),
                  pl.BlockSpec((1, c), lambda bi, li: (0, 0))],
        out_specs=pl.BlockSpec((1, tl, c), lambda bi, li: (bi, li, 0)),
        out_shape=jax.ShapeDtypeStruct((b, l, c), BF16),
        scratch_shapes=[pltpu.VMEM(((tl + 2 * CONV_HALO) * nt, LANES), F32), pltpu.VMEM((tl * nt, LANES), F32)],
        compiler_params=_params("parallel", "arbitrary"),
        name="conv_module",
    )(u, u, u, w_dw, b_dw, ln_g, ln_b)


def _merge_kernel(h_ref, a_ref, c_ref, wga_ref, wgc_ref, wa_ref, wc_ref, o_ref):
    h = h_ref[...]
    g_a = _sigmoid(jnp.dot(h, wga_ref[...], preferred_element_type=F32))
    g_c = _sigmoid(jnp.dot(h, wgc_ref[...], preferred_element_type=F32))
    a = jnp.dot(a_ref[...], wa_ref[...].astype(BF16), preferred_element_type=F32)
    cb = jnp.dot(c_ref[...], wc_ref[...].astype(BF16), preferred_element_type=F32)
    o_ref[...] = (g_a * a + g_c * cb).astype(o_ref.dtype)


def _merge(h2d, attn, conv, w_in, gate_off, wa, wc, tm=512, tn=512):
    m, d = h2d.shape
    ka = attn.shape[1]
    kc = conv.shape[1]
    nj = d // tn
    ja = gate_off // tn
    jc = (gate_off + d) // tn
    return pl.pallas_call(
        _merge_kernel,
        grid=(nj, m // tm),
        in_specs=[pl.BlockSpec((tm, d), lambda j, i: (i, 0)),
                  pl.BlockSpec((tm, ka), lambda j, i: (i, 0)),
                  pl.BlockSpec((tm, kc), lambda j, i: (i, 0)),
                  pl.BlockSpec((d, tn), lambda j, i: (0, ja + j)),
                  pl.BlockSpec((d, tn), lambda j, i: (0, jc + j)),
                  pl.BlockSpec((ka, tn), lambda j, i: (0, j)),
                  pl.BlockSpec((kc, tn), lambda j, i: (0, j))],
        out_specs=pl.BlockSpec((tm, tn), lambda j, i: (i, j)),
        out_shape=jax.ShapeDtypeStruct((m, d), BF16),
        compiler_params=_params("parallel", "arbitrary"),
        name="merge_branches",
    )(h2d, attn, conv, w_in, w_in, wa, wc)


def _out_proj_kernel(m_ref, w_ref, x_ref, ga_ref, o_ref):
    acc = jnp.dot(m_ref[0], w_ref[...], preferred_element_type=F32)
    o_ref[0] = x_ref[0] + ga_ref[0] * acc


def _out_proj(mrg, w, x, gate, tm=1024, tn=1024):
    b, l, d = x.shape
    return pl.pallas_call(
        _out_proj_kernel,
        grid=(b, l // tm, d // tn),
        in_specs=[pl.BlockSpec((1, tm, d), lambda bi, i, j: (bi, i, 0)),
                  pl.BlockSpec((d, tn), lambda bi, i, j: (0, j)),
                  pl.BlockSpec((1, tm, tn), lambda bi, i, j: (bi, i, j)),
                  pl.BlockSpec((1, 1, tn), lambda bi, i, j: (bi, 0, j))],
        out_specs=pl.BlockSpec((1, tm, tn), lambda bi, i, j: (bi, i, j)),
        out_shape=jax.ShapeDtypeStruct((b, l, d), F32),
        compiler_params=_params("parallel", "parallel", "arbitrary"),
        name="out_proj_residual",
    )(mrg, w, x, gate)


def _pack_halves(y):
    n = y.shape[1] // 2
    return pltpu.pack_elementwise([y[:, :n], y[:, n:]], packed_dtype=BF16)


def _unpack_halves(p):
    lo = pltpu.unpack_elementwise(p, index=0, packed_dtype=BF16, unpacked_dtype=F32)
    hi = pltpu.unpack_elementwise(p, index=1, packed_dtype=BF16, unpacked_dtype=F32)
    return lo, hi


def _norm2_router_kernel(x_ref, g_ref, sh_ref, sc_ref, wr_ref, br_ref, hp_ref, lg_ref):
    y = _rms(x_ref[0], g_ref[...]) * (1.0 + sc_ref[0]) + sh_ref[0]
    packed = _pack_halves(y)
    tl = packed.shape[0]
    nt = packed.shape[1] // LANES
    for j in range(nt):
        hp_ref[pl.ds(j, tl, stride=nt), :] = packed[:, j * LANES:(j + 1) * LANES]
    lg_ref[0] = jnp.dot(y.astype(BF16), wr_ref[...], preferred_element_type=F32) + br_ref[...]


def _norm2_router(x, g, shift, scale, w_r, b_r, tl):
    b, l, d = x.shape
    nt = d // 2 // LANES
    lb = l // tl
    return pl.pallas_call(
        _norm2_router_kernel,
        grid=(b, lb),
        in_specs=[pl.BlockSpec((1, tl, d), lambda bi, li: (bi, li, 0)),
                  pl.BlockSpec((1, d), lambda bi, li: (0, 0)),
                  pl.BlockSpec((1, 1, d), lambda bi, li: (bi, 0, 0)),
                  pl.BlockSpec((1, 1, d), lambda bi, li: (bi, 0, 0)),
                  pl.BlockSpec((d, ROUTER_LANES), lambda bi, li: (0, 0)),
                  pl.BlockSpec((1, ROUTER_LANES), lambda bi, li: (0, 0))],
        out_specs=[pl.BlockSpec((tl * nt, LANES), lambda bi, li: (bi * lb + li, 0)),
                   pl.BlockSpec((1, tl, ROUTER_LANES), lambda bi, li: (bi, li, 0))],
        out_shape=[jax.ShapeDtypeStruct((b * l * nt, LANES), jnp.int32),
                   jax.ShapeDtypeStruct((b, l, ROUTER_LANES), F32)],
        compiler_params=_params("parallel", "parallel"),
        name="norm2_router",
    )(x, g, shift, scale, w_r, b_r)


def _first_lane(mask, lane):
    return jnp.min(jnp.where(mask, lane, LANES), axis=-1, keepdims=True)


def _route_kernel(lg_ref, meta_ref, wts_ref, cnt_ref, carry_ref):
    @pl.when(pl.program_id(0) == 0)
    def _():
        carry_ref[...] = jnp.zeros_like(carry_ref)

    lg = lg_ref[...]
    tb = lg.shape[0]
    lane = lax.broadcasted_iota(jnp.int32, lg.shape, 1)
    neg_inf = jnp.float32(-jnp.inf)
    is_group = lane < N_GROUPS
    gl = jnp.where(is_group, lg, neg_inf)
    g_max = jnp.max(gl, axis=-1, keepdims=True)
    g_sel = _first_lane(gl == g_max, lane)
    p_g = 1.0 / jnp.sum(jnp.where(is_group, jnp.exp(lg - g_max), 0.0), axis=-1, keepdims=True)

    e_idx = lane - N_GROUPS
    in_group = (e_idx >= g_sel * EXPERTS_PER_GROUP) & (e_idx < (g_sel + 1) * EXPERTS_PER_GROUP)
    ev = jnp.where(in_group, lg, neg_inf)
    v1 = jnp.max(ev, axis=-1, keepdims=True)
    i1 = _first_lane(ev == v1, lane)
    ev2 = jnp.where(lane == i1, neg_inf, ev)
    v2 = jnp.max(ev2, axis=-1, keepdims=True)
    i2 = _first_lane(ev2 == v2, lane)
    t = jnp.exp(v2 - v1)
    w1 = p_g / (1.0 + t)
    w2 = w1 * t

    oh1 = lane == i1
    oh2 = lane == i2
    oh = (oh1 | oh2).astype(BF16)
    earlier = (lax.broadcasted_iota(jnp.int32, (tb, tb), 0) > lax.broadcasted_iota(jnp.int32, (tb, tb), 1)).astype(BF16)
    before = jnp.dot(earlier, oh, preferred_element_type=F32) + carry_ref[...]
    r1 = jnp.sum(jnp.where(oh1, before, 0.0), axis=-1, keepdims=True).astype(jnp.int32)
    r2 = jnp.sum(jnp.where(oh2, before, 0.0), axis=-1, keepdims=True).astype(jnp.int32)
    carry_ref[...] += jnp.sum(oh.astype(F32), axis=0, keepdims=True)

    meta_ref[...] = jnp.where(lane == 0, i1 - N_GROUPS, jnp.where(lane == 1, i2 - N_GROUPS,
                              jnp.where(lane == 2, r1, jnp.where(lane == 3, r2, 0))))
    wts_ref[...] = jnp.where(lane == 0, w1, jnp.where(lane == 1, w2, 0.0))
    cnt_ref[...] = carry_ref[...]


def _route(logits, tb=512):
    n = logits.shape[0]
    return pl.pallas_call(
        _route_kernel,
        grid=(n // tb,),
        in_specs=[pl.BlockSpec((tb, LANES), lambda i: (i, 0))],
        out_specs=[pl.BlockSpec((tb, LANES), lambda i: (i, 0)),
                   pl.BlockSpec((tb, LANES), lambda i: (i, 0)),
                   pl.BlockSpec((1, LANES), lambda i: (0, 0))],
        out_shape=[jax.ShapeDtypeStruct((n, LANES), jnp.int32),
                   jax.ShapeDtypeStruct((n, LANES), F32),
                   jax.ShapeDtypeStruct((1, LANES), F32)],
        scratch_shapes=[pltpu.VMEM((1, LANES), F32)],
        compiler_params=_params("arbitrary"),
        name="moe_route",
    )(logits)


def _dest_kernel(meta_ref, pst_ref, o_ref):
    meta = meta_ref[...]
    lane = lax.broadcasted_iota(jnp.int32, meta.shape, 1)
    pst = pst_ref[...]

    def row_of(slot):
        e = meta[:, slot:slot + 1]
        start = jnp.sum(jnp.where(lane == e + N_GROUPS, pst, 0), axis=-1, keepdims=True)
        return start + meta[:, TOP_K + slot:TOP_K + slot + 1]

    o_ref[...] = jnp.where(lane == 0, row_of(0), jnp.where(lane == 1, row_of(1), 0))


def _dest_rows(meta, pst, tb=512):
    n = meta.shape[0]
    return pl.pallas_call(
        _dest_kernel,
        grid=(n // tb,),
        in_specs=[pl.BlockSpec((tb, LANES), lambda i: (i, 0)),
                  pl.BlockSpec((1, LANES), lambda i: (0, 0))],
        out_specs=pl.BlockSpec((tb, LANES), lambda i: (i, 0)),
        out_shape=jax.ShapeDtypeStruct((n, LANES), jnp.int32),
        compiler_params=_params("parallel"),
        name="moe_dest_rows",
    )(meta, pst)


def _zero_tail_kernel(lb_ref, o_ref):
    o_ref[...] = jnp.zeros_like(o_ref)


def _zero_tails(last_blk, total, tmb, nt):
    return pl.pallas_call(
        _zero_tail_kernel,
        grid_spec=pltpu.PrefetchScalarGridSpec(
            num_scalar_prefetch=1,
            grid=(last_blk.shape[0],),
            in_specs=[],
            out_specs=pl.BlockSpec((tmb * nt, LANES), lambda e, lb: (lb[e], 0))),
        out_shape=jax.ShapeDtypeStruct((total * nt, LANES), jnp.int32),
        compiler_params=_params("arbitrary"),
        name="moe_zero_tails",
    )(last_blk)


def _dispatch_kernel(dest_ref, hp_ref, xs_in_ref, xs_ref, sem):
    rows = hp_ref.shape[0]

    def slab_copy(r, k):
        return pltpu.make_async_copy(hp_ref.at[r], xs_ref.at[dest_ref[0, 0, r * TOP_K + k]], sem)

    def start(r, c):
        for k in range(TOP_K):
            slab_copy(r, k).start(priority=k % 2)
        return c

    def wait(r, c):
        for k in range(TOP_K):
            slab_copy(r, k).wait()
        return c

    lax.fori_loop(0, rows, start, 0, unroll=DMA_LOOP_UNROLL)
    lax.fori_loop(0, rows, wait, 0, unroll=DMA_LOOP_UNROLL)


def _dispatch(hp3, dest, xs0, rows):
    n, nt, _ = hp3.shape
    steps = n // rows
    return pl.pallas_call(
        _dispatch_kernel,
        grid=(steps,),
        in_specs=[pl.BlockSpec((1, 1, rows * TOP_K), lambda i: (i, 0, 0), memory_space=pltpu.SMEM),
                  pl.BlockSpec((rows, nt, LANES), lambda i: (i, 0, 0)),
                  pl.BlockSpec(memory_space=pl.ANY)],
        out_specs=pl.BlockSpec(memory_space=pl.ANY),
        out_shape=jax.ShapeDtypeStruct(xs0.shape, xs0.dtype),
        scratch_shapes=[pltpu.SemaphoreType.DMA(())],
        input_output_aliases={2: 0},
        compiler_params=_params("arbitrary"),
        name="moe_dispatch",
    )(dest.reshape(steps, 1, rows * TOP_K), hp3, xs0)


def _by_valid_rows(nv, tmb, compute, o_ref):
    quarter = tmb // 4
    per_row = o_ref.shape[0] // tmb

    for nq in range(1, 5):
        rows = nq * quarter

        @pl.when((nv > rows - quarter) & (nv <= rows))
        def _(rows=rows):
            compute(rows)
            if rows < tmb:
                o_ref[rows * per_row:, :] = jnp.zeros(((tmb - rows) * per_row, o_ref.shape[1]), o_ref.dtype)

    @pl.when(nv == 0)
    def _():
        o_ref[...] = jnp.zeros_like(o_ref)


def _expert_weights(plan, w_hbm, wbuf, sem):
    be_ref, first_ref, slot_ref, nxt_ref = plan
    bi = pl.program_id(0)
    slot = slot_ref[bi]

    def copy(e, s):
        return pltpu.make_async_copy(w_hbm.at[e], wbuf.at[s], sem.at[s])

    @pl.when(bi == 0)
    def _():
        copy(be_ref[0], 0).start()

    @pl.when((first_ref[bi] == 1) & (nxt_ref[bi] >= 0))
    def _():
        copy(nxt_ref[bi], 1 - slot).start()

    @pl.when(first_ref[bi] == 1)
    def _():
        copy(be_ref[bi], slot).wait()

    return wbuf.at[slot]


def _expert_in_kernel(be_ref, first_ref, slot_ref, nxt_ref, nu_ref, nv_ref, x_ref, w_hbm, *rest, nt, tn):
    gate_ref = rest[0] if len(rest) == 4 else None
    o_ref, wbuf, sem = rest[-3:]
    w_ref = _expert_weights((be_ref, first_ref, slot_ref, nxt_ref), w_hbm, wbuf, sem)

    def compute(rows):
        halves = [_unpack_halves(x_ref[pl.ds(j, rows, stride=nt), :]) for j in range(nt)]
        lo = jnp.concatenate([h[0].astype(BF16) for h in halves], axis=1)
        hi = jnp.concatenate([h[1].astype(BF16) for h in halves], axis=1)
        half = nt * LANES
        for c0 in range(0, o_ref.shape[1], tn):
            y = (jnp.dot(lo, w_ref[:half, c0:c0 + tn].astype(BF16), preferred_element_type=F32)
                 + jnp.dot(hi, w_ref[half:, c0:c0 + tn].astype(BF16), preferred_element_type=F32))
            if gate_ref is None:
                y = _silu(y)
            else:
                y = gate_ref[:rows, c0:c0 + tn].astype(F32) * y
            o_ref[:rows, c0:c0 + tn] = y.astype(o_ref.dtype)

    _by_valid_rows(nv_ref[pl.program_id(0)], o_ref.shape[0], compute, o_ref)


N_PLAN = 6


def _blk_clamped(bi, *plan):
    return jnp.minimum(bi, plan[4][0] - 1)


def _expert_in(xs, w, gate, plan, tmb, name, tn=256):
    _, d, ff = w.shape
    nt = d // 2 // LANES
    total = xs.shape[0] // nt
    nblk = total // tmb
    in_specs = [pl.BlockSpec((tmb * nt, LANES), lambda bi, *p: (_blk_clamped(bi, *p), 0)),
                pl.BlockSpec(memory_space=pl.ANY)]
    args = [xs, w]
    if gate is not None:
        in_specs.append(pl.BlockSpec((tmb, ff), lambda bi, *p: (_blk_clamped(bi, *p), 0)))
        args.append(gate)
    return pl.pallas_call(
        functools.partial(_expert_in_kernel, nt=nt, tn=tn),
        grid_spec=pltpu.PrefetchScalarGridSpec(
            num_scalar_prefetch=N_PLAN,
            grid=(nblk,),
            in_specs=in_specs,
            out_specs=pl.BlockSpec((tmb, ff), lambda bi, *p: (bi, 0)),
            scratch_shapes=[pltpu.VMEM((2, d, ff), w.dtype), pltpu.SemaphoreType.DMA((2,))]),
        out_shape=jax.ShapeDtypeStruct((total, ff), BF16),
        compiler_params=_params("arbitrary"),
        name=name,
    )(*plan, *args)


def _expert_down_kernel(be_ref, first_ref, slot_ref, nxt_ref, nu_ref, nv_ref, a_ref, w_hbm, o_ref, wbuf, sem,
                        *, nt, tn):
    tmb = a_ref.shape[0]
    w_ref = _expert_weights((be_ref, first_ref, slot_ref, nxt_ref), w_hbm, wbuf, sem)

    def compute(rows):
        a = a_ref[:rows, :]
        half = nt * LANES
        for c0 in range(0, half, tn):
            ylo = jnp.dot(a, w_ref[:, c0:c0 + tn].astype(BF16), preferred_element_type=F32)
            yhi = jnp.dot(a, w_ref[:, half + c0:half + c0 + tn].astype(BF16), preferred_element_type=F32)
            packed = pltpu.pack_elementwise([ylo, yhi], packed_dtype=BF16)
            for j in range(tn // LANES):
                o_ref[pl.ds(c0 // LANES + j, rows, stride=nt), :] = packed[:, j * LANES:(j + 1) * LANES]

    _by_valid_rows(nv_ref[pl.program_id(0)], tmb, compute, o_ref)


def _expert_down(act, w_down, plan, tmb, tn=512):
    total, ff = act.shape
    d = w_down.shape[2]
    nt = d // 2 // LANES
    nblk = total // tmb
    return pl.pallas_call(
        functools.partial(_expert_down_kernel, nt=nt, tn=tn),
        grid_spec=pltpu.PrefetchScalarGridSpec(
            num_scalar_prefetch=N_PLAN,
            grid=(nblk,),
            in_specs=[pl.BlockSpec((tmb, ff), lambda bi, *p: (_blk_clamped(bi, *p), 0)),
                      pl.BlockSpec(memory_space=pl.ANY)],
            out_specs=pl.BlockSpec((tmb * nt, LANES), lambda bi, *p: (bi, 0)),
            scratch_shapes=[pltpu.VMEM((2, ff, d), w_down.dtype), pltpu.SemaphoreType.DMA((2,))]),
        out_shape=jax.ShapeDtypeStruct((total * nt, LANES), jnp.int32),
        compiler_params=_params("arbitrary"),
        name="expert_down",
    )(*plan, act, w_down)


def _combine_kernel(pos_ref, nxt_ref, ys_ref, w_ref, x_ref, ga_ref, g_ref, o_ref, *scratch):
    n_buf = 2 * TOP_K
    slabs, sems = scratch[:n_buf], scratch[n_buf:]
    rows = x_ref.shape[1]
    hr = rows // 2
    nt = ys_ref.shape[1]
    half = nt * LANES
    step = pl.program_id(0) * pl.num_programs(1) + pl.program_id(1)
    n_steps = pl.num_programs(0) * pl.num_programs(1)

    def slab_copy(idx_ref, h, r, k):
        dst = slabs[h * TOP_K + k].at[pl.ds(pl.multiple_of(r * SLAB_PITCH, SUBLANES), nt)]
        return pltpu.make_async_copy(ys_ref.at[idx_ref[0, 0, (h * hr + r) * TOP_K + k]], dst, sems[h * TOP_K + k])

    def issue(idx_ref, h):
        def body(r, c):
            for k in range(TOP_K):
                slab_copy(idx_ref, h, r, k).start(priority=k % 2)
            return c
        lax.fori_loop(0, hr, body, 0, unroll=DMA_LOOP_UNROLL)

    def wait(idx_ref, h):
        def body(r, c):
            for k in range(TOP_K):
                slab_copy(idx_ref, h, r, k).wait()
            return c
        lax.fori_loop(0, hr, body, 0, unroll=DMA_LOOP_UNROLL)

    def compute(h):
        r0 = h * hr
        w0 = w_ref[r0:r0 + hr, 0:1]
        w1 = w_ref[r0:r0 + hr, 1:2]
        ss = jnp.zeros((hr, 1), F32)
        for j in range(nt):
            lo0, hi0 = _unpack_halves(slabs[h * TOP_K][pl.ds(j, hr, stride=SLAB_PITCH), :])
            lo1, hi1 = _unpack_halves(slabs[h * TOP_K + 1][pl.ds(j, hr, stride=SLAB_PITCH), :])
            for c0, y in ((j * LANES, w0 * lo0 + w1 * lo1), (half + j * LANES, w0 * hi0 + w1 * hi1)):
                z = x_ref[0, r0:r0 + hr, c0:c0 + LANES] + ga_ref[0, :, c0:c0 + LANES] * y
                ss = ss + jnp.sum(z * z, axis=-1, keepdims=True)
                o_ref[0, r0:r0 + hr, c0:c0 + LANES] = z
        inv = lax.rsqrt(ss / (2 * half) + EPS)
        o_ref[0, r0:r0 + hr, :] = o_ref[0, r0:r0 + hr, :] * inv * g_ref[...]

    @pl.when(step == 0)
    def _():
        issue(pos_ref, 0)

    issue(pos_ref, 1)
    wait(pos_ref, 0)
    compute(0)

    @pl.when(step + 1 < n_steps)
    def _():
        issue(nxt_ref, 0)

    wait(pos_ref, 1)
    compute(1)


def _combine(ys3, pos, wts, x, gate, g, rows):
    b, l, d = x.shape
    lb = l // rows
    n_steps = b * lb
    pos3 = pos.reshape(n_steps, 1, rows * TOP_K)
    slab = pltpu.VMEM((rows // 2 * SLAB_PITCH, LANES), ys3.dtype)
    return pl.pallas_call(
        _combine_kernel,
        grid=(b, lb),
        in_specs=[pl.BlockSpec((1, 1, rows * TOP_K), lambda bi, i: (bi * lb + i, 0, 0), memory_space=pltpu.SMEM),
                  pl.BlockSpec((1, 1, rows * TOP_K), lambda bi, i: (jnp.minimum(bi * lb + i + 1, n_steps - 1), 0, 0),
                               memory_space=pltpu.SMEM),
                  pl.BlockSpec(memory_space=pl.ANY),
                  pl.BlockSpec((rows, LANES), lambda bi, i: (bi * lb + i, 0)),
                  pl.BlockSpec((1, rows, d), lambda bi, i: (bi, i, 0)),
                  pl.BlockSpec((1, 1, d), lambda bi, i: (bi, 0, 0)),
                  pl.BlockSpec((1, d), lambda bi, i: (0, 0))],
        out_specs=pl.BlockSpec((1, rows, d), lambda bi, i: (bi, i, 0)),
        out_shape=jax.ShapeDtypeStruct((b, l, d), F32),
        scratch_shapes=[slab] * (2 * TOP_K) + [pltpu.SemaphoreType.DMA(())] * (2 * TOP_K),
        compiler_params=_params("arbitrary", "arbitrary"),
        name="moe_combine_norm",
    )(pos3, pos3, ys3, wts, x, gate, g)


def _block_layout(counts, n_pairs, tmb):
    nblk = (n_pairs + N_EXPERTS * (tmb - 1) + tmb - 1) // tmb
    blocks = (counts + tmb - 1) // tmb
    bend = jnp.cumsum(blocks)
    bstart = bend - blocks
    pstart = bstart * tmb
    n_used = bend[-1]
    ids = jnp.arange(nblk, dtype=jnp.int32)
    blk_e = jnp.minimum(jnp.searchsorted(bend, jnp.minimum(ids, n_used - 1), side="right"),
                        N_EXPERTS - 1).astype(jnp.int32)
    last_blk = jnp.maximum(bend - 1, 0).astype(jnp.int32)
    used = ids < n_used
    n_valid = jnp.where(used, jnp.clip(counts[blk_e] - (ids - bstart[blk_e]) * tmb, 0, tmb), 0)
    first = used & ((ids == 0) | (blk_e != jnp.roll(blk_e, 1)))
    slot = (jnp.cumsum(first.astype(jnp.int32)) - 1) % 2
    experts = jnp.arange(N_EXPERTS, dtype=jnp.int32)
    later = jnp.where((blocks > 0)[None, :] & (experts[None, :] > experts[:, None]), experts[None, :], N_EXPERTS)
    nxt_e = jnp.min(later, axis=1)
    nxt = jnp.where(nxt_e[blk_e] < N_EXPERTS, nxt_e[blk_e], -1)
    i32 = lambda a: a.astype(jnp.int32)
    plan = (blk_e, i32(first), i32(slot), i32(nxt), i32(n_used).reshape(1), i32(n_valid))
    return nblk, i32(pstart), last_blk, plan


def _rope_tables(n_tokens):
    rows = n_tokens // GRID_W
    row, col = jnp.meshgrid(jnp.arange(rows), jnp.arange(GRID_W), indexing="ij")
    pos = jnp.stack([row.reshape(-1), col.reshape(-1)], axis=-1).astype(F32)
    inv = ROPE_THETA ** (-jnp.arange(0, ROPE_AXIS_DIM, 2, dtype=F32) / ROPE_AXIS_DIM)
    ang = pos[:, :, None] * inv[None, None, :]
    cos, sin = jnp.cos(ang), jnp.sin(ang)
    cos_t = jnp.concatenate([cos[:, 0], cos[:, 0], cos[:, 1], cos[:, 1]], axis=-1)
    sin_t = jnp.concatenate([-sin[:, 0], sin[:, 0], -sin[:, 1], sin[:, 1]], axis=-1)
    return cos_t, sin_t


def kernel(x, c, ctx, c_ctx, norm1_g, w_mod, b_mod, w_in, q_norm_g, k_norm_g, w_attn_out, conv_dw_w, conv_dw_b, conv_ln_g, conv_ln_b, w_conv_out, w_out, norm2_g, w_router_group, b_router_group, w_router_expert, b_router_expert, w_exp_gate, w_exp_up, w_exp_down, norm_f_g):
    b, s, d = x.shape
    n_ctx = ctx.shape[1]
    assert w_in.shape[0] == 1, "single-layer stack"
    conv_width = conv_dw_w.shape[-1]
    k_off = ATTN_WIDTH
    glu_off = k_off + 2 * KV_WIDTH
    gate_off = glu_off + 2 * conv_width

    n_c = b + 1
    cvec = jnp.zeros((SUBLANES * ((n_c + SUBLANES - 1) // SUBLANES), d), F32).at[:b].set(c).at[b].set(c_ctx)
    mod = _mod_vectors(cvec, w_mod[0], b_mod.reshape(1, -1))
    sh1, sc1, ga1, sh2, sc2, ga2 = [mod[:b, i * d:(i + 1) * d].reshape(b, 1, d) for i in range(N_MOD)]
    csh1, csc1 = [mod[b:b + 1, i * d:(i + 1) * d].reshape(1, 1, d) for i in range(2)]

    g1 = norm1_g.reshape(1, d)
    h = _norm_mod(x, g1, sh1, sc1, tl=512)
    hc = _norm_mod(ctx, g1, csh1, csc1, tl=n_ctx)
    w_in_b = w_in[0].astype(BF16)
    cos_t, sin_t = _rope_tables(s)
    qg = q_norm_g.reshape(1, HEAD_DIM)
    kg = k_norm_g.reshape(1, HEAD_DIM)
    q = _q_proj(h, w_in_b, qg, cos_t, sin_t, 0, HEAD_DIM ** -0.5 * LOG2E, tm=1024)
    k, v = _kv_proj(h, w_in_b, kg, cos_t, sin_t, k_off, tm=1024)
    kc, vc = _kv_proj(hc, w_in_b, kg, None, None, k_off, tm=n_ctx)
    attn = _attention(q, k, v, kc, vc, tq=128, n_groups=1, tk=1024)

    h2d = h.reshape(b * s, d)
    u = _glu_proj(h2d, w_in_b, glu_off, conv_width, tm=1024)
    conv = _conv_module(u.reshape(b, s, conv_width), conv_dw_w.reshape(CONV_TAPS, conv_width // LANES, LANES),
                        conv_dw_b.reshape(conv_width // LANES, LANES), conv_ln_g.reshape(1, -1),
                        conv_ln_b.reshape(1, -1), tl=256)
    mrg = _merge(h2d, attn.reshape(b * s, ATTN_WIDTH), conv.reshape(b * s, conv_width), w_in_b, gate_off,
                 w_attn_out[0], w_conv_out[0])
    x1 = _out_proj(mrg.reshape(b, s, d), w_out[0].astype(BF16), x, ga1)

    w_r = jnp.zeros((d, ROUTER_LANES), F32).at[:, :N_GROUPS].set(w_router_group[0]) \
        .at[:, N_GROUPS:N_GROUPS + N_EXPERTS].set(w_router_expert[0])
    b_r = jnp.zeros((1, ROUTER_LANES), F32).at[0, :N_GROUPS].set(b_router_group[0]) \
        .at[0, N_GROUPS:N_GROUPS + N_EXPERTS].set(b_router_expert[0])
    hp, logits = _norm2_router(x1, norm2_g.reshape(1, d), sh2, sc2, w_r.astype(BF16), b_r, tl=256)
    n = b * s
    nt = d // 2 // LANES
    tmb = 512
    meta, wts, cnt = _route(logits.reshape(n, ROUTER_LANES))
    counts = cnt[0, N_GROUPS:N_GROUPS + N_EXPERTS].astype(jnp.int32)
    nblk, pstart, last_blk, plan = _block_layout(counts, n * TOP_K, tmb)
    pst = jnp.zeros((1, LANES), jnp.int32).at[0, N_GROUPS:N_GROUPS + N_EXPERTS].set(pstart)
    dest = _dest_rows(meta, pst)[:, :TOP_K].reshape(-1)
    xs0 = _zero_tails(last_blk, nblk * tmb, tmb, nt)
    xs = _dispatch(hp.reshape(n, nt, LANES), dest, xs0.reshape(nblk * tmb, nt, LANES), rows=256)
    xs2 = xs.reshape(nblk * tmb * nt, LANES)
    sg = _expert_in(xs2, w_exp_gate[0], None, plan, tmb, "expert_gate")
    act = _expert_in(xs2, w_exp_up[0], sg, plan, tmb, "expert_up")
    ys = _expert_down(act, w_exp_down[0], plan, tmb)
    return _combine(ys.reshape(nblk * tmb, nt, LANES), dest, wts, x1, ga2, norm_f_g.reshape(1, d), rows=256)
```

```python
import functools

import jax
import jax.numpy as jnp
from jax import lax
from jax.experimental import pallas as pl
from jax.experimental.pallas import tpu as pltpu

F32 = jnp.float32
BF16 = jnp.bfloat16

GRID_W = 64
HEAD_DIM = 128
N_Q_HEADS = 16
N_KV_HEADS = 4
Q_PER_KV = N_Q_HEADS // N_KV_HEADS
ATTN_WIDTH = N_Q_HEADS * HEAD_DIM
KV_WIDTH = N_KV_HEADS * HEAD_DIM
CONV_TAPS = 31
CONV_HALO = 16
ROPE_THETA = 10000.0
ROPE_AXIS_DIM = HEAD_DIM // 2
N_GROUPS = 4
EXPERTS_PER_GROUP = 8
N_EXPERTS = N_GROUPS * EXPERTS_PER_GROUP
TOP_K = 2
N_MOD = 6
EPS = 1e-6
LOG2E = 1.4426950408889634
LANES = 128
SUBLANES = 8
ROUTER_LANES = LANES
SLAB_PITCH = 24
DMA_LOOP_UNROLL = 4

V7X_VMEM_LIMIT = 56 * 1024 * 1024


def _params(*sem):
    return pltpu.CompilerParams(dimension_semantics=sem, vmem_limit_bytes=V7X_VMEM_LIMIT)


def _sigmoid(x):
    return 1.0 / (1.0 + jnp.exp(-x))


def _silu(x):
    return x * _sigmoid(x)


def _rms(x, g):
    return x * lax.rsqrt(jnp.mean(x * x, axis=-1, keepdims=True) + EPS) * g


def _mod_kernel(c_ref, w_ref, b_ref, o_ref):
    s = _silu(c_ref[...]).astype(BF16)
    o_ref[...] = jnp.dot(s, w_ref[...].astype(BF16), preferred_element_type=F32) + b_ref[...]


def _mod_vectors(cvec, w_mod, b_mod, tn=1024):
    m, d = cvec.shape
    n = w_mod.shape[1]
    return pl.pallas_call(
        _mod_kernel,
        grid=(n // tn,),
        in_specs=[pl.BlockSpec((m, d), lambda j: (0, 0)),
                  pl.BlockSpec((d, tn), lambda j: (0, j)),
                  pl.BlockSpec((1, tn), lambda j: (0, j))],
        out_specs=pl.BlockSpec((m, tn), lambda j: (0, j)),
        out_shape=jax.ShapeDtypeStruct((m, n), F32),
        compiler_params=_params("arbitrary"),
        name="mod_vectors",
    )(cvec, w_mod, b_mod)


def _norm_mod_kernel(x_ref, g_ref, sh_ref, sc_ref, o_ref):
    y = _rms(x_ref[0], g_ref[...])
    o_ref[0] = (y * (1.0 + sc_ref[0]) + sh_ref[0]).astype(o_ref.dtype)


def _norm_mod(x, g, shift, scale, tl):
    b, l, d = x.shape
    per_batch = shift.shape[0] > 1
    mod_map = (lambda bi, li: (bi, 0, 0)) if per_batch else (lambda bi, li: (0, 0, 0))
    return pl.pallas_call(
        _norm_mod_kernel,
        grid=(b, l // tl),
        in_specs=[pl.BlockSpec((1, tl, d), lambda bi, li: (bi, li, 0)),
                  pl.BlockSpec((1, d), lambda bi, li: (0, 0)),
                  pl.BlockSpec((1, 1, d), mod_map),
                  pl.BlockSpec((1, 1, d), mod_map)],
        out_specs=pl.BlockSpec((1, tl, d), lambda bi, li: (bi, li, 0)),
        out_shape=jax.ShapeDtypeStruct((b, l, d), BF16),
        compiler_params=_params("parallel", "parallel"),
        name="norm_modulate",
    )(x, g, shift, scale)


def _head_norm_rope(a, g, cos, sin):
    y = _rms(a, g)
    if cos is None:
        return y
    lane = lax.broadcasted_iota(jnp.int32, y.shape, 1)
    quarter = ROPE_AXIS_DIM // 2
    partner = jnp.where((lane % ROPE_AXIS_DIM) < quarter,
                        pltpu.roll(y, HEAD_DIM - quarter, 1), pltpu.roll(y, quarter, 1))
    return y * cos + partner * sin


def _q_proj_kernel(h_ref, w_ref, g_ref, cos_ref, sin_ref, o_ref, *, scale):
    tm = h_ref.shape[1]
    for r0 in range(0, tm, tm // 4):
        rows = slice(r0, r0 + tm // 4)
        acc = jnp.dot(h_ref[0, rows, :], w_ref[...], preferred_element_type=F32)
        for hh in range(o_ref.shape[1]):
            a = acc[:, hh * HEAD_DIM:(hh + 1) * HEAD_DIM]
            y = _head_norm_rope(a, g_ref[...], cos_ref[rows, :], sin_ref[rows, :])
            o_ref[0, hh, rows, :] = (y * scale).astype(o_ref.dtype)


def _q_proj(h, w, g, cos_t, sin_t, col_off, scale, tm, tn=1024):
    b, l, d = h.shape
    jb = col_off // tn
    hpt = tn // HEAD_DIM
    return pl.pallas_call(
        functools.partial(_q_proj_kernel, scale=scale),
        grid=(b, l // tm, ATTN_WIDTH // tn),
        in_specs=[pl.BlockSpec((1, tm, d), lambda bi, i, j: (bi, i, 0)),
                  pl.BlockSpec((d, tn), lambda bi, i, j: (0, jb + j)),
                  pl.BlockSpec((1, HEAD_DIM), lambda bi, i, j: (0, 0)),
                  pl.BlockSpec((tm, HEAD_DIM), lambda bi, i, j: (i, 0)),
                  pl.BlockSpec((tm, HEAD_DIM), lambda bi, i, j: (i, 0))],
        out_specs=pl.BlockSpec((1, hpt, tm, HEAD_DIM), lambda bi, i, j: (bi, j, i, 0)),
        out_shape=jax.ShapeDtypeStruct((b, N_Q_HEADS, l, HEAD_DIM), BF16),
        compiler_params=_params("parallel", "parallel", "arbitrary"),
        name="q_proj",
    )(h, w, g, cos_t, sin_t)


def _kv_proj_kernel(h_ref, wk_ref, wv_ref, g_ref, *rest, rope):
    if rope:
        cos_ref, sin_ref, k_ref, v_ref = rest
        cos, sin = cos_ref[...], sin_ref[...]
    else:
        k_ref, v_ref = rest
        cos = sin = None
    tm = h_ref.shape[1]
    n_groups = 2 if tm >= 512 else 1
    for r0 in range(0, tm, tm // n_groups):
        rows = slice(r0, r0 + tm // n_groups)
        h = h_ref[0, rows, :]
        acc = jnp.dot(h, wk_ref[...], preferred_element_type=F32)
        for hh in range(N_KV_HEADS):
            a = acc[:, hh * HEAD_DIM:(hh + 1) * HEAD_DIM]
            y = _head_norm_rope(a, g_ref[...], None if cos is None else cos[rows], None if sin is None else sin[rows])
            k_ref[0, rows, hh * HEAD_DIM:(hh + 1) * HEAD_DIM] = y.astype(k_ref.dtype)
        v_ref[0, rows, :] = jnp.dot(h, wv_ref[...], preferred_element_type=F32).astype(v_ref.dtype)


def _kv_proj(h, w, g, cos_t, sin_t, k_off, tm):
    b, l, d = h.shape
    rope = cos_t is not None
    jk = k_off // KV_WIDTH
    in_specs = [pl.BlockSpec((1, tm, d), lambda bi, i: (bi, i, 0)),
                pl.BlockSpec((d, KV_WIDTH), lambda bi, i: (0, jk)),
                pl.BlockSpec((d, KV_WIDTH), lambda bi, i: (0, jk + 1)),
                pl.BlockSpec((1, HEAD_DIM), lambda bi, i: (0, 0))]
    args = [h, w, w, g]
    if rope:
        in_specs += [pl.BlockSpec((tm, HEAD_DIM), lambda bi, i: (i, 0))] * 2
        args += [cos_t, sin_t]
    return pl.pallas_call(
        functools.partial(_kv_proj_kernel, rope=rope),
        grid=(b, l // tm),
        in_specs=in_specs,
        out_specs=[pl.BlockSpec((1, tm, KV_WIDTH), lambda bi, i: (bi, i, 0))] * 2,
        out_shape=[jax.ShapeDtypeStruct((b, l, KV_WIDTH), BF16)] * 2,
        compiler_params=_params("parallel", "parallel"),
        name="kv_proj_rope" if rope else "kv_proj_ctx",
    )(*args)


def _glu_proj_kernel(h_ref, wa_ref, wg_ref, o_ref):
    a = jnp.dot(h_ref[...], wa_ref[...], preferred_element_type=F32)
    gt = jnp.dot(h_ref[...], wg_ref[...], preferred_element_type=F32)
    o_ref[...] = (a * _sigmoid(gt)).astype(o_ref.dtype)


def _glu_proj(h2d, w, col_off, width, tm, tn=512):
    m, d = h2d.shape
    ja = col_off // tn
    jg = (col_off + width) // tn
    return pl.pallas_call(
        _glu_proj_kernel,
        grid=(m // tm, width // tn),
        in_specs=[pl.BlockSpec((tm, d), lambda i, j: (i, 0)),
                  pl.BlockSpec((d, tn), lambda i, j: (0, ja + j)),
                  pl.BlockSpec((d, tn), lambda i, j: (0, jg + j))],
        out_specs=pl.BlockSpec((tm, tn), lambda i, j: (i, j)),
        out_shape=jax.ShapeDtypeStruct((m, width), BF16),
        compiler_params=_params("parallel", "arbitrary"),
        name="glu_proj",
    )(h2d, w, w)


def _attn_kernel(q_ref, k_ref, v_ref, kc_ref, vc_ref, o_ref, *scratch, tk, rb, n_groups):
    g, tq, dh = q_ref.shape[1:]
    per_group = len(scratch) // n_groups
    tqs = tq // n_groups
    rows = g * tqs
    chunks = [(k_ref, v_ref, c * tk, tk) for c in range(k_ref.shape[1] // tk)]
    chunks.append((kc_ref, vc_ref, 0, kc_ref.shape[1]))

    def bufs(gi):
        s0, s1, p0, p1, m_ref, al_ref, acc_ref = scratch[gi * per_group:(gi + 1) * per_group]
        return (s0, s1), (p0, p1), m_ref, al_ref, acc_ref

    def scores(j, gi):
        kr, _, st, n = chunks[j]
        q = q_ref[0, :, gi * tqs:(gi + 1) * tqs, :].reshape(rows, dh)
        bufs(gi)[0][j % 2][:, :n] = lax.dot_general(q, kr[0, st:st + n, :], (((1,), (1,)), ((), ())),
                                                    preferred_element_type=F32)

    def softmax(j, gi):
        n = chunks[j][3]
        s_refs, p_refs, m_ref, al_ref, _ = bufs(gi)
        s_ref, p_ref = s_refs[j % 2], p_refs[j % 2]
        for r0 in range(0, rows, rb):
            sblk = s_ref[r0:r0 + rb, :n]
            mn = jnp.max(sblk, axis=-1, keepdims=True)
            if j > 0:
                mo = m_ref[r0:r0 + rb, :]
                mn = jnp.maximum(mo, mn)
                al_ref[r0:r0 + rb, :] = jnp.exp2(mo - mn)
            m_ref[r0:r0 + rb, :] = mn
            p_ref[r0:r0 + rb, :n] = jnp.exp2(sblk - mn).astype(BF16)

    def weighted_values(j, gi):
        _, vr, st, n = chunks[j]
        _, p_refs, _, al_ref, acc_ref = bufs(gi)
        ones_col = (lax.broadcasted_iota(jnp.int32, (n, dh), 1) == 0).astype(BF16)
        v1 = jnp.concatenate([vr[0, st:st + n, :], ones_col], axis=1)
        upd = jnp.dot(p_refs[j % 2][:, :n], v1, preferred_element_type=F32)
        if j == 0:
            acc_ref[...] = upd
        else:
            acc_ref[...] = al_ref[...] * acc_ref[...] + upd

    def finish(gi):
        acc = bufs(gi)[4][...]
        o = acc[:, :dh] / acc[:, dh:dh + 1]
        for hi in range(g):
            o_ref[0, gi * tqs:(gi + 1) * tqs, hi * dh:(hi + 1) * dh] = o[hi * tqs:(hi + 1) * tqs].astype(o_ref.dtype)

    for gi in range(n_groups):
        scores(0, gi)
    for j in range(len(chunks)):
        for gi in range(n_groups):
            if j + 1 < len(chunks):
                scores(j + 1, gi)
            softmax(j, gi)
            weighted_values(j, gi)
    for gi in range(n_groups):
        finish(gi)


def _attention(q, k, v, kc, vc, tq, n_groups, tk, rb=8):
    b, _, l, dh = q.shape
    lc = kc.shape[1]
    rows = Q_PER_KV * tq // n_groups
    gdh = Q_PER_KV * dh
    group_scratch = [pltpu.VMEM((rows, tk), F32), pltpu.VMEM((rows, tk), F32),
                     pltpu.VMEM((rows, tk), BF16), pltpu.VMEM((rows, tk), BF16),
                     pltpu.VMEM((rows, 1), F32), pltpu.VMEM((rows, 1), F32),
                     pltpu.VMEM((rows, 2 * dh), F32)]
    return pl.pallas_call(
        functools.partial(_attn_kernel, tk=tk, rb=rb, n_groups=n_groups),
        grid=(b, N_KV_HEADS, l // tq),
        in_specs=[pl.BlockSpec((1, Q_PER_KV, tq, dh), lambda bi, kh, qi: (bi, kh, qi, 0)),
                  pl.BlockSpec((1, l, dh), lambda bi, kh, qi: (bi, 0, kh)),
                  pl.BlockSpec((1, l, dh), lambda bi, kh, qi: (bi, 0, kh)),
                  pl.BlockSpec((1, lc, dh), lambda bi, kh, qi: (bi, 0, kh)),
                  pl.BlockSpec((1, lc, dh), lambda bi, kh, qi: (bi, 0, kh))],
        out_specs=pl.BlockSpec((1, tq, gdh), lambda bi, kh, qi: (bi, qi, kh)),
        out_shape=jax.ShapeDtypeStruct((b, l, ATTN_WIDTH), BF16),
        scratch_shapes=group_scratch * n_groups,
        compiler_params=_params("parallel", "parallel", "arbitrary"),
        name="attention",
    )(q, k, v, kc, vc)


def _conv_kernel(prev_ref, cur_ref, next_ref, w_ref, b_ref, g_ref, beta_ref, o_ref, win_ref, y_ref, *, tc, rc):
    li = pl.program_id(1)
    tl, c = cur_ref.shape[1:]
    halo = prev_ref.shape[1]
    nt = c // LANES

    def put_tokens(vals, tok0):
        for j in range(nt):
            win_ref[pl.ds(tok0 * nt + j, vals.shape[0], stride=nt), :] = vals[:, j * LANES:(j + 1) * LANES]

    def put_chunk(ci, carry):
        r0 = pl.multiple_of(ci * rc, rc)
        put_tokens(cur_ref[0, pl.ds(r0, rc), :].astype(F32), halo + r0)
        return carry

    put_tokens(jnp.where(li > 0, prev_ref[0].astype(F32), 0.0), 0)
    lax.fori_loop(0, tl // rc, put_chunk, 0)
    put_tokens(jnp.where(li < pl.num_programs(1) - 1, next_ref[0].astype(F32), 0.0), halo + tl)

    first = halo - CONV_TAPS // 2
    bias = b_ref[...][None]

    def token_chunk(ci, carry):
        tok = ci * tc
        acc = jnp.zeros((tc, nt, LANES), F32) + bias
        for t in range(CONV_TAPS):
            r0 = pl.multiple_of((tok + first + t) * nt, nt)
            acc = acc + win_ref[pl.ds(r0, tc * nt), :].reshape(tc, nt, LANES) * w_ref[t][None]
        y_ref[pl.ds(pl.multiple_of(tok * nt, nt), tc * nt), :] = acc.reshape(tc * nt, LANES)
        return carry

    lax.fori_loop(0, tl // tc, token_chunk, 0)

    def norm_chunk(ci, carry):
        r0 = pl.multiple_of(ci * rc, rc)
        y = jnp.concatenate([y_ref[pl.ds(r0 * nt + j, rc, stride=nt), :] for j in range(nt)], axis=1)
        mu = jnp.mean(y, axis=-1, keepdims=True)
        yc = y - mu
        var = jnp.mean(yc * yc, axis=-1, keepdims=True)
        z = yc * lax.rsqrt(var + EPS) * g_ref[...] + beta_ref[...]
        o_ref[0, pl.ds(r0, rc), :] = _silu(z).astype(o_ref.dtype)
        return carry

    lax.fori_loop(0, tl // rc, norm_chunk, 0, unroll=2)


def _conv_module(u, w_dw, b_dw, ln_g, ln_b, tl, tc=16, rc=32):
    b, l, c = u.shape
    nt = c // LANES
    hb = tl // CONV_HALO
    n_halo = l // CONV_HALO
    return pl.pallas_call(
        functools.partial(_conv_kernel, tc=tc, rc=rc),
        grid=(b, l // tl),
        in_specs=[pl.BlockSpec((1, CONV_HALO, c), lambda bi, li: (bi, jnp.maximum(li * hb - 1, 0), 0)),
                  pl.BlockSpec((1, tl, c), lambda bi, li: (bi, li, 0)),
                  pl.BlockSpec((1, CONV_HALO, c), lambda bi, li: (bi, jnp.minimum((li + 1) * hb, n_halo - 1), 0)),
                  pl.BlockSpec((CONV_TAPS, nt, LANES), lambda bi, li: (0, 0, 0)),
                  pl.BlockSpec((nt, LANES), lambda bi, li: (0, 0)),
                  pl.BlockSpec((1, c), lambda bi, li: (0, 0)),
                  pl.BlockSpec((1, c), lambda bi, li: (0, 0))],
        out_specs=pl.BlockSpec((1, tl, c), lambda bi, li: (bi, li, 0)),
        out_shape=jax.ShapeDtypeStruct((b, l, c), BF16),
        scratch_shapes=[pltpu.VMEM(((tl + 2 * CONV_HALO) * nt, LANES), F32), pltpu.VMEM((tl * nt, LANES), F32)],
        compiler_params=_params("parallel", "arbitrary"),
        name="conv_module",
    )(u, u, u, w_dw, b_dw, ln_g, ln_b)


def _merge_kernel(h_ref, a_ref, c_ref, wga_ref, wgc_ref, wa_ref, wc_ref, o_ref):
    h = h_ref[...]
    g_a = _sigmoid(jnp.dot(h, wga_ref[...], preferred_element_type=F32))
    g_c = _sigmoid(jnp.dot(h, wgc_ref[...], preferred_element_type=F32))
    a = jnp.dot(a_ref[...], wa_ref[...].astype(BF16), preferred_element_type=F32)
    cb = jnp.dot(c_ref[...], wc_ref[...].astype(BF16), preferred_element_type=F32)
    o_ref[...] = (g_a * a + g_c * cb).astype(o_ref.dtype)


def _merge(h2d, attn, conv, w_in, gate_off, wa, wc, tm=512, tn=512):
    m, d = h2d.shape
    ka = attn.shape[1]
    kc = conv.shape[1]
    nj = d // tn
    ja = gate_off // tn
    jc = (gate_off + d) // tn
    return pl.pallas_call(
        _merge_kernel,
        grid=(nj, m // tm),
        in_specs=[pl.BlockSpec((tm, d), lambda j, i: (i, 0)),
                  pl.BlockSpec((tm, ka), lambda j, i: (i, 0)),
                  pl.BlockSpec((tm, kc), lambda j, i: (i, 0)),
                  pl.BlockSpec((d, tn), lambda j, i: (0, ja + j)),
                  pl.BlockSpec((d, tn), lambda j, i: (0, jc + j)),
                  pl.BlockSpec((ka, tn), lambda j, i: (0, j)),
                  pl.BlockSpec((kc, tn), lambda j, i: (0, j))],
        out_specs=pl.BlockSpec((tm, tn), lambda j, i: (i, j)),
        out_shape=jax.ShapeDtypeStruct((m, d), BF16),
        compiler_params=_params("parallel", "arbitrary"),
        name="merge_branches",
    )(h2d, attn, conv, w_in, w_in, wa, wc)


def _out_proj_kernel(m_ref, w_ref, x_ref, ga_ref, o_ref):
    acc = jnp.dot(m_ref[0], w_ref[...], preferred_element_type=F32)
    o_ref[0] = x_ref[0] + ga_ref[0] * acc


def _out_proj(mrg, w, x, gate, tm=1024, tn=1024):
    b, l, d = x.shape
    return pl.pallas_call(
        _out_proj_kernel,
        grid=(b, l // tm, d // tn),
        in_specs=[pl.BlockSpec((1, tm, d), lambda bi, i, j: (bi, i, 0)),
                  pl.BlockSpec((d, tn), lambda bi, i, j: (0, j)),
                  pl.BlockSpec((1, tm, tn), lambda bi, i, j: (bi, i, j)),
                  pl.BlockSpec((1, 1, tn), lambda bi, i, j: (bi, 0, j))],
        out_specs=pl.BlockSpec((1, tm, tn), lambda bi, i, j: (bi, i, j)),
        out_shape=jax.ShapeDtypeStruct((b, l, d), F32),
        compiler_params=_params("parallel", "parallel", "arbitrary"),
        name="out_proj_residual",
    )(mrg, w, x, gate)


def _pack_halves(y):
    n = y.shape[1] // 2
    return pltpu.pack_elementwise([y[:, :n], y[:, n:]], packed_dtype=BF16)


def _unpack_halves(p):
    lo = pltpu.unpack_elementwise(p, index=0, packed_dtype=BF16, unpacked_dtype=F32)
    hi = pltpu.unpack_elementwise(p, index=1, packed_dtype=BF16, unpacked_dtype=F32)
    return lo, hi


def _norm2_router_kernel(x_ref, g_ref, sh_ref, sc_ref, wr_ref, br_ref, hp_ref, lg_ref):
    y = _rms(x_ref[0], g_ref[...]) * (1.0 + sc_ref[0]) + sh_ref[0]
    packed = _pack_halves(y)
    tl = packed.shape[0]
    nt = packed.shape[1] // LANES
    for j in range(nt):
        hp_ref[pl.ds(j, tl, stride=nt), :] = packed[:, j * LANES:(j + 1) * LANES]
    lg_ref[0] = jnp.dot(y.astype(BF16), wr_ref[...], preferred_element_type=F32) + br_ref[...]


def _norm2_router(x, g, shift, scale, w_r, b_r, tl):
    b, l, d = x.shape
    nt = d // 2 // LANES
    lb = l // tl
    return pl.pallas_call(
        _norm2_router_kernel,
        grid=(b, lb),
        in_specs=[pl.BlockSpec((1, tl, d), lambda bi, li: (bi, li, 0)),
                  pl.BlockSpec((1, d), lambda bi, li: (0, 0)),
                  pl.BlockSpec((1, 1, d), lambda bi, li: (bi, 0, 0)),
                  pl.BlockSpec((1, 1, d), lambda bi, li: (bi, 0, 0)),
                  pl.BlockSpec((d, ROUTER_LANES), lambda bi, li: (0, 0)),
                  pl.BlockSpec((1, ROUTER_LANES), lambda bi, li: (0, 0))],
        out_specs=[pl.BlockSpec((tl * nt, LANES), lambda bi, li: (bi * lb + li, 0)),
                   pl.BlockSpec((1, tl, ROUTER_LANES), lambda bi, li: (bi, li, 0))],
        out_shape=[jax.ShapeDtypeStruct((b * l * nt, LANES), jnp.int32),
                   jax.ShapeDtypeStruct((b, l, ROUTER_LANES), F32)],
        compiler_params=_params("parallel", "parallel"),
        name="norm2_router",
    )(x, g, shift, scale, w_r, b_r)


def _first_lane(mask, lane):
    return jnp.min(jnp.where(mask, lane, LANES), axis=-1, keepdims=True)


def _route_kernel(lg_ref, meta_ref, wts_ref, cnt_ref, carry_ref):
    @pl.when(pl.program_id(0) == 0)
    def _():
        carry_ref[...] = jnp.zeros_like(carry_ref)

    lg = lg_ref[...]
    tb = lg.shape[0]
    lane = lax.broadcasted_iota(jnp.int32, lg.shape, 1)
    neg_inf = jnp.float32(-jnp.inf)
    is_group = lane < N_GROUPS
    gl = jnp.where(is_group, lg, neg_inf)
    g_max = jnp.max(gl, axis=-1, keepdims=True)
    g_sel = _first_lane(gl == g_max, lane)
    p_g = 1.0 / jnp.sum(jnp.where(is_group, jnp.exp(lg - g_max), 0.0), axis=-1, keepdims=True)

    e_idx = lane - N_GROUPS
    in_group = (e_idx >= g_sel * EXPERTS_PER_GROUP) & (e_idx < (g_sel + 1) * EXPERTS_PER_GROUP)
    ev = jnp.where(in_group, lg, neg_inf)
    v1 = jnp.max(ev, axis=-1, keepdims=True)
    i1 = _first_lane(ev == v1, lane)
    ev2 = jnp.where(lane == i1, neg_inf, ev)
    v2 = jnp.max(ev2, axis=-1, keepdims=True)
    i2 = _first_lane(ev2 == v2, lane)
    t = jnp.exp(v2 - v1)
    w1 = p_g / (1.0 + t)
    w2 = w1 * t

    oh1 = lane == i1
    oh2 = lane == i2
    oh = (oh1 | oh2).astype(BF16)
    earlier = (lax.broadcasted_iota(jnp.int32, (tb, tb), 0) > lax.broadcasted_iota(jnp.int32, (tb, tb), 1)).astype(BF16)
    before = jnp.dot(earlier, oh, preferred_element_type=F32) + carry_ref[...]
    r1 = jnp.sum(jnp.where(oh1, before, 0.0), axis=-1, keepdims=True).astype(jnp.int32)
    r2 = jnp.sum(jnp.where(oh2, before, 0.0), axis=-1, keepdims=True).astype(jnp.int32)
    carry_ref[...] += jnp.sum(oh.astype(F32), axis=0, keepdims=True)

    meta_ref[...] = jnp.where(lane == 0, i1 - N_GROUPS, jnp.where(lane == 1, i2 - N_GROUPS,
                              jnp.where(lane == 2, r1, jnp.where(lane == 3, r2, 0))))
    wts_ref[...] = jnp.where(lane == 0, w1, jnp.where(lane == 1, w2, 0.0))
    cnt_ref[...] = carry_ref[...]


def _route(logits, tb=512):
    n = logits.shape[0]
    return pl.pallas_call(
        _route_kernel,
        grid=(n // tb,),
        in_specs=[pl.BlockSpec((tb, LANES), lambda i: (i, 0))],
        out_specs=[pl.BlockSpec((tb, LANES), lambda i: (i, 0)),
                   pl.BlockSpec((tb, LANES), lambda i: (i, 0)),
                   pl.BlockSpec((1, LANES), lambda i: (0, 0))],
        out_shape=[jax.ShapeDtypeStruct((n, LANES), jnp.int32),
                   jax.ShapeDtypeStruct((n, LANES), F32),
                   jax.ShapeDtypeStruct((1, LANES), F32)],
        scratch_shapes=[pltpu.VMEM((1, LANES), F32)],
        compiler_params=_params("arbitrary"),
        name="moe_route",
    )(logits)


def _dest_kernel(meta_ref, pst_ref, o_ref):
    meta = meta_ref[...]
    lane = lax.broadcasted_iota(jnp.int32, meta.shape, 1)
    pst = pst_ref[...]

    def row_of(slot):
        e = meta[:, slot:slot + 1]
        start = jnp.sum(jnp.where(lane == e + N_GROUPS, pst, 0), axis=-1, keepdims=True)
        return start + meta[:, TOP_K + slot:TOP_K + slot + 1]

    o_ref[...] = jnp.where(lane == 0, row_of(0), jnp.where(lane == 1, row_of(1), 0))


def _dest_rows(meta, pst, tb=512):
    n = meta.shape[0]
    return pl.pallas_call(
        _dest_kernel,
        grid=(n // tb,),
        in_specs=[pl.BlockSpec((tb, LANES), lambda i: (i, 0)),
                  pl.BlockSpec((1, LANES), lambda i: (0, 0))],
        out_specs=pl.BlockSpec((tb, LANES), lambda i: (i, 0)),
        out_shape=jax.ShapeDtypeStruct((n, LANES), jnp.int32),
        compiler_params=_params("parallel"),
        name="moe_dest_rows",
    )(meta, pst)


def _zero_tail_kernel(lb_ref, o_ref):
    o_ref[...] = jnp.zeros_like(o_ref)


def _zero_tails(last_blk, total, tmb, nt):
    return pl.pallas_call(
        _zero_tail_kernel,
        grid_spec=pltpu.PrefetchScalarGridSpec(
            num_scalar_prefetch=1,
            grid=(last_blk.shape[0],),
            in_specs=[],
            out_specs=pl.BlockSpec((tmb * nt, LANES), lambda e, lb: (lb[e], 0))),
        out_shape=jax.ShapeDtypeStruct((total * nt, LANES), jnp.int32),
        compiler_params=_params("arbitrary"),
        name="moe_zero_tails",
    )(last_blk)


def _dispatch_kernel(dest_ref, hp_ref, xs_in_ref, xs_ref, sem):
    rows = hp_ref.shape[0]

    def slab_copy(r, k):
        return pltpu.make_async_copy(hp_ref.at[r], xs_ref.at[dest_ref[0, 0, r * TOP_K + k]], sem)

    def start(r, c):
        for k in range(TOP_K):
            slab_copy(r, k).start(priority=k % 2)
        return c

    def wait(r, c):
        for k in range(TOP_K):
            slab_copy(r, k).wait()
        return c

    lax.fori_loop(0, rows, start, 0, unroll=DMA_LOOP_UNROLL)
    lax.fori_loop(0, rows, wait, 0, unroll=DMA_LOOP_UNROLL)


def _dispatch(hp3, dest, xs0, rows):
    n, nt, _ = hp3.shape
    steps = n // rows
    return pl.pallas_call(
        _dispatch_kernel,
        grid=(steps,),
        in_specs=[pl.BlockSpec((1, 1, rows * TOP_K), lambda i: (i, 0, 0), memory_space=pltpu.SMEM),
                  pl.BlockSpec((rows, nt, LANES), lambda i: (i, 0, 0)),
                  pl.BlockSpec(memory_space=pl.ANY)],
        out_specs=pl.BlockSpec(memory_space=pl.ANY),
        out_shape=jax.ShapeDtypeStruct(xs0.shape, xs0.dtype),
        scratch_shapes=[pltpu.SemaphoreType.DMA(())],
        input_output_aliases={2: 0},
        compiler_params=_params("arbitrary"),
        name="moe_dispatch",
    )(dest.reshape(steps, 1, rows * TOP_K), hp3, xs0)


def _by_valid_rows(nv, tmb, compute, o_ref):
    quarter = tmb // 4
    per_row = o_ref.shape[0] // tmb

    for nq in range(1, 5):
        rows = nq * quarter

        @pl.when((nv > rows - quarter) & (nv <= rows))
        def _(rows=rows):
            compute(rows)
            if rows < tmb:
                o_ref[rows * per_row:, :] = jnp.zeros(((tmb - rows) * per_row, o_ref.shape[1]), o_ref.dtype)

    @pl.when(nv == 0)
    def _():
        o_ref[...] = jnp.zeros_like(o_ref)


def _expert_weights(plan, w_hbm, wbuf, sem):
    be_ref, first_ref, slot_ref, nxt_ref = plan
    bi = pl.program_id(0)
    slot = slot_ref[bi]

    def copy(e, s):
        return pltpu.make_async_copy(w_hbm.at[e], wbuf.at[s], sem.at[s])

    @pl.when(bi == 0)
    def _():
        copy(be_ref[0], 0).start()

    @pl.when((first_ref[bi] == 1) & (nxt_ref[bi] >= 0))
    def _():
        copy(nxt_ref[bi], 1 - slot).start()

    @pl.when(first_ref[bi] == 1)
    def _():
        copy(be_ref[bi], slot).wait()

    return wbuf.at[slot]


def _expert_in_kernel(be_ref, first_ref, slot_ref, nxt_ref, nu_ref, nv_ref, x_ref, w_hbm, *rest, nt, tn):
    gate_ref = rest[0] if len(rest) == 4 else None
    o_ref, wbuf, sem = rest[-3:]
    w_ref = _expert_weights((be_ref, first_ref, slot_ref, nxt_ref), w_hbm, wbuf, sem)

    def compute(rows):
        halves = [_unpack_halves(x_ref[pl.ds(j, rows, stride=nt), :]) for j in range(nt)]
        lo = jnp.concatenate([h[0].astype(BF16) for h in halves], axis=1)
        hi = jnp.concatenate([h[1].astype(BF16) for h in halves], axis=1)
        half = nt * LANES
        for c0 in range(0, o_ref.shape[1], tn):
            y = (jnp.dot(lo, w_ref[:half, c0:c0 + tn].astype(BF16), preferred_element_type=F32)
                 + jnp.dot(hi, w_ref[half:, c0:c0 + tn].astype(BF16), preferred_element_type=F32))
            if gate_ref is None:
                y = _silu(y)
            else:
                y = gate_ref[:rows, c0:c0 + tn].astype(F32) * y
            o_ref[:rows, c0:c0 + tn] = y.astype(o_ref.dtype)

    _by_valid_rows(nv_ref[pl.program_id(0)], o_ref.shape[0], compute, o_ref)


N_PLAN = 6


def _blk_clamped(bi, *plan):
    return jnp.minimum(bi, plan[4][0] - 1)


def _expert_in(xs, w, gate, plan, tmb, name, tn=256):
    _, d, ff = w.shape
    nt = d // 2 // LANES
    total = xs.shape[0] // nt
    nblk = total // tmb
    in_specs = [pl.BlockSpec((tmb * nt, LANES), lambda bi, *p: (_blk_clamped(bi, *p), 0)),
                pl.BlockSpec(memory_space=pl.ANY)]
    args = [xs, w]
    if gate is not None:
        in_specs.append(pl.BlockSpec((tmb, ff), lambda bi, *p: (_blk_clamped(bi, *p), 0)))
        args.append(gate)
    return pl.pallas_call(
        functools.partial(_expert_in_kernel, nt=nt, tn=tn),
        grid_spec=pltpu.PrefetchScalarGridSpec(
            num_scalar_prefetch=N_PLAN,
            grid=(nblk,),
            in_specs=in_specs,
            out_specs=pl.BlockSpec((tmb, ff), lambda bi, *p: (bi, 0)),
            scratch_shapes=[pltpu.VMEM((2, d, ff), w.dtype), pltpu.SemaphoreType.DMA((2,))]),
        out_shape=jax.ShapeDtypeStruct((total, ff), BF16),
        compiler_params=_params("arbitrary"),
        name=name,
    )(*plan, *args)


def _expert_down_kernel(be_ref, first_ref, slot_ref, nxt_ref, nu_ref, nv_ref, a_ref, w_hbm, o_ref, wbuf, sem,
                        *, nt, tn):
    tmb = a_ref.shape[0]
    w_ref = _expert_weights((be_ref, first_ref, slot_ref, nxt_ref), w_hbm, wbuf, sem)

    def compute(rows):
        a = a_ref[:rows, :]
        half = nt * LANES
        for c0 in range(0, half, tn):
            ylo = jnp.dot(a, w_ref[:, c0:c0 + tn].astype(BF16), preferred_element_type=F32)
            yhi = jnp.dot(a, w_ref[:, half + c0:half + c0 + tn].astype(BF16), preferred_element_type=F32)
            packed = pltpu.pack_elementwise([ylo, yhi], packed_dtype=BF16)
            for j in range(tn // LANES):
                o_ref[pl.ds(c0 // LANES + j, rows, stride=nt), :] = packed[:, j * LANES:(j + 1) * LANES]

    _by_valid_rows(nv_ref[pl.program_id(0)], tmb, compute, o_ref)


def _expert_down(act, w_down, plan, tmb, tn=512):
    total, ff = act.shape
    d = w_down.shape[2]
    nt = d // 2 // LANES
    nblk = total // tmb
    return pl.pallas_call(
        functools.partial(_expert_down_kernel, nt=nt, tn=tn),
        grid_spec=pltpu.PrefetchScalarGridSpec(
            num_scalar_prefetch=N_PLAN,
            grid=(nblk,),
            in_specs=[pl.BlockSpec((tmb, ff), lambda bi, *p: (_blk_clamped(bi, *p), 0)),
                      pl.BlockSpec(memory_space=pl.ANY)],
            out_specs=pl.BlockSpec((tmb * nt, LANES), lambda bi, *p: (bi, 0)),
            scratch_shapes=[pltpu.VMEM((2, ff, d), w_down.dtype), pltpu.SemaphoreType.DMA((2,))]),
        out_shape=jax.ShapeDtypeStruct((total * nt, LANES), jnp.int32),
        compiler_params=_params("arbitrary"),
        name="expert_down",
    )(*plan, act, w_down)


def _combine_kernel(pos_ref, nxt_ref, ys_ref, w_ref, x_ref, ga_ref, g_ref, o_ref, *scratch):
    n_buf = 2 * TOP_K
    slabs, sems = scratch[:n_buf], scratch[n_buf:]
    rows = x_ref.shape[1]
    hr = rows // 2
    nt = ys_ref.shape[1]
    half = nt * LANES
    step = pl.program_id(0) * pl.num_programs(1) + pl.program_id(1)
    n_steps = pl.num_programs(0) * pl.num_programs(1)

    def slab_copy(idx_ref, h, r, k):
        dst = slabs[h * TOP_K + k].at[pl.ds(pl.multiple_of(r * SLAB_PITCH, SUBLANES), nt)]
        return pltpu.make_async_copy(ys_ref.at[idx_ref[0, 0, (h * hr + r) * TOP_K + k]], dst, sems[h * TOP_K + k])

    def issue(idx_ref, h):
        def body(r, c):
            for k in range(TOP_K):
                slab_copy(idx_ref, h, r, k).start(priority=k % 2)
            return c
        lax.fori_loop(0, hr, body, 0, unroll=DMA_LOOP_UNROLL)

    def wait(idx_ref, h):
        def body(r, c):
            for k in range(TOP_K):
                slab_copy(idx_ref, h, r, k).wait()
            return c
        lax.fori_loop(0, hr, body, 0, unroll=DMA_LOOP_UNROLL)

    def compute(h):
        r0 = h * hr
        w0 = w_ref[r0:r0 + hr, 0:1]
        w1 = w_ref[r0:r0 + hr, 1:2]
        ss = jnp.zeros((hr, 1), F32)
        for j in range(nt):
            lo0, hi0 = _unpack_halves(slabs[h * TOP_K][pl.ds(j, hr, stride=SLAB_PITCH), :])
            lo1, hi1 = _unpack_halves(slabs[h * TOP_K + 1][pl.ds(j, hr, stride=SLAB_PITCH), :])
            for c0, y in ((j * LANES, w0 * lo0 + w1 * lo1), (half + j * LANES, w0 * hi0 + w1 * hi1)):
                z = x_ref[0, r0:r0 + hr, c0:c0 + LANES] + ga_ref[0, :, c0:c0 + LANES] * y
                ss = ss + jnp.sum(z * z, axis=-1, keepdims=True)
                o_ref[0, r0:r0 + hr, c0:c0 + LANES] = z
        inv = lax.rsqrt(ss / (2 * half) + EPS)
        o_ref[0, r0:r0 + hr, :] = o_ref[0, r0:r0 + hr, :] * inv * g_ref[...]

    @pl.when(step == 0)
    def _():
        issue(pos_ref, 0)

    issue(pos_ref, 1)
    wait(pos_ref, 0)
    compute(0)

    @pl.when(step + 1 < n_steps)
    def _():
        issue(nxt_ref, 0)

    wait(pos_ref, 1)
    compute(1)


def _combine(ys3, pos, wts, x, gate, g, rows):
    b, l, d = x.shape
    lb = l // rows
    n_steps = b * lb
    pos3 = pos.reshape(n_steps, 1, rows * TOP_K)
    slab = pltpu.VMEM((rows // 2 * SLAB_PITCH, LANES), ys3.dtype)
    return pl.pallas_call(
        _combine_kernel,
        grid=(b, lb),
        in_specs=[pl.BlockSpec((1, 1, rows * TOP_K), lambda bi, i: (bi * lb + i, 0, 0), memory_space=pltpu.SMEM),
                  pl.BlockSpec((1, 1, rows * TOP_K), lambda bi, i: (jnp.minimum(bi * lb + i + 1, n_steps - 1), 0, 0),
                               memory_space=pltpu.SMEM),
                  pl.BlockSpec(memory_space=pl.ANY),
                  pl.BlockSpec((rows, LANES), lambda bi, i: (bi * lb + i, 0)),
                  pl.BlockSpec((1, rows, d), lambda bi, i: (bi, i, 0)),
                  pl.BlockSpec((1, 1, d), lambda bi, i: (bi, 0, 0)),
                  pl.BlockSpec((1, d), lambda bi, i: (0, 0))],
        out_specs=pl.BlockSpec((1, rows, d), lambda bi, i: (bi, i, 0)),
        out_shape=jax.ShapeDtypeStruct((b, l, d), F32),
        scratch_shapes=[slab] * (2 * TOP_K) + [pltpu.SemaphoreType.DMA(())] * (2 * TOP_K),
        compiler_params=_params("arbitrary", "arbitrary"),
        name="moe_combine_norm",
    )(pos3, pos3, ys3, wts, x, gate, g)


def _block_layout(counts, n_pairs, tmb):
    nblk = (n_pairs + N_EXPERTS * (tmb - 1) + tmb - 1) // tmb
    blocks = (counts + tmb - 1) // tmb
    bend = jnp.cumsum(blocks)
    bstart = bend - blocks
    pstart = bstart * tmb
    n_used = bend[-1]
    ids = jnp.arange(nblk, dtype=jnp.int32)
    blk_e = jnp.minimum(jnp.searchsorted(bend, jnp.minimum(ids, n_used - 1), side="right"),
                        N_EXPERTS - 1).astype(jnp.int32)
    last_blk = jnp.maximum(bend - 1, 0).astype(jnp.int32)
    used = ids < n_used
    n_valid = jnp.where(used, jnp.clip(counts[blk_e] - (ids - bstart[blk_e]) * tmb, 0, tmb), 0)
    first = used & ((ids == 0) | (blk_e != jnp.roll(blk_e, 1)))
    slot = (jnp.cumsum(first.astype(jnp.int32)) - 1) % 2
    experts = jnp.arange(N_EXPERTS, dtype=jnp.int32)
    later = jnp.where((blocks > 0)[None, :] & (experts[None, :] > experts[:, None]), experts[None, :], N_EXPERTS)
    nxt_e = jnp.min(later, axis=1)
    nxt = jnp.where(nxt_e[blk_e] < N_EXPERTS, nxt_e[blk_e], -1)
    i32 = lambda a: a.astype(jnp.int32)
    plan = (blk_e, i32(first), i32(slot), i32(nxt), i32(n_used).reshape(1), i32(n_valid))
    return nblk, i32(pstart), last_blk, plan


def _rope_tables(n_tokens):
    rows = n_tokens // GRID_W
    row, col = jnp.meshgrid(jnp.arange(rows), jnp.arange(GRID_W), indexing="ij")
    pos = jnp.stack([row.reshape(-1), col.reshape(-1)], axis=-1).astype(F32)
    inv = ROPE_THETA ** (-jnp.arange(0, ROPE_AXIS_DIM, 2, dtype=F32) / ROPE_AXIS_DIM)
    ang = pos[:, :, None] * inv[None, None, :]
    cos, sin = jnp.cos(ang), jnp.sin(ang)
    cos_t = jnp.concatenate([cos[:, 0], cos[:, 0], cos[:, 1], cos[:, 1]], axis=-1)
    sin_t = jnp.concatenate([-sin[:, 0], sin[:, 0], -sin[:, 1], sin[:, 1]], axis=-1)
    return cos_t, sin_t


def kernel(x, c, ctx, c_ctx, norm1_g, w_mod, b_mod, w_in, q_norm_g, k_norm_g, w_attn_out, conv_dw_w, conv_dw_b, conv_ln_g, conv_ln_b, w_conv_out, w_out, norm2_g, w_router_group, b_router_group, w_router_expert, b_router_expert, w_exp_gate, w_exp_up, w_exp_down, norm_f_g):
    b, s, d = x.shape
    n_ctx = ctx.shape[1]
    assert w_in.shape[0] == 1, "single-layer stack"
    conv_width = conv_dw_w.shape[-1]
    k_off = ATTN_WIDTH
    glu_off = k_off + 2 * KV_WIDTH
    gate_off = glu_off + 2 * conv_width

    n_c = b + 1
    cvec = jnp.zeros((SUBLANES * ((n_c + SUBLANES - 1) // SUBLANES), d), F32).at[:b].set(c).at[b].set(c_ctx)
    mod = _mod_vectors(cvec, w_mod[0], b_mod.reshape(1, -1))
    sh1, sc1, ga1, sh2, sc2, ga2 = [mod[:b, i * d:(i + 1) * d].reshape(b, 1, d) for i in range(N_MOD)]
    csh1, csc1 = [mod[b:b + 1, i * d:(i + 1) * d].reshape(1, 1, d) for i in range(2)]

    g1 = norm1_g.reshape(1, d)
    h = _norm_mod(x, g1, sh1, sc1, tl=512)
    hc = _norm_mod(ctx, g1, csh1, csc1, tl=n_ctx)
    w_in_b = w_in[0].astype(BF16)
    cos_t, sin_t = _rope_tables(s)
    qg = q_norm_g.reshape(1, HEAD_DIM)
    kg = k_norm_g.reshape(1, HEAD_DIM)
    q = _q_proj(h, w_in_b, qg, cos_t, sin_t, 0, HEAD_DIM ** -0.5 * LOG2E, tm=1024)
    k, v = _kv_proj(h, w_in_b, kg, cos_t, sin_t, k_off, tm=1024)
    kc, vc = _kv_proj(hc, w_in_b, kg, None, None, k_off, tm=n_ctx)
    attn = _attention(q, k, v, kc, vc, tq=128, n_groups=1, tk=1024)

    h2d = h.reshape(b * s, d)
    u = _glu_proj(h2d, w_in_b, glu_off, conv_width, tm=1024)
    conv = _conv_module(u.reshape(b, s, conv_width), conv_dw_w.reshape(CONV_TAPS, conv_width // LANES, LANES),
                        conv_dw_b.reshape(conv_width // LANES, LANES), conv_ln_g.reshape(1, -1),
                        conv_ln_b.reshape(1, -1), tl=256)
    mrg = _merge(h2d, attn.reshape(b * s, ATTN_WIDTH), conv.reshape(b * s, conv_width), w_in_b, gate_off,
                 w_attn_out[0], w_conv_out[0])
    x1 = _out_proj(mrg.reshape(b, s, d), w_out[0].astype(BF16), x, ga1)

    w_r = jnp.zeros((d, ROUTER_LANES), F32).at[:, :N_GROUPS].set(w_router_group[0]) \
        .at[:, N_GROUPS:N_GROUPS + N_EXPERTS].set(w_router_expert[0])
    b_r = jnp.zeros((1, ROUTER_LANES), F32).at[0, :N_GROUPS].set(b_router_group[0]) \
        .at[0, N_GROUPS:N_GROUPS + N_EXPERTS].set(b_router_expert[0])
    hp, logits = _norm2_router(x1, norm2_g.reshape(1, d), sh2, sc2, w_r.astype(BF16), b_r, tl=256)
    n = b * s
    nt = d // 2 // LANES
    tmb = 512
    meta, wts, cnt = _route(logits.reshape(n, ROUTER_LANES))
    counts = cnt[0, N_GROUPS:N_GROUPS + N_EXPERTS].astype(jnp.int32)
    nblk, pstart, last_blk, plan = _block_layout(counts, n * TOP_K, tmb)
    pst = jnp.zeros((1, LANES), jnp.int32).at[0, N_GROUPS:N_GROUPS + N_EXPERTS].set(pstart)
    dest = _dest_rows(meta, pst)[:, :TOP_K].reshape(-1)
    xs0 = _zero_tails(last_blk, nblk * tmb, tmb, nt)
    xs = _dispatch(hp.reshape(n, nt, LANES), dest, xs0.reshape(nblk * tmb, nt, LANES), rows=512)
    xs2 = xs.reshape(nblk * tmb * nt, LANES)
    sg = _expert_in(xs2, w_exp_gate[0], None, plan, tmb, "expert_gate")
    act = _expert_in(xs2, w_exp_up[0], sg, plan, tmb, "expert_up")
    ys = _expert_down(act, w_exp_down[0], plan, tmb)
    return _combine(ys.reshape(nblk * tmb, nt, LANES), dest, wts, x1, ga2, norm_f_g.reshape(1, d), rows=512)
```

```python
import functools

import jax
import jax.numpy as jnp
from jax import lax
from jax.experimental import pallas as pl
from jax.experimental.pallas import tpu as pltpu

F32 = jnp.float32
BF16 = jnp.bfloat16

GRID_W = 64
HEAD_DIM = 128
N_Q_HEADS = 16
N_KV_HEADS = 4
Q_PER_KV = N_Q_HEADS // N_KV_HEADS
ATTN_WIDTH = N_Q_HEADS * HEAD_DIM
KV_WIDTH = N_KV_HEADS * HEAD_DIM
CONV_TAPS = 31
CONV_HALO = 16
ROPE_THETA = 10000.0
ROPE_AXIS_DIM = HEAD_DIM // 2
N_GROUPS = 4
EXPERTS_PER_GROUP = 8
N_EXPERTS = N_GROUPS * EXPERTS_PER_GROUP
TOP_K = 2
N_MOD = 6
EPS = 1e-6
LOG2E = 1.4426950408889634
LANES = 128
SUBLANES = 8
ROUTER_LANES = LANES
SLAB_PITCH = 24
DMA_LOOP_UNROLL = 4

V7X_VMEM_LIMIT = 56 * 1024 * 1024


def _params(*sem):
    return pltpu.CompilerParams(dimension_semantics=sem, vmem_limit_bytes=V7X_VMEM_LIMIT)


def _sigmoid(x):
    return 1.0 / (1.0 + jnp.exp(-x))


def _silu(x):
    return x * _sigmoid(x)


def _rms(x, g):
    return x * lax.rsqrt(jnp.mean(x * x, axis=-1, keepdims=True) + EPS) * g


def _mod_kernel(c_ref, w_ref, b_ref, o_ref):
    s = _silu(c_ref[...]).astype(BF16)
    o_ref[...] = jnp.dot(s, w_ref[...].astype(BF16), preferred_element_type=F32) + b_ref[...]


def _mod_vectors(cvec, w_mod, b_mod, tn=1024):
    m, d = cvec.shape
    n = w_mod.shape[1]
    return pl.pallas_call(
        _mod_kernel,
        grid=(n // tn,),
        in_specs=[pl.BlockSpec((m, d), lambda j: (0, 0)),
                  pl.BlockSpec((d, tn), lambda j: (0, j)),
                  pl.BlockSpec((1, tn), lambda j: (0, j))],
        out_specs=pl.BlockSpec((m, tn), lambda j: (0, j)),
        out_shape=jax.ShapeDtypeStruct((m, n), F32),
        compiler_params=_params("arbitrary"),
        name="mod_vectors",
    )(cvec, w_mod, b_mod)


def _norm_mod_kernel(x_ref, g_ref, sh_ref, sc_ref, o_ref):
    y = _rms(x_ref[0], g_ref[...])
    o_ref[0] = (y * (1.0 + sc_ref[0]) + sh_ref[0]).astype(o_ref.dtype)


def _norm_mod(x, g, shift, scale, tl):
    b, l, d = x.shape
    per_batch = shift.shape[0] > 1
    mod_map = (lambda bi, li: (bi, 0, 0)) if per_batch else (lambda bi, li: (0, 0, 0))
    return pl.pallas_call(
        _norm_mod_kernel,
        grid=(b, l // tl),
        in_specs=[pl.BlockSpec((1, tl, d), lambda bi, li: (bi, li, 0)),
                  pl.BlockSpec((1, d), lambda bi, li: (0, 0)),
                  pl.BlockSpec((1, 1, d), mod_map),
                  pl.BlockSpec((1, 1, d), mod_map)],
        out_specs=pl.BlockSpec((1, tl, d), lambda bi, li: (bi, li, 0)),
        out_shape=jax.ShapeDtypeStruct((b, l, d), BF16),
        compiler_params=_params("parallel", "parallel"),
        name="norm_modulate",
    )(x, g, shift, scale)


def _head_norm_rope(a, g, cos, sin):
    y = _rms(a, g)
    if cos is None:
        return y
    lane = lax.broadcasted_iota(jnp.int32, y.shape, 1)
    quarter = ROPE_AXIS_DIM // 2
    partner = jnp.where((lane % ROPE_AXIS_DIM) < quarter,
                        pltpu.roll(y, HEAD_DIM - quarter, 1), pltpu.roll(y, quarter, 1))
    return y * cos + partner * sin


def _q_proj_kernel(h_ref, w_ref, g_ref, cos_ref, sin_ref, o_ref, *, scale):
    tm = h_ref.shape[1]
    for r0 in range(0, tm, tm // 4):
        rows = slice(r0, r0 + tm // 4)
        acc = jnp.dot(h_ref[0, rows, :], w_ref[...], preferred_element_type=F32)
        for hh in range(o_ref.shape[1]):
            a = acc[:, hh * HEAD_DIM:(hh + 1) * HEAD_DIM]
            y = _head_norm_rope(a, g_ref[...], cos_ref[rows, :], sin_ref[rows, :])
            o_ref[0, hh, rows, :] = (y * scale).astype(o_ref.dtype)


def _q_proj(h, w, g, cos_t, sin_t, col_off, scale, tm, tn=1024):
    b, l, d = h.shape
    jb = col_off // tn
    hpt = tn // HEAD_DIM
    return pl.pallas_call(
        functools.partial(_q_proj_kernel, scale=scale),
        grid=(b, l // tm, ATTN_WIDTH // tn),
        in_specs=[pl.BlockSpec((1, tm, d), lambda bi, i, j: (bi, i, 0)),
                  pl.BlockSpec((d, tn), lambda bi, i, j: (0, jb + j)),
                  pl.BlockSpec((1, HEAD_DIM), lambda bi, i, j: (0, 0)),
                  pl.BlockSpec((tm, HEAD_DIM), lambda bi, i, j: (i, 0)),
                  pl.BlockSpec((tm, HEAD_DIM), lambda bi, i, j: (i, 0))],
        out_specs=pl.BlockSpec((1, hpt, tm, HEAD_DIM), lambda bi, i, j: (bi, j, i, 0)),
        out_shape=jax.ShapeDtypeStruct((b, N_Q_HEADS, l, HEAD_DIM), BF16),
        compiler_params=_params("parallel", "parallel", "arbitrary"),
        name="q_proj",
    )(h, w, g, cos_t, sin_t)


def _kv_proj_kernel(h_ref, wk_ref, wv_ref, g_ref, *rest, rope):
    if rope:
        cos_ref, sin_ref, k_ref, v_ref = rest
        cos, sin = cos_ref[...], sin_ref[...]
    else:
        k_ref, v_ref = rest
        cos = sin = None
    tm = h_ref.shape[1]
    n_groups = 2 if tm >= 512 else 1
    for r0 in range(0, tm, tm // n_groups):
        rows = slice(r0, r0 + tm // n_groups)
        h = h_ref[0, rows, :]
        acc = jnp.dot(h, wk_ref[...], preferred_element_type=F32)
        for hh in range(N_KV_HEADS):
            a = acc[:, hh * HEAD_DIM:(hh + 1) * HEAD_DIM]
            y = _head_norm_rope(a, g_ref[...], None if cos is None else cos[rows], None if sin is None else sin[rows])
            k_ref[0, rows, hh * HEAD_DIM:(hh + 1) * HEAD_DIM] = y.astype(k_ref.dtype)
        v_ref[0, rows, :] = jnp.dot(h, wv_ref[...], preferred_element_type=F32).astype(v_ref.dtype)


def _kv_proj(h, w, g, cos_t, sin_t, k_off, tm):
    b, l, d = h.shape
    rope = cos_t is not None
    jk = k_off // KV_WIDTH
    in_specs = [pl.BlockSpec((1, tm, d), lambda bi, i: (bi, i, 0)),
                pl.BlockSpec((d, KV_WIDTH), lambda bi, i: (0, jk)),
                pl.BlockSpec((d, KV_WIDTH), lambda bi, i: (0, jk + 1)),
                pl.BlockSpec((1, HEAD_DIM), lambda bi, i: (0, 0))]
    args = [h, w, w, g]
    if rope:
        in_specs += [pl.BlockSpec((tm, HEAD_DIM), lambda bi, i: (i, 0))] * 2
        args += [cos_t, sin_t]
    return pl.pallas_call(
        functools.partial(_kv_proj_kernel, rope=rope),
        grid=(b, l // tm),
        in_specs=in_specs,
        out_specs=[pl.BlockSpec((1, tm, KV_WIDTH), lambda bi, i: (bi, i, 0))] * 2,
        out_shape=[jax.ShapeDtypeStruct((b, l, KV_WIDTH), BF16)] * 2,
        compiler_params=_params("parallel", "parallel"),
        name="kv_proj_rope" if rope else "kv_proj_ctx",
    )(*args)


def _glu_proj_kernel(h_ref, wa_ref, wg_ref, o_ref):
    tm = h_ref.shape[0]
    for r0 in range(0, tm, tm // 2):
        rows = slice(r0, r0 + tm // 2)
        a = jnp.dot(h_ref[rows, :], wa_ref[...], preferred_element_type=F32)
        gt = jnp.dot(h_ref[rows, :], wg_ref[...], preferred_element_type=F32)
        o_ref[rows, :] = (a * _sigmoid(gt)).astype(o_ref.dtype)


def _glu_proj(h2d, w, col_off, width, tm, tn=512):
    m, d = h2d.shape
    ja = col_off // tn
    jg = (col_off + width) // tn
    return pl.pallas_call(
        _glu_proj_kernel,
        grid=(m // tm, width // tn),
        in_specs=[pl.BlockSpec((tm, d), lambda i, j: (i, 0)),
                  pl.BlockSpec((d, tn), lambda i, j: (0, ja + j)),
                  pl.BlockSpec((d, tn), lambda i, j: (0, jg + j))],
        out_specs=pl.BlockSpec((tm, tn), lambda i, j: (i, j)),
        out_shape=jax.ShapeDtypeStruct((m, width), BF16),
        compiler_params=_params("parallel", "arbitrary"),
        name="glu_proj",
    )(h2d, w, w)


def _attn_kernel(q_ref, k_ref, v_ref, kc_ref, vc_ref, o_ref, *scratch, tk, rb, n_groups):
    g, tq, dh = q_ref.shape[1:]
    per_group = len(scratch) // n_groups
    tqs = tq // n_groups
    rows = g * tqs
    chunks = [(k_ref, v_ref, c * tk, tk) for c in range(k_ref.shape[1] // tk)]
    chunks.append((kc_ref, vc_ref, 0, kc_ref.shape[1]))

    def bufs(gi):
        s0, s1, p0, p1, m_ref, al_ref, acc_ref = scratch[gi * per_group:(gi + 1) * per_group]
        return (s0, s1), (p0, p1), m_ref, al_ref, acc_ref

    def scores(j, gi):
        kr, _, st, n = chunks[j]
        q = q_ref[0, :, gi * tqs:(gi + 1) * tqs, :].reshape(rows, dh)
        bufs(gi)[0][j % 2][:, :n] = lax.dot_general(q, kr[0, st:st + n, :], (((1,), (1,)), ((), ())),
                                                    preferred_element_type=F32)

    def softmax(j, gi):
        n = chunks[j][3]
        s_refs, p_refs, m_ref, al_ref, _ = bufs(gi)
        s_ref, p_ref = s_refs[j % 2], p_refs[j % 2]
        for r0 in range(0, rows, rb):
            sblk = s_ref[r0:r0 + rb, :n]
            mn = jnp.max(sblk, axis=-1, keepdims=True)
            if j > 0:
                mo = m_ref[r0:r0 + rb, :]
                mn = jnp.maximum(mo, mn)
                al_ref[r0:r0 + rb, :] = jnp.exp2(mo - mn)
            m_ref[r0:r0 + rb, :] = mn
            p_ref[r0:r0 + rb, :n] = jnp.exp2(sblk - mn).astype(BF16)

    def weighted_values(j, gi):
        _, vr, st, n = chunks[j]
        _, p_refs, _, al_ref, acc_ref = bufs(gi)
        ones_col = (lax.broadcasted_iota(jnp.int32, (n, dh), 1) == 0).astype(BF16)
        v1 = jnp.concatenate([vr[0, st:st + n, :], ones_col], axis=1)
        upd = jnp.dot(p_refs[j % 2][:, :n], v1, preferred_element_type=F32)
        if j == 0:
            acc_ref[...] = upd
        else:
            acc_ref[...] = al_ref[...] * acc_ref[...] + upd

    def finish(gi):
        acc = bufs(gi)[4][...]
        o = acc[:, :dh] / acc[:, dh:dh + 1]
        for hi in range(g):
            o_ref[0, gi * tqs:(gi + 1) * tqs, hi * dh:(hi + 1) * dh] = o[hi * tqs:(hi + 1) * tqs].astype(o_ref.dtype)

    for gi in range(n_groups):
        scores(0, gi)
    for j in range(len(chunks)):
        for gi in range(n_groups):
            if j + 1 < len(chunks):
                scores(j + 1, gi)
            softmax(j, gi)
            weighted_values(j, gi)
    for gi in range(n_groups):
        finish(gi)


def _attention(q, k, v, kc, vc, tq, n_groups, tk, rb=8):
    b, _, l, dh = q.shape
    lc = kc.shape[1]
    rows = Q_PER_KV * tq // n_groups
    gdh = Q_PER_KV * dh
    group_scratch = [pltpu.VMEM((rows, tk), F32), pltpu.VMEM((rows, tk), F32),
                     pltpu.VMEM((rows, tk), BF16), pltpu.VMEM((rows, tk), BF16),
                     pltpu.VMEM((rows, 1), F32), pltpu.VMEM((rows, 1), F32),
                     pltpu.VMEM((rows, 2 * dh), F32)]
    return pl.pallas_call(
        functools.partial(_attn_kernel, tk=tk, rb=rb, n_groups=n_groups),
        grid=(b, N_KV_HEADS, l // tq),
        in_specs=[pl.BlockSpec((1, Q_PER_KV, tq, dh), lambda bi, kh, qi: (bi, kh, qi, 0)),
                  pl.BlockSpec((1, l, dh), lambda bi, kh, qi: (bi, 0, kh)),
                  pl.BlockSpec((1, l, dh), lambda bi, kh, qi: (bi, 0, kh)),
                  pl.BlockSpec((1, lc, dh), lambda bi, kh, qi: (bi, 0, kh)),
                  pl.BlockSpec((1, lc, dh), lambda bi, kh, qi: (bi, 0, kh))],
        out_specs=pl.BlockSpec((1, tq, gdh), lambda bi, kh, qi: (bi, qi, kh)),
        out_shape=jax.ShapeDtypeStruct((b, l, ATTN_WIDTH), BF16),
        scratch_shapes=group_scratch * n_groups,
        compiler_params=_params("parallel", "parallel", "arbitrary"),
        name="attention",
    )(q, k, v, kc, vc)


def _conv_kernel(prev_ref, cur_ref, next_ref, w_ref, b_ref, g_ref, beta_ref, o_ref, win_ref, y_ref, *, tc, rc):
    li = pl.program_id(1)
    tl, c = cur_ref.shape[1:]
    halo = prev_ref.shape[1]
    nt = c // LANES

    def put_tokens(vals, tok0):
        for j in range(nt):
            win_ref[pl.ds(tok0 * nt + j, vals.shape[0], stride=nt), :] = vals[:, j * LANES:(j + 1) * LANES]

    def put_chunk(ci, carry):
        r0 = pl.multiple_of(ci * rc, rc)
        put_tokens(cur_ref[0, pl.ds(r0, rc), :].astype(F32), halo + r0)
        return carry

    put_tokens(jnp.where(li > 0, prev_ref[0].astype(F32), 0.0), 0)
    lax.fori_loop(0, tl // rc, put_chunk, 0)
    put_tokens(jnp.where(li < pl.num_programs(1) - 1, next_ref[0].astype(F32), 0.0), halo + tl)

    first = halo - CONV_TAPS // 2
    bias = b_ref[...][None]

    def token_chunk(ci, carry):
        tok = ci * tc
        acc = jnp.zeros((tc, nt, LANES), F32) + bias
        for t in range(CONV_TAPS):
            r0 = pl.multiple_of((tok + first + t) * nt, nt)
            acc = acc + win_ref[pl.ds(r0, tc * nt), :].reshape(tc, nt, LANES) * w_ref[t][None]
        y_ref[pl.ds(pl.multiple_of(tok * nt, nt), tc * nt), :] = acc.reshape(tc * nt, LANES)
        return carry

    lax.fori_loop(0, tl // tc, token_chunk, 0)

    def norm_chunk(ci, carry):
        r0 = pl.multiple_of(ci * rc, rc)
        y = jnp.concatenate([y_ref[pl.ds(r0 * nt + j, rc, stride=nt), :] for j in range(nt)], axis=1)
        mu = jnp.mean(y, axis=-1, keepdims=True)
        yc = y - mu
        var = jnp.mean(yc * yc, axis=-1, keepdims=True)
        z = yc * lax.rsqrt(var + EPS) * g_ref[...] + beta_ref[...]
        o_ref[0, pl.ds(r0, rc), :] = _silu(z).astype(o_ref.dtype)
        return carry

    lax.fori_loop(0, tl // rc, norm_chunk, 0, unroll=2)


def _conv_module(u, w_dw, b_dw, ln_g, ln_b, tl, tc=16, rc=32):
    b, l, c = u.shape
    nt = c // LANES
    hb = tl // CONV_HALO
    n_halo = l // CONV_HALO
    return pl.pallas_call(
        functools.partial(_conv_kernel, tc=tc, rc=rc),
        grid=(b, l // tl),
        in_specs=[pl.BlockSpec((1, CONV_HALO, c), lambda bi, li: (bi, jnp.maximum(li * hb - 1, 0), 0)),
                  pl.BlockSpec((1, tl, c), lambda bi, li: (bi, li, 0)),
                  pl.BlockSpec((1, CONV_HALO, c), lambda bi, li: (bi, jnp.minimum((li + 1) * hb, n_halo - 1), 0)),
                  pl.BlockSpec((CONV_TAPS, nt, LANES), lambda bi, li: (0, 0, 0)),
                  pl.BlockSpec((nt, LANES), lambda bi, li: (0, 0)),
                  pl.BlockSpec((1, c), lambda bi, li: (0, 0)),
                  pl.BlockSpec((1, c), lambda bi, li: (0, 0))],
        out_specs=pl.BlockSpec((1, tl, c), lambda bi, li: (bi, li, 0)),
        out_shape=jax.ShapeDtypeStruct((b, l, c), BF16),
        scratch_shapes=[pltpu.VMEM(((tl + 2 * CONV_HALO) * nt, LANES), F32), pltpu.VMEM((tl * nt, LANES), F32)],
        compiler_params=_params("parallel", "arbitrary"),
        name="conv_module",
    )(u, u, u, w_dw, b_dw, ln_g, ln_b)


def _merge_kernel(h_ref, a_ref, c_ref, wga_ref, wgc_ref, wa_ref, wc_ref, o_ref):
    h = h_ref[...]
    g_a = _sigmoid(jnp.dot(h, wga_ref[...], preferred_element_type=F32))
    g_c = _sigmoid(jnp.dot(h, wgc_ref[...], preferred_element_type=F32))
    a = jnp.dot(a_ref[...], wa_ref[...].astype(BF16), preferred_element_type=F32)
    cb = jnp.dot(c_ref[...], wc_ref[...].astype(BF16), preferred_element_type=F32)
    o_ref[...] = (g_a * a + g_c * cb).astype(o_ref.dtype)


def _merge(h2d, attn, conv, w_in, gate_off, wa, wc, tm=512, tn=512):
    m, d = h2d.shape
    ka = attn.shape[1]
    kc = conv.shape[1]
    nj = d // tn
    ja = gate_off // tn
    jc = (gate_off + d) // tn
    return pl.pallas_call(
        _merge_kernel,
        grid=(nj, m // tm),
        in_specs=[pl.BlockSpec((tm, d), lambda j, i: (i, 0)),
                  pl.BlockSpec((tm, ka), lambda j, i: (i, 0)),
                  pl.BlockSpec((tm, kc), lambda j, i: (i, 0)),
                  pl.BlockSpec((d, tn), lambda j, i: (0, ja + j)),
                  pl.BlockSpec((d, tn), lambda j, i: (0, jc + j)),
                  pl.BlockSpec((ka, tn), lambda j, i: (0, j)),
                  pl.BlockSpec((kc, tn), lambda j, i: (0, j))],
        out_specs=pl.BlockSpec((tm, tn), lambda j, i: (i, j)),
        out_shape=jax.ShapeDtypeStruct((m, d), BF16),
        compiler_params=_params("parallel", "arbitrary"),
        name="merge_branches",
    )(h2d, attn, conv, w_in, w_in, wa, wc)


def _out_proj_kernel(m_ref, w_ref, x_ref, ga_ref, o_ref):
    acc = jnp.dot(m_ref[0], w_ref[...], preferred_element_type=F32)
    o_ref[0] = x_ref[0] + ga_ref[0] * acc


def _out_proj(mrg, w, x, gate, tm=1024, tn=1024):
    b, l, d = x.shape
    return pl.pallas_call(
        _out_proj_kernel,
        grid=(b, l // tm, d // tn),
        in_specs=[pl.BlockSpec((1, tm, d), lambda bi, i, j: (bi, i, 0)),
                  pl.BlockSpec((d, tn), lambda bi, i, j: (0, j)),
                  pl.BlockSpec((1, tm, tn), lambda bi, i, j: (bi, i, j)),
                  pl.BlockSpec((1, 1, tn), lambda bi, i, j: (bi, 0, j))],
        out_specs=pl.BlockSpec((1, tm, tn), lambda bi, i, j: (bi, i, j)),
        out_shape=jax.ShapeDtypeStruct((b, l, d), F32),
        compiler_params=_params("parallel", "parallel", "arbitrary"),
        name="out_proj_residual",
    )(mrg, w, x, gate)


def _pack_halves(y):
    n = y.shape[1] // 2
    return pltpu.pack_elementwise([y[:, :n], y[:, n:]], packed_dtype=BF16)


def _unpack_halves(p):
    lo = pltpu.unpack_elementwise(p, index=0, packed_dtype=BF16, unpacked_dtype=F32)
    hi = pltpu.unpack_elementwise(p, index=1, packed_dtype=BF16, unpacked_dtype=F32)
    return lo, hi


def _norm2_router_kernel(x_ref, g_ref, sh_ref, sc_ref, wr_ref, br_ref, hp_ref, lg_ref):
    y = _rms(x_ref[0], g_ref[...]) * (1.0 + sc_ref[0]) + sh_ref[0]
    packed = _pack_halves(y)
    tl = packed.shape[0]
    nt = packed.shape[1] // LANES
    for j in range(nt):
        hp_ref[pl.ds(j, tl, stride=nt), :] = packed[:, j * LANES:(j + 1) * LANES]
    lg_ref[0] = jnp.dot(y.astype(BF16), wr_ref[...], preferred_element_type=F32) + br_ref[...]


def _norm2_router(x, g, shift, scale, w_r, b_r, tl):
    b, l, d = x.shape
    nt = d // 2 // LANES
    lb = l // tl
    return pl.pallas_call(
        _norm2_router_kernel,
        grid=(b, lb),
        in_specs=[pl.BlockSpec((1, tl, d), lambda bi, li: (bi, li, 0)),
                  pl.BlockSpec((1, d), lambda bi, li: (0, 0)),
                  pl.BlockSpec((1, 1, d), lambda bi, li: (bi, 0, 0)),
                  pl.BlockSpec((1, 1, d), lambda bi, li: (bi, 0, 0)),
                  pl.BlockSpec((d, ROUTER_LANES), lambda bi, li: (0, 0)),
                  pl.BlockSpec((1, ROUTER_LANES), lambda bi, li: (0, 0))],
        out_specs=[pl.BlockSpec((tl * nt, LANES), lambda bi, li: (bi * lb + li, 0)),
                   pl.BlockSpec((1, tl, ROUTER_LANES), lambda bi, li: (bi, li, 0))],
        out_shape=[jax.ShapeDtypeStruct((b * l * nt, LANES), jnp.int32),
                   jax.ShapeDtypeStruct((b, l, ROUTER_LANES), F32)],
        compiler_params=_params("parallel", "parallel"),
        name="norm2_router",
    )(x, g, shift, scale, w_r, b_r)


def _first_lane(mask, lane):
    return jnp.min(jnp.where(mask, lane, LANES), axis=-1, keepdims=True)


def _route_kernel(lg_ref, meta_ref, wts_ref, cnt_ref, carry_ref):
    @pl.when(pl.program_id(0) == 0)
    def _():
        carry_ref[...] = jnp.zeros_like(carry_ref)

    lg = lg_ref[...]
    tb = lg.shape[0]
    lane = lax.broadcasted_iota(jnp.int32, lg.shape, 1)
    neg_inf = jnp.float32(-jnp.inf)
    is_group = lane < N_GROUPS
    gl = jnp.where(is_group, lg, neg_inf)
    g_max = jnp.max(gl, axis=-1, keepdims=True)
    g_sel = _first_lane(gl == g_max, lane)
    p_g = 1.0 / jnp.sum(jnp.where(is_group, jnp.exp(lg - g_max), 0.0), axis=-1, keepdims=True)

    e_idx = lane - N_GROUPS
    in_group = (e_idx >= g_sel * EXPERTS_PER_GROUP) & (e_idx < (g_sel + 1) * EXPERTS_PER_GROUP)
    ev = jnp.where(in_group, lg, neg_inf)
    v1 = jnp.max(ev, axis=-1, keepdims=True)
    i1 = _first_lane(ev == v1, lane)
    ev2 = jnp.where(lane == i1, neg_inf, ev)
    v2 = jnp.max(ev2, axis=-1, keepdims=True)
    i2 = _first_lane(ev2 == v2, lane)
    t = jnp.exp(v2 - v1)
    w1 = p_g / (1.0 + t)
    w2 = w1 * t

    oh1 = lane == i1
    oh2 = lane == i2
    oh = (oh1 | oh2).astype(BF16)
    earlier = (lax.broadcasted_iota(jnp.int32, (tb, tb), 0) > lax.broadcasted_iota(jnp.int32, (tb, tb), 1)).astype(BF16)
    before = jnp.dot(earlier, oh, preferred_element_type=F32) + carry_ref[...]
    r1 = jnp.sum(jnp.where(oh1, before, 0.0), axis=-1, keepdims=True).astype(jnp.int32)
    r2 = jnp.sum(jnp.where(oh2, before, 0.0), axis=-1, keepdims=True).astype(jnp.int32)
    carry_ref[...] += jnp.sum(oh.astype(F32), axis=0, keepdims=True)

    meta_ref[...] = jnp.where(lane == 0, i1 - N_GROUPS, jnp.where(lane == 1, i2 - N_GROUPS,
                              jnp.where(lane == 2, r1, jnp.where(lane == 3, r2, 0))))
    wts_ref[...] = jnp.where(lane == 0, w1, jnp.where(lane == 1, w2, 0.0))
    cnt_ref[...] = carry_ref[...]


def _route(logits, tb=1024):
    n = logits.shape[0]
    return pl.pallas_call(
        _route_kernel,
        grid=(n // tb,),
        in_specs=[pl.BlockSpec((tb, LANES), lambda i: (i, 0))],
        out_specs=[pl.BlockSpec((tb, LANES), lambda i: (i, 0)),
                   pl.BlockSpec((tb, LANES), lambda i: (i, 0)),
                   pl.BlockSpec((1, LANES), lambda i: (0, 0))],
        out_shape=[jax.ShapeDtypeStruct((n, LANES), jnp.int32),
                   jax.ShapeDtypeStruct((n, LANES), F32),
                   jax.ShapeDtypeStruct((1, LANES), F32)],
        scratch_shapes=[pltpu.VMEM((1, LANES), F32)],
        compiler_params=_params("arbitrary"),
        name="moe_route",
    )(logits)


def _dest_kernel(meta_ref, pst_ref, o_ref):
    meta = meta_ref[...]
    lane = lax.broadcasted_iota(jnp.int32, meta.shape, 1)
    pst = pst_ref[...]

    def row_of(slot):
        e = meta[:, slot:slot + 1]
        start = jnp.sum(jnp.where(lane == e + N_GROUPS, pst, 0), axis=-1, keepdims=True)
        return start + meta[:, TOP_K + slot:TOP_K + slot + 1]

    o_ref[...] = jnp.where(lane == 0, row_of(0), jnp.where(lane == 1, row_of(1), 0))


def _dest_rows(meta, pst, tb=2048):
    n = meta.shape[0]
    return pl.pallas_call(
        _dest_kernel,
        grid=(n // tb,),
        in_specs=[pl.BlockSpec((tb, LANES), lambda i: (i, 0)),
                  pl.BlockSpec((1, LANES), lambda i: (0, 0))],
        out_specs=pl.BlockSpec((tb, LANES), lambda i: (i, 0)),
        out_shape=jax.ShapeDtypeStruct((n, LANES), jnp.int32),
        compiler_params=_params("parallel"),
        name="moe_dest_rows",
    )(meta, pst)


def _zero_tail_kernel(lb_ref, o_ref):
    o_ref[...] = jnp.zeros_like(o_ref)


def _zero_tails(last_blk, total, tmb, nt):
    return pl.pallas_call(
        _zero_tail_kernel,
        grid_spec=pltpu.PrefetchScalarGridSpec(
            num_scalar_prefetch=1,
            grid=(last_blk.shape[0],),
            in_specs=[],
            out_specs=pl.BlockSpec((tmb * nt, LANES), lambda e, lb: (lb[e], 0))),
        out_shape=jax.ShapeDtypeStruct((total * nt, LANES), jnp.int32),
        compiler_params=_params("arbitrary"),
        name="moe_zero_tails",
    )(last_blk)


def _dispatch_kernel(dest_ref, hp_ref, xs_in_ref, xs_ref, sem):
    rows = hp_ref.shape[0]

    def slab_copy(r, k):
        return pltpu.make_async_copy(hp_ref.at[r], xs_ref.at[dest_ref[0, 0, r * TOP_K + k]], sem)

    def start(r, c):
        for k in range(TOP_K):
            slab_copy(r, k).start(priority=k % 2)
        return c

    def wait(r, c):
        for k in range(TOP_K):
            slab_copy(r, k).wait()
        return c

    lax.fori_loop(0, rows, start, 0, unroll=DMA_LOOP_UNROLL)
    lax.fori_loop(0, rows, wait, 0, unroll=DMA_LOOP_UNROLL)


def _dispatch(hp3, dest, xs0, rows):
    n, nt, _ = hp3.shape
    steps = n // rows
    return pl.pallas_call(
        _dispatch_kernel,
        grid=(steps,),
        in_specs=[pl.BlockSpec((1, 1, rows * TOP_K), lambda i: (i, 0, 0), memory_space=pltpu.SMEM),
                  pl.BlockSpec((rows, nt, LANES), lambda i: (i, 0, 0)),
                  pl.BlockSpec(memory_space=pl.ANY)],
        out_specs=pl.BlockSpec(memory_space=pl.ANY),
        out_shape=jax.ShapeDtypeStruct(xs0.shape, xs0.dtype),
        scratch_shapes=[pltpu.SemaphoreType.DMA(())],
        input_output_aliases={2: 0},
        compiler_params=_params("arbitrary"),
        name="moe_dispatch",
    )(dest.reshape(steps, 1, rows * TOP_K), hp3, xs0)


def _by_valid_rows(nv, tmb, compute, o_ref):
    quarter = tmb // 4
    per_row = o_ref.shape[0] // tmb

    for nq in range(1, 5):
        rows = nq * quarter

        @pl.when((nv > rows - quarter) & (nv <= rows))
        def _(rows=rows):
            compute(rows)
            if rows < tmb:
                o_ref[rows * per_row:, :] = jnp.zeros(((tmb - rows) * per_row, o_ref.shape[1]), o_ref.dtype)

    @pl.when(nv == 0)
    def _():
        o_ref[...] = jnp.zeros_like(o_ref)


def _expert_weights(plan, w_hbm, wbuf, sem):
    be_ref, first_ref, slot_ref, nxt_ref = plan
    bi = pl.program_id(0)
    slot = slot_ref[bi]

    def copy(e, s):
        return pltpu.make_async_copy(w_hbm.at[e], wbuf.at[s], sem.at[s])

    @pl.when(bi == 0)
    def _():
        copy(be_ref[0], 0).start()

    @pl.when((first_ref[bi] == 1) & (nxt_ref[bi] >= 0))
    def _():
        copy(nxt_ref[bi], 1 - slot).start()

    @pl.when(first_ref[bi] == 1)
    def _():
        copy(be_ref[bi], slot).wait()

    return wbuf.at[slot]


def _expert_in_kernel(be_ref, first_ref, slot_ref, nxt_ref, nu_ref, nv_ref, x_ref, w_hbm, *rest, nt, tn):
    gate_ref = rest[0] if len(rest) == 4 else None
    o_ref, wbuf, sem = rest[-3:]
    w_ref = _expert_weights((be_ref, first_ref, slot_ref, nxt_ref), w_hbm, wbuf, sem)

    def compute(rows):
        halves = [_unpack_halves(x_ref[pl.ds(j, rows, stride=nt), :]) for j in range(nt)]
        lo = jnp.concatenate([h[0].astype(BF16) for h in halves], axis=1)
        hi = jnp.concatenate([h[1].astype(BF16) for h in halves], axis=1)
        half = nt * LANES
        for c0 in range(0, o_ref.shape[1], tn):
            y = (jnp.dot(lo, w_ref[:half, c0:c0 + tn].astype(BF16), preferred_element_type=F32)
                 + jnp.dot(hi, w_ref[half:, c0:c0 + tn].astype(BF16), preferred_element_type=F32))
            if gate_ref is None:
                y = _silu(y)
            else:
                y = gate_ref[:rows, c0:c0 + tn].astype(F32) * y
            o_ref[:rows, c0:c0 + tn] = y.astype(o_ref.dtype)

    _by_valid_rows(nv_ref[pl.program_id(0)], o_ref.shape[0], compute, o_ref)


N_PLAN = 6


def _blk_clamped(bi, *plan):
    return jnp.minimum(bi, plan[4][0] - 1)


def _expert_in(xs, w, gate, plan, tmb, name, tn=256):
    _, d, ff = w.shape
    nt = d // 2 // LANES
    total = xs.shape[0] // nt
    nblk = total // tmb
    in_specs = [pl.BlockSpec((tmb * nt, LANES), lambda bi, *p: (_blk_clamped(bi, *p), 0)),
                pl.BlockSpec(memory_space=pl.ANY)]
    args = [xs, w]
    if gate is not None:
        in_specs.append(pl.BlockSpec((tmb, ff), lambda bi, *p: (_blk_clamped(bi, *p), 0)))
        args.append(gate)
    return pl.pallas_call(
        functools.partial(_expert_in_kernel, nt=nt, tn=tn),
        grid_spec=pltpu.PrefetchScalarGridSpec(
            num_scalar_prefetch=N_PLAN,
            grid=(nblk,),
            in_specs=in_specs,
            out_specs=pl.BlockSpec((tmb, ff), lambda bi, *p: (bi, 0)),
            scratch_shapes=[pltpu.VMEM((2, d, ff), w.dtype), pltpu.SemaphoreType.DMA((2,))]),
        out_shape=jax.ShapeDtypeStruct((total, ff), BF16),
        compiler_params=_params("arbitrary"),
        name=name,
    )(*plan, *args)


def _expert_down_kernel(be_ref, first_ref, slot_ref, nxt_ref, nu_ref, nv_ref, a_ref, w_hbm, o_ref, wbuf, sem,
                        *, nt, tn):
    tmb = a_ref.shape[0]
    w_ref = _expert_weights((be_ref, first_ref, slot_ref, nxt_ref), w_hbm, wbuf, sem)

    def compute(rows):
        a = a_ref[:rows, :]
        half = nt * LANES
        for c0 in range(0, half, tn):
            ylo = jnp.dot(a, w_ref[:, c0:c0 + tn].astype(BF16), preferred_element_type=F32)
            yhi = jnp.dot(a, w_ref[:, half + c0:half + c0 + tn].astype(BF16), preferred_element_type=F32)
            packed = pltpu.pack_elementwise([ylo, yhi], packed_dtype=BF16)
            for j in range(tn // LANES):
                o_ref[pl.ds(c0 // LANES + j, rows, stride=nt), :] = packed[:, j * LANES:(j + 1) * LANES]

    _by_valid_rows(nv_ref[pl.program_id(0)], tmb, compute, o_ref)


def _expert_down(act, w_down, plan, tmb, tn=512):
    total, ff = act.shape
    d = w_down.shape[2]
    nt = d // 2 // LANES
    nblk = total // tmb
    return pl.pallas_call(
        functools.partial(_expert_down_kernel, nt=nt, tn=tn),
        grid_spec=pltpu.PrefetchScalarGridSpec(
            num_scalar_prefetch=N_PLAN,
            grid=(nblk,),
            in_specs=[pl.BlockSpec((tmb, ff), lambda bi, *p: (_blk_clamped(bi, *p), 0)),
                      pl.BlockSpec(memory_space=pl.ANY)],
            out_specs=pl.BlockSpec((tmb * nt, LANES), lambda bi, *p: (bi, 0)),
            scratch_shapes=[pltpu.VMEM((2, ff, d), w_down.dtype), pltpu.SemaphoreType.DMA((2,))]),
        out_shape=jax.ShapeDtypeStruct((total * nt, LANES), jnp.int32),
        compiler_params=_params("arbitrary"),
        name="expert_down",
    )(*plan, act, w_down)


def _combine_kernel(pos_ref, nxt_ref, ys_ref, w_ref, x_ref, ga_ref, g_ref, o_ref, *scratch):
    n_buf = 2 * TOP_K
    slabs, sems = scratch[:n_buf], scratch[n_buf:]
    rows = x_ref.shape[1]
    hr = rows // 2
    nt = ys_ref.shape[1]
    half = nt * LANES
    step = pl.program_id(0) * pl.num_programs(1) + pl.program_id(1)
    n_steps = pl.num_programs(0) * pl.num_programs(1)

    def slab_copy(idx_ref, h, r, k):
        dst = slabs[h * TOP_K + k].at[pl.ds(pl.multiple_of(r * SLAB_PITCH, SUBLANES), nt)]
        return pltpu.make_async_copy(ys_ref.at[idx_ref[0, 0, (h * hr + r) * TOP_K + k]], dst, sems[h * TOP_K + k])

    def issue(idx_ref, h):
        def body(r, c):
            for k in range(TOP_K):
                slab_copy(idx_ref, h, r, k).start(priority=k % 2)
            return c
        lax.fori_loop(0, hr, body, 0, unroll=DMA_LOOP_UNROLL)

    def wait(idx_ref, h):
        def body(r, c):
            for k in range(TOP_K):
                slab_copy(idx_ref, h, r, k).wait()
            return c
        lax.fori_loop(0, hr, body, 0, unroll=DMA_LOOP_UNROLL)

    def compute(h):
        r0 = h * hr
        w0 = w_ref[r0:r0 + hr, 0:1]
        w1 = w_ref[r0:r0 + hr, 1:2]
        ss = jnp.zeros((hr, 1), F32)
        for j in range(nt):
            lo0, hi0 = _unpack_halves(slabs[h * TOP_K][pl.ds(j, hr, stride=SLAB_PITCH), :])
            lo1, hi1 = _unpack_halves(slabs[h * TOP_K + 1][pl.ds(j, hr, stride=SLAB_PITCH), :])
            for c0, y in ((j * LANES, w0 * lo0 + w1 * lo1), (half + j * LANES, w0 * hi0 + w1 * hi1)):
                z = x_ref[0, r0:r0 + hr, c0:c0 + LANES] + ga_ref[0, :, c0:c0 + LANES] * y
                ss = ss + jnp.sum(z * z, axis=-1, keepdims=True)
                o_ref[0, r0:r0 + hr, c0:c0 + LANES] = z
        inv = lax.rsqrt(ss / (2 * half) + EPS)
        o_ref[0, r0:r0 + hr, :] = o_ref[0, r0:r0 + hr, :] * inv * g_ref[...]

    @pl.when(step == 0)
    def _():
        issue(pos_ref, 0)

    issue(pos_ref, 1)
    wait(pos_ref, 0)
    compute(0)

    @pl.when(step + 1 < n_steps)
    def _():
        issue(nxt_ref, 0)

    wait(pos_ref, 1)
    compute(1)


def _combine(ys3, pos, wts, x, gate, g, rows):
    b, l, d = x.shape
    lb = l // rows
    n_steps = b * lb
    pos3 = pos.reshape(n_steps, 1, rows * TOP_K)
    slab = pltpu.VMEM((rows // 2 * SLAB_PITCH, LANES), ys3.dtype)
    return pl.pallas_call(
        _combine_kernel,
        grid=(b, lb),
        in_specs=[pl.BlockSpec((1, 1, rows * TOP_K), lambda bi, i: (bi * lb + i, 0, 0), memory_space=pltpu.SMEM),
                  pl.BlockSpec((1, 1, rows * TOP_K), lambda bi, i: (jnp.minimum(bi * lb + i + 1, n_steps - 1), 0, 0),
                               memory_space=pltpu.SMEM),
                  pl.BlockSpec(memory_space=pl.ANY),
                  pl.BlockSpec((rows, LANES), lambda bi, i: (bi * lb + i, 0)),
                  pl.BlockSpec((1, rows, d), lambda bi, i: (bi, i, 0)),
                  pl.BlockSpec((1, 1, d), lambda bi, i: (bi, 0, 0)),
                  pl.BlockSpec((1, d), lambda bi, i: (0, 0))],
        out_specs=pl.BlockSpec((1, rows, d), lambda bi, i: (bi, i, 0)),
        out_shape=jax.ShapeDtypeStruct((b, l, d), F32),
        scratch_shapes=[slab] * (2 * TOP_K) + [pltpu.SemaphoreType.DMA(())] * (2 * TOP_K),
        compiler_params=_params("arbitrary", "arbitrary"),
        name="moe_combine_norm",
    )(pos3, pos3, ys3, wts, x, gate, g)


def _block_layout(counts, n_pairs, tmb):
    nblk = (n_pairs + N_EXPERTS * (tmb - 1) + tmb - 1) // tmb
    blocks = (counts + tmb - 1) // tmb
    bend = jnp.cumsum(blocks)
    bstart = bend - blocks
    pstart = bstart * tmb
    n_used = bend[-1]
    ids = jnp.arange(nblk, dtype=jnp.int32)
    blk_e = jnp.minimum(jnp.searchsorted(bend, jnp.minimum(ids, n_used - 1), side="right"),
                        N_EXPERTS - 1).astype(jnp.int32)
    last_blk = jnp.maximum(bend - 1, 0).astype(jnp.int32)
    used = ids < n_used
    n_valid = jnp.where(used, jnp.clip(counts[blk_e] - (ids - bstart[blk_e]) * tmb, 0, tmb), 0)
    first = used & ((ids == 0) | (blk_e != jnp.roll(blk_e, 1)))
    slot = (jnp.cumsum(first.astype(jnp.int32)) - 1) % 2
    experts = jnp.arange(N_EXPERTS, dtype=jnp.int32)
    later = jnp.where((blocks > 0)[None, :] & (experts[None, :] > experts[:, None]), experts[None, :], N_EXPERTS)
    nxt_e = jnp.min(later, axis=1)
    nxt = jnp.where(nxt_e[blk_e] < N_EXPERTS, nxt_e[blk_e], -1)
    i32 = lambda a: a.astype(jnp.int32)
    plan = (blk_e, i32(first), i32(slot), i32(nxt), i32(n_used).reshape(1), i32(n_valid))
    return nblk, i32(pstart), last_blk, plan


def _rope_tables(n_tokens):
    rows = n_tokens // GRID_W
    row, col = jnp.meshgrid(jnp.arange(rows), jnp.arange(GRID_W), indexing="ij")
    pos = jnp.stack([row.reshape(-1), col.reshape(-1)], axis=-1).astype(F32)
    inv = ROPE_THETA ** (-jnp.arange(0, ROPE_AXIS_DIM, 2, dtype=F32) / ROPE_AXIS_DIM)
    ang = pos[:, :, None] * inv[None, None, :]
    cos, sin = jnp.cos(ang), jnp.sin(ang)
    cos_t = jnp.concatenate([cos[:, 0], cos[:, 0], cos[:, 1], cos[:, 1]], axis=-1)
    sin_t = jnp.concatenate([-sin[:, 0], sin[:, 0], -sin[:, 1], sin[:, 1]], axis=-1)
    return cos_t, sin_t


def kernel(x, c, ctx, c_ctx, norm1_g, w_mod, b_mod, w_in, q_norm_g, k_norm_g, w_attn_out, conv_dw_w, conv_dw_b, conv_ln_g, conv_ln_b, w_conv_out, w_out, norm2_g, w_router_group, b_router_group, w_router_expert, b_router_expert, w_exp_gate, w_exp_up, w_exp_down, norm_f_g):
    b, s, d = x.shape
    n_ctx = ctx.shape[1]
    assert w_in.shape[0] == 1, "single-layer stack"
    conv_width = conv_dw_w.shape[-1]
    k_off = ATTN_WIDTH
    glu_off = k_off + 2 * KV_WIDTH
    gate_off = glu_off + 2 * conv_width

    n_c = b + 1
    cvec = jnp.zeros((SUBLANES * ((n_c + SUBLANES - 1) // SUBLANES), d), F32).at[:b].set(c).at[b].set(c_ctx)
    mod = _mod_vectors(cvec, w_mod[0], b_mod.reshape(1, -1))
    sh1, sc1, ga1, sh2, sc2, ga2 = [mod[:b, i * d:(i + 1) * d].reshape(b, 1, d) for i in range(N_MOD)]
    csh1, csc1 = [mod[b:b + 1, i * d:(i + 1) * d].reshape(1, 1, d) for i in range(2)]

    g1 = norm1_g.reshape(1, d)
    h = _norm_mod(x, g1, sh1, sc1, tl=512)
    hc = _norm_mod(ctx, g1, csh1, csc1, tl=n_ctx)
    w_in_b = w_in[0].astype(BF16)
    cos_t, sin_t = _rope_tables(s)
    qg = q_norm_g.reshape(1, HEAD_DIM)
    kg = k_norm_g.reshape(1, HEAD_DIM)
    q = _q_proj(h, w_in_b, qg, cos_t, sin_t, 0, HEAD_DIM ** -0.5 * LOG2E, tm=1024)
    k, v = _kv_proj(h, w_in_b, kg, cos_t, sin_t, k_off, tm=1024)
    kc, vc = _kv_proj(hc, w_in_b, kg, None, None, k_off, tm=n_ctx)
    attn = _attention(q, k, v, kc, vc, tq=128, n_groups=1, tk=1024)

    h2d = h.reshape(b * s, d)
    u = _glu_proj(h2d, w_in_b, glu_off, conv_width, tm=1024)
    conv = _conv_module(u.reshape(b, s, conv_width), conv_dw_w.reshape(CONV_TAPS, conv_width // LANES, LANES),
                        conv_dw_b.reshape(conv_width // LANES, LANES), conv_ln_g.reshape(1, -1),
                        conv_ln_b.reshape(1, -1), tl=256)
    mrg = _merge(h2d, attn.reshape(b * s, ATTN_WIDTH), conv.reshape(b * s, conv_width), w_in_b, gate_off,
                 w_attn_out[0], w_conv_out[0])
    x1 = _out_proj(mrg.reshape(b, s, d), w_out[0].astype(BF16), x, ga1)

    w_r = jnp.zeros((d, ROUTER_LANES), F32).at[:, :N_GROUPS].set(w_router_group[0]) \
        .at[:, N_GROUPS:N_GROUPS + N_EXPERTS].set(w_router_expert[0])
    b_r = jnp.zeros((1, ROUTER_LANES), F32).at[0, :N_GROUPS].set(b_router_group[0]) \
        .at[0, N_GROUPS:N_GROUPS + N_EXPERTS].set(b_router_expert[0])
    hp, logits = _norm2_router(x1, norm2_g.reshape(1, d), sh2, sc2, w_r.astype(BF16), b_r, tl=512)
    n = b * s
    nt = d // 2 // LANES
    tmb = 512
    meta, wts, cnt = _route(logits.reshape(n, ROUTER_LANES))
    counts = cnt[0, N_GROUPS:N_GROUPS + N_EXPERTS].astype(jnp.int32)
    nblk, pstart, last_blk, plan = _block_layout(counts, n * TOP_K, tmb)
    pst = jnp.zeros((1, LANES), jnp.int32).at[0, N_GROUPS:N_GROUPS + N_EXPERTS].set(pstart)
    dest = _dest_rows(meta, pst)[:, :TOP_K].reshape(-1)
    xs0 = _zero_tails(last_blk, nblk * tmb, tmb, nt)
    xs = _dispatch(hp.reshape(n, nt, LANES), dest, xs0.reshape(nblk * tmb, nt, LANES), rows=1024)
    xs2 = xs.reshape(nblk * tmb * nt, LANES)
    sg = _expert_in(xs2, w_exp_gate[0], None, plan, tmb, "expert_gate")
    act = _expert_in(xs2, w_exp_up[0], sg, plan, tmb, "expert_up")
    ys = _expert_down(act, w_exp_down[0], plan, tmb)
    return _combine(ys.reshape(nblk * tmb, nt, LANES), dest, wts, x1, ga2, norm_f_g.reshape(1, d), rows=512)
```

```python
import functools

import jax
import jax.numpy as jnp
from jax import lax
from jax.experimental import pallas as pl
from jax.experimental.pallas import tpu as pltpu

F32 = jnp.float32
BF16 = jnp.bfloat16

GRID_W = 64
HEAD_DIM = 128
N_Q_HEADS = 16
N_KV_HEADS = 4
Q_PER_KV = N_Q_HEADS // N_KV_HEADS
ATTN_WIDTH = N_Q_HEADS * HEAD_DIM
KV_WIDTH = N_KV_HEADS * HEAD_DIM
CONV_TAPS = 31
CONV_HALO = 16
ROPE_THETA = 10000.0
ROPE_AXIS_DIM = HEAD_DIM // 2
N_GROUPS = 4
EXPERTS_PER_GROUP = 8
N_EXPERTS = N_GROUPS * EXPERTS_PER_GROUP
TOP_K = 2
N_MOD = 6
EPS = 1e-6
LOG2E = 1.4426950408889634
LANES = 128
SUBLANES = 8
ROUTER_LANES = LANES
SLAB_PITCH = 24
DMA_LOOP_UNROLL = 4
EXPERT_BLOCK_PARTS = 4

V7X_VMEM_LIMIT = 56 * 1024 * 1024


def _params(*sem):
    return pltpu.CompilerParams(dimension_semantics=sem, vmem_limit_bytes=V7X_VMEM_LIMIT)


def _sigmoid(x):
    return 1.0 / (1.0 + jnp.exp(-x))


def _silu(x):
    return x * _sigmoid(x)


def _rms(x, g):
    return x * lax.rsqrt(jnp.mean(x * x, axis=-1, keepdims=True) + EPS) * g


def _mod_kernel(c_ref, w_ref, b_ref, o_ref):
    s = _silu(c_ref[...]).astype(BF16)
    o_ref[...] = jnp.dot(s, w_ref[...].astype(BF16), preferred_element_type=F32) + b_ref[...]


def _mod_vectors(cvec, w_mod, b_mod, tn=1024):
    m, d = cvec.shape
    n = w_mod.shape[1]
    return pl.pallas_call(
        _mod_kernel,
        grid=(n // tn,),
        in_specs=[pl.BlockSpec((m, d), lambda j: (0, 0)),
                  pl.BlockSpec((d, tn), lambda j: (0, j)),
                  pl.BlockSpec((1, tn), lambda j: (0, j))],
        out_specs=pl.BlockSpec((m, tn), lambda j: (0, j)),
        out_shape=jax.ShapeDtypeStruct((m, n), F32),
        compiler_params=_params("arbitrary"),
        name="mod_vectors",
    )(cvec, w_mod, b_mod)


def _norm_mod_kernel(x_ref, g_ref, sh_ref, sc_ref, o_ref):
    y = _rms(x_ref[0], g_ref[...])
    o_ref[0] = (y * (1.0 + sc_ref[0]) + sh_ref[0]).astype(o_ref.dtype)


def _norm_mod(x, g, shift, scale, tl):
    b, l, d = x.shape
    per_batch = shift.shape[0] > 1
    mod_map = (lambda bi, li: (bi, 0, 0)) if per_batch else (lambda bi, li: (0, 0, 0))
    return pl.pallas_call(
        _norm_mod_kernel,
        grid=(b, l // tl),
        in_specs=[pl.BlockSpec((1, tl, d), lambda bi, li: (bi, li, 0)),
                  pl.BlockSpec((1, d), lambda bi, li: (0, 0)),
                  pl.BlockSpec((1, 1, d), mod_map),
                  pl.BlockSpec((1, 1, d), mod_map)],
        out_specs=pl.BlockSpec((1, tl, d), lambda bi, li: (bi, li, 0)),
        out_shape=jax.ShapeDtypeStruct((b, l, d), BF16),
        compiler_params=_params("parallel", "parallel"),
        name="norm_modulate",
    )(x, g, shift, scale)


def _head_norm_rope(a, g, cos, sin):
    y = _rms(a, g)
    if cos is None:
        return y
    lane = lax.broadcasted_iota(jnp.int32, y.shape, 1)
    quarter = ROPE_AXIS_DIM // 2
    partner = jnp.where((lane % ROPE_AXIS_DIM) < quarter,
                        pltpu.roll(y, HEAD_DIM - quarter, 1), pltpu.roll(y, quarter, 1))
    return y * cos + partner * sin


def _q_proj_kernel(h_ref, w_ref, g_ref, cos_ref, sin_ref, o_ref, *, scale):
    tm = h_ref.shape[1]
    for r0 in range(0, tm, tm // 4):
        rows = slice(r0, r0 + tm // 4)
        acc = jnp.dot(h_ref[0, rows, :], w_ref[...], preferred_element_type=F32)
        for hh in range(o_ref.shape[1]):
            a = acc[:, hh * HEAD_DIM:(hh + 1) * HEAD_DIM]
            y = _head_norm_rope(a, g_ref[...], cos_ref[rows, :], sin_ref[rows, :])
            o_ref[0, hh, rows, :] = (y * scale).astype(o_ref.dtype)


def _q_proj(h, w, g, cos_t, sin_t, col_off, scale, tm, tn=1024):
    b, l, d = h.shape
    jb = col_off // tn
    hpt = tn // HEAD_DIM
    return pl.pallas_call(
        functools.partial(_q_proj_kernel, scale=scale),
        grid=(b, l // tm, ATTN_WIDTH // tn),
        in_specs=[pl.BlockSpec((1, tm, d), lambda bi, i, j: (bi, i, 0)),
                  pl.BlockSpec((d, tn), lambda bi, i, j: (0, jb + j)),
                  pl.BlockSpec((1, HEAD_DIM), lambda bi, i, j: (0, 0)),
                  pl.BlockSpec((tm, HEAD_DIM), lambda bi, i, j: (i, 0)),
                  pl.BlockSpec((tm, HEAD_DIM), lambda bi, i, j: (i, 0))],
        out_specs=pl.BlockSpec((1, hpt, tm, HEAD_DIM), lambda bi, i, j: (bi, j, i, 0)),
        out_shape=jax.ShapeDtypeStruct((b, N_Q_HEADS, l, HEAD_DIM), BF16),
        compiler_params=_params("parallel", "parallel", "arbitrary"),
        name="q_proj",
    )(h, w, g, cos_t, sin_t)


def _kv_proj_kernel(h_ref, wk_ref, wv_ref, g_ref, *rest, rope):
    if rope:
        cos_ref, sin_ref, k_ref, v_ref = rest
        cos, sin = cos_ref[...], sin_ref[...]
    else:
        k_ref, v_ref = rest
        cos = sin = None
    tm = h_ref.shape[1]
    n_groups = 2 if tm >= 512 else 1
    for r0 in range(0, tm, tm // n_groups):
        rows = slice(r0, r0 + tm // n_groups)
        h = h_ref[0, rows, :]
        acc = jnp.dot(h, wk_ref[...], preferred_element_type=F32)
        for hh in range(N_KV_HEADS):
            a = acc[:, hh * HEAD_DIM:(hh + 1) * HEAD_DIM]
            y = _head_norm_rope(a, g_ref[...], None if cos is None else cos[rows], None if sin is None else sin[rows])
            k_ref[0, rows, hh * HEAD_DIM:(hh + 1) * HEAD_DIM] = y.astype(k_ref.dtype)
        v_ref[0, rows, :] = jnp.dot(h, wv_ref[...], preferred_element_type=F32).astype(v_ref.dtype)


def _kv_proj(h, w, g, cos_t, sin_t, k_off, tm):
    b, l, d = h.shape
    rope = cos_t is not None
    jk = k_off // KV_WIDTH
    in_specs = [pl.BlockSpec((1, tm, d), lambda bi, i: (bi, i, 0)),
                pl.BlockSpec((d, KV_WIDTH), lambda bi, i: (0, jk)),
                pl.BlockSpec((d, KV_WIDTH), lambda bi, i: (0, jk + 1)),
                pl.BlockSpec((1, HEAD_DIM), lambda bi, i: (0, 0))]
    args = [h, w, w, g]
    if rope:
        in_specs += [pl.BlockSpec((tm, HEAD_DIM), lambda bi, i: (i, 0))] * 2
        args += [cos_t, sin_t]
    return pl.pallas_call(
        functools.partial(_kv_proj_kernel, rope=rope),
        grid=(b, l // tm),
        in_specs=in_specs,
        out_specs=[pl.BlockSpec((1, tm, KV_WIDTH), lambda bi, i: (bi, i, 0))] * 2,
        out_shape=[jax.ShapeDtypeStruct((b, l, KV_WIDTH), BF16)] * 2,
        compiler_params=_params("parallel", "parallel"),
        name="kv_proj_rope" if rope else "kv_proj_ctx",
    )(*args)


def _glu_proj_kernel(h_ref, wa_ref, wg_ref, o_ref):
    tm = h_ref.shape[0]
    for r0 in range(0, tm, tm // 2):
        rows = slice(r0, r0 + tm // 2)
        a = jnp.dot(h_ref[rows, :], wa_ref[...], preferred_element_type=F32)
        gt = jnp.dot(h_ref[rows, :], wg_ref[...], preferred_element_type=F32)
        o_ref[rows, :] = (a * _sigmoid(gt)).astype(o_ref.dtype)


def _glu_proj(h2d, w, col_off, width, tm, tn=512):
    m, d = h2d.shape
    ja = col_off // tn
    jg = (col_off + width) // tn
    return pl.pallas_call(
        _glu_proj_kernel,
        grid=(m // tm, width // tn),
        in_specs=[pl.BlockSpec((tm, d), lambda i, j: (i, 0)),
                  pl.BlockSpec((d, tn), lambda i, j: (0, ja + j)),
                  pl.BlockSpec((d, tn), lambda i, j: (0, jg + j))],
        out_specs=pl.BlockSpec((tm, tn), lambda i, j: (i, j)),
        out_shape=jax.ShapeDtypeStruct((m, width), BF16),
        compiler_params=_params("parallel", "arbitrary"),
        name="glu_proj",
    )(h2d, w, w)


def _attn_kernel(q_ref, k_ref, v_ref, kc_ref, vc_ref, o_ref, *scratch, tk, rb, n_groups):
    g, tq, dh = q_ref.shape[1:]
    per_group = len(scratch) // n_groups
    tqs = tq // n_groups
    rows = g * tqs
    chunks = [(k_ref, v_ref, c * tk, tk) for c in range(k_ref.shape[1] // tk)]
    chunks.append((kc_ref, vc_ref, 0, kc_ref.shape[1]))

    def bufs(gi):
        s0, s1, p0, p1, m_ref, al_ref, acc_ref = scratch[gi * per_group:(gi + 1) * per_group]
        return (s0, s1), (p0, p1), m_ref, al_ref, acc_ref

    def scores(j, gi):
        kr, _, st, n = chunks[j]
        q = q_ref[0, :, gi * tqs:(gi + 1) * tqs, :].reshape(rows, dh)
        bufs(gi)[0][j % 2][:, :n] = lax.dot_general(q, kr[0, st:st + n, :], (((1,), (1,)), ((), ())),
                                                    preferred_element_type=F32)

    def softmax(j, gi):
        n = chunks[j][3]
        s_refs, p_refs, m_ref, al_ref, _ = bufs(gi)
        s_ref, p_ref = s_refs[j % 2], p_refs[j % 2]
        for r0 in range(0, rows, rb):
            sblk = s_ref[r0:r0 + rb, :n]
            mn = jnp.max(sblk, axis=-1, keepdims=True)
            if j > 0:
                mo = m_ref[r0:r0 + rb, :]
                mn = jnp.maximum(mo, mn)
                al_ref[r0:r0 + rb, :] = jnp.exp2(mo - mn)
            m_ref[r0:r0 + rb, :] = mn
            p_ref[r0:r0 + rb, :n] = jnp.exp2(sblk - mn).astype(BF16)

    def weighted_values(j, gi):
        _, vr, st, n = chunks[j]
        _, p_refs, _, al_ref, acc_ref = bufs(gi)
        ones_col = (lax.broadcasted_iota(jnp.int32, (n, dh), 1) == 0).astype(BF16)
        v1 = jnp.concatenate([vr[0, st:st + n, :], ones_col], axis=1)
        upd = jnp.dot(p_refs[j % 2][:, :n], v1, preferred_element_type=F32)
        if j == 0:
            acc_ref[...] = upd
        else:
            acc_ref[...] = al_ref[...] * acc_ref[...] + upd

    def finish(gi):
        acc = bufs(gi)[4][...]
        o = acc[:, :dh] / acc[:, dh:dh + 1]
        for hi in range(g):
            o_ref[0, gi * tqs:(gi + 1) * tqs, hi * dh:(hi + 1) * dh] = o[hi * tqs:(hi + 1) * tqs].astype(o_ref.dtype)

    for gi in range(n_groups):
        scores(0, gi)
    for j in range(len(chunks)):
        for gi in range(n_groups):
            if j + 1 < len(chunks):
                scores(j + 1, gi)
            softmax(j, gi)
            weighted_values(j, gi)
    for gi in range(n_groups):
        finish(gi)


def _attention(q, k, v, kc, vc, tq, n_groups, tk, rb=8):
    b, _, l, dh = q.shape
    lc = kc.shape[1]
    rows = Q_PER_KV * tq // n_groups
    gdh = Q_PER_KV * dh
    group_scratch = [pltpu.VMEM((rows, tk), F32), pltpu.VMEM((rows, tk), F32),
                     pltpu.VMEM((rows, tk), BF16), pltpu.VMEM((rows, tk), BF16),
                     pltpu.VMEM((rows, 1), F32), pltpu.VMEM((rows, 1), F32),
                     pltpu.VMEM((rows, 2 * dh), F32)]
    return pl.pallas_call(
        functools.partial(_attn_kernel, tk=tk, rb=rb, n_groups=n_groups),
        grid=(b, N_KV_HEADS, l // tq),
        in_specs=[pl.BlockSpec((1, Q_PER_KV, tq, dh), lambda bi, kh, qi: (bi, kh, qi, 0)),
                  pl.BlockSpec((1, l, dh), lambda bi, kh, qi: (bi, 0, kh)),
                  pl.BlockSpec((1, l, dh), lambda bi, kh, qi: (bi, 0, kh)),
                  pl.BlockSpec((1, lc, dh), lambda bi, kh, qi: (bi, 0, kh)),
                  pl.BlockSpec((1, lc, dh), lambda bi, kh, qi: (bi, 0, kh))],
        out_specs=pl.BlockSpec((1, tq, gdh), lambda bi, kh, qi: (bi, qi, kh)),
        out_shape=jax.ShapeDtypeStruct((b, l, ATTN_WIDTH), BF16),
        scratch_shapes=group_scratch * n_groups,
        compiler_params=_params("parallel", "parallel", "arbitrary"),
        name="attention",
    )(q, k, v, kc, vc)


def _conv_kernel(prev_ref, cur_ref, next_ref, w_ref, b_ref, g_ref, beta_ref, o_ref, win_ref, y_ref, *, tc, rc):
    li = pl.program_id(1)
    tl, c = cur_ref.shape[1:]
    halo = prev_ref.shape[1]
    nt = c // LANES

    def put_tokens(vals, tok0):
        for j in range(nt):
            win_ref[pl.ds(tok0 * nt + j, vals.shape[0], stride=nt), :] = vals[:, j * LANES:(j + 1) * LANES]

    def put_chunk(ci, carry):
        r0 = pl.multiple_of(ci * rc, rc)
        put_tokens(cur_ref[0, pl.ds(r0, rc), :].astype(F32), halo + r0)
        return carry

    put_tokens(jnp.where(li > 0, prev_ref[0].astype(F32), 0.0), 0)
    lax.fori_loop(0, tl // rc, put_chunk, 0)
    put_tokens(jnp.where(li < pl.num_programs(1) - 1, next_ref[0].astype(F32), 0.0), halo + tl)

    first = halo - CONV_TAPS // 2
    bias = b_ref[...][None]

    def token_chunk(ci, carry):
        tok = ci * tc
        acc = jnp.zeros((tc, nt, LANES), F32) + bias
        for t in range(CONV_TAPS):
            r0 = pl.multiple_of((tok + first + t) * nt, nt)
            acc = acc + win_ref[pl.ds(r0, tc * nt), :].reshape(tc, nt, LANES) * w_ref[t][None]
        y_ref[pl.ds(pl.multiple_of(tok * nt, nt), tc * nt), :] = acc.reshape(tc * nt, LANES)
        return carry

    lax.fori_loop(0, tl // tc, token_chunk, 0)

    def norm_chunk(ci, carry):
        r0 = pl.multiple_of(ci * rc, rc)
        y = jnp.concatenate([y_ref[pl.ds(r0 * nt + j, rc, stride=nt), :] for j in range(nt)], axis=1)
        mu = jnp.mean(y, axis=-1, keepdims=True)
        yc = y - mu
        var = jnp.mean(yc * yc, axis=-1, keepdims=True)
        z = yc * lax.rsqrt(var + EPS) * g_ref[...] + beta_ref[...]
        o_ref[0, pl.ds(r0, rc), :] = _silu(z).astype(o_ref.dtype)
        return carry

    lax.fori_loop(0, tl // rc, norm_chunk, 0, unroll=2)


def _conv_module(u, w_dw, b_dw, ln_g, ln_b, tl, tc=16, rc=32):
    b, l, c = u.shape
    nt = c // LANES
    hb = tl // CONV_HALO
    n_halo = l // CONV_HALO
    return pl.pallas_call(
        functools.partial(_conv_kernel, tc=tc, rc=rc),
        grid=(b, l // tl),
        in_specs=[pl.BlockSpec((1, CONV_HALO, c), lambda bi, li: (bi, jnp.maximum(li * hb - 1, 0), 0)),
                  pl.BlockSpec((1, tl, c), lambda bi, li: (bi, li, 0)),
                  pl.BlockSpec((1, CONV_HALO, c), lambda bi, li: (bi, jnp.minimum((li + 1) * hb, n_halo - 1), 0)),
                  pl.BlockSpec((CONV_TAPS, nt, LANES), lambda bi, li: (0, 0, 0)),
                  pl.BlockSpec((nt, LANES), lambda bi, li: (0, 0)),
                  pl.BlockSpec((1, c), lambda bi, li: (0, 0)),
                  pl.BlockSpec((1, c), lambda bi, li: (0, 0))],
        out_specs=pl.BlockSpec((1, tl, c), lambda bi, li: (bi, li, 0)),
        out_shape=jax.ShapeDtypeStruct((b, l, c), BF16),
        scratch_shapes=[pltpu.VMEM(((tl + 2 * CONV_HALO) * nt, LANES), F32), pltpu.VMEM((tl * nt, LANES), F32)],
        compiler_params=_params("parallel", "arbitrary"),
        name="conv_module",
    )(u, u, u, w_dw, b_dw, ln_g, ln_b)


def _merge_kernel(h_ref, a_ref, c_ref, wga_ref, wgc_ref, wa_ref, wc_ref, o_ref):
    h = h_ref[...]
    g_a = _sigmoid(jnp.dot(h, wga_ref[...], preferred_element_type=F32))
    g_c = _sigmoid(jnp.dot(h, wgc_ref[...], preferred_element_type=F32))
    a = jnp.dot(a_ref[...], wa_ref[...].astype(BF16), preferred_element_type=F32)
    cb = jnp.dot(c_ref[...], wc_ref[...].astype(BF16), preferred_element_type=F32)
    o_ref[...] = (g_a * a + g_c * cb).astype(o_ref.dtype)


def _merge(h2d, attn, conv, w_in, gate_off, wa, wc, tm=512, tn=512):
    m, d = h2d.shape
    ka = attn.shape[1]
    kc = conv.shape[1]
    nj = d // tn
    ja = gate_off // tn
    jc = (gate_off + d) // tn
    return pl.pallas_call(
        _merge_kernel,
        grid=(nj, m // tm),
        in_specs=[pl.BlockSpec((tm, d), lambda j, i: (i, 0)),
                  pl.BlockSpec((tm, ka), lambda j, i: (i, 0)),
                  pl.BlockSpec((tm, kc), lambda j, i: (i, 0)),
                  pl.BlockSpec((d, tn), lambda j, i: (0, ja + j)),
                  pl.BlockSpec((d, tn), lambda j, i: (0, jc + j)),
                  pl.BlockSpec((ka, tn), lambda j, i: (0, j)),
                  pl.BlockSpec((kc, tn), lambda j, i: (0, j))],
        out_specs=pl.BlockSpec((tm, tn), lambda j, i: (i, j)),
        out_shape=jax.ShapeDtypeStruct((m, d), BF16),
        compiler_params=_params("parallel", "arbitrary"),
        name="merge_branches",
    )(h2d, attn, conv, w_in, w_in, wa, wc)


def _out_proj_kernel(m_ref, w_ref, x_ref, ga_ref, o_ref):
    acc = jnp.dot(m_ref[0], w_ref[...], preferred_element_type=F32)
    o_ref[0] = x_ref[0] + ga_ref[0] * acc


def _out_proj(mrg, w, x, gate, tm=1024, tn=1024):
    b, l, d = x.shape
    return pl.pallas_call(
        _out_proj_kernel,
        grid=(b, l // tm, d // tn),
        in_specs=[pl.BlockSpec((1, tm, d), lambda bi, i, j: (bi, i, 0)),
                  pl.BlockSpec((d, tn), lambda bi, i, j: (0, j)),
                  pl.BlockSpec((1, tm, tn), lambda bi, i, j: (bi, i, j)),
                  pl.BlockSpec((1, 1, tn), lambda bi, i, j: (bi, 0, j))],
        out_specs=pl.BlockSpec((1, tm, tn), lambda bi, i, j: (bi, i, j)),
        out_shape=jax.ShapeDtypeStruct((b, l, d), F32),
        compiler_params=_params("parallel", "parallel", "arbitrary"),
        name="out_proj_residual",
    )(mrg, w, x, gate)


def _pack_halves(y):
    n = y.shape[1] // 2
    return pltpu.pack_elementwise([y[:, :n], y[:, n:]], packed_dtype=BF16)


def _unpack_halves(p):
    lo = pltpu.unpack_elementwise(p, index=0, packed_dtype=BF16, unpacked_dtype=F32)
    hi = pltpu.unpack_elementwise(p, index=1, packed_dtype=BF16, unpacked_dtype=F32)
    return lo, hi


def _norm2_router_kernel(x_ref, g_ref, sh_ref, sc_ref, wr_ref, br_ref, hp_ref, lg_ref):
    y = _rms(x_ref[0], g_ref[...]) * (1.0 + sc_ref[0]) + sh_ref[0]
    packed = _pack_halves(y)
    tl = packed.shape[0]
    nt = packed.shape[1] // LANES
    for j in range(nt):
        hp_ref[pl.ds(j, tl, stride=nt), :] = packed[:, j * LANES:(j + 1) * LANES]
    lg_ref[0] = jnp.dot(y.astype(BF16), wr_ref[...], preferred_element_type=F32) + br_ref[...]


def _norm2_router(x, g, shift, scale, w_r, b_r, tl):
    b, l, d = x.shape
    nt = d // 2 // LANES
    lb = l // tl
    return pl.pallas_call(
        _norm2_router_kernel,
        grid=(b, lb),
        in_specs=[pl.BlockSpec((1, tl, d), lambda bi, li: (bi, li, 0)),
                  pl.BlockSpec((1, d), lambda bi, li: (0, 0)),
                  pl.BlockSpec((1, 1, d), lambda bi, li: (bi, 0, 0)),
                  pl.BlockSpec((1, 1, d), lambda bi, li: (bi, 0, 0)),
                  pl.BlockSpec((d, ROUTER_LANES), lambda bi, li: (0, 0)),
                  pl.BlockSpec((1, ROUTER_LANES), lambda bi, li: (0, 0))],
        out_specs=[pl.BlockSpec((tl * nt, LANES), lambda bi, li: (bi * lb + li, 0)),
                   pl.BlockSpec((1, tl, ROUTER_LANES), lambda bi, li: (bi, li, 0))],
        out_shape=[jax.ShapeDtypeStruct((b * l * nt, LANES), jnp.int32),
                   jax.ShapeDtypeStruct((b, l, ROUTER_LANES), F32)],
        compiler_params=_params("parallel", "parallel"),
        name="norm2_router",
    )(x, g, shift, scale, w_r, b_r)


def _first_lane(mask, lane):
    return jnp.min(jnp.where(mask, lane, LANES), axis=-1, keepdims=True)


def _route_kernel(lg_ref, meta_ref, wts_ref, cnt_ref, carry_ref):
    @pl.when(pl.program_id(0) == 0)
    def _():
        carry_ref[...] = jnp.zeros_like(carry_ref)

    lg = lg_ref[...]
    tb = lg.shape[0]
    lane = lax.broadcasted_iota(jnp.int32, lg.shape, 1)
    neg_inf = jnp.float32(-jnp.inf)
    is_group = lane < N_GROUPS
    gl = jnp.where(is_group, lg, neg_inf)
    g_max = jnp.max(gl, axis=-1, keepdims=True)
    g_sel = _first_lane(gl == g_max, lane)
    p_g = 1.0 / jnp.sum(jnp.where(is_group, jnp.exp(lg - g_max), 0.0), axis=-1, keepdims=True)

    e_idx = lane - N_GROUPS
    in_group = (e_idx >= g_sel * EXPERTS_PER_GROUP) & (e_idx < (g_sel + 1) * EXPERTS_PER_GROUP)
    ev = jnp.where(in_group, lg, neg_inf)
    v1 = jnp.max(ev, axis=-1, keepdims=True)
    i1 = _first_lane(ev == v1, lane)
    ev2 = jnp.where(lane == i1, neg_inf, ev)
    v2 = jnp.max(ev2, axis=-1, keepdims=True)
    i2 = _first_lane(ev2 == v2, lane)
    t = jnp.exp(v2 - v1)
    w1 = p_g / (1.0 + t)
    w2 = w1 * t

    oh1 = lane == i1
    oh2 = lane == i2
    oh = (oh1 | oh2).astype(BF16)
    earlier = (lax.broadcasted_iota(jnp.int32, (tb, tb), 0) > lax.broadcasted_iota(jnp.int32, (tb, tb), 1)).astype(BF16)
    before = jnp.dot(earlier, oh, preferred_element_type=F32) + carry_ref[...]
    r1 = jnp.sum(jnp.where(oh1, before, 0.0), axis=-1, keepdims=True).astype(jnp.int32)
    r2 = jnp.sum(jnp.where(oh2, before, 0.0), axis=-1, keepdims=True).astype(jnp.int32)
    carry_ref[...] += jnp.sum(oh.astype(F32), axis=0, keepdims=True)

    meta_ref[...] = jnp.where(lane == 0, i1 - N_GROUPS, jnp.where(lane == 1, i2 - N_GROUPS,
                              jnp.where(lane == 2, r1, jnp.where(lane == 3, r2, 0))))
    wts_ref[...] = jnp.where(lane == 0, w1, jnp.where(lane == 1, w2, 0.0))
    cnt_ref[...] = carry_ref[...]


def _route(logits, tb=1024):
    n = logits.shape[0]
    return pl.pallas_call(
        _route_kernel,
        grid=(n // tb,),
        in_specs=[pl.BlockSpec((tb, LANES), lambda i: (i, 0))],
        out_specs=[pl.BlockSpec((tb, LANES), lambda i: (i, 0)),
                   pl.BlockSpec((tb, LANES), lambda i: (i, 0)),
                   pl.BlockSpec((1, LANES), lambda i: (0, 0))],
        out_shape=[jax.ShapeDtypeStruct((n, LANES), jnp.int32),
                   jax.ShapeDtypeStruct((n, LANES), F32),
                   jax.ShapeDtypeStruct((1, LANES), F32)],
        scratch_shapes=[pltpu.VMEM((1, LANES), F32)],
        compiler_params=_params("arbitrary"),
        name="moe_route",
    )(logits)


def _dest_kernel(meta_ref, pst_ref, o_ref):
    meta = meta_ref[...]
    lane = lax.broadcasted_iota(jnp.int32, meta.shape, 1)
    pst = pst_ref[...]

    def row_of(slot):
        e = meta[:, slot:slot + 1]
        start = jnp.sum(jnp.where(lane == e + N_GROUPS, pst, 0), axis=-1, keepdims=True)
        return start + meta[:, TOP_K + slot:TOP_K + slot + 1]

    o_ref[...] = jnp.where(lane == 0, row_of(0), jnp.where(lane == 1, row_of(1), 0))


def _dest_rows(meta, pst, tb=2048):
    n = meta.shape[0]
    return pl.pallas_call(
        _dest_kernel,
        grid=(n // tb,),
        in_specs=[pl.BlockSpec((tb, LANES), lambda i: (i, 0)),
                  pl.BlockSpec((1, LANES), lambda i: (0, 0))],
        out_specs=pl.BlockSpec((tb, LANES), lambda i: (i, 0)),
        out_shape=jax.ShapeDtypeStruct((n, LANES), jnp.int32),
        compiler_params=_params("parallel"),
        name="moe_dest_rows",
    )(meta, pst)


def _zero_tail_kernel(lb_ref, o_ref):
    o_ref[...] = jnp.zeros_like(o_ref)


def _zero_tails(last_part, total, part_rows, nt):
    return pl.pallas_call(
        _zero_tail_kernel,
        grid_spec=pltpu.PrefetchScalarGridSpec(
            num_scalar_prefetch=1,
            grid=(last_part.shape[0],),
            in_specs=[],
            out_specs=pl.BlockSpec((part_rows * nt, LANES), lambda e, lp: (lp[e], 0))),
        out_shape=jax.ShapeDtypeStruct((total * nt, LANES), jnp.int32),
        compiler_params=_params("arbitrary"),
        name="moe_zero_tails",
    )(last_part)


def _dispatch_kernel(dest_ref, hp_ref, xs_in_ref, xs_ref, sem):
    rows = hp_ref.shape[0]

    def slab_copy(r, k):
        return pltpu.make_async_copy(hp_ref.at[r], xs_ref.at[dest_ref[0, 0, r * TOP_K + k]], sem)

    def start(r, c):
        for k in range(TOP_K):
            slab_copy(r, k).start(priority=k % 2)
        return c

    def wait(r, c):
        for k in range(TOP_K):
            slab_copy(r, k).wait()
        return c

    lax.fori_loop(0, rows, start, 0, unroll=DMA_LOOP_UNROLL)
    lax.fori_loop(0, rows, wait, 0, unroll=DMA_LOOP_UNROLL)


def _dispatch(hp3, dest, xs0, rows):
    n, nt, _ = hp3.shape
    steps = n // rows
    return pl.pallas_call(
        _dispatch_kernel,
        grid=(steps,),
        in_specs=[pl.BlockSpec((1, 1, rows * TOP_K), lambda i: (i, 0, 0), memory_space=pltpu.SMEM),
                  pl.BlockSpec((rows, nt, LANES), lambda i: (i, 0, 0)),
                  pl.BlockSpec(memory_space=pl.ANY)],
        out_specs=pl.BlockSpec(memory_space=pl.ANY),
        out_shape=jax.ShapeDtypeStruct(xs0.shape, xs0.dtype),
        scratch_shapes=[pltpu.SemaphoreType.DMA(())],
        input_output_aliases={2: 0},
        compiler_params=_params("arbitrary"),
        name="moe_dispatch",
    )(dest.reshape(steps, 1, rows * TOP_K), hp3, xs0)


def _by_valid_rows(nv, tmb, compute, o_ref):
    quarter = tmb // EXPERT_BLOCK_PARTS
    per_row = o_ref.shape[0] // tmb

    for nq in range(1, EXPERT_BLOCK_PARTS + 1):
        rows = nq * quarter

        @pl.when((nv > rows - quarter) & (nv <= rows))
        def _(rows=rows):
            compute(rows)
            if rows < tmb:
                o_ref[rows * per_row:, :] = jnp.zeros(((tmb - rows) * per_row, o_ref.shape[1]), o_ref.dtype)

    @pl.when(nv == 0)
    def _():
        o_ref[...] = jnp.zeros_like(o_ref)


def _expert_weights(plan, w_hbm, wbuf, sem):
    be_ref, first_ref, slot_ref, nxt_ref = plan
    bi = pl.program_id(0)
    slot = slot_ref[bi]

    def copy(e, s):
        return pltpu.make_async_copy(w_hbm.at[e], wbuf.at[s], sem.at[s])

    @pl.when(bi == 0)
    def _():
        copy(be_ref[0], 0).start()

    @pl.when((first_ref[bi] == 1) & (nxt_ref[bi] >= 0))
    def _():
        copy(nxt_ref[bi], 1 - slot).start()

    @pl.when(first_ref[bi] == 1)
    def _():
        copy(be_ref[bi], slot).wait()

    return wbuf.at[slot]


def _expert_in_kernel(be_ref, first_ref, slot_ref, nxt_ref, nu_ref, nv_ref, x_ref, w_hbm, *rest, nt, tn):
    gate_ref = rest[0] if len(rest) == 4 else None
    o_ref, wbuf, sem = rest[-3:]
    w_ref = _expert_weights((be_ref, first_ref, slot_ref, nxt_ref), w_hbm, wbuf, sem)

    def compute(rows):
        halves = [_unpack_halves(x_ref[pl.ds(j, rows, stride=nt), :]) for j in range(nt)]
        lo = jnp.concatenate([h[0].astype(BF16) for h in halves], axis=1)
        hi = jnp.concatenate([h[1].astype(BF16) for h in halves], axis=1)
        half = nt * LANES
        for c0 in range(0, o_ref.shape[1], tn):
            y = (jnp.dot(lo, w_ref[:half, c0:c0 + tn].astype(BF16), preferred_element_type=F32)
                 + jnp.dot(hi, w_ref[half:, c0:c0 + tn].astype(BF16), preferred_element_type=F32))
            if gate_ref is None:
                y = _silu(y)
            else:
                y = gate_ref[:rows, c0:c0 + tn].astype(F32) * y
            o_ref[:rows, c0:c0 + tn] = y.astype(o_ref.dtype)

    _by_valid_rows(nv_ref[pl.program_id(0)], o_ref.shape[0], compute, o_ref)


N_PLAN = 6


def _blk_clamped(bi, *plan):
    return jnp.minimum(bi, plan[4][0] - 1)


def _expert_in(xs, w, gate, plan, tmb, name, tn=256):
    _, d, ff = w.shape
    nt = d // 2 // LANES
    total = xs.shape[0] // nt
    nblk = total // tmb
    in_specs = [pl.BlockSpec((tmb * nt, LANES), lambda bi, *p: (_blk_clamped(bi, *p), 0)),
                pl.BlockSpec(memory_space=pl.ANY)]
    args = [xs, w]
    if gate is not None:
        in_specs.append(pl.BlockSpec((tmb, ff), lambda bi, *p: (_blk_clamped(bi, *p), 0)))
        args.append(gate)
    return pl.pallas_call(
        functools.partial(_expert_in_kernel, nt=nt, tn=tn),
        grid_spec=pltpu.PrefetchScalarGridSpec(
            num_scalar_prefetch=N_PLAN,
            grid=(nblk,),
            in_specs=in_specs,
            out_specs=pl.BlockSpec((tmb, ff), lambda bi, *p: (bi, 0)),
            scratch_shapes=[pltpu.VMEM((2, d, ff), w.dtype), pltpu.SemaphoreType.DMA((2,))]),
        out_shape=jax.ShapeDtypeStruct((total, ff), BF16),
        compiler_params=_params("arbitrary"),
        name=name,
    )(*plan, *args)


def _expert_down_kernel(be_ref, first_ref, slot_ref, nxt_ref, nu_ref, nv_ref, a_ref, w_hbm, o_ref, wbuf, sem,
                        *, nt, tn):
    tmb = a_ref.shape[0]
    w_ref = _expert_weights((be_ref, first_ref, slot_ref, nxt_ref), w_hbm, wbuf, sem)

    def compute(rows):
        a = a_ref[:rows, :]
        half = nt * LANES
        for c0 in range(0, half, tn):
            ylo = jnp.dot(a, w_ref[:, c0:c0 + tn].astype(BF16), preferred_element_type=F32)
            yhi = jnp.dot(a, w_ref[:, half + c0:half + c0 + tn].astype(BF16), preferred_element_type=F32)
            packed = pltpu.pack_elementwise([ylo, yhi], packed_dtype=BF16)
            for j in range(tn // LANES):
                o_ref[pl.ds(c0 // LANES + j, rows, stride=nt), :] = packed[:, j * LANES:(j + 1) * LANES]

    _by_valid_rows(nv_ref[pl.program_id(0)], tmb, compute, o_ref)


def _expert_down(act, w_down, plan, tmb, tn=512):
    total, ff = act.shape
    d = w_down.shape[2]
    nt = d // 2 // LANES
    nblk = total // tmb
    return pl.pallas_call(
        functools.partial(_expert_down_kernel, nt=nt, tn=tn),
        grid_spec=pltpu.PrefetchScalarGridSpec(
            num_scalar_prefetch=N_PLAN,
            grid=(nblk,),
            in_specs=[pl.BlockSpec((tmb, ff), lambda bi, *p: (_blk_clamped(bi, *p), 0)),
                      pl.BlockSpec(memory_space=pl.ANY)],
            out_specs=pl.BlockSpec((tmb * nt, LANES), lambda bi, *p: (bi, 0)),
            scratch_shapes=[pltpu.VMEM((2, ff, d), w_down.dtype), pltpu.SemaphoreType.DMA((2,))]),
        out_shape=jax.ShapeDtypeStruct((total * nt, LANES), jnp.int32),
        compiler_params=_params("arbitrary"),
        name="expert_down",
    )(*plan, act, w_down)


def _combine_kernel(pos_ref, nxt_ref, ys_ref, w_ref, x_ref, ga_ref, g_ref, o_ref, *scratch):
    n_buf = 2 * TOP_K
    slabs, sems = scratch[:n_buf], scratch[n_buf:]
    rows = x_ref.shape[1]
    hr = rows // 2
    nt = ys_ref.shape[1]
    half = nt * LANES
    step = pl.program_id(0) * pl.num_programs(1) + pl.program_id(1)
    n_steps = pl.num_programs(0) * pl.num_programs(1)

    def slab_copy(idx_ref, h, r, k):
        dst = slabs[h * TOP_K + k].at[pl.ds(pl.multiple_of(r * SLAB_PITCH, SUBLANES), nt)]
        return pltpu.make_async_copy(ys_ref.at[idx_ref[0, 0, (h * hr + r) * TOP_K + k]], dst, sems[h * TOP_K + k])

    def issue(idx_ref, h):
        def body(r, c):
            for k in range(TOP_K):
                slab_copy(idx_ref, h, r, k).start(priority=k % 2)
            return c
        lax.fori_loop(0, hr, body, 0, unroll=DMA_LOOP_UNROLL)

    def wait(idx_ref, h):
        def body(r, c):
            for k in range(TOP_K):
                slab_copy(idx_ref, h, r, k).wait()
            return c
        lax.fori_loop(0, hr, body, 0, unroll=DMA_LOOP_UNROLL)

    def compute(h):
        r0 = h * hr
        w0 = w_ref[r0:r0 + hr, 0:1]
        w1 = w_ref[r0:r0 + hr, 1:2]
        ss = jnp.zeros((hr, 1), F32)
        for j in range(nt):
            lo0, hi0 = _unpack_halves(slabs[h * TOP_K][pl.ds(j, hr, stride=SLAB_PITCH), :])
            lo1, hi1 = _unpack_halves(slabs[h * TOP_K + 1][pl.ds(j, hr, stride=SLAB_PITCH), :])
            for c0, y in ((j * LANES, w0 * lo0 + w1 * lo1), (half + j * LANES, w0 * hi0 + w1 * hi1)):
                z = x_ref[0, r0:r0 + hr, c0:c0 + LANES] + ga_ref[0, :, c0:c0 + LANES] * y
                ss = ss + jnp.sum(z * z, axis=-1, keepdims=True)
                o_ref[0, r0:r0 + hr, c0:c0 + LANES] = z
        inv = lax.rsqrt(ss / (2 * half) + EPS)
        o_ref[0, r0:r0 + hr, :] = o_ref[0, r0:r0 + hr, :] * inv * g_ref[...]

    @pl.when(step == 0)
    def _():
        issue(pos_ref, 0)

    issue(pos_ref, 1)
    wait(pos_ref, 0)
    compute(0)

    @pl.when(step + 1 < n_steps)
    def _():
        issue(nxt_ref, 0)

    wait(pos_ref, 1)
    compute(1)


def _combine(ys3, pos, wts, x, gate, g, rows):
    b, l, d = x.shape
    lb = l // rows
    n_steps = b * lb
    pos3 = pos.reshape(n_steps, 1, rows * TOP_K)
    slab = pltpu.VMEM((rows // 2 * SLAB_PITCH, LANES), ys3.dtype)
    return pl.pallas_call(
        _combine_kernel,
        grid=(b, lb),
        in_specs=[pl.BlockSpec((1, 1, rows * TOP_K), lambda bi, i: (bi * lb + i, 0, 0), memory_space=pltpu.SMEM),
                  pl.BlockSpec((1, 1, rows * TOP_K), lambda bi, i: (jnp.minimum(bi * lb + i + 1, n_steps - 1), 0, 0),
                               memory_space=pltpu.SMEM),
                  pl.BlockSpec(memory_space=pl.ANY),
                  pl.BlockSpec((rows, LANES), lambda bi, i: (bi * lb + i, 0)),
                  pl.BlockSpec((1, rows, d), lambda bi, i: (bi, i, 0)),
                  pl.BlockSpec((1, 1, d), lambda bi, i: (bi, 0, 0)),
                  pl.BlockSpec((1, d), lambda bi, i: (0, 0))],
        out_specs=pl.BlockSpec((1, rows, d), lambda bi, i: (bi, i, 0)),
        out_shape=jax.ShapeDtypeStruct((b, l, d), F32),
        scratch_shapes=[slab] * (2 * TOP_K) + [pltpu.SemaphoreType.DMA(())] * (2 * TOP_K),
        compiler_params=_params("arbitrary", "arbitrary"),
        name="moe_combine_norm",
    )(pos3, pos3, ys3, wts, x, gate, g)


def _block_layout(counts, n_pairs, tmb):
    nblk = (n_pairs + N_EXPERTS * (tmb - 1) + tmb - 1) // tmb
    blocks = (counts + tmb - 1) // tmb
    bend = jnp.cumsum(blocks)
    bstart = bend - blocks
    pstart = bstart * tmb
    n_used = bend[-1]
    ids = jnp.arange(nblk, dtype=jnp.int32)
    blk_e = jnp.minimum(jnp.searchsorted(bend, jnp.minimum(ids, n_used - 1), side="right"),
                        N_EXPERTS - 1).astype(jnp.int32)
    part = tmb // EXPERT_BLOCK_PARTS
    last_part = ((pstart + jnp.maximum(counts - 1, 0) // part * part) // part).astype(jnp.int32)
    used = ids < n_used
    n_valid = jnp.where(used, jnp.clip(counts[blk_e] - (ids - bstart[blk_e]) * tmb, 0, tmb), 0)
    first = used & ((ids == 0) | (blk_e != jnp.roll(blk_e, 1)))
    slot = (jnp.cumsum(first.astype(jnp.int32)) - 1) % 2
    experts = jnp.arange(N_EXPERTS, dtype=jnp.int32)
    later = jnp.where((blocks > 0)[None, :] & (experts[None, :] > experts[:, None]), experts[None, :], N_EXPERTS)
    nxt_e = jnp.min(later, axis=1)
    nxt = jnp.where(nxt_e[blk_e] < N_EXPERTS, nxt_e[blk_e], -1)
    i32 = lambda a: a.astype(jnp.int32)
    plan = (blk_e, i32(first), i32(slot), i32(nxt), i32(n_used).reshape(1), i32(n_valid))
    return nblk, i32(pstart), last_part, plan


def _rope_tables(n_tokens):
    rows = n_tokens // GRID_W
    row, col = jnp.meshgrid(jnp.arange(rows), jnp.arange(GRID_W), indexing="ij")
    pos = jnp.stack([row.reshape(-1), col.reshape(-1)], axis=-1).astype(F32)
    inv = ROPE_THETA ** (-jnp.arange(0, ROPE_AXIS_DIM, 2, dtype=F32) / ROPE_AXIS_DIM)
    ang = pos[:, :, None] * inv[None, None, :]
    cos, sin = jnp.cos(ang), jnp.sin(ang)
    cos_t = jnp.concatenate([cos[:, 0], cos[:, 0], cos[:, 1], cos[:, 1]], axis=-1)
    sin_t = jnp.concatenate([-sin[:, 0], sin[:, 0], -sin[:, 1], sin[:, 1]], axis=-1)
    return cos_t, sin_t


def kernel(x, c, ctx, c_ctx, norm1_g, w_mod, b_mod, w_in, q_norm_g, k_norm_g, w_attn_out, conv_dw_w, conv_dw_b, conv_ln_g, conv_ln_b, w_conv_out, w_out, norm2_g, w_router_group, b_router_group, w_router_expert, b_router_expert, w_exp_gate, w_exp_up, w_exp_down, norm_f_g):
    b, s, d = x.shape
    n_ctx = ctx.shape[1]
    assert w_in.shape[0] == 1, "single-layer stack"
    conv_width = conv_dw_w.shape[-1]
    k_off = ATTN_WIDTH
    glu_off = k_off + 2 * KV_WIDTH
    gate_off = glu_off + 2 * conv_width

    n_c = b + 1
    cvec = jnp.zeros((SUBLANES * ((n_c + SUBLANES - 1) // SUBLANES), d), F32).at[:b].set(c).at[b].set(c_ctx)
    mod = _mod_vectors(cvec, w_mod[0], b_mod.reshape(1, -1))
    sh1, sc1, ga1, sh2, sc2, ga2 = [mod[:b, i * d:(i + 1) * d].reshape(b, 1, d) for i in range(N_MOD)]
    csh1, csc1 = [mod[b:b + 1, i * d:(i + 1) * d].reshape(1, 1, d) for i in range(2)]

    g1 = norm1_g.reshape(1, d)
    h = _norm_mod(x, g1, sh1, sc1, tl=512)
    hc = _norm_mod(ctx, g1, csh1, csc1, tl=n_ctx)
    w_in_b = w_in[0].astype(BF16)
    cos_t, sin_t = _rope_tables(s)
    qg = q_norm_g.reshape(1, HEAD_DIM)
    kg = k_norm_g.reshape(1, HEAD_DIM)
    q = _q_proj(h, w_in_b, qg, cos_t, sin_t, 0, HEAD_DIM ** -0.5 * LOG2E, tm=1024)
    k, v = _kv_proj(h, w_in_b, kg, cos_t, sin_t, k_off, tm=1024)
    kc, vc = _kv_proj(hc, w_in_b, kg, None, None, k_off, tm=n_ctx)
    attn = _attention(q, k, v, kc, vc, tq=128, n_groups=1, tk=1024)

    h2d = h.reshape(b * s, d)
    u = _glu_proj(h2d, w_in_b, glu_off, conv_width, tm=1024)
    conv = _conv_module(u.reshape(b, s, conv_width), conv_dw_w.reshape(CONV_TAPS, conv_width // LANES, LANES),
                        conv_dw_b.reshape(conv_width // LANES, LANES), conv_ln_g.reshape(1, -1),
                        conv_ln_b.reshape(1, -1), tl=256)
    mrg = _merge(h2d, attn.reshape(b * s, ATTN_WIDTH), conv.reshape(b * s, conv_width), w_in_b, gate_off,
                 w_attn_out[0], w_conv_out[0])
    x1 = _out_proj(mrg.reshape(b, s, d), w_out[0].astype(BF16), x, ga1)

    w_r = jnp.zeros((d, ROUTER_LANES), F32).at[:, :N_GROUPS].set(w_router_group[0]) \
        .at[:, N_GROUPS:N_GROUPS + N_EXPERTS].set(w_router_expert[0])
    b_r = jnp.zeros((1, ROUTER_LANES), F32).at[0, :N_GROUPS].set(b_router_group[0]) \
        .at[0, N_GROUPS:N_GROUPS + N_EXPERTS].set(b_router_expert[0])
    hp, logits = _norm2_router(x1, norm2_g.reshape(1, d), sh2, sc2, w_r.astype(BF16), b_r, tl=512)
    n = b * s
    nt = d // 2 // LANES
    tmb = 512
    meta, wts, cnt = _route(logits.reshape(n, ROUTER_LANES))
    counts = cnt[0, N_GROUPS:N_GROUPS + N_EXPERTS].astype(jnp.int32)
    nblk, pstart, last_part, plan = _block_layout(counts, n * TOP_K, tmb)
    pst = jnp.zeros((1, LANES), jnp.int32).at[0, N_GROUPS:N_GROUPS + N_EXPERTS].set(pstart)
    dest = _dest_rows(meta, pst)[:, :TOP_K].reshape(-1)
    xs0 = _zero_tails(last_part, nblk * tmb, tmb // EXPERT_BLOCK_PARTS, nt)
    xs = _dispatch(hp.reshape(n, nt, LANES), dest, xs0.reshape(nblk * tmb, nt, LANES), rows=1024)
    xs2 = xs.reshape(nblk * tmb * nt, LANES)
    sg = _expert_in(xs2, w_exp_gate[0], None, plan, tmb, "expert_gate")
    act = _expert_in(xs2, w_exp_up[0], sg, plan, tmb, "expert_up")
    ys = _expert_down(act, w_exp_down[0], plan, tmb)
    return _combine(ys.reshape(nblk * tmb, nt, LANES), dest, wts, x1, ga2, norm_f_g.reshape(1, d), rows=512)
```

```python
import functools

import jax
import jax.numpy as jnp
from jax import lax
from jax.experimental import pallas as pl
from jax.experimental.pallas import tpu as pltpu

F32 = jnp.float32
BF16 = jnp.bfloat16

GRID_W = 64
HEAD_DIM = 128
N_Q_HEADS = 16
N_KV_HEADS = 4
Q_PER_KV = N_Q_HEADS // N_KV_HEADS
ATTN_WIDTH = N_Q_HEADS * HEAD_DIM
KV_WIDTH = N_KV_HEADS * HEAD_DIM
CONV_TAPS = 31
CONV_HALO = 16
ROPE_THETA = 10000.0
ROPE_AXIS_DIM = HEAD_DIM // 2
N_GROUPS = 4
EXPERTS_PER_GROUP = 8
N_EXPERTS = N_GROUPS * EXPERTS_PER_GROUP
TOP_K = 2
N_MOD = 6
EPS = 1e-6
LOG2E = 1.4426950408889634
LANES = 128
SUBLANES = 8
ROUTER_LANES = LANES
SLAB_PITCH = 24
DMA_LOOP_UNROLL = 4
EXPERT_BLOCK_PARTS = 4

V7X_VMEM_LIMIT = 56 * 1024 * 1024


def _params(*sem):
    return pltpu.CompilerParams(dimension_semantics=sem, vmem_limit_bytes=V7X_VMEM_LIMIT)


def _sigmoid(x):
    return 1.0 / (1.0 + jnp.exp(-x))


def _silu(x):
    return x * _sigmoid(x)


def _rms(x, g):
    return x * lax.rsqrt(jnp.mean(x * x, axis=-1, keepdims=True) + EPS) * g


def _mod_kernel(c_ref, w_ref, b_ref, o_ref):
    s = _silu(c_ref[...]).astype(BF16)
    o_ref[...] = jnp.dot(s, w_ref[...].astype(BF16), preferred_element_type=F32) + b_ref[...]


def _mod_vectors(cvec, w_mod, b_mod, tn=1024):
    m, d = cvec.shape
    n = w_mod.shape[1]
    return pl.pallas_call(
        _mod_kernel,
        grid=(n // tn,),
        in_specs=[pl.BlockSpec((m, d), lambda j: (0, 0)),
                  pl.BlockSpec((d, tn), lambda j: (0, j)),
                  pl.BlockSpec((1, tn), lambda j: (0, j))],
        out_specs=pl.BlockSpec((m, tn), lambda j: (0, j)),
        out_shape=jax.ShapeDtypeStruct((m, n), F32),
        compiler_params=_params("arbitrary"),
        name="mod_vectors",
    )(cvec, w_mod, b_mod)


def _norm_mod_kernel(x_ref, g_ref, sh_ref, sc_ref, o_ref):
    y = _rms(x_ref[0], g_ref[...])
    o_ref[0] = (y * (1.0 + sc_ref[0]) + sh_ref[0]).astype(o_ref.dtype)


def _norm_mod(x, g, shift, scale, tl):
    b, l, d = x.shape
    per_batch = shift.shape[0] > 1
    mod_map = (lambda bi, li: (bi, 0, 0)) if per_batch else (lambda bi, li: (0, 0, 0))
    return pl.pallas_call(
        _norm_mod_kernel,
        grid=(b, l // tl),
        in_specs=[pl.BlockSpec((1, tl, d), lambda bi, li: (bi, li, 0)),
                  pl.BlockSpec((1, d), lambda bi, li: (0, 0)),
                  pl.BlockSpec((1, 1, d), mod_map),
                  pl.BlockSpec((1, 1, d), mod_map)],
        out_specs=pl.BlockSpec((1, tl, d), lambda bi, li: (bi, li, 0)),
        out_shape=jax.ShapeDtypeStruct((b, l, d), BF16),
        compiler_params=_params("parallel", "parallel"),
        name="norm_modulate",
    )(x, g, shift, scale)


def _head_norm_rope(a, g, cos, sin):
    y = _rms(a, g)
    if cos is None:
        return y
    lane = lax.broadcasted_iota(jnp.int32, y.shape, 1)
    quarter = ROPE_AXIS_DIM // 2
    partner = jnp.where((lane % ROPE_AXIS_DIM) < quarter,
                        pltpu.roll(y, HEAD_DIM - quarter, 1), pltpu.roll(y, quarter, 1))
    return y * cos + partner * sin


def _q_proj_kernel(h_ref, w_ref, g_ref, cos_ref, sin_ref, o_ref, *, scale):
    tm = h_ref.shape[1]
    for r0 in range(0, tm, tm // 4):
        rows = slice(r0, r0 + tm // 4)
        acc = jnp.dot(h_ref[0, rows, :], w_ref[...], preferred_element_type=F32)
        for hh in range(o_ref.shape[1]):
            a = acc[:, hh * HEAD_DIM:(hh + 1) * HEAD_DIM]
            y = _head_norm_rope(a, g_ref[...], cos_ref[rows, :], sin_ref[rows, :])
            o_ref[0, hh, rows, :] = (y * scale).astype(o_ref.dtype)


def _q_proj(h, w, g, cos_t, sin_t, col_off, scale, tm, tn=1024):
    b, l, d = h.shape
    jb = col_off // tn
    hpt = tn // HEAD_DIM
    return pl.pallas_call(
        functools.partial(_q_proj_kernel, scale=scale),
        grid=(b, l // tm, ATTN_WIDTH // tn),
        in_specs=[pl.BlockSpec((1, tm, d), lambda bi, i, j: (bi, i, 0)),
                  pl.BlockSpec((d, tn), lambda bi, i, j: (0, jb + j)),
                  pl.BlockSpec((1, HEAD_DIM), lambda bi, i, j: (0, 0)),
                  pl.BlockSpec((tm, HEAD_DIM), lambda bi, i, j: (i, 0)),
                  pl.BlockSpec((tm, HEAD_DIM), lambda bi, i, j: (i, 0))],
        out_specs=pl.BlockSpec((1, hpt, tm, HEAD_DIM), lambda bi, i, j: (bi, j, i, 0)),
        out_shape=jax.ShapeDtypeStruct((b, N_Q_HEADS, l, HEAD_DIM), BF16),
        compiler_params=_params("parallel", "parallel", "arbitrary"),
        name="q_proj",
    )(h, w, g, cos_t, sin_t)


def _kv_proj_kernel(h_ref, wk_ref, wv_ref, g_ref, *rest, rope):
    if rope:
        cos_ref, sin_ref, k_ref, v_ref = rest
        cos, sin = cos_ref[...], sin_ref[...]
    else:
        k_ref, v_ref = rest
        cos = sin = None
    tm = h_ref.shape[1]
    n_groups = 2 if tm >= 512 else 1
    for r0 in range(0, tm, tm // n_groups):
        rows = slice(r0, r0 + tm // n_groups)
        h = h_ref[0, rows, :]
        acc = jnp.dot(h, wk_ref[...], preferred_element_type=F32)
        for hh in range(N_KV_HEADS):
            a = acc[:, hh * HEAD_DIM:(hh + 1) * HEAD_DIM]
            y = _head_norm_rope(a, g_ref[...], None if cos is None else cos[rows], None if sin is None else sin[rows])
            k_ref[0, rows, hh * HEAD_DIM:(hh + 1) * HEAD_DIM] = y.astype(k_ref.dtype)
        v_ref[0, rows, :] = jnp.dot(h, wv_ref[...], preferred_element_type=F32).astype(v_ref.dtype)


def _kv_proj(h, w, g, cos_t, sin_t, k_off, tm):
    b, l, d = h.shape
    rope = cos_t is not None
    jk = k_off // KV_WIDTH
    in_specs = [pl.BlockSpec((1, tm, d), lambda bi, i: (bi, i, 0)),
                pl.BlockSpec((d, KV_WIDTH), lambda bi, i: (0, jk)),
                pl.BlockSpec((d, KV_WIDTH), lambda bi, i: (0, jk + 1)),
                pl.BlockSpec((1, HEAD_DIM), lambda bi, i: (0, 0))]
    args = [h, w, w, g]
    if rope:
        in_specs += [pl.BlockSpec((tm, HEAD_DIM), lambda bi, i: (i, 0))] * 2
        args += [cos_t, sin_t]
    return pl.pallas_call(
        functools.partial(_kv_proj_kernel, rope=rope),
        grid=(b, l // tm),
        in_specs=in_specs,
        out_specs=[pl.BlockSpec((1, tm, KV_WIDTH), lambda bi, i: (bi, i, 0))] * 2,
        out_shape=[jax.ShapeDtypeStruct((b, l, KV_WIDTH), BF16)] * 2,
        compiler_params=_params("parallel", "parallel"),
        name="kv_proj_rope" if rope else "kv_proj_ctx",
    )(*args)


def _glu_proj_kernel(h_ref, wa_ref, wg_ref, o_ref):
    tm = h_ref.shape[0]
    for r0 in range(0, tm, tm // 2):
        rows = slice(r0, r0 + tm // 2)
        a = jnp.dot(h_ref[rows, :], wa_ref[...], preferred_element_type=F32)
        gt = jnp.dot(h_ref[rows, :], wg_ref[...], preferred_element_type=F32)
        o_ref[rows, :] = (a * _sigmoid(gt)).astype(o_ref.dtype)


def _glu_proj(h2d, w, col_off, width, tm, tn=512):
    m, d = h2d.shape
    ja = col_off // tn
    jg = (col_off + width) // tn
    return pl.pallas_call(
        _glu_proj_kernel,
        grid=(m // tm, width // tn),
        in_specs=[pl.BlockSpec((tm, d), lambda i, j: (i, 0)),
                  pl.BlockSpec((d, tn), lambda i, j: (0, ja + j)),
                  pl.BlockSpec((d, tn), lambda i, j: (0, jg + j))],
        out_specs=pl.BlockSpec((tm, tn), lambda i, j: (i, j)),
        out_shape=jax.ShapeDtypeStruct((m, width), BF16),
        compiler_params=_params("parallel", "arbitrary"),
        name="glu_proj",
    )(h2d, w, w)


def _attn_kernel(q_ref, k_ref, v_ref, kc_ref, vc_ref, o_ref, *scratch, tk, rb, n_groups):
    g, tq, dh = q_ref.shape[1:]
    per_group = len(scratch) // n_groups
    tqs = tq // n_groups
    rows = g * tqs
    chunks = [(k_ref, v_ref, c * tk, tk) for c in range(k_ref.shape[1] // tk)]
    chunks.append((kc_ref, vc_ref, 0, kc_ref.shape[1]))

    def bufs(gi):
        s0, s1, p0, p1, m_ref, al_ref, acc_ref = scratch[gi * per_group:(gi + 1) * per_group]
        return (s0, s1), (p0, p1), m_ref, al_ref, acc_ref

    def scores(j, gi):
        kr, _, st, n = chunks[j]
        q = q_ref[0, :, gi * tqs:(gi + 1) * tqs, :].reshape(rows, dh)
        bufs(gi)[0][j % 2][:, :n] = lax.dot_general(q, kr[0, st:st + n, :], (((1,), (1,)), ((), ())),
                                                    preferred_element_type=F32)

    def softmax(j, gi):
        n = chunks[j][3]
        s_refs, p_refs, m_ref, al_ref, _ = bufs(gi)
        s_ref, p_ref = s_refs[j % 2], p_refs[j % 2]
        for r0 in range(0, rows, rb):
            sblk = s_ref[r0:r0 + rb, :n]
            mn = jnp.max(sblk, axis=-1, keepdims=True)
            if j > 0:
                mo = m_ref[r0:r0 + rb, :]
                mn = jnp.maximum(mo, mn)
                al_ref[r0:r0 + rb, :] = jnp.exp2(mo - mn)
            m_ref[r0:r0 + rb, :] = mn
            p_ref[r0:r0 + rb, :n] = jnp.exp2(sblk - mn).astype(BF16)

    def weighted_values(j, gi):
        _, vr, st, n = chunks[j]
        _, p_refs, _, al_ref, acc_ref = bufs(gi)
        ones_col = (lax.broadcasted_iota(jnp.int32, (n, dh), 1) == 0).astype(BF16)
        v1 = jnp.concatenate([vr[0, st:st + n, :], ones_col], axis=1)
        upd = jnp.dot(p_refs[j % 2][:, :n], v1, preferred_element_type=F32)
        if j == 0:
            acc_ref[...] = upd
        else:
            acc_ref[...] = al_ref[...] * acc_ref[...] + upd

    def finish(gi):
        acc = bufs(gi)[4][...]
        o = acc[:, :dh] / acc[:, dh:dh + 1]
        for hi in range(g):
            o_ref[0, gi * tqs:(gi + 1) * tqs, hi * dh:(hi + 1) * dh] = o[hi * tqs:(hi + 1) * tqs].astype(o_ref.dtype)

    for gi in range(n_groups):
        scores(0, gi)
    for j in range(len(chunks)):
        for gi in range(n_groups):
            if j + 1 < len(chunks):
                scores(j + 1, gi)
            softmax(j, gi)
            weighted_values(j, gi)
    for gi in range(n_groups):
        finish(gi)


def _attention(q, k, v, kc, vc, tq, n_groups, tk, rb=8):
    b, _, l, dh = q.shape
    lc = kc.shape[1]
    rows = Q_PER_KV * tq // n_groups
    gdh = Q_PER_KV * dh
    group_scratch = [pltpu.VMEM((rows, tk), F32), pltpu.VMEM((rows, tk), F32),
                     pltpu.VMEM((rows, tk), BF16), pltpu.VMEM((rows, tk), BF16),
                     pltpu.VMEM((rows, 1), F32), pltpu.VMEM((rows, 1), F32),
                     pltpu.VMEM((rows, 2 * dh), F32)]
    return pl.pallas_call(
        functools.partial(_attn_kernel, tk=tk, rb=rb, n_groups=n_groups),
        grid=(b, N_KV_HEADS, l // tq),
        in_specs=[pl.BlockSpec((1, Q_PER_KV, tq, dh), lambda bi, kh, qi: (bi, kh, qi, 0)),
                  pl.BlockSpec((1, l, dh), lambda bi, kh, qi: (bi, 0, kh)),
                  pl.BlockSpec((1, l, dh), lambda bi, kh, qi: (bi, 0, kh)),
                  pl.BlockSpec((1, lc, dh), lambda bi, kh, qi: (bi, 0, kh)),
                  pl.BlockSpec((1, lc, dh), lambda bi, kh, qi: (bi, 0, kh))],
        out_specs=pl.BlockSpec((1, tq, gdh), lambda bi, kh, qi: (bi, qi, kh)),
        out_shape=jax.ShapeDtypeStruct((b, l, ATTN_WIDTH), BF16),
        scratch_shapes=group_scratch * n_groups,
        compiler_params=_params("parallel", "parallel", "arbitrary"),
        name="attention",
    )(q, k, v, kc, vc)


def _conv_kernel(prev_ref, cur_ref, next_ref, w_ref, b_ref, g_ref, beta_ref, o_ref, win_ref, y_ref, *, tc, rc):
    li = pl.program_id(1)
    tl, c = cur_ref.shape[1:]
    halo = prev_ref.shape[1]
    nt = c // LANES

    def put_tokens(vals, tok0):
        for j in range(nt):
            win_ref[pl.ds(tok0 * nt + j, vals.shape[0], stride=nt), :] = vals[:, j * LANES:(j + 1) * LANES]

    def put_chunk(ci, carry):
        r0 = pl.multiple_of(ci * rc, rc)
        put_tokens(cur_ref[0, pl.ds(r0, rc), :].astype(F32), halo + r0)
        return carry

    put_tokens(jnp.where(li > 0, prev_ref[0].astype(F32), 0.0), 0)
    lax.fori_loop(0, tl // rc, put_chunk, 0)
    put_tokens(jnp.where(li < pl.num_programs(1) - 1, next_ref[0].astype(F32), 0.0), halo + tl)

    first = halo - CONV_TAPS // 2
    bias = b_ref[...][None]

    def token_chunk(ci, carry):
        tok = ci * tc
        acc = jnp.zeros((tc, nt, LANES), F32) + bias
        for t in range(CONV_TAPS):
            r0 = pl.multiple_of((tok + first + t) * nt, nt)
            acc = acc + win_ref[pl.ds(r0, tc * nt), :].reshape(tc, nt, LANES) * w_ref[t][None]
        y_ref[pl.ds(pl.multiple_of(tok * nt, nt), tc * nt), :] = acc.reshape(tc * nt, LANES)
        return carry

    lax.fori_loop(0, tl // tc, token_chunk, 0)

    def norm_chunk(ci, carry):
        r0 = pl.multiple_of(ci * rc, rc)
        y = jnp.concatenate([y_ref[pl.ds(r0 * nt + j, rc, stride=nt), :] for j in range(nt)], axis=1)
        mu = jnp.mean(y, axis=-1, keepdims=True)
        yc = y - mu
        var = jnp.mean(yc * yc, axis=-1, keepdims=True)
        z = yc * lax.rsqrt(var + EPS) * g_ref[...] + beta_ref[...]
        o_ref[0, pl.ds(r0, rc), :] = _silu(z).astype(o_ref.dtype)
        return carry

    lax.fori_loop(0, tl // rc, norm_chunk, 0, unroll=2)


def _conv_module(u, w_dw, b_dw, ln_g, ln_b, tl, tc=16, rc=32):
    b, l, c = u.shape
    nt = c // LANES
    hb = tl // CONV_HALO
    n_halo = l // CONV_HALO
    return pl.pallas_call(
        functools.partial(_conv_kernel, tc=tc, rc=rc),
        grid=(b, l // tl),
        in_specs=[pl.BlockSpec((1, CONV_HALO, c), lambda bi, li: (bi, jnp.maximum(li * hb - 1, 0), 0)),
                  pl.BlockSpec((1, tl, c), lambda bi, li: (bi, li, 0)),
                  pl.BlockSpec((1, CONV_HALO, c), lambda bi, li: (bi, jnp.minimum((li + 1) * hb, n_halo - 1), 0)),
                  pl.BlockSpec((CONV_TAPS, nt, LANES), lambda bi, li: (0, 0, 0)),
                  pl.BlockSpec((nt, LANES), lambda bi, li: (0, 0)),
                  pl.BlockSpec((1, c), lambda bi, li: (0, 0)),
                  pl.BlockSpec((1, c), lambda bi, li: (0, 0))],
        out_specs=pl.BlockSpec((1, tl, c), lambda bi, li: (bi, li, 0)),
        out_shape=jax.ShapeDtypeStruct((b, l, c), BF16),
        scratch_shapes=[pltpu.VMEM(((tl + 2 * CONV_HALO) * nt, LANES), F32), pltpu.VMEM((tl * nt, LANES), F32)],
        compiler_params=_params("parallel", "arbitrary"),
        name="conv_module",
    )(u, u, u, w_dw, b_dw, ln_g, ln_b)


def _merge_kernel(h_ref, a_ref, c_ref, wga_ref, wgc_ref, wa_ref, wc_ref, o_ref):
    h = h_ref[...]
    g_a = _sigmoid(jnp.dot(h, wga_ref[...], preferred_element_type=F32))
    g_c = _sigmoid(jnp.dot(h, wgc_ref[...], preferred_element_type=F32))
    a = jnp.dot(a_ref[...], wa_ref[...].astype(BF16), preferred_element_type=F32)
    cb = jnp.dot(c_ref[...], wc_ref[...].astype(BF16), preferred_element_type=F32)
    o_ref[...] = (g_a * a + g_c * cb).astype(o_ref.dtype)


def _merge(h2d, attn, conv, w_in, gate_off, wa, wc, tm=512, tn=512):
    m, d = h2d.shape
    ka = attn.shape[1]
    kc = conv.shape[1]
    nj = d // tn
    ja = gate_off // tn
    jc = (gate_off + d) // tn
    return pl.pallas_call(
        _merge_kernel,
        grid=(nj, m // tm),
        in_specs=[pl.BlockSpec((tm, d), lambda j, i: (i, 0)),
                  pl.BlockSpec((tm, ka), lambda j, i: (i, 0)),
                  pl.BlockSpec((tm, kc), lambda j, i: (i, 0)),
                  pl.BlockSpec((d, tn), lambda j, i: (0, ja + j)),
                  pl.BlockSpec((d, tn), lambda j, i: (0, jc + j)),
                  pl.BlockSpec((ka, tn), lambda j, i: (0, j)),
                  pl.BlockSpec((kc, tn), lambda j, i: (0, j))],
        out_specs=pl.BlockSpec((tm, tn), lambda j, i: (i, j)),
        out_shape=jax.ShapeDtypeStruct((m, d), BF16),
        compiler_params=_params("parallel", "arbitrary"),
        name="merge_branches",
    )(h2d, attn, conv, w_in, w_in, wa, wc)


def _out_proj_kernel(m_ref, w_ref, x_ref, ga_ref, o_ref):
    acc = jnp.dot(m_ref[0], w_ref[...], preferred_element_type=F32)
    o_ref[0] = x_ref[0] + ga_ref[0] * acc


def _out_proj(mrg, w, x, gate, tm=1024, tn=1024):
    b, l, d = x.shape
    return pl.pallas_call(
        _out_proj_kernel,
        grid=(b, l // tm, d // tn),
        in_specs=[pl.BlockSpec((1, tm, d), lambda bi, i, j: (bi, i, 0)),
                  pl.BlockSpec((d, tn), lambda bi, i, j: (0, j)),
                  pl.BlockSpec((1, tm, tn), lambda bi, i, j: (bi, i, j)),
                  pl.BlockSpec((1, 1, tn), lambda bi, i, j: (bi, 0, j))],
        out_specs=pl.BlockSpec((1, tm, tn), lambda bi, i, j: (bi, i, j)),
        out_shape=jax.ShapeDtypeStruct((b, l, d), F32),
        compiler_params=_params("parallel", "parallel", "arbitrary"),
        name="out_proj_residual",
    )(mrg, w, x, gate)


def _pack_halves(y):
    n = y.shape[1] // 2
    return pltpu.pack_elementwise([y[:, :n], y[:, n:]], packed_dtype=BF16)


def _unpack_halves(p):
    lo = pltpu.unpack_elementwise(p, index=0, packed_dtype=BF16, unpacked_dtype=F32)
    hi = pltpu.unpack_elementwise(p, index=1, packed_dtype=BF16, unpacked_dtype=F32)
    return lo, hi


def _norm2_router_kernel(x_ref, g_ref, sh_ref, sc_ref, wr_ref, br_ref, hp_ref, lg_ref):
    y = _rms(x_ref[0], g_ref[...]) * (1.0 + sc_ref[0]) + sh_ref[0]
    packed = _pack_halves(y)
    tl = packed.shape[0]
    nt = packed.shape[1] // LANES
    for j in range(nt):
        hp_ref[pl.ds(j, tl, stride=nt), :] = packed[:, j * LANES:(j + 1) * LANES]
    lg_ref[0] = jnp.dot(y.astype(BF16), wr_ref[...], preferred_element_type=F32) + br_ref[...]


def _norm2_router(x, g, shift, scale, w_r, b_r, tl):
    b, l, d = x.shape
    nt = d // 2 // LANES
    lb = l // tl
    return pl.pallas_call(
        _norm2_router_kernel,
        grid=(b, lb),
        in_specs=[pl.BlockSpec((1, tl, d), lambda bi, li: (bi, li, 0)),
                  pl.BlockSpec((1, d), lambda bi, li: (0, 0)),
                  pl.BlockSpec((1, 1, d), lambda bi, li: (bi, 0, 0)),
                  pl.BlockSpec((1, 1, d), lambda bi, li: (bi, 0, 0)),
                  pl.BlockSpec((d, ROUTER_LANES), lambda bi, li: (0, 0)),
                  pl.BlockSpec((1, ROUTER_LANES), lambda bi, li: (0, 0))],
        out_specs=[pl.BlockSpec((tl * nt, LANES), lambda bi, li: (bi * lb + li, 0)),
                   pl.BlockSpec((1, tl, ROUTER_LANES), lambda bi, li: (bi, li, 0))],
        out_shape=[jax.ShapeDtypeStruct((b * l * nt, LANES), jnp.int32),
                   jax.ShapeDtypeStruct((b, l, ROUTER_LANES), F32)],
        compiler_params=_params("parallel", "parallel"),
        name="norm2_router",
    )(x, g, shift, scale, w_r, b_r)


def _first_lane(mask, lane):
    return jnp.min(jnp.where(mask, lane, LANES), axis=-1, keepdims=True)


def _route_kernel(lg_ref, meta_ref, wts_ref, cnt_ref, carry_ref):
    @pl.when(pl.program_id(0) == 0)
    def _():
        carry_ref[...] = jnp.zeros_like(carry_ref)

    lg = lg_ref[...]
    tb = lg.shape[0]
    lane = lax.broadcasted_iota(jnp.int32, lg.shape, 1)
    neg_inf = jnp.float32(-jnp.inf)
    is_group = lane < N_GROUPS
    gl = jnp.where(is_group, lg, neg_inf)
    g_max = jnp.max(gl, axis=-1, keepdims=True)
    g_sel = _first_lane(gl == g_max, lane)
    p_g = 1.0 / jnp.sum(jnp.where(is_group, jnp.exp(lg - g_max), 0.0), axis=-1, keepdims=True)

    e_idx = lane - N_GROUPS
    in_group = (e_idx >= g_sel * EXPERTS_PER_GROUP) & (e_idx < (g_sel + 1) * EXPERTS_PER_GROUP)
    ev = jnp.where(in_group, lg, neg_inf)
    v1 = jnp.max(ev, axis=-1, keepdims=True)
    i1 = _first_lane(ev == v1, lane)
    ev2 = jnp.where(lane == i1, neg_inf, ev)
    v2 = jnp.max(ev2, axis=-1, keepdims=True)
    i2 = _first_lane(ev2 == v2, lane)
    t = jnp.exp(v2 - v1)
    w1 = p_g / (1.0 + t)
    w2 = w1 * t

    oh1 = lane == i1
    oh2 = lane == i2
    oh = (oh1 | oh2).astype(BF16)
    earlier = (lax.broadcasted_iota(jnp.int32, (tb, tb), 0) > lax.broadcasted_iota(jnp.int32, (tb, tb), 1)).astype(BF16)
    before = jnp.dot(earlier, oh, preferred_element_type=F32) + carry_ref[...]
    r1 = jnp.sum(jnp.where(oh1, before, 0.0), axis=-1, keepdims=True).astype(jnp.int32)
    r2 = jnp.sum(jnp.where(oh2, before, 0.0), axis=-1, keepdims=True).astype(jnp.int32)
    carry_ref[...] += jnp.sum(oh.astype(F32), axis=0, keepdims=True)

    meta_ref[...] = jnp.where(lane == 0, i1 - N_GROUPS, jnp.where(lane == 1, i2 - N_GROUPS,
                              jnp.where(lane == 2, r1, jnp.where(lane == 3, r2, 0))))
    wts_ref[...] = jnp.where(lane == 0, w1, jnp.where(lane == 1, w2, 0.0))
    cnt_ref[...] = carry_ref[...]


def _route(logits, tb=1024):
    n = logits.shape[0]
    return pl.pallas_call(
        _route_kernel,
        grid=(n // tb,),
        in_specs=[pl.BlockSpec((tb, LANES), lambda i: (i, 0))],
        out_specs=[pl.BlockSpec((tb, LANES), lambda i: (i, 0)),
                   pl.BlockSpec((tb, LANES), lambda i: (i, 0)),
                   pl.BlockSpec((1, LANES), lambda i: (0, 0))],
        out_shape=[jax.ShapeDtypeStruct((n, LANES), jnp.int32),
                   jax.ShapeDtypeStruct((n, LANES), F32),
                   jax.ShapeDtypeStruct((1, LANES), F32)],
        scratch_shapes=[pltpu.VMEM((1, LANES), F32)],
        compiler_params=_params("arbitrary"),
        name="moe_route",
    )(logits)


def _dest_kernel(meta_ref, pst_ref, o_ref):
    meta = meta_ref[...]
    lane = lax.broadcasted_iota(jnp.int32, meta.shape, 1)
    pst = pst_ref[...]

    def row_of(slot):
        e = meta[:, slot:slot + 1]
        start = jnp.sum(jnp.where(lane == e + N_GROUPS, pst, 0), axis=-1, keepdims=True)
        return start + meta[:, TOP_K + slot:TOP_K + slot + 1]

    o_ref[...] = jnp.where(lane == 0, row_of(0), jnp.where(lane == 1, row_of(1), 0))


def _dest_rows(meta, pst, tb=2048):
    n = meta.shape[0]
    return pl.pallas_call(
        _dest_kernel,
        grid=(n // tb,),
        in_specs=[pl.BlockSpec((tb, LANES), lambda i: (i, 0)),
                  pl.BlockSpec((1, LANES), lambda i: (0, 0))],
        out_specs=pl.BlockSpec((tb, LANES), lambda i: (i, 0)),
        out_shape=jax.ShapeDtypeStruct((n, LANES), jnp.int32),
        compiler_params=_params("parallel"),
        name="moe_dest_rows",
    )(meta, pst)


def _zero_tail_kernel(lb_ref, o_ref):
    o_ref[...] = jnp.zeros_like(o_ref)


def _zero_tails(last_part, total, part_rows, nt):
    return pl.pallas_call(
        _zero_tail_kernel,
        grid_spec=pltpu.PrefetchScalarGridSpec(
            num_scalar_prefetch=1,
            grid=(last_part.shape[0],),
            in_specs=[],
            out_specs=pl.BlockSpec((part_rows * nt, LANES), lambda e, lp: (lp[e], 0))),
        out_shape=jax.ShapeDtypeStruct((total * nt, LANES), jnp.int32),
        compiler_params=_params("arbitrary"),
        name="moe_zero_tails",
    )(last_part)


def _dispatch_kernel(dest_ref, hp_ref, xs_in_ref, xs_ref, sem):
    rows = hp_ref.shape[0]

    def slab_copy(r, k):
        return pltpu.make_async_copy(hp_ref.at[r], xs_ref.at[dest_ref[0, 0, r * TOP_K + k]], sem)

    def start(r, c):
        for k in range(TOP_K):
            slab_copy(r, k).start(priority=k % 2)
        return c

    def wait(r, c):
        for k in range(TOP_K):
            slab_copy(r, k).wait()
        return c

    lax.fori_loop(0, rows, start, 0, unroll=DMA_LOOP_UNROLL)
    lax.fori_loop(0, rows, wait, 0, unroll=DMA_LOOP_UNROLL)


def _dispatch(hp3, dest, xs0, rows):
    n, nt, _ = hp3.shape
    steps = n // rows
    return pl.pallas_call(
        _dispatch_kernel,
        grid=(steps,),
        in_specs=[pl.BlockSpec((1, 1, rows * TOP_K), lambda i: (i, 0, 0), memory_space=pltpu.SMEM),
                  pl.BlockSpec((rows, nt, LANES), lambda i: (i, 0, 0)),
                  pl.BlockSpec(memory_space=pl.ANY)],
        out_specs=pl.BlockSpec(memory_space=pl.ANY),
        out_shape=jax.ShapeDtypeStruct(xs0.shape, xs0.dtype),
        scratch_shapes=[pltpu.SemaphoreType.DMA(())],
        input_output_aliases={2: 0},
        compiler_params=_params("arbitrary"),
        name="moe_dispatch",
    )(dest.reshape(steps, 1, rows * TOP_K), hp3, xs0)


def _by_valid_rows(nv, tmb, compute, o_ref):
    quarter = tmb // EXPERT_BLOCK_PARTS
    per_row = o_ref.shape[0] // tmb

    for nq in range(1, EXPERT_BLOCK_PARTS + 1):
        rows = nq * quarter

        @pl.when((nv > rows - quarter) & (nv <= rows))
        def _(rows=rows):
            compute(rows)
            if rows < tmb:
                o_ref[rows * per_row:, :] = jnp.zeros(((tmb - rows) * per_row, o_ref.shape[1]), o_ref.dtype)

    @pl.when(nv == 0)
    def _():
        o_ref[...] = jnp.zeros_like(o_ref)


def _expert_weights(plan, w_hbm, wbuf, sem):
    be_ref, first_ref, slot_ref, nxt_ref = plan
    bi = pl.program_id(0)
    slot = slot_ref[bi]

    def copy(e, s):
        return pltpu.make_async_copy(w_hbm.at[e], wbuf.at[s], sem.at[s])

    @pl.when(bi == 0)
    def _():
        copy(be_ref[0], 0).start()

    @pl.when((first_ref[bi] == 1) & (nxt_ref[bi] >= 0))
    def _():
        copy(nxt_ref[bi], 1 - slot).start()

    @pl.when(first_ref[bi] == 1)
    def _():
        copy(be_ref[bi], slot).wait()

    return wbuf.at[slot]


def _expert_in_kernel(be_ref, first_ref, slot_ref, nxt_ref, nu_ref, nv_ref, x_ref, w_hbm, *rest, nt, tn):
    gate_ref = rest[0] if len(rest) == 4 else None
    o_ref, wbuf, sem = rest[-3:]
    w_ref = _expert_weights((be_ref, first_ref, slot_ref, nxt_ref), w_hbm, wbuf, sem)

    def compute(rows):
        halves = [_unpack_halves(x_ref[pl.ds(j, rows, stride=nt), :]) for j in range(nt)]
        lo = jnp.concatenate([h[0].astype(BF16) for h in halves], axis=1)
        hi = jnp.concatenate([h[1].astype(BF16) for h in halves], axis=1)
        half = nt * LANES
        for c0 in range(0, o_ref.shape[1], tn):
            y = (jnp.dot(lo, w_ref[:half, c0:c0 + tn].astype(BF16), preferred_element_type=F32)
                 + jnp.dot(hi, w_ref[half:, c0:c0 + tn].astype(BF16), preferred_element_type=F32))
            if gate_ref is None:
                y = _silu(y)
            else:
                y = gate_ref[:rows, c0:c0 + tn].astype(F32) * y
            o_ref[:rows, c0:c0 + tn] = y.astype(o_ref.dtype)

    _by_valid_rows(nv_ref[pl.program_id(0)], o_ref.shape[0], compute, o_ref)


N_PLAN = 6


def _blk_clamped(bi, *plan):
    return jnp.minimum(bi, plan[4][0] - 1)


def _expert_in(xs, w, gate, plan, tmb, name, tn=256):
    _, d, ff = w.shape
    nt = d // 2 // LANES
    total = xs.shape[0] // nt
    nblk = total // tmb
    in_specs = [pl.BlockSpec((tmb * nt, LANES), lambda bi, *p: (_blk_clamped(bi, *p), 0)),
                pl.BlockSpec(memory_space=pl.ANY)]
    args = [xs, w]
    if gate is not None:
        in_specs.append(pl.BlockSpec((tmb, ff), lambda bi, *p: (_blk_clamped(bi, *p), 0)))
        args.append(gate)
    return pl.pallas_call(
        functools.partial(_expert_in_kernel, nt=nt, tn=tn),
        grid_spec=pltpu.PrefetchScalarGridSpec(
            num_scalar_prefetch=N_PLAN,
            grid=(nblk,),
            in_specs=in_specs,
            out_specs=pl.BlockSpec((tmb, ff), lambda bi, *p: (bi, 0)),
            scratch_shapes=[pltpu.VMEM((2, d, ff), w.dtype), pltpu.SemaphoreType.DMA((2,))]),
        out_shape=jax.ShapeDtypeStruct((total, ff), BF16),
        compiler_params=_params("arbitrary"),
        name=name,
    )(*plan, *args)


def _expert_down_kernel(be_ref, first_ref, slot_ref, nxt_ref, nu_ref, nv_ref, a_ref, w_hbm, o_ref, wbuf, sem,
                        *, nt, tn):
    tmb = a_ref.shape[0]
    w_ref = _expert_weights((be_ref, first_ref, slot_ref, nxt_ref), w_hbm, wbuf, sem)

    def compute(rows):
        a = a_ref[:rows, :]
        half = nt * LANES
        for c0 in range(0, half, tn):
            ylo = jnp.dot(a, w_ref[:, c0:c0 + tn].astype(BF16), preferred_element_type=F32)
            yhi = jnp.dot(a, w_ref[:, half + c0:half + c0 + tn].astype(BF16), preferred_element_type=F32)
            packed = pltpu.pack_elementwise([ylo, yhi], packed_dtype=BF16)
            for j in range(tn // LANES):
                o_ref[pl.ds(c0 // LANES + j, rows, stride=nt), :] = packed[:, j * LANES:(j + 1) * LANES]

    _by_valid_rows(nv_ref[pl.program_id(0)], tmb, compute, o_ref)


def _expert_down(act, w_down, plan, tmb, tn=512):
    total, ff = act.shape
    d = w_down.shape[2]
    nt = d // 2 // LANES
    nblk = total // tmb
    return pl.pallas_call(
        functools.partial(_expert_down_kernel, nt=nt, tn=tn),
        grid_spec=pltpu.PrefetchScalarGridSpec(
            num_scalar_prefetch=N_PLAN,
            grid=(nblk,),
            in_specs=[pl.BlockSpec((tmb, ff), lambda bi, *p: (_blk_clamped(bi, *p), 0)),
                      pl.BlockSpec(memory_space=pl.ANY)],
            out_specs=pl.BlockSpec((tmb * nt, LANES), lambda bi, *p: (bi, 0)),
            scratch_shapes=[pltpu.VMEM((2, ff, d), w_down.dtype), pltpu.SemaphoreType.DMA((2,))]),
        out_shape=jax.ShapeDtypeStruct((total * nt, LANES), jnp.int32),
        compiler_params=_params("arbitrary"),
        name="expert_down",
    )(*plan, act, w_down)


def _combine_kernel(pos_ref, nxt_ref, ys_ref, w_ref, x_ref, ga_ref, g_ref, o_ref, *scratch):
    n_buf = 2 * TOP_K
    slabs, sems = scratch[:n_buf], scratch[n_buf:]
    rows = x_ref.shape[1]
    hr = rows // 2
    nt = ys_ref.shape[1]
    half = nt * LANES
    step = pl.program_id(0) * pl.num_programs(1) + pl.program_id(1)
    n_steps = pl.num_programs(0) * pl.num_programs(1)

    def slab_copy(idx_ref, h, r, k):
        dst = slabs[h * TOP_K + k].at[pl.ds(pl.multiple_of(r * SLAB_PITCH, SUBLANES), nt)]
        return pltpu.make_async_copy(ys_ref.at[idx_ref[0, 0, (h * hr + r) * TOP_K + k]], dst, sems[h * TOP_K + k])

    def issue(idx_ref, h):
        def body(r, c):
            for k in range(TOP_K):
                slab_copy(idx_ref, h, r, k).start(priority=k % 2)
            return c
        lax.fori_loop(0, hr, body, 0, unroll=DMA_LOOP_UNROLL)

    def wait(idx_ref, h):
        def body(r, c):
            for k in range(TOP_K):
                slab_copy(idx_ref, h, r, k).wait()
            return c
        lax.fori_loop(0, hr, body, 0, unroll=DMA_LOOP_UNROLL)

    def compute(h):
        r0 = h * hr
        w0 = w_ref[r0:r0 + hr, 0:1]
        w1 = w_ref[r0:r0 + hr, 1:2]
        ss = jnp.zeros((hr, 1), F32)
        for j in range(nt):
            lo0, hi0 = _unpack_halves(slabs[h * TOP_K][pl.ds(j, hr, stride=SLAB_PITCH), :])
            lo1, hi1 = _unpack_halves(slabs[h * TOP_K + 1][pl.ds(j, hr, stride=SLAB_PITCH), :])
            for c0, y in ((j * LANES, w0 * lo0 + w1 * lo1), (half + j * LANES, w0 * hi0 + w1 * hi1)):
                z = x_ref[0, r0:r0 + hr, c0:c0 + LANES] + ga_ref[0, :, c0:c0 + LANES] * y
                ss = ss + jnp.sum(z * z, axis=-1, keepdims=True)
                o_ref[0, r0:r0 + hr, c0:c0 + LANES] = z
        inv = lax.rsqrt(ss / (2 * half) + EPS)
        o_ref[0, r0:r0 + hr, :] = o_ref[0, r0:r0 + hr, :] * inv * g_ref[...]

    @pl.when(step == 0)
    def _():
        issue(pos_ref, 0)

    issue(pos_ref, 1)
    wait(pos_ref, 0)
    compute(0)

    @pl.when(step + 1 < n_steps)
    def _():
        issue(nxt_ref, 0)

    wait(pos_ref, 1)
    compute(1)


def _combine(ys3, pos, wts, x, gate, g, rows):
    b, l, d = x.shape
    lb = l // rows
    n_steps = b * lb
    pos3 = pos.reshape(n_steps, 1, rows * TOP_K)
    slab = pltpu.VMEM((rows // 2 * SLAB_PITCH, LANES), ys3.dtype)
    return pl.pallas_call(
        _combine_kernel,
        grid=(b, lb),
        in_specs=[pl.BlockSpec((1, 1, rows * TOP_K), lambda bi, i: (bi * lb + i, 0, 0), memory_space=pltpu.SMEM),
                  pl.BlockSpec((1, 1, rows * TOP_K), lambda bi, i: (jnp.minimum(bi * lb + i + 1, n_steps - 1), 0, 0),
                               memory_space=pltpu.SMEM),
                  pl.BlockSpec(memory_space=pl.ANY),
                  pl.BlockSpec((rows, LANES), lambda bi, i: (bi * lb + i, 0)),
                  pl.BlockSpec((1, rows, d), lambda bi, i: (bi, i, 0)),
                  pl.BlockSpec((1, 1, d), lambda bi, i: (bi, 0, 0)),
                  pl.BlockSpec((1, d), lambda bi, i: (0, 0))],
        out_specs=pl.BlockSpec((1, rows, d), lambda bi, i: (bi, i, 0)),
        out_shape=jax.ShapeDtypeStruct((b, l, d), F32),
        scratch_shapes=[slab] * (2 * TOP_K) + [pltpu.SemaphoreType.DMA(())] * (2 * TOP_K),
        compiler_params=_params("arbitrary", "arbitrary"),
        name="moe_combine_norm",
    )(pos3, pos3, ys3, wts, x, gate, g)


def _block_layout(counts, n_pairs, tmb):
    nblk = (n_pairs + N_EXPERTS * (tmb - 1) + tmb - 1) // tmb
    blocks = (counts + tmb - 1) // tmb
    bend = jnp.cumsum(blocks)
    bstart = bend - blocks
    pstart = bstart * tmb
    n_used = bend[-1]
    ids = jnp.arange(nblk, dtype=jnp.int32)
    blk_e = jnp.minimum(jnp.searchsorted(bend, jnp.minimum(ids, n_used - 1), side="right"),
                        N_EXPERTS - 1).astype(jnp.int32)
    part = tmb // EXPERT_BLOCK_PARTS
    last_part = ((pstart + jnp.maximum(counts - 1, 0) // part * part) // part).astype(jnp.int32)
    used = ids < n_used
    n_valid = jnp.where(used, jnp.clip(counts[blk_e] - (ids - bstart[blk_e]) * tmb, 0, tmb), 0)
    first = used & ((ids == 0) | (blk_e != jnp.roll(blk_e, 1)))
    slot = (jnp.cumsum(first.astype(jnp.int32)) - 1) % 2
    experts = jnp.arange(N_EXPERTS, dtype=jnp.int32)
    later = jnp.where((blocks > 0)[None, :] & (experts[None, :] > experts[:, None]), experts[None, :], N_EXPERTS)
    nxt_e = jnp.min(later, axis=1)
    nxt = jnp.where(nxt_e[blk_e] < N_EXPERTS, nxt_e[blk_e], -1)
    i32 = lambda a: a.astype(jnp.int32)
    plan = (blk_e, i32(first), i32(slot), i32(nxt), i32(n_used).reshape(1), i32(n_valid))
    return nblk, i32(pstart), last_part, plan


def _rope_tables(n_tokens):
    rows = n_tokens // GRID_W
    row, col = jnp.meshgrid(jnp.arange(rows), jnp.arange(GRID_W), indexing="ij")
    pos = jnp.stack([row.reshape(-1), col.reshape(-1)], axis=-1).astype(F32)
    inv = ROPE_THETA ** (-jnp.arange(0, ROPE_AXIS_DIM, 2, dtype=F32) / ROPE_AXIS_DIM)
    ang = pos[:, :, None] * inv[None, None, :]
    cos, sin = jnp.cos(ang), jnp.sin(ang)
    cos_t = jnp.concatenate([cos[:, 0], cos[:, 0], cos[:, 1], cos[:, 1]], axis=-1)
    sin_t = jnp.concatenate([-sin[:, 0], sin[:, 0], -sin[:, 1], sin[:, 1]], axis=-1)
    return cos_t, sin_t


def kernel(x, c, ctx, c_ctx, norm1_g, w_mod, b_mod, w_in, q_norm_g, k_norm_g, w_attn_out, conv_dw_w, conv_dw_b, conv_ln_g, conv_ln_b, w_conv_out, w_out, norm2_g, w_router_group, b_router_group, w_router_expert, b_router_expert, w_exp_gate, w_exp_up, w_exp_down, norm_f_g):
    b, s, d = x.shape
    n_ctx = ctx.shape[1]
    assert w_in.shape[0] == 1, "single-layer stack"
    conv_width = conv_dw_w.shape[-1]
    k_off = ATTN_WIDTH
    glu_off = k_off + 2 * KV_WIDTH
    gate_off = glu_off + 2 * conv_width

    n_c = b + 1
    cvec = jnp.zeros((SUBLANES * ((n_c + SUBLANES - 1) // SUBLANES), d), F32).at[:b].set(c).at[b].set(c_ctx)
    mod = _mod_vectors(cvec, w_mod[0], b_mod.reshape(1, -1))
    sh1, sc1, ga1, sh2, sc2, ga2 = [mod[:b, i * d:(i + 1) * d].reshape(b, 1, d) for i in range(N_MOD)]
    csh1, csc1 = [mod[b:b + 1, i * d:(i + 1) * d].reshape(1, 1, d) for i in range(2)]

    g1 = norm1_g.reshape(1, d)
    h = _norm_mod(x, g1, sh1, sc1, tl=512)
    hc = _norm_mod(ctx, g1, csh1, csc1, tl=n_ctx)
    w_in_b = w_in[0].astype(BF16)
    cos_t, sin_t = _rope_tables(s)
    qg = q_norm_g.reshape(1, HEAD_DIM)
    kg = k_norm_g.reshape(1, HEAD_DIM)
    q = _q_proj(h, w_in_b, qg, cos_t, sin_t, 0, HEAD_DIM ** -0.5 * LOG2E, tm=1024)
    k, v = _kv_proj(h, w_in_b, kg, cos_t, sin_t, k_off, tm=1024)
    kc, vc = _kv_proj(hc, w_in_b, kg, None, None, k_off, tm=n_ctx)
    attn = _attention(q, k, v, kc, vc, tq=128, n_groups=1, tk=1024)

    h2d = h.reshape(b * s, d)
    u = _glu_proj(h2d, w_in_b, glu_off, conv_width, tm=1024)
    conv = _conv_module(u.reshape(b, s, conv_width), conv_dw_w.reshape(CONV_TAPS, conv_width // LANES, LANES),
                        conv_dw_b.reshape(conv_width // LANES, LANES), conv_ln_g.reshape(1, -1),
                        conv_ln_b.reshape(1, -1), tl=512)
    mrg = _merge(h2d, attn.reshape(b * s, ATTN_WIDTH), conv.reshape(b * s, conv_width), w_in_b, gate_off,
                 w_attn_out[0], w_conv_out[0])
    x1 = _out_proj(mrg.reshape(b, s, d), w_out[0].astype(BF16), x, ga1)

    w_r = jnp.zeros((d, ROUTER_LANES), F32).at[:, :N_GROUPS].set(w_router_group[0]) \
        .at[:, N_GROUPS:N_GROUPS + N_EXPERTS].set(w_router_expert[0])
    b_r = jnp.zeros((1, ROUTER_LANES), F32).at[0, :N_GROUPS].set(b_router_group[0]) \
        .at[0, N_GROUPS:N_GROUPS + N_EXPERTS].set(b_router_expert[0])
    hp, logits = _norm2_router(x1, norm2_g.reshape(1, d), sh2, sc2, w_r.astype(BF16), b_r, tl=512)
    n = b * s
    nt = d // 2 // LANES
    tmb = 512
    meta, wts, cnt = _route(logits.reshape(n, ROUTER_LANES))
    counts = cnt[0, N_GROUPS:N_GROUPS + N_EXPERTS].astype(jnp.int32)
    nblk, pstart, last_part, plan = _block_layout(counts, n * TOP_K, tmb)
    pst = jnp.zeros((1, LANES), jnp.int32).at[0, N_GROUPS:N_GROUPS + N_EXPERTS].set(pstart)
    dest = _dest_rows(meta, pst)[:, :TOP_K].reshape(-1)
    xs0 = _zero_tails(last_part, nblk * tmb, tmb // EXPERT_BLOCK_PARTS, nt)
    xs = _dispatch(hp.reshape(n, nt, LANES), dest, xs0.reshape(nblk * tmb, nt, LANES), rows=1024)
    xs2 = xs.reshape(nblk * tmb * nt, LANES)
    sg = _expert_in(xs2, w_exp_gate[0], None, plan, tmb, "expert_gate")
    act = _expert_in(xs2, w_exp_up[0], sg, plan, tmb, "expert_up")
    ys = _expert_down(act, w_exp_down[0], plan, tmb)
    return _combine(ys.reshape(nblk * tmb, nt, LANES), dest, wts, x1, ga2, norm_f_g.reshape(1, d), rows=512)
```
